```python
import math
import jax, jax.numpy as jnp
from jax import lax
import numpy as np

D_MODEL = 1024
BATCH = 8
SEQ = 4096
DEPTH = 2

N_BRANCH = 4
BR_WIDTH = D_MODEL // N_BRANCH
HEAD_DIM = 64
N_HEADS = BR_WIDTH // HEAD_DIM
CHUNK = 128
Q_BLOCK = 128
SHORT_CONV = 3
CONF_CONV = 31
EPS = 1e-6

SPLIT_SIZES = (
    2 * BR_WIDTH,
    BR_WIDTH,
    3 * BR_WIDTH,
    N_HEADS,
    BR_WIDTH,
    3 * BR_WIDTH,
    BR_WIDTH,
    2 * BR_WIDTH,
    BR_WIDTH,
    N_BRANCH * D_MODEL,
)
IN_COLS = sum(SPLIT_SIZES)
SPLIT_POINTS = tuple(int(v) for v in np.cumsum(SPLIT_SIZES)[:-1])

kernel_name = "hybrid_gated_parallel_mixers"


def rmsnorm(x, g):
    x32 = x.astype(jnp.float32)
    y = x32 * lax.rsqrt(jnp.mean(x32 * x32, axis=-1, keepdims=True) + EPS)
    return y.astype(x.dtype) * g


def layernorm(x, g, b):
    x32 = x.astype(jnp.float32)
    mu = jnp.mean(x32, axis=-1, keepdims=True)
    xc = x32 - mu
    var = jnp.mean(xc * xc, axis=-1, keepdims=True)
    return (xc * lax.rsqrt(var + EPS)).astype(x.dtype) * g + b


def causal_depthwise_conv(x, w):
    k_width, ch = w.shape
    return lax.conv_general_dilated(
        x, w.astype(x.dtype)[:, None, :],
        window_strides=(1,), padding=[(k_width - 1, 0)],
        dimension_numbers=("NWC", "WIO", "NWC"),
        feature_group_count=ch)


def gmlp_chunk_mixer(uv, sgu_w, sgu_b, ln_g, ln_b):
    bsz, seq, _ = uv.shape
    u, v = jnp.split(jax.nn.gelu(uv), 2, axis=-1)
    v = layernorm(v, ln_g, ln_b)
    n_chunks = seq // CHUNK
    vc = v.reshape(bsz, n_chunks, CHUNK, N_HEADS, HEAD_DIM)
    causal = jnp.tril(jnp.ones((CHUNK, CHUNK), dtype=bool))
    w = jnp.where(causal[None], sgu_w, jnp.zeros_like(sgu_w))
    mixed = jnp.einsum("hts,bnshd->bnthd", w, vc) + sgu_b.T[None, None, :, :, None]
    return u * mixed.reshape(bsz, seq, BR_WIDTH)


def forgetting_attention(qkv, f_logit, f_bias):
    bsz, seq, _ = qkv.shape
    q, k, v = jnp.split(qkv, 3, axis=-1)
    q = q.reshape(bsz, seq, N_HEADS, HEAD_DIM)
    k = k.reshape(bsz, seq, N_HEADS, HEAD_DIM)
    v = v.reshape(bsz, seq, N_HEADS, HEAD_DIM)
    log_f = jax.nn.log_sigmoid(f_logit.astype(jnp.float32) + f_bias.astype(jnp.float32))
    cum = jnp.cumsum(log_f, axis=1).transpose(0, 2, 1)
    n_blocks = seq // Q_BLOCK
    q_blocks = q.reshape(bsz, n_blocks, Q_BLOCK, N_HEADS, HEAD_DIM).transpose(1, 0, 2, 3, 4)
    c_blocks = cum.reshape(bsz, N_HEADS, n_blocks, Q_BLOCK).transpose(2, 0, 1, 3)
    key_pos = jnp.arange(seq)
    scale = 1.0 / math.sqrt(HEAD_DIM)
    neg = jnp.finfo(jnp.float32).min

    def one_block(args):
        qb, cb, idx = args
        s = jnp.einsum("bqhd,bkhd->bhqk", qb, k).astype(jnp.float32) * scale
        s = s + (cb[..., :, None] - cum[:, :, None, :])
        q_pos = idx * Q_BLOCK + jnp.arange(Q_BLOCK)
        mask = key_pos[None, :] <= q_pos[:, None]
        p = jax.nn.softmax(jnp.where(mask, s, neg), axis=-1).astype(v.dtype)
        return jnp.einsum("bhqk,bkhd->bqhd", p, v)

    out = lax.map(one_block, (q_blocks, c_blocks, jnp.arange(n_blocks)))
    return out.transpose(1, 0, 2, 3, 4).reshape(bsz, seq, BR_WIDTH)


def short_gated_conv(cin, conv_w):
    b_gate, c_gate, xin = jnp.split(cin, 3, axis=-1)
    return b_gate * causal_depthwise_conv(c_gate * xin, conv_w)


def conformer_conv(glu_in, dw_w, dw_b, ln_g, ln_b):
    a, g = jnp.split(glu_in, 2, axis=-1)
    h = a * jax.nn.sigmoid(g)
    h = causal_depthwise_conv(h, dw_w) + dw_b
    return jax.nn.silu(layernorm(h, ln_g, ln_b))


def _fwd_setup_inputs(seed: int = 0) -> dict:
    key = jax.random.key(seed)
    ks = jax.random.split(key, 18)
    f32 = jnp.float32
    nrm = lambda k, shape, s: jax.random.normal(k, shape, f32) * s
    return {
        "x": jax.random.normal(ks[0], (BATCH, SEQ, D_MODEL), f32),
        "norm_g": 1.0 + nrm(ks[1], (DEPTH, D_MODEL), 0.02),
        "w_in": nrm(ks[2], (DEPTH, D_MODEL, IN_COLS), D_MODEL ** -0.5),
        "f_bias": jax.random.uniform(ks[3], (DEPTH, N_HEADS), f32, 1.0, 4.0),
        "sgu_w": nrm(ks[4], (DEPTH, N_HEADS, CHUNK, CHUNK), CHUNK ** -0.5),
        "sgu_b": 1.0 + nrm(ks[5], (DEPTH, N_HEADS, CHUNK), 0.02),
        "sgu_ln_g": 1.0 + nrm(ks[6], (DEPTH, BR_WIDTH), 0.02),
        "sgu_ln_b": nrm(ks[7], (DEPTH, BR_WIDTH), 0.02),
        "short_conv_w": nrm(ks[8], (DEPTH, SHORT_CONV, BR_WIDTH), SHORT_CONV ** -0.5),
        "conf_dw_w": nrm(ks[9], (DEPTH, CONF_CONV, BR_WIDTH), CONF_CONV ** -0.5),
        "conf_dw_b": nrm(ks[10], (DEPTH, BR_WIDTH), 0.02),
        "conf_ln_g": 1.0 + nrm(ks[11], (DEPTH, BR_WIDTH), 0.02),
        "conf_ln_b": nrm(ks[12], (DEPTH, BR_WIDTH), 0.02),
        "w_branch": nrm(ks[13], (DEPTH, N_BRANCH, BR_WIDTH, D_MODEL), BR_WIDTH ** -0.5),
        "w_out": nrm(ks[14], (DEPTH, D_MODEL, D_MODEL), D_MODEL ** -0.5),
        "final_g": 1.0 + nrm(ks[15], (D_MODEL,), 0.02),
    }


def _fwd_reference(x, norm_g, w_in, f_bias, sgu_w, sgu_b, sgu_ln_g, sgu_ln_b,
              short_conv_w, conf_dw_w, conf_dw_b, conf_ln_g, conf_ln_b,
              w_branch, w_out, final_g):
    bsz, seq, _ = x.shape
    for layer in range(DEPTH):
        h = rmsnorm(x, norm_g[layer])
        proj = jnp.einsum("bsd,dc->bsc", h, w_in[layer])
        (a_uv, a_gate, b_qkv, b_f, b_gate, c_in, c_gate,
         d_glu, d_gate, merge_logits) = jnp.split(proj, SPLIT_POINTS, axis=-1)

        y_a = gmlp_chunk_mixer(a_uv, sgu_w[layer], sgu_b[layer],
                               sgu_ln_g[layer], sgu_ln_b[layer]) * jax.nn.silu(a_gate)
        y_b = forgetting_attention(b_qkv, b_f, f_bias[layer]) * jax.nn.silu(b_gate)
        y_c = short_gated_conv(c_in, short_conv_w[layer]) * jax.nn.silu(c_gate)
        y_d = conformer_conv(d_glu, conf_dw_w[layer], conf_dw_b[layer],
                             conf_ln_g[layer], conf_ln_b[layer]) * jax.nn.silu(d_gate)

        branches = jnp.stack([y_a, y_b, y_c, y_d], axis=2)
        projected = jnp.einsum("bsnc,ncd->bsnd", branches, w_branch[layer])
        gates = jax.nn.sigmoid(merge_logits.reshape(bsz, seq, N_BRANCH, D_MODEL))
        merged = jnp.sum(gates * projected, axis=2)
        x = x + jnp.einsum("bsd,de->bse", merged, w_out[layer])
    return rmsnorm(x, final_g)


import jax as _jax
import jax.numpy as _jnp

TWIN_FORMAT = 'train_step'
FWD_PARAMS = ['x', 'norm_g', 'w_in', 'f_bias', 'sgu_w', 'sgu_b', 'sgu_ln_g', 'sgu_ln_b', 'short_conv_w', 'conf_dw_w', 'conf_dw_b', 'conf_ln_g', 'conf_ln_b', 'w_branch', 'w_out', 'final_g']
TWIN_WEIGHTS = ['norm_g', 'w_in', 'f_bias', 'sgu_w', 'sgu_b', 'sgu_ln_g', 'sgu_ln_b', 'short_conv_w', 'conf_dw_w', 'conf_dw_b', 'conf_ln_g', 'conf_ln_b', 'w_branch', 'w_out', 'final_g']
TWIN_DIFF_INPUT = 'x'
TWIN_INPUTS = ['x', 'norm_g', 'w_in', 'f_bias', 'sgu_w', 'sgu_b', 'sgu_ln_g', 'sgu_ln_b', 'short_conv_w', 'conf_dw_w', 'conf_dw_b', 'conf_ln_g', 'conf_ln_b', 'w_branch', 'w_out', 'final_g', 'loss_target', 'm_norm_g', 'm_w_in', 'm_f_bias', 'm_sgu_w', 'm_sgu_b', 'm_sgu_ln_g', 'm_sgu_ln_b', 'm_short_conv_w', 'm_conf_dw_w', 'm_conf_dw_b', 'm_conf_ln_g', 'm_conf_ln_b', 'm_w_branch', 'm_w_out', 'm_final_g', 'v_norm_g', 'v_w_in', 'v_f_bias', 'v_sgu_w', 'v_sgu_b', 'v_sgu_ln_g', 'v_sgu_ln_b', 'v_short_conv_w', 'v_conf_dw_w', 'v_conf_dw_b', 'v_conf_ln_g', 'v_conf_ln_b', 'v_w_branch', 'v_w_out', 'v_final_g']
TWIN_OUTPUTS = ['loss', 'grad_x', 'grad_norm_g', 'grad_w_in', 'grad_f_bias', 'grad_sgu_w', 'grad_sgu_b', 'grad_sgu_ln_g', 'grad_sgu_ln_b', 'grad_short_conv_w', 'grad_conf_dw_w', 'grad_conf_dw_b', 'grad_conf_ln_g', 'grad_conf_ln_b', 'grad_w_branch', 'grad_w_out', 'grad_final_g', 'delta_norm_g', 'delta_w_in', 'delta_f_bias', 'delta_sgu_w', 'delta_sgu_b', 'delta_sgu_ln_g', 'delta_sgu_ln_b', 'delta_short_conv_w', 'delta_conf_dw_w', 'delta_conf_dw_b', 'delta_conf_ln_g', 'delta_conf_ln_b', 'delta_w_branch', 'delta_w_out', 'delta_final_g', 'new_m_norm_g', 'new_m_w_in', 'new_m_f_bias', 'new_m_sgu_w', 'new_m_sgu_b', 'new_m_sgu_ln_g', 'new_m_sgu_ln_b', 'new_m_short_conv_w', 'new_m_conf_dw_w', 'new_m_conf_dw_b', 'new_m_conf_ln_g', 'new_m_conf_ln_b', 'new_m_w_branch', 'new_m_w_out', 'new_m_final_g', 'new_v_norm_g', 'new_v_w_in', 'new_v_f_bias', 'new_v_sgu_w', 'new_v_sgu_b', 'new_v_sgu_ln_g', 'new_v_sgu_ln_b', 'new_v_short_conv_w', 'new_v_conf_dw_w', 'new_v_conf_dw_b', 'new_v_conf_ln_g', 'new_v_conf_ln_b', 'new_v_w_branch', 'new_v_w_out', 'new_v_final_g']
TWIN_LEAF_KINDS = {'loss': 'loss', 'grad_x': 'grad_x', 'grad_norm_g': 'grad_w', 'grad_w_in': 'grad_w', 'grad_f_bias': 'grad_w', 'grad_sgu_w': 'grad_w', 'grad_sgu_b': 'grad_w', 'grad_sgu_ln_g': 'grad_w', 'grad_sgu_ln_b': 'grad_w', 'grad_short_conv_w': 'grad_w', 'grad_conf_dw_w': 'grad_w', 'grad_conf_dw_b': 'grad_w', 'grad_conf_ln_g': 'grad_w', 'grad_conf_ln_b': 'grad_w', 'grad_w_branch': 'grad_w', 'grad_w_out': 'grad_w', 'grad_final_g': 'grad_w', 'delta_norm_g': 'delta_w', 'delta_w_in': 'delta_w', 'delta_f_bias': 'delta_w', 'delta_sgu_w': 'delta_w', 'delta_sgu_b': 'delta_w', 'delta_sgu_ln_g': 'delta_w', 'delta_sgu_ln_b': 'delta_w', 'delta_short_conv_w': 'delta_w', 'delta_conf_dw_w': 'delta_w', 'delta_conf_dw_b': 'delta_w', 'delta_conf_ln_g': 'delta_w', 'delta_conf_ln_b': 'delta_w', 'delta_w_branch': 'delta_w', 'delta_w_out': 'delta_w', 'delta_final_g': 'delta_w', 'new_m_norm_g': 'new_m', 'new_m_w_in': 'new_m', 'new_m_f_bias': 'new_m', 'new_m_sgu_w': 'new_m', 'new_m_sgu_b': 'new_m', 'new_m_sgu_ln_g': 'new_m', 'new_m_sgu_ln_b': 'new_m', 'new_m_short_conv_w': 'new_m', 'new_m_conf_dw_w': 'new_m', 'new_m_conf_dw_b': 'new_m', 'new_m_conf_ln_g': 'new_m', 'new_m_conf_ln_b': 'new_m', 'new_m_w_branch': 'new_m', 'new_m_w_out': 'new_m', 'new_m_final_g': 'new_m', 'new_v_norm_g': 'new_v', 'new_v_w_in': 'new_v', 'new_v_f_bias': 'new_v', 'new_v_sgu_w': 'new_v', 'new_v_sgu_b': 'new_v', 'new_v_sgu_ln_g': 'new_v', 'new_v_sgu_ln_b': 'new_v', 'new_v_short_conv_w': 'new_v', 'new_v_conf_dw_w': 'new_v', 'new_v_conf_dw_b': 'new_v', 'new_v_conf_ln_g': 'new_v', 'new_v_conf_ln_b': 'new_v', 'new_v_w_branch': 'new_v', 'new_v_w_out': 'new_v', 'new_v_final_g': 'new_v'}


def _forward(args):
    return _fwd_reference(*[args[k] for k in FWD_PARAMS])


def _output_shape():
    out = _jax.eval_shape(lambda: _forward(_fwd_setup_inputs(0)))
    return out.shape, out.dtype

N_MICROBATCH = 1
ADAM_LR = 0.001
ADAM_B1 = 0.9
ADAM_B2 = 0.999
ADAM_EPS = 1e-08
ADAM_WD = 0.01
ADAM_STEP = 10
PER_EXAMPLE_BATCH_AXIS = {'x': 0, 'loss_target': 0}
SHARED_INPUTS = []
_WEIGHT_DTYPES = {'norm_g': _jnp.float32, 'w_in': _jnp.float32, 'f_bias': _jnp.float32, 'sgu_w': _jnp.float32, 'sgu_b': _jnp.float32, 'sgu_ln_g': _jnp.float32, 'sgu_ln_b': _jnp.float32, 'short_conv_w': _jnp.float32, 'conf_dw_w': _jnp.float32, 'conf_dw_b': _jnp.float32, 'conf_ln_g': _jnp.float32, 'conf_ln_b': _jnp.float32, 'w_branch': _jnp.float32, 'w_out': _jnp.float32, 'final_g': _jnp.float32}
MOMENT_SCALE = {'norm_g': 1.457413e-01, 'w_in': 5.291822e-02, 'f_bias': 1.936622e-01, 'sgu_w': 3.571312e-02, 'sgu_b': 5.374574e-02, 'sgu_ln_g': 5.039171e-02, 'sgu_ln_b': 5.016007e-02, 'short_conv_w': 1.089323e-01, 'conf_dw_w': 6.854213e-02, 'conf_dw_b': 1.404139e-01, 'conf_ln_g': 7.620797e-02, 'conf_ln_b': 6.552810e-02, 'w_branch': 3.869584e-02, 'w_out': 7.745748e-02, 'final_g': 3.194655e+01}


def _to_microbatches(a, axis):
    t = _jnp.moveaxis(a, axis, 0)
    t = t.reshape((N_MICROBATCH, t.shape[0] // N_MICROBATCH) + t.shape[1:])
    return _jnp.moveaxis(t, 1, axis + 1)


def setup_inputs(seed: int = 0) -> dict:
    inp = _fwd_setup_inputs(seed)
    key = _jax.random.fold_in(_jax.random.key(seed), 7919)
    shape, _ = _output_shape()
    out = dict(inp)
    out["loss_target"] = _jax.random.normal(_jax.random.fold_in(key, 0), shape, _jnp.float32)
    for i, name in enumerate(TWIN_WEIGHTS):
        w = inp[name].astype(_jnp.float32)
        if MOMENT_SCALE is None:
            s = _jnp.sqrt(_jnp.mean(_jnp.square(w)) + 1e-30)
        else:
            s = MOMENT_SCALE[name]
        km, kv = _jax.random.split(_jax.random.fold_in(key, i + 1))
        out[name] = w
        out["m_" + name] = s * _jax.random.normal(km, w.shape, _jnp.float32)
        out["v_" + name] = (s * s) * _jax.random.uniform(kv, w.shape, _jnp.float32, 0.5, 1.5)
    if N_MICROBATCH > 1:
        for name, axis in PER_EXAMPLE_BATCH_AXIS.items():
            out[name] = _to_microbatches(out[name], axis)
    return {'x': out['x'], 'norm_g': out['norm_g'], 'w_in': out['w_in'], 'f_bias': out['f_bias'], 'sgu_w': out['sgu_w'], 'sgu_b': out['sgu_b'], 'sgu_ln_g': out['sgu_ln_g'], 'sgu_ln_b': out['sgu_ln_b'], 'short_conv_w': out['short_conv_w'], 'conf_dw_w': out['conf_dw_w'], 'conf_dw_b': out['conf_dw_b'], 'conf_ln_g': out['conf_ln_g'], 'conf_ln_b': out['conf_ln_b'], 'w_branch': out['w_branch'], 'w_out': out['w_out'], 'final_g': out['final_g'], 'loss_target': out['loss_target'], 'm_norm_g': out['m_norm_g'], 'm_w_in': out['m_w_in'], 'm_f_bias': out['m_f_bias'], 'm_sgu_w': out['m_sgu_w'], 'm_sgu_b': out['m_sgu_b'], 'm_sgu_ln_g': out['m_sgu_ln_g'], 'm_sgu_ln_b': out['m_sgu_ln_b'], 'm_short_conv_w': out['m_short_conv_w'], 'm_conf_dw_w': out['m_conf_dw_w'], 'm_conf_dw_b': out['m_conf_dw_b'], 'm_conf_ln_g': out['m_conf_ln_g'], 'm_conf_ln_b': out['m_conf_ln_b'], 'm_w_branch': out['m_w_branch'], 'm_w_out': out['m_w_out'], 'm_final_g': out['m_final_g'], 'v_norm_g': out['v_norm_g'], 'v_w_in': out['v_w_in'], 'v_f_bias': out['v_f_bias'], 'v_sgu_w': out['v_sgu_w'], 'v_sgu_b': out['v_sgu_b'], 'v_sgu_ln_g': out['v_sgu_ln_g'], 'v_sgu_ln_b': out['v_sgu_ln_b'], 'v_short_conv_w': out['v_short_conv_w'], 'v_conf_dw_w': out['v_conf_dw_w'], 'v_conf_dw_b': out['v_conf_dw_b'], 'v_conf_ln_g': out['v_conf_ln_g'], 'v_conf_ln_b': out['v_conf_ln_b'], 'v_w_branch': out['v_w_branch'], 'v_w_out': out['v_w_out'], 'v_final_g': out['v_final_g']}


def _loss(weights, diff, rest, loss_target):
    with _jax.named_scope("forward"):
        args = {**rest, TWIN_DIFF_INPUT: diff, **{k: w.astype(_WEIGHT_DTYPES[k]) for k, w in weights.items()}}
        y = _forward(args)
    with _jax.named_scope("loss_head"):
        err = _jnp.square(y.astype(_jnp.float32) - loss_target)
        return 0.5 * _jnp.sum(_jnp.mean(err, axis=-1)) if err.ndim else 0.5 * err


def _adamw(w, g, m, v):
    m = ADAM_B1 * m + (1.0 - ADAM_B1) * g
    v = ADAM_B2 * v + (1.0 - ADAM_B2) * _jnp.square(g)
    m_hat = m / (1.0 - ADAM_B1 ** ADAM_STEP)
    v_hat = v / (1.0 - ADAM_B2 ** ADAM_STEP)
    delta = -ADAM_LR * (m_hat / (_jnp.sqrt(v_hat) + ADAM_EPS) + ADAM_WD * w)
    return delta, m, v


def reference(x, norm_g, w_in, f_bias, sgu_w, sgu_b, sgu_ln_g, sgu_ln_b, short_conv_w, conf_dw_w, conf_dw_b, conf_ln_g, conf_ln_b, w_branch, w_out, final_g, loss_target, m_norm_g, m_w_in, m_f_bias, m_sgu_w, m_sgu_b, m_sgu_ln_g, m_sgu_ln_b, m_short_conv_w, m_conf_dw_w, m_conf_dw_b, m_conf_ln_g, m_conf_ln_b, m_w_branch, m_w_out, m_final_g, v_norm_g, v_w_in, v_f_bias, v_sgu_w, v_sgu_b, v_sgu_ln_g, v_sgu_ln_b, v_short_conv_w, v_conf_dw_w, v_conf_dw_b, v_conf_ln_g, v_conf_ln_b, v_w_branch, v_w_out, v_final_g):
    given = dict(x=x, norm_g=norm_g, w_in=w_in, f_bias=f_bias, sgu_w=sgu_w, sgu_b=sgu_b, sgu_ln_g=sgu_ln_g, sgu_ln_b=sgu_ln_b, short_conv_w=short_conv_w, conf_dw_w=conf_dw_w, conf_dw_b=conf_dw_b, conf_ln_g=conf_ln_g, conf_ln_b=conf_ln_b, w_branch=w_branch, w_out=w_out, final_g=final_g, loss_target=loss_target, m_norm_g=m_norm_g, m_w_in=m_w_in, m_f_bias=m_f_bias, m_sgu_w=m_sgu_w, m_sgu_b=m_sgu_b, m_sgu_ln_g=m_sgu_ln_g, m_sgu_ln_b=m_sgu_ln_b, m_short_conv_w=m_short_conv_w, m_conf_dw_w=m_conf_dw_w, m_conf_dw_b=m_conf_dw_b, m_conf_ln_g=m_conf_ln_g, m_conf_ln_b=m_conf_ln_b, m_w_branch=m_w_branch, m_w_out=m_w_out, m_final_g=m_final_g, v_norm_g=v_norm_g, v_w_in=v_w_in, v_f_bias=v_f_bias, v_sgu_w=v_sgu_w, v_sgu_b=v_sgu_b, v_sgu_ln_g=v_sgu_ln_g, v_sgu_ln_b=v_sgu_ln_b, v_short_conv_w=v_short_conv_w, v_conf_dw_w=v_conf_dw_w, v_conf_dw_b=v_conf_dw_b, v_conf_ln_g=v_conf_ln_g, v_conf_ln_b=v_conf_ln_b, v_w_branch=v_w_branch, v_w_out=v_w_out, v_final_g=v_final_g)
    weights = {n: given[n] for n in TWIN_WEIGHTS}
    shared = {n: given[n] for n in SHARED_INPUTS}
    per_example = {n: given[n] for n in ['x']}
    grad_fn = _jax.value_and_grad(_loss, argnums=(0, 1))

    def one_microbatch(ex, loss_target):
        ex = dict(ex)
        diff = ex.pop(TWIN_DIFF_INPUT)
        return grad_fn(weights, diff, {**shared, **ex}, loss_target)

    if N_MICROBATCH == 1:
        loss, (grad_w, grad_x) = one_microbatch(per_example, given["loss_target"])
    else:
        def body(carry, xs):
            loss_sum, grad_sum = carry
            l_k, (gw_k, gx_k) = one_microbatch(xs[0], xs[1])
            with _jax.named_scope("update"):
                return (loss_sum + l_k, _jax.tree.map(_jnp.add, grad_sum, gw_k)), gx_k

        init = (_jnp.zeros((), _jnp.float32), _jax.tree.map(_jnp.zeros_like, weights))
        (loss, grad_w), grad_x = _jax.lax.scan(body, init, (per_example, given["loss_target"]))
    with _jax.named_scope("update"):
        delta_w, new_m, new_v = {}, {}, {}
        for n in TWIN_WEIGHTS:
            delta_w[n], new_m[n], new_v[n] = _adamw(weights[n], grad_w[n], given["m_" + n], given["v_" + n])
    return (loss, grad_x, *[grad_w[n] for n in TWIN_WEIGHTS], *[delta_w[n] for n in TWIN_WEIGHTS],
            *[new_m[n] for n in TWIN_WEIGHTS], *[new_v[n] for n in TWIN_WEIGHTS])
```

```python
import functools
import math

import jax
import jax.numpy as jnp
from jax import lax
from jax.experimental import pallas as pl
from jax.experimental.pallas import tpu as pltpu

F32 = jnp.float32
BF16 = jnp.bfloat16

D_MODEL = 1024
DEPTH = 2
N_BRANCH = 4
BR = 256
N_HEADS = 4
HEAD_DIM = 64
CHUNK = 128
SHORT_CONV = 3
CONF_CONV = 31
EPS = 1e-6
N_DEV = 8

ADAM_LR = 0.001
ADAM_B1 = 0.9
ADAM_B2 = 0.999
ADAM_EPS = 1e-08
ADAM_WD = 0.01
ADAM_STEP = 10

O_UV, O_AG, O_QKV, O_BG, O_DG, O_CIN, O_CG, O_GLU = 0, 512, 768, 1536, 1792, 2048, 2816, 3072
N_MIX = 3584
N_MERGE = N_BRANCH * D_MODEL
N_F = 128
IN_COLS = 7684
HALO = 32
TM = 512
TMG = 256
BQ = 512
SUB = 64
VMEM_LIMIT = 56 * 1024 * 1024
NEG = -1e30
SCALE = 1.0 / math.sqrt(HEAD_DIM)
GELU_K = math.sqrt(2.0 / math.pi)


def _cp(sem=None):
    return pltpu.CompilerParams(dimension_semantics=sem, vmem_limit_bytes=VMEM_LIMIT)


def _sig(x):
    return 1.0 / (1.0 + jnp.exp(-x))


def _silu(x):
    return x * _sig(x)


def _dsilu(x):
    s = _sig(x)
    return s * (1.0 + x * (1.0 - s))


def _gelu(x):
    return 0.5 * x * (1.0 + jnp.tanh(GELU_K * (x + 0.044715 * x * x * x)))


def _dgelu(x):
    t = jnp.tanh(GELU_K * (x + 0.044715 * x * x * x))
    return 0.5 * (1.0 + t) + 0.5 * x * (1.0 - t * t) * GELU_K * (1.0 + 3.0 * 0.044715 * x * x)


def _ln_hat(x):
    mu = jnp.mean(x, axis=-1, keepdims=True)
    xc = x - mu
    rs = lax.rsqrt(jnp.mean(xc * xc, axis=-1, keepdims=True) + EPS)
    return xc * rs, rs


def _ln_bwd(dhat, hat, rs):
    return rs * (dhat - jnp.mean(dhat, axis=-1, keepdims=True) - hat * jnp.mean(dhat * hat, axis=-1, keepdims=True))


def _dot(a, b):
    return jnp.dot(a, b, preferred_element_type=F32)


def _dot_nt(a, b):
    return lax.dot_general(a, b, (((1,), (1,)), ((), ())), preferred_element_type=F32)


def _dot_tn(a, b):
    return lax.dot_general(a, b, (((0,), (0,)), ((), ())), preferred_element_type=F32)


def _fold8(x):
    acc = x[0:8]
    for r in range(1, x.shape[0] // 8):
        acc = acc + x[8 * r:8 * r + 8]
    return acc


def _rms_fwd(x, g, name):
    s = x.shape[0]

    def body(x_ref, g_ref, h_ref):
        xv = x_ref[...]
        r = lax.rsqrt(jnp.mean(xv * xv, axis=-1, keepdims=True) + EPS)
        h_ref[...] = (xv * r * g_ref[...]).astype(BF16)

    return pl.pallas_call(
        body, grid=(s // TM,),
        in_specs=[pl.BlockSpec((TM, D_MODEL), lambda i: (i, 0)), pl.BlockSpec((1, D_MODEL), lambda i: (0, 0))],
        out_specs=pl.BlockSpec((TM, D_MODEL), lambda i: (i, 0)),
        out_shape=jax.ShapeDtypeStruct((s, D_MODEL), BF16), compiler_params=_cp(("parallel",)), name=name)(x, g)


def _rms_bwd(dh, x, g, dx_next, name):
    s = x.shape[0]

    def body(dh_ref, x_ref, g_ref, dxn_ref, dx_ref, dg_ref):
        i = pl.program_id(0)
        xv = x_ref[...]
        r = lax.rsqrt(jnp.mean(xv * xv, axis=-1, keepdims=True) + EPS)
        xn = xv * r
        dhv = dh_ref[...]
        dxn = dhv * g_ref[...]
        dx_ref[...] = dxn_ref[...] + r * (dxn - xn * jnp.mean(dxn * xn, axis=-1, keepdims=True))

        @pl.when(i == 0)
        def _():
            dg_ref[...] = jnp.zeros_like(dg_ref)

        dg_ref[...] += _fold8(dhv * xn)

    tile = pl.BlockSpec((TM, D_MODEL), lambda i: (i, 0))
    return pl.pallas_call(
        body, grid=(s // TM,),
        in_specs=[tile, tile, pl.BlockSpec((1, D_MODEL), lambda i: (0, 0)), tile],
        out_specs=[tile, pl.BlockSpec((8, D_MODEL), lambda i: (0, 0))],
        out_shape=[jax.ShapeDtypeStruct((s, D_MODEL), F32), jax.ShapeDtypeStruct((8, D_MODEL), F32)],
        compiler_params=_cp(("arbitrary",)), name=name)(dh, x, g, dx_next)


def _loss_head(x, g, target):
    s = x.shape[0]

    def body(x_ref, g_ref, t_ref, loss_ref, dx_ref, dg_ref):
        i = pl.program_id(0)
        xv = x_ref[...]
        r = lax.rsqrt(jnp.mean(xv * xv, axis=-1, keepdims=True) + EPS)
        xn = xv * r
        err = xn * g_ref[...] - t_ref[...]
        dy = err * (1.0 / D_MODEL)
        dxn = dy * g_ref[...]
        dx_ref[...] = r * (dxn - xn * jnp.mean(dxn * xn, axis=-1, keepdims=True))

        @pl.when(i == 0)
        def _():
            dg_ref[...] = jnp.zeros_like(dg_ref)
            loss_ref[...] = jnp.zeros_like(loss_ref)

        dg_ref[...] += _fold8(dy * xn)
        loss_ref[...] += _fold8(err * err)

    tile = pl.BlockSpec((TM, D_MODEL), lambda i: (i, 0))
    acc = pl.BlockSpec((8, D_MODEL), lambda i: (0, 0))
    return pl.pallas_call(
        body, grid=(s // TM,),
        in_specs=[tile, pl.BlockSpec((1, D_MODEL), lambda i: (0, 0)), tile],
        out_specs=[acc, tile, acc],
        out_shape=[jax.ShapeDtypeStruct((8, D_MODEL), F32), jax.ShapeDtypeStruct((s, D_MODEL), F32),
                   jax.ShapeDtypeStruct((8, D_MODEL), F32)],
        compiler_params=_cp(("arbitrary",)), name="loss_head")(x, g, target)


def _mm_nn(a, b, tn, name):
    m, k = a.shape
    n = b.shape[1]
    tm = 512

    def body(a_ref, b_ref, o_ref):
        o_ref[...] = _dot(a_ref[...], b_ref[...])

    return pl.pallas_call(
        body, grid=(n // tn, m // tm),
        in_specs=[pl.BlockSpec((tm, k), lambda j, i: (i, 0)), pl.BlockSpec((k, tn), lambda j, i: (0, j))],
        out_specs=pl.BlockSpec((tm, tn), lambda j, i: (i, j)),
        out_shape=jax.ShapeDtypeStruct((m, n), F32), compiler_params=_cp(("parallel", "parallel")), name=name)(a, b)


def _mm_nt_acc(d, w, acc, tk, name):
    m, n = d.shape
    k = w.shape[0]
    tm = 512

    def body(d_ref, w_ref, acc_ref, o_ref):
        j = pl.program_id(1)

        @pl.when(j == 0)
        def _():
            o_ref[...] = acc_ref[...]

        o_ref[...] += _dot_nt(d_ref[...], w_ref[...])

    return pl.pallas_call(
        body, grid=(m // tm, n // tk),
        in_specs=[pl.BlockSpec((tm, tk), lambda i, j: (i, j)), pl.BlockSpec((k, tk), lambda i, j: (0, j)),
                  pl.BlockSpec((tm, k), lambda i, j: (i, 0))],
        out_specs=pl.BlockSpec((tm, k), lambda i, j: (i, 0)),
        out_shape=jax.ShapeDtypeStruct((m, k), F32), compiler_params=_cp(("parallel", "arbitrary")), name=name)(d, w, acc)


def _mm_tn(a, d, tn, name):
    m, k = a.shape
    n = d.shape[1]
    tm = 512

    def body(a_ref, d_ref, o_ref):
        i = pl.program_id(1)

        @pl.when(i == 0)
        def _():
            o_ref[...] = jnp.zeros_like(o_ref)

        o_ref[...] += _dot_tn(a_ref[...], d_ref[...])

    return pl.pallas_call(
        body, grid=(n // tn, m // tm),
        in_specs=[pl.BlockSpec((tm, k), lambda j, i: (i, 0)), pl.BlockSpec((tm, tn), lambda j, i: (i, j))],
        out_specs=pl.BlockSpec((k, tn), lambda j, i: (0, j)),
        out_shape=jax.ShapeDtypeStruct((k, n), F32), compiler_params=_cp(("parallel", "arbitrary")), name=name)(a, d)


def _lane_head():
    return lax.broadcasted_iota(jnp.int32, (1, BR), 1) // HEAD_DIM


def _gmlp_chunk_fwd(p_ref, r0, gw_ref, gb_ref, lg, lb):
    uv = p_ref[r0:r0 + CHUNK, O_UV:O_UV + 2 * BR]
    u = _gelu(uv[:, :BR])
    vhat, rs = _ln_hat(_gelu(uv[:, BR:]))
    vn = (vhat * lg + lb).astype(BF16)
    head = _lane_head()
    mixed = gb_ref[...]
    for h in range(N_HEADS):
        mixed = mixed + jnp.where(head == h, _dot(gw_ref[h], vn), 0.0)
    return uv, u, vhat, rs, vn, mixed


def _conv_sub_blocks(rows):
    out = [(r, SUB) for r in range(0, rows - rows % SUB, SUB)]
    if rows % SUB:
        out.append((rows - rows % SUB, rows % SUB))
    return out


def _mix_fwd(pm, wl, name):
    s = pm.shape[0]
    nt = s // TM

    def body(p_ref, ph_ref, gw_ref, gb_ref, lg_ref, lb_ref, scw_ref, dww_ref, dwb_ref, clg_ref, clb_ref,
             ya_ref, yc_ref, yd_ref, zbuf, hbuf):
        i = pl.program_id(0)
        lg = lg_ref[...]
        lb = lb_ref[...]
        for c in range(TM // CHUNK):
            r0 = c * CHUNK
            _, u, _, _, _, mixed = _gmlp_chunk_fwd(p_ref, r0, gw_ref, gb_ref, lg, lb)
            ag = p_ref[r0:r0 + CHUNK, O_AG:O_AG + BR]
            ya_ref[r0:r0 + CHUNK, :] = (u * mixed * _silu(ag)).astype(BF16)

        first = i > 0
        zbuf[0:HALO, :] = jnp.where(first, ph_ref[:, O_CIN + BR:O_CIN + 2 * BR] * ph_ref[:, O_CIN + 2 * BR:O_CIN + 3 * BR], 0.0)
        zbuf[HALO:HALO + TM, :] = p_ref[:, O_CIN + BR:O_CIN + 2 * BR] * p_ref[:, O_CIN + 2 * BR:O_CIN + 3 * BR]
        hbuf[0:HALO, :] = jnp.where(first, ph_ref[:, O_GLU:O_GLU + BR] * _sig(ph_ref[:, O_GLU + BR:O_GLU + 2 * BR]), 0.0)
        hbuf[HALO:HALO + TM, :] = p_ref[:, O_GLU:O_GLU + BR] * _sig(p_ref[:, O_GLU + BR:O_GLU + 2 * BR])

        clg = clg_ref[...]
        clb = clb_ref[...]
        for r0, nr in _conv_sub_blocks(TM):
            yc = jnp.zeros((nr, BR), F32)
            for k in range(SHORT_CONV):
                yc = yc + scw_ref[k:k + 1, :] * zbuf[HALO - (SHORT_CONV - 1) + k + r0:HALO - (SHORT_CONV - 1) + k + r0 + nr, :]
            bgate = p_ref[r0:r0 + nr, O_CIN:O_CIN + BR]
            cg = p_ref[r0:r0 + nr, O_CG:O_CG + BR]
            yc_ref[r0:r0 + nr, :] = (bgate * yc * _silu(cg)).astype(BF16)

            cc = jnp.zeros((nr, BR), F32) + dwb_ref[...]
            for k in range(CONF_CONV):
                cc = cc + dww_ref[k:k + 1, :] * hbuf[HALO - (CONF_CONV - 1) + k + r0:HALO - (CONF_CONV - 1) + k + r0 + nr, :]
            chat, _ = _ln_hat(cc)
            dg = p_ref[r0:r0 + nr, O_DG:O_DG + BR]
            yd_ref[r0:r0 + nr, :] = (_silu(chat * clg + clb) * _silu(dg)).astype(BF16)

    full = lambda shape: pl.BlockSpec(shape, lambda i: tuple(0 for _ in shape))
    ytile = pl.BlockSpec((TM, BR), lambda i: (i, 0))
    yshape = jax.ShapeDtypeStruct((s, BR), BF16)
    return pl.pallas_call(
        body, grid=(nt,),
        in_specs=[pl.BlockSpec((TM, N_MIX), lambda i: (i, 0)),
                  pl.BlockSpec((HALO, N_MIX), lambda i: (jnp.maximum(i * (TM // HALO) - 1, 0), 0)),
                  full((N_HEADS, CHUNK, CHUNK)), full((CHUNK, BR)), full((1, BR)), full((1, BR)),
                  full((8, BR)), full((32, BR)), full((1, BR)), full((1, BR)), full((1, BR))],
        out_specs=[ytile, ytile, ytile],
        out_shape=[yshape, yshape, yshape],
        scratch_shapes=[pltpu.VMEM((HALO + TM, BR), F32), pltpu.VMEM((HALO + TM, BR), F32)],
        compiler_params=_cp(("parallel",)), name=name)(
            pm, pm, wl["gw"], wl["gb"], wl["sgu_ln_g"], wl["sgu_ln_b"], wl["scw"], wl["dww"], wl["conf_dw_b"],
            wl["conf_ln_g"], wl["conf_ln_b"])


def _mix_bwd(pm, dya, dyc, dyd, dqkv, dbg, wl, name):
    s = pm.shape[0]
    nt = s // TM
    ext = TM + HALO

    def body(p_ref, ph_ref, pn_ref, dya_ref, dyc_ref, dycn_ref, dyd_ref, dydn_ref, dqkv_ref, dbg_ref,
             gw_ref, gwt_ref, gb_ref, lg_ref, lb_ref, scw_ref, dww_ref, dwb_ref, clg_ref, clb_ref,
             dp_ref, dgw_ref, dgb_ref, vec_ref, dscw_ref, ddww_ref, zbuf, dcb, hbuf, dcc):
        i = pl.program_id(0)

        @pl.when(i == 0)
        def _():
            dgw_ref[...] = jnp.zeros_like(dgw_ref)
            dgb_ref[...] = jnp.zeros_like(dgb_ref)
            vec_ref[...] = jnp.zeros_like(vec_ref)
            dscw_ref[...] = jnp.zeros_like(dscw_ref)
            ddww_ref[...] = jnp.zeros_like(ddww_ref)

        lg = lg_ref[...]
        lb = lb_ref[...]
        head = _lane_head()
        d_lg = jnp.zeros((1, BR), F32)
        d_lb = jnp.zeros((1, BR), F32)
        for c in range(TM // CHUNK):
            r0 = c * CHUNK
            uv, u, vhat, rs, vn, mixed = _gmlp_chunk_fwd(p_ref, r0, gw_ref, gb_ref, lg, lb)
            ag = p_ref[r0:r0 + CHUNK, O_AG:O_AG + BR]
            dy = dya_ref[r0:r0 + CHUNK, :]
            sa = _silu(ag)
            du = dy * mixed * sa
            dmx = dy * u * sa
            dp_ref[r0:r0 + CHUNK, O_AG:O_AG + BR] = (dy * u * mixed * _dsilu(ag)).astype(BF16)
            dgb_ref[...] += dmx
            dmx_b = dmx.astype(BF16)
            dvn = jnp.zeros((CHUNK, BR), F32)
            for h in range(N_HEADS):
                sel = head == h
                dgw_ref[h] += _dot_nt(jnp.where(sel, dmx, 0.0).astype(BF16), vn)
                dvn = dvn + jnp.where(sel, _dot(gwt_ref[h], dmx_b), 0.0)
            d_lg = d_lg + jnp.sum(dvn * vhat, axis=0, keepdims=True)
            d_lb = d_lb + jnp.sum(dvn, axis=0, keepdims=True)
            dv0 = _ln_bwd(dvn * lg, vhat, rs)
            dp_ref[r0:r0 + CHUNK, O_UV:O_UV + BR] = (du * _dgelu(uv[:, :BR])).astype(BF16)
            dp_ref[r0:r0 + CHUNK, O_UV + BR:O_UV + 2 * BR] = (dv0 * _dgelu(uv[:, BR:])).astype(BF16)
        vec_ref[0:1, :] += d_lg
        vec_ref[1:2, :] += d_lb

        dp_ref[:, O_QKV:O_QKV + 3 * BR] = dqkv_ref[...]
        dp_ref[:, O_BG:O_BG + BR] = dbg_ref[...].astype(BF16)

        first = i > 0
        last = i < nt - 1
        zbuf[0:HALO, :] = jnp.where(first, ph_ref[:, O_CIN + BR:O_CIN + 2 * BR] * ph_ref[:, O_CIN + 2 * BR:O_CIN + 3 * BR], 0.0)
        zbuf[HALO:HALO + TM, :] = p_ref[:, O_CIN + BR:O_CIN + 2 * BR] * p_ref[:, O_CIN + 2 * BR:O_CIN + 3 * BR]
        dcb[0:TM, :] = dyc_ref[...] * p_ref[:, O_CIN:O_CIN + BR] * _silu(p_ref[:, O_CG:O_CG + BR])
        dcb[TM:ext, :] = jnp.where(last, dycn_ref[...] * pn_ref[:, O_CIN:O_CIN + BR] * _silu(pn_ref[:, O_CG:O_CG + BR]), 0.0)
        for r0, nr in _conv_sub_blocks(TM):
            yc = jnp.zeros((nr, BR), F32)
            dz = jnp.zeros((nr, BR), F32)
            dcur = dcb[r0:r0 + nr, :]
            for k in range(SHORT_CONV):
                zk = zbuf[HALO - (SHORT_CONV - 1) + k + r0:HALO - (SHORT_CONV - 1) + k + r0 + nr, :]
                yc = yc + scw_ref[k:k + 1, :] * zk
                dscw_ref[8 * k:8 * k + 8, :] += _fold8(dcur * zk)
                dz = dz + scw_ref[k:k + 1, :] * dcb[(SHORT_CONV - 1) - k + r0:(SHORT_CONV - 1) - k + r0 + nr, :]
            dy = dyc_ref[r0:r0 + nr, :]
            bgate = p_ref[r0:r0 + nr, O_CIN:O_CIN + BR]
            cg = p_ref[r0:r0 + nr, O_CG:O_CG + BR]
            dp_ref[r0:r0 + nr, O_CIN:O_CIN + BR] = (dy * yc * _silu(cg)).astype(BF16)
            dp_ref[r0:r0 + nr, O_CIN + BR:O_CIN + 2 * BR] = (dz * p_ref[r0:r0 + nr, O_CIN + 2 * BR:O_CIN + 3 * BR]).astype(BF16)
            dp_ref[r0:r0 + nr, O_CIN + 2 * BR:O_CIN + 3 * BR] = (dz * p_ref[r0:r0 + nr, O_CIN + BR:O_CIN + 2 * BR]).astype(BF16)
            dp_ref[r0:r0 + nr, O_CG:O_CG + BR] = (dy * bgate * yc * _dsilu(cg)).astype(BF16)

        hbuf[0:HALO, :] = jnp.where(first, ph_ref[:, O_GLU:O_GLU + BR] * _sig(ph_ref[:, O_GLU + BR:O_GLU + 2 * BR]), 0.0)
        hbuf[HALO:HALO + TM, :] = p_ref[:, O_GLU:O_GLU + BR] * _sig(p_ref[:, O_GLU + BR:O_GLU + 2 * BR])
        hbuf[HALO + TM:HALO + ext, :] = jnp.where(last, pn_ref[:, O_GLU:O_GLU + BR] * _sig(pn_ref[:, O_GLU + BR:O_GLU + 2 * BR]), 0.0)
        clg = clg_ref[...]
        clb = clb_ref[...]
        d_clg = jnp.zeros((1, BR), F32)
        d_clb = jnp.zeros((1, BR), F32)
        d_dwb = jnp.zeros((1, BR), F32)
        for r0, nr in _conv_sub_blocks(ext):
            in_tile = r0 < TM
            cc = jnp.zeros((nr, BR), F32) + dwb_ref[...]
            for k in range(CONF_CONV):
                cc = cc + dww_ref[k:k + 1, :] * hbuf[HALO - (CONF_CONV - 1) + k + r0:HALO - (CONF_CONV - 1) + k + r0 + nr, :]
            chat, rs = _ln_hat(cc)
            ln = chat * clg + clb
            if in_tile:
                dy = dyd_ref[r0:r0 + nr, :]
                dg = p_ref[r0:r0 + nr, O_DG:O_DG + BR]
            else:
                dy = jnp.where(last, dydn_ref[...], 0.0)
                dg = pn_ref[:, O_DG:O_DG + BR]
            dln = dy * _silu(dg) * _dsilu(ln)
            dc = _ln_bwd(dln * clg, chat, rs)
            dcc[r0:r0 + nr, :] = dc
            if in_tile:
                dp_ref[r0:r0 + nr, O_DG:O_DG + BR] = (dy * _silu(ln) * _dsilu(dg)).astype(BF16)
                d_clg = d_clg + jnp.sum(dln * chat, axis=0, keepdims=True)
                d_clb = d_clb + jnp.sum(dln, axis=0, keepdims=True)
                d_dwb = d_dwb + jnp.sum(dc, axis=0, keepdims=True)
        vec_ref[2:3, :] += d_dwb
        vec_ref[3:4, :] += d_clg
        vec_ref[4:5, :] += d_clb
        for r0, nr in _conv_sub_blocks(TM):
            dcur = dcc[r0:r0 + nr, :]
            dhh = jnp.zeros((nr, BR), F32)
            for k in range(CONF_CONV):
                hk = hbuf[HALO - (CONF_CONV - 1) + k + r0:HALO - (CONF_CONV - 1) + k + r0 + nr, :]
                ddww_ref[8 * k:8 * k + 8, :] += _fold8(dcur * hk)
                dhh = dhh + dww_ref[k:k + 1, :] * dcc[(CONF_CONV - 1) - k + r0:(CONF_CONV - 1) - k + r0 + nr, :]
            a = p_ref[r0:r0 + nr, O_GLU:O_GLU + BR]
            sg = _sig(p_ref[r0:r0 + nr, O_GLU + BR:O_GLU + 2 * BR])
            dp_ref[r0:r0 + nr, O_GLU:O_GLU + BR] = (dhh * sg).astype(BF16)
            dp_ref[r0:r0 + nr, O_GLU + BR:O_GLU + 2 * BR] = (dhh * a * sg * (1.0 - sg)).astype(BF16)

    full = lambda shape: pl.BlockSpec(shape, lambda i: tuple(0 for _ in shape))
    rpt = TM // HALO
    prev_map = lambda i: (jnp.maximum(i * rpt - 1, 0), 0)
    next_map = lambda i: (jnp.minimum((i + 1) * rpt, nt * rpt - 1), 0)
    ytile = pl.BlockSpec((TM, BR), lambda i: (i, 0))
    return pl.pallas_call(
        body, grid=(nt,),
        in_specs=[pl.BlockSpec((TM, N_MIX), lambda i: (i, 0)), pl.BlockSpec((HALO, N_MIX), prev_map),
                  pl.BlockSpec((HALO, N_MIX), next_map),
                  ytile, ytile, pl.BlockSpec((HALO, BR), next_map), ytile, pl.BlockSpec((HALO, BR), next_map),
                  pl.BlockSpec((TM, 3 * BR), lambda i: (i, 0)), ytile,
                  full((N_HEADS, CHUNK, CHUNK)), full((N_HEADS, CHUNK, CHUNK)), full((CHUNK, BR)), full((1, BR)), full((1, BR)),
                  full((8, BR)), full((32, BR)), full((1, BR)), full((1, BR)), full((1, BR))],
        out_specs=[pl.BlockSpec((TM, N_MIX), lambda i: (i, 0)), full((N_HEADS, CHUNK, CHUNK)), full((CHUNK, BR)),
                   full((16, BR)), full((64, BR)), full((256, BR))],
        out_shape=[jax.ShapeDtypeStruct((s, N_MIX), BF16), jax.ShapeDtypeStruct((N_HEADS, CHUNK, CHUNK), F32),
                   jax.ShapeDtypeStruct((CHUNK, BR), F32), jax.ShapeDtypeStruct((16, BR), F32),
                   jax.ShapeDtypeStruct((64, BR), F32), jax.ShapeDtypeStruct((256, BR), F32)],
        scratch_shapes=[pltpu.VMEM((HALO + TM, BR), F32), pltpu.VMEM((ext, BR), F32),
                        pltpu.VMEM((HALO + ext, BR), F32), pltpu.VMEM((ext, BR), F32)],
        compiler_params=_cp(("arbitrary",)), name=name)(
            pm, pm, pm, dya, dyc, dyc, dyd, dyd, dqkv, dbg,
            wl["gw"], wl["gwt"], wl["gb"], wl["sgu_ln_g"], wl["sgu_ln_b"], wl["scw"], wl["dww"], wl["conf_dw_b"],
            wl["conf_ln_g"], wl["conf_ln_b"])


def _tri(lower):
    r = lax.broadcasted_iota(jnp.int32, (CHUNK, CHUNK), 0)
    c = lax.broadcasted_iota(jnp.int32, (CHUNK, CHUNK), 1)
    return jnp.where((r >= c) if lower else (r <= c), 1.0, 0.0).astype(F32)


def _dot_hi(a, b):
    return jnp.dot(a, b, preferred_element_type=F32, precision=lax.Precision.HIGHEST)


def _cum_fwd(pf, fb, name):
    s = pf.shape[0]

    def body(pf_ref, fb_ref, cum_ref, carry):
        i = pl.program_id(0)

        @pl.when(i == 0)
        def _():
            carry[...] = jnp.zeros_like(carry)

        z = pf_ref[...] + fb_ref[...]
        logf = jnp.minimum(z, 0.0) - jnp.log(1.0 + jnp.exp(-jnp.abs(z)))
        cum_ref[...] = _dot_hi(_tri(True), logf) + carry[...]
        carry[...] += jnp.sum(logf, axis=0, keepdims=True)

    return pl.pallas_call(
        body, grid=(s // CHUNK,),
        in_specs=[pl.BlockSpec((CHUNK, N_F), lambda i: (i, 0)), pl.BlockSpec((1, N_F), lambda i: (0, 0))],
        out_specs=pl.BlockSpec((CHUNK, N_F), lambda i: (i, 0)),
        out_shape=jax.ShapeDtypeStruct((s, N_F), F32),
        scratch_shapes=[pltpu.VMEM((1, N_F), F32)],
        compiler_params=_cp(("arbitrary",)), name=name)(pf, fb)


def _cum_bwd(dcum, pf, fb, name):
    s = pf.shape[0]
    nb = s // CHUNK

    def body(dc_ref, pf_ref, fb_ref, dpf_ref, dfb_ref, carry):
        i = pl.program_id(0)

        @pl.when(i == 0)
        def _():
            carry[...] = jnp.zeros_like(carry)
            dfb_ref[...] = jnp.zeros_like(dfb_ref)

        dc = dc_ref[...]
        dlogf = _dot_hi(_tri(False), dc) + carry[...]
        carry[...] += jnp.sum(dc, axis=0, keepdims=True)
        z = pf_ref[...] + fb_ref[...]
        dz = dlogf * (1.0 - _sig(z))
        dpf_ref[...] = dz.astype(BF16)
        dfb_ref[...] += _fold8(dz)

    rev = lambda i: (nb - 1 - i, 0)
    return pl.pallas_call(
        body, grid=(nb,),
        in_specs=[pl.BlockSpec((CHUNK, N_F), rev), pl.BlockSpec((CHUNK, N_F), rev), pl.BlockSpec((1, N_F), lambda i: (0, 0))],
        out_specs=[pl.BlockSpec((CHUNK, N_F), rev), pl.BlockSpec((8, N_F), lambda i: (0, 0))],
        out_shape=[jax.ShapeDtypeStruct((s, N_F), BF16), jax.ShapeDtypeStruct((8, N_F), F32)],
        scratch_shapes=[pltpu.VMEM((1, N_F), F32)],
        compiler_params=_cp(("arbitrary",)), name=name)(dcum, pf, fb)


def _causal_mask():
    r = lax.broadcasted_iota(jnp.int32, (BQ, BQ), 0)
    c = lax.broadcasted_iota(jnp.int32, (BQ, BQ), 1)
    return r >= c


def _attn_fwd(q, k, v, cq, ck, name):
    s = q.shape[1]
    nb = s // BQ

    def body(q_ref, k_ref, v_ref, cq_ref, ck_ref, o_ref, lse_ref):
        for qi in range(nb):
            qs = qi * BQ
            qb = q_ref[0, qs:qs + BQ, :]
            cqb = cq_ref[0, qs:qs + BQ, :]

            def block(kj, carry, masked):
                m, l, acc = carry
                ks = pl.multiple_of(kj * BQ, BQ)
                kb = k_ref[0, pl.ds(ks, BQ), :]
                vb = v_ref[0, pl.ds(ks, BQ), :]
                sc = _dot_nt(qb, kb) * SCALE + (cqb - ck_ref[0, kj])
                if masked:
                    sc = jnp.where(_causal_mask(), sc, NEG)
                m_new = jnp.maximum(m, jnp.max(sc, axis=-1, keepdims=True))
                alpha = jnp.exp(m - m_new)
                p = jnp.exp(sc - m_new)
                l = alpha * l + jnp.sum(p, axis=-1, keepdims=True)
                acc = alpha * acc + _dot(p.astype(BF16), vb)
                return m_new, l, acc

            carry = (jnp.full((BQ, 1), NEG, F32), jnp.zeros((BQ, 1), F32), jnp.zeros((BQ, HEAD_DIM), F32))
            if qi > 0:
                carry = lax.fori_loop(0, qi, lambda kj, cr: block(kj, cr, False), carry)
            m, l, acc = block(qi, carry, True)
            o_ref[0, qs:qs + BQ, :] = acc / l
            lse_ref[0, qs:qs + BQ, :] = m + jnp.log(l)

    hblk = pl.BlockSpec((1, s, HEAD_DIM), lambda h: (h, 0, 0))
    cblk = pl.BlockSpec((1, s, 1), lambda h: (h, 0, 0))
    return pl.pallas_call(
        body, grid=(N_HEADS,),
        in_specs=[hblk, hblk, hblk, cblk, pl.BlockSpec((1, nb, 1, BQ), lambda h: (h, 0, 0, 0))],
        out_specs=[hblk, cblk],
        out_shape=[jax.ShapeDtypeStruct((N_HEADS, s, HEAD_DIM), F32), jax.ShapeDtypeStruct((N_HEADS, s, 1), F32)],
        compiler_params=_cp(("parallel",)), name=name)(q, k, v, cq, ck)


def _attn_bwd(q, k, v, cq, ck, o, lse, do, name):
    s = q.shape[1]
    nb = s // BQ

    def body(q_ref, k_ref, v_ref, cq_ref, ck_ref, o_ref, lse_ref, do_ref, dq_ref, dk_ref, dv_ref, dcq_ref, dck_ref, delta):
        delta[...] = jnp.sum(do_ref[0] * o_ref[0], axis=-1, keepdims=True)
        dq_ref[...] = jnp.zeros_like(dq_ref)
        dcq_ref[...] = jnp.zeros_like(dcq_ref)
        for kj in range(nb):
            ks = kj * BQ
            kb = k_ref[0, ks:ks + BQ, :]
            vb = v_ref[0, ks:ks + BQ, :]
            ckb = ck_ref[0, kj]

            def block(qi, carry, masked):
                dk_acc, dv_acc, dck_acc = carry
                qs = pl.multiple_of(qi * BQ, BQ)
                qb = q_ref[0, pl.ds(qs, BQ), :]
                dob = do_ref[0, pl.ds(qs, BQ), :].astype(BF16)
                sc = _dot_nt(qb, kb) * SCALE + (cq_ref[0, pl.ds(qs, BQ), :] - ckb)
                p = jnp.exp(sc - lse_ref[0, pl.ds(qs, BQ), :])
                if masked:
                    p = jnp.where(_causal_mask(), p, 0.0)
                dp = _dot_nt(dob, vb)
                ds = p * (dp - delta[pl.ds(qs, BQ), :])
                ds_b = ds.astype(BF16)
                dv_acc = dv_acc + _dot_tn(p.astype(BF16), dob)
                dk_acc = dk_acc + _dot_tn(ds_b, qb) * SCALE
                dq_ref[0, pl.ds(qs, BQ), :] += _dot(ds_b, kb) * SCALE
                dcq_ref[0, pl.ds(qs, BQ), :] += jnp.sum(ds, axis=-1, keepdims=True)
                dck_acc = dck_acc - jnp.sum(ds, axis=0, keepdims=True)
                return dk_acc, dv_acc, dck_acc

            carry = (jnp.zeros((BQ, HEAD_DIM), F32), jnp.zeros((BQ, HEAD_DIM), F32), jnp.zeros((1, BQ), F32))
            carry = block(kj, carry, True)
            if kj < nb - 1:
                carry = lax.fori_loop(kj + 1, nb, lambda qi, cr: block(qi, cr, False), carry)
            dk_ref[0, ks:ks + BQ, :] = carry[0]
            dv_ref[0, ks:ks + BQ, :] = carry[1]
            dck_ref[0, kj] = carry[2]

    hblk = pl.BlockSpec((1, s, HEAD_DIM), lambda h: (h, 0, 0))
    cblk = pl.BlockSpec((1, s, 1), lambda h: (h, 0, 0))
    kblk = pl.BlockSpec((1, nb, 1, BQ), lambda h: (h, 0, 0, 0))
    hshape = jax.ShapeDtypeStruct((N_HEADS, s, HEAD_DIM), F32)
    return pl.pallas_call(
        body, grid=(N_HEADS,),
        in_specs=[hblk, hblk, hblk, cblk, kblk, hblk, cblk, hblk],
        out_specs=[hblk, hblk, hblk, cblk, kblk],
        out_shape=[hshape, hshape, hshape, jax.ShapeDtypeStruct((N_HEADS, s, 1), F32),
                   jax.ShapeDtypeStruct((N_HEADS, nb, 1, BQ), F32)],
        scratch_shapes=[pltpu.VMEM((s, 1), F32)],
        compiler_params=_cp(("parallel",)), name=name)(q, k, v, cq, ck, o, lse, do)


def _merge_fwd(x, ya, yc, yd, o, pm, pg, wb, wo, name):
    s = x.shape[0]

    def body(x_ref, ya_ref, yc_ref, yd_ref, o_ref, bg_ref, pg_ref, wb_ref, wo_ref, xo_ref, yb_ref):
        yb = (o_ref[...] * _silu(bg_ref[...])).astype(BF16)
        yb_ref[...] = yb
        ys = (ya_ref[...], yb, yc_ref[...], yd_ref[...])
        merged = jnp.zeros((TMG, D_MODEL), F32)
        for n in range(N_BRANCH):
            merged = merged + _sig(pg_ref[:, n * D_MODEL:(n + 1) * D_MODEL]) * _dot(ys[n], wb_ref[n])
        xo_ref[...] = x_ref[...] + _dot(merged.astype(BF16), wo_ref[...])

    xt = pl.BlockSpec((TMG, D_MODEL), lambda i: (i, 0))
    yt = pl.BlockSpec((TMG, BR), lambda i: (i, 0))
    return pl.pallas_call(
        body, grid=(s // TMG,),
        in_specs=[xt, yt, yt, yt, yt, pl.BlockSpec((TMG, BR), lambda i: (i, O_BG // BR)),
                  pl.BlockSpec((TMG, N_MERGE), lambda i: (i, 0)),
                  pl.BlockSpec((N_BRANCH, BR, D_MODEL), lambda i: (0, 0, 0)), pl.BlockSpec((D_MODEL, D_MODEL), lambda i: (0, 0))],
        out_specs=[xt, yt],
        out_shape=[jax.ShapeDtypeStruct((s, D_MODEL), F32), jax.ShapeDtypeStruct((s, BR), BF16)],
        compiler_params=_cp(("parallel",)), name=name)(x, ya, yc, yd, o, pm, pg, wb, wo)


def _merge_bwd(dx, ya, yb, yc, yd, o, pm, pg, wb, wo, name):
    s = dx.shape[0]

    def body(dx_ref, ya_ref, yb_ref, yc_ref, yd_ref, o_ref, bg_ref, pg_ref, wb_ref, wo_ref,
             dpg_ref, dya_ref, do_ref, dbg_ref, dyc_ref, dyd_ref, dwb_ref, dwo_ref):
        i = pl.program_id(0)

        @pl.when(i == 0)
        def _():
            dwb_ref[...] = jnp.zeros_like(dwb_ref)
            dwo_ref[...] = jnp.zeros_like(dwo_ref)

        dxb = dx_ref[...].astype(BF16)
        dmerged = _dot_nt(dxb, wo_ref[...])
        ys = (ya_ref[...], yb_ref[...], yc_ref[...], yd_ref[...])
        dys = (dya_ref, None, dyc_ref, dyd_ref)
        merged = jnp.zeros((TMG, D_MODEL), F32)
        for n in range(N_BRANCH):
            gate = _sig(pg_ref[:, n * D_MODEL:(n + 1) * D_MODEL])
            pr = _dot(ys[n], wb_ref[n])
            merged = merged + gate * pr
            dpg_ref[:, n * D_MODEL:(n + 1) * D_MODEL] = (dmerged * pr * gate * (1.0 - gate)).astype(BF16)
            dpr = (gate * dmerged).astype(BF16)
            dwb_ref[n] += _dot_tn(ys[n], dpr)
            dyn = _dot_nt(dpr, wb_ref[n])
            if n == 1:
                bg = bg_ref[...]
                do_ref[...] = dyn * _silu(bg)
                dbg_ref[...] = dyn * o_ref[...] * _dsilu(bg)
            else:
                dys[n][...] = dyn
        dwo_ref[...] += _dot_tn(merged.astype(BF16), dxb)

    xt = pl.BlockSpec((TMG, D_MODEL), lambda i: (i, 0))
    yt = pl.BlockSpec((TMG, BR), lambda i: (i, 0))
    gt = pl.BlockSpec((TMG, N_MERGE), lambda i: (i, 0))
    wbs = pl.BlockSpec((N_BRANCH, BR, D_MODEL), lambda i: (0, 0, 0))
    wos = pl.BlockSpec((D_MODEL, D_MODEL), lambda i: (0, 0))
    yf = jax.ShapeDtypeStruct((s, BR), F32)
    return pl.pallas_call(
        body, grid=(s // TMG,),
        in_specs=[xt, yt, yt, yt, yt, yt, pl.BlockSpec((TMG, BR), lambda i: (i, O_BG // BR)), gt, wbs, wos],
        out_specs=[gt, yt, yt, yt, yt, yt, wbs, wos],
        out_shape=[jax.ShapeDtypeStruct((s, N_MERGE), BF16), yf, yf, yf, yf, yf,
                   jax.ShapeDtypeStruct((N_BRANCH, BR, D_MODEL), F32), jax.ShapeDtypeStruct((D_MODEL, D_MODEL), F32)],
        compiler_params=_cp(("arbitrary",)), name=name)(dx, ya, yb, yc, yd, o, pm, pg, wb, wo)


def _heads(a):
    s = a.shape[0]
    return a.reshape(s, N_HEADS, HEAD_DIM).transpose(1, 0, 2)


def _unheads(a):
    s = a.shape[1]
    return a.transpose(1, 0, 2).reshape(s, N_HEADS * HEAD_DIM)


def _layer_fwd(x, wl, tag):
    s = x.shape[0]
    h = _rms_fwd(x, wl["norm_g"], "rms_fwd" + tag)
    pm = _mm_nn(h, wl["w_mix"], 1792, "proj_mix" + tag)
    pg = _mm_nn(h, wl["w_merge"], 2048, "proj_merge" + tag)
    pf = _mm_nn(h, wl["w_f"], N_F, "proj_f" + tag)
    ya, yc, yd = _mix_fwd(pm, wl, "mix_fwd" + tag)
    cum = _cum_fwd(pf, wl["f_bias"], "cum_fwd" + tag)
    cum_t = cum[:, :N_HEADS].T
    cq = cum_t.reshape(N_HEADS, s, 1)
    ck = cum_t.reshape(N_HEADS, s // BQ, 1, BQ)
    qkv = pm[:, O_QKV:O_QKV + 3 * BR].astype(BF16)
    q, k, v = (_heads(qkv[:, j * BR:(j + 1) * BR]) for j in range(3))
    o_h, lse = _attn_fwd(q, k, v, cq, ck, "attn_fwd" + tag)
    o = _unheads(o_h)
    x_next, yb = _merge_fwd(x, ya, yc, yd, o, pm, pg, wl["wb"], wl["wo"], "merge_fwd" + tag)
    saved = dict(x=x, h=h, pm=pm, pg=pg, pf=pf, ya=ya, yb=yb, yc=yc, yd=yd, q=q, k=k, v=v, cq=cq, ck=ck,
                 o_h=o_h, o=o, lse=lse)
    return x_next, saved


def _layer_bwd(dx_next, sv, wl, tag):
    s = dx_next.shape[0]
    dpg, dya, do, dbg, dyc, dyd, dwb, dwo = _merge_bwd(
        dx_next, sv["ya"], sv["yb"], sv["yc"], sv["yd"], sv["o"], sv["pm"], sv["pg"], wl["wb"], wl["wo"], "merge_bwd" + tag)
    dq, dk, dv, dcq, dck = _attn_bwd(sv["q"], sv["k"], sv["v"], sv["cq"], sv["ck"], sv["o_h"], sv["lse"], _heads(do),
                                     "attn_bwd" + tag)
    dqkv = jnp.concatenate([_unheads(dq), _unheads(dk), _unheads(dv)], axis=1).astype(BF16)
    dcum_t = dcq.reshape(N_HEADS, s) + dck.reshape(N_HEADS, s)
    dcum = jnp.pad(dcum_t.T, ((0, 0), (0, N_F - N_HEADS)))
    dpf, dfb = _cum_bwd(dcum, sv["pf"], wl["f_bias"], "cum_bwd" + tag)
    dpm, dgw, dgb, vec, dscw, ddww = _mix_bwd(sv["pm"], dya, dyc, dyd, dqkv, dbg, wl, "mix_bwd" + tag)
    dh = _mm_nt_acc(dpf, wl["w_f"], jnp.zeros((s, D_MODEL), F32), N_F, "dh_f" + tag)
    dh = _mm_nt_acc(dpm, wl["w_mix"], dh, 512, "dh_mix" + tag)
    dh = _mm_nt_acc(dpg, wl["w_merge"], dh, 512, "dh_merge" + tag)
    dw_mix = _mm_tn(sv["h"], dpm, 512, "dw_mix" + tag)
    dw_merge = _mm_tn(sv["h"], dpg, 512, "dw_merge" + tag)
    dw_f = _mm_tn(sv["h"], dpf, N_F, "dw_f" + tag)
    dx, dng = _rms_bwd(dh, sv["x"], wl["norm_g"], dx_next, "rms_bwd" + tag)
    causal = jnp.tril(jnp.ones((CHUNK, CHUNK), bool))
    grads = dict(
        w_mix=dw_mix, w_merge=dw_merge, w_f=dw_f, wb=dwb, wo=dwo,
        norm_g=dng.sum(0),
        f_bias=dfb.sum(0)[:N_HEADS],
        sgu_w=jnp.where(causal[None], dgw, 0.0),
        sgu_b=dgb.reshape(CHUNK, N_HEADS, HEAD_DIM).sum(-1).T,
        sgu_ln_g=vec[0], sgu_ln_b=vec[1], conf_dw_b=vec[2], conf_ln_g=vec[3], conf_ln_b=vec[4],
        short_conv_w=dscw.reshape(8, 8, BR).sum(1)[:SHORT_CONV],
        conf_dw_w=ddww.reshape(32, 8, BR).sum(1)[:CONF_CONV],
    )
    return dx, grads


def _prep_layer_small(norm_g, f_bias, sgu_w, sgu_b, sgu_ln_g, sgu_ln_b, scw, dww, conf_dw_b, conf_ln_g, conf_ln_b):
    causal = jnp.tril(jnp.ones((CHUNK, CHUNK), bool))
    gw = jnp.where(causal[None], sgu_w, 0.0)
    row = lambda a: a.reshape(1, -1)
    return dict(
        norm_g=row(norm_g),
        f_bias=jnp.pad(row(f_bias), ((0, 0), (0, N_F - N_HEADS))),
        gw=gw.astype(BF16), gwt=gw.transpose(0, 2, 1).astype(BF16),
        gb=jnp.repeat(sgu_b.T, HEAD_DIM, axis=1),
        sgu_ln_g=row(sgu_ln_g), sgu_ln_b=row(sgu_ln_b),
        scw=jnp.pad(scw, ((0, 8 - SHORT_CONV), (0, 0))), dww=jnp.pad(dww, ((0, 32 - CONF_CONV), (0, 0))),
        conf_dw_b=row(conf_dw_b), conf_ln_g=row(conf_ln_g), conf_ln_b=row(conf_ln_b))


def _local_step(x, target, layers, final_g):
    saved = []
    for l in range(DEPTH):
        x, sv = _layer_fwd(x, layers[l], str(l))
        saved.append(sv)
    loss_p, dx, dfg = _loss_head(x, final_g.reshape(1, D_MODEL), target)
    grads = [None] * DEPTH
    for l in reversed(range(DEPTH)):
        dx, grads[l] = _layer_bwd(dx, saved[l], layers[l], str(l))
    return 0.5 / D_MODEL * jnp.sum(loss_p), dx, grads, dfg.sum(0)


def _exchange(arrays, gather, name):
    n = len(arrays)

    def body(*refs):
        ins, outs = refs[:n], refs[n:2 * n]
        send_sems, recv_sems, local_sems = refs[2 * n:]
        x, y, c = lax.axis_index("x"), lax.axis_index("y"), lax.axis_index("c")
        me = 4 * x + 2 * y + c
        copies = []
        for a in range(n):
            src = ins[a] if gather else ins[a].at[me]
            cp = pltpu.make_async_copy(src, outs[a].at[me], local_sems.at[a])
            cp.start()
            copies.append(cp)
        remote = []
        for kk in (6, 4, 2, 7, 5, 3, 1):
            kx, ky, kc = kk >> 2 & 1, kk >> 1 & 1, kk & 1
            px, py, pc = lax.rem(x + kx, 2), lax.rem(y + ky, 2), lax.rem(c + kc, 2)
            peer = 4 * px + 2 * py + pc
            for a in range(n):
                src = ins[a] if gather else ins[a].at[peer]
                cp = pltpu.make_async_remote_copy(
                    src_ref=src, dst_ref=outs[a].at[me], send_sem=send_sems.at[a, kk], recv_sem=recv_sems.at[a, kk],
                    device_id=(px, py, pc), device_id_type=pl.DeviceIdType.MESH)
                cp.start()
                remote.append((cp, a, peer, kk))
        for cp in copies:
            cp.wait()
        for cp, a, peer, kk in remote:
            cp.wait_send()
        for cp, a, peer, kk in remote:
            pltpu.make_async_remote_copy(
                src_ref=outs[a].at[peer], dst_ref=outs[a].at[peer], send_sem=send_sems.at[a, kk],
                recv_sem=recv_sems.at[a, kk], device_id=(x, y, c), device_id_type=pl.DeviceIdType.MESH).wait_recv()

    anyspec = pl.BlockSpec(memory_space=pl.ANY)
    out_shape = [jax.ShapeDtypeStruct((N_DEV,) + tuple(a.shape[(0 if gather else 1):]), a.dtype) for a in arrays]
    return pl.pallas_call(
        body, in_specs=[anyspec] * n, out_specs=[anyspec] * n, out_shape=out_shape,
        scratch_shapes=[pltpu.SemaphoreType.DMA((n, N_DEV)), pltpu.SemaphoreType.DMA((n, N_DEV)), pltpu.SemaphoreType.DMA((n,))],
        name=name)(*arrays)


def _sum8(a, name):
    _, r, c = a.shape
    tr = r
    while tr * c * 4 * N_DEV > 4 * 1024 * 1024 and tr % 16 == 0:
        tr //= 2

    def body(a_ref, o_ref):
        acc = a_ref[0]
        for d in range(1, N_DEV):
            acc = acc + a_ref[d]
        o_ref[...] = acc

    return pl.pallas_call(
        body, grid=(r // tr,),
        in_specs=[pl.BlockSpec((N_DEV, tr, c), lambda i: (0, i, 0))],
        out_specs=pl.BlockSpec((tr, c), lambda i: (i, 0)),
        out_shape=jax.ShapeDtypeStruct((r, c), F32), compiler_params=_cp(("parallel",)), name=name)(a)


def _adamw(w, g, m, v, name):
    r, c = w.shape
    tr = r
    while tr * c * 4 > 1024 * 1024 and tr % 16 == 0:
        tr //= 2
    c1 = 1.0 - ADAM_B1 ** ADAM_STEP
    c2 = 1.0 - ADAM_B2 ** ADAM_STEP

    def body(w_ref, g_ref, m_ref, v_ref, d_ref, mo_ref, vo_ref):
        gv = g_ref[...]
        mn = ADAM_B1 * m_ref[...] + (1.0 - ADAM_B1) * gv
        vn = ADAM_B2 * v_ref[...] + (1.0 - ADAM_B2) * (gv * gv)
        mo_ref[...] = mn
        vo_ref[...] = vn
        d_ref[...] = -ADAM_LR * ((mn / c1) / (jnp.sqrt(vn / c2) + ADAM_EPS) + ADAM_WD * w_ref[...])

    blk = pl.BlockSpec((tr, c), lambda i: (i, 0))
    shp = jax.ShapeDtypeStruct((r, c), F32)
    return pl.pallas_call(
        body, grid=(r // tr,), in_specs=[blk] * 4, out_specs=[blk] * 3, out_shape=[shp] * 3,
        compiler_params=_cp(("parallel",)), name=name)(w, g, m, v)


def _pack(parts):
    rows, spans, r = [], [], 0
    for p in parts:
        flat = p.reshape(-1)
        nr = -(-flat.shape[0] // 128)
        rows.append(jnp.pad(flat, (0, nr * 128 - flat.shape[0])).reshape(nr, 128))
        spans.append((r, nr, p.shape))
        r += nr
    pad = -r % 8
    if pad:
        rows.append(jnp.zeros((pad, 128), F32))
    return jnp.concatenate(rows, axis=0), spans


def _unpack(slab, spans):
    out = []
    for r, nr, shape in spans:
        size = math.prod(shape)
        out.append(slab[r:r + nr].reshape(-1)[:size].reshape(shape))
    return out


def _split_w_in(w):
    mix = jnp.concatenate([w[..., 0:1536], w[..., 1540:1796], w[..., 3332:3588], w[..., 1796:2820], w[..., 2820:3332]], axis=-1)
    return mix, w[..., 3588:7684], w[..., 1536:1540]


def _join_w_in(mix, merge, f):
    return jnp.concatenate([mix[..., 0:1536], f, mix[..., 1536:1792], mix[..., 2048:3072], mix[..., 3072:3584],
                            mix[..., 1792:2048], merge], axis=-1)


SMALL = ("norm_g", "f_bias", "sgu_w", "sgu_b", "sgu_ln_g", "sgu_ln_b", "short_conv_w", "conf_dw_w", "conf_dw_b",
         "conf_ln_g", "conf_ln_b")


def kernel(x, norm_g, w_in, f_bias, sgu_w, sgu_b, sgu_ln_g, sgu_ln_b, short_conv_w, conf_dw_w, conf_dw_b, conf_ln_g, conf_ln_b, w_branch, w_out, final_g, loss_target, m_norm_g, m_w_in, m_f_bias, m_sgu_w, m_sgu_b, m_sgu_ln_g, m_sgu_ln_b, m_short_conv_w, m_conf_dw_w, m_conf_dw_b, m_conf_ln_g, m_conf_ln_b, m_w_branch, m_w_out, m_final_g, v_norm_g, v_w_in, v_f_bias, v_sgu_w, v_sgu_b, v_sgu_ln_g, v_sgu_ln_b, v_short_conv_w, v_conf_dw_w, v_conf_dw_b, v_conf_ln_g, v_conf_ln_b, v_w_branch, v_w_out, v_final_g):
    me = 4 * lax.axis_index("x") + 2 * lax.axis_index("y") + lax.axis_index("c")
    rows = D_MODEL // N_DEV
    cshard = BR // N_DEV

    shards = []
    for l in range(DEPTH):
        mix, merge, f = _split_w_in(w_in[l])
        shards += [mix.astype(BF16), merge.astype(BF16), jnp.pad(f, ((0, 0), (0, N_F - N_HEADS))).astype(BF16),
                   w_branch[l].astype(BF16), w_out[l].astype(BF16)]
    conv_slab, conv_spans = _pack([short_conv_w, conf_dw_w])
    gathered = _exchange(shards + [conv_slab], True, "gather_weights")
    conv_full = [_unpack(gathered[-1][d], conv_spans) for d in range(N_DEV)]
    scw_full = jnp.concatenate([cf[0] for cf in conv_full], axis=-1)
    dww_full = jnp.concatenate([cf[1] for cf in conv_full], axis=-1)
    layers = []
    for l in range(DEPTH):
        gm, gg, gf, gb, go = gathered[5 * l:5 * l + 5]
        wl = _prep_layer_small(norm_g[l], f_bias[l], sgu_w[l], sgu_b[l], sgu_ln_g[l], sgu_ln_b[l], scw_full[l], dww_full[l],
                               conf_dw_b[l], conf_ln_g[l], conf_ln_b[l])
        wl.update(w_mix=gm.reshape(D_MODEL, N_MIX), w_merge=gg.reshape(D_MODEL, N_MERGE), w_f=gf.reshape(D_MODEL, N_F),
                  wb=gb.transpose(1, 2, 0, 3).reshape(N_BRANCH, BR, D_MODEL), wo=go.reshape(D_MODEL, D_MODEL))
        layers.append(wl)

    loss_local, dx, grads, dfg = _local_step(x[0], loss_target[0], layers, final_g)
    loss = lax.psum(loss_local, ("x", "y", "c"))

    big = []
    for l in range(DEPTH):
        g = grads[l]
        big += [g["w_mix"].reshape(N_DEV, rows, N_MIX), g["w_merge"].reshape(N_DEV, rows, N_MERGE),
                g["w_f"].reshape(N_DEV, rows, N_F),
                g["wb"].reshape(N_BRANCH * BR, N_DEV, rows).transpose(1, 0, 2),
                g["wo"].reshape(N_DEV, rows, D_MODEL)]
    traded = _exchange(big, False, "trade_grads")
    red = [_sum8(t, "sum_grads%d" % j) for j, t in enumerate(traded)]
    small_parts = [jnp.stack([grads[l][nm] for l in range(DEPTH)]) for nm in SMALL] + [dfg]
    small_slab, small_spans = _pack(small_parts)
    small_all = _exchange([small_slab], True, "gather_small_grads")[0]
    small_red = _unpack(_sum8(small_all, "sum_small_grads"), small_spans)
    gs = dict(zip(SMALL + ("final_g",), small_red))
    gs["short_conv_w"] = lax.dynamic_slice_in_dim(gs["short_conv_w"], me * cshard, cshard, axis=2)
    gs["conf_dw_w"] = lax.dynamic_slice_in_dim(gs["conf_dw_w"], me * cshard, cshard, axis=2)
    g_w_in = jnp.stack([_join_w_in(red[5 * l], red[5 * l + 1], red[5 * l + 2][:, :N_HEADS]) for l in range(DEPTH)])
    g_w_branch = jnp.stack([red[5 * l + 3].reshape(N_BRANCH, BR, rows) for l in range(DEPTH)])
    g_w_out = jnp.stack([red[5 * l + 4] for l in range(DEPTH)])

    d_w_in, nm_w_in, nv_w_in = (a.reshape(DEPTH, rows, IN_COLS) for a in _adamw(
        w_in.reshape(DEPTH * rows, IN_COLS), g_w_in.reshape(DEPTH * rows, IN_COLS),
        m_w_in.reshape(DEPTH * rows, IN_COLS), v_w_in.reshape(DEPTH * rows, IN_COLS), "adamw_w_in"))
    d_w_branch, nm_w_branch, nv_w_branch = (a.reshape(w_branch.shape) for a in _adamw(
        w_branch.reshape(-1, rows), g_w_branch.reshape(-1, rows), m_w_branch.reshape(-1, rows),
        v_w_branch.reshape(-1, rows), "adamw_w_branch"))
    d_w_out, nm_w_out, nv_w_out = (a.reshape(w_out.shape) for a in _adamw(
        w_out.reshape(-1, D_MODEL), g_w_out.reshape(-1, D_MODEL), m_w_out.reshape(-1, D_MODEL),
        v_w_out.reshape(-1, D_MODEL), "adamw_w_out"))
    names = SMALL + ("final_g",)
    ws = dict(zip(names, (norm_g, f_bias, sgu_w, sgu_b, sgu_ln_g, sgu_ln_b, short_conv_w, conf_dw_w, conf_dw_b, conf_ln_g,
                          conf_ln_b, final_g)))
    ms = dict(zip(names, (m_norm_g, m_f_bias, m_sgu_w, m_sgu_b, m_sgu_ln_g, m_sgu_ln_b, m_short_conv_w, m_conf_dw_w,
                          m_conf_dw_b, m_conf_ln_g, m_conf_ln_b, m_final_g)))
    vs = dict(zip(names, (v_norm_g, v_f_bias, v_sgu_w, v_sgu_b, v_sgu_ln_g, v_sgu_ln_b, v_short_conv_w, v_conf_dw_w,
                          v_conf_dw_b, v_conf_ln_g, v_conf_ln_b, v_final_g)))
    w_slab, spans = _pack([ws[nm] for nm in names])
    g_slab, _ = _pack([gs[nm] for nm in names])
    m_slab, _ = _pack([ms[nm] for nm in names])
    v_slab, _ = _pack([vs[nm] for nm in names])
    d_s, nm_s, nv_s = (dict(zip(names, _unpack(a, spans))) for a in _adamw(w_slab, g_slab, m_slab, v_slab, "adamw_small"))

    def ordered(small, w_in_v, w_branch_v, w_out_v):
        return [small["norm_g"], w_in_v, small["f_bias"], small["sgu_w"], small["sgu_b"], small["sgu_ln_g"],
                small["sgu_ln_b"], small["short_conv_w"], small["conf_dw_w"], small["conf_dw_b"], small["conf_ln_g"],
                small["conf_ln_b"], w_branch_v, w_out_v, small["final_g"]]

    return (loss, dx[None], *ordered(gs, g_w_in, g_w_branch, g_w_out), *ordered(d_s, d_w_in, d_w_branch, d_w_out),
            *ordered(nm_s, nm_w_in, nm_w_branch, nm_w_out), *ordered(nv_s, nv_w_in, nv_w_branch, nv_w_out))
```

```python
import functools
import math

import jax
import jax.numpy as jnp
from jax import lax
from jax.experimental import pallas as pl
from jax.experimental.pallas import tpu as pltpu

F32 = jnp.float32
BF16 = jnp.bfloat16

D_MODEL = 1024
DEPTH = 2
N_BRANCH = 4
BR = 256
N_HEADS = 4
HEAD_DIM = 64
CHUNK = 128
SHORT_CONV = 3
CONF_CONV = 31
EPS = 1e-6
N_DEV = 8

ADAM_LR = 0.001
ADAM_B1 = 0.9
ADAM_B2 = 0.999
ADAM_EPS = 1e-08
ADAM_WD = 0.01
ADAM_STEP = 10

O_UV, O_AG, O_QKV, O_BG, O_DG, O_CIN, O_CG, O_GLU = 0, 512, 768, 1536, 1792, 2048, 2816, 3072
N_MIX = 3584
N_MERGE = N_BRANCH * D_MODEL
N_F = 128
IN_COLS = 7684
HALO = 32
TM = 512
TMG = 256
BQ = 512
SUB = 64
VMEM_LIMIT = 56 * 1024 * 1024
NEG = -1e30
SCALE = 1.0 / math.sqrt(HEAD_DIM)
GELU_K = math.sqrt(2.0 / math.pi)


def _cp(sem=None):
    return pltpu.CompilerParams(dimension_semantics=sem, vmem_limit_bytes=VMEM_LIMIT)


PEER_ORDER = (6, 4, 2, 7, 5, 3, 1)


def _xchg(cin, cout, send, recv, loc, modes, start):
    x, y, c = lax.axis_index("x"), lax.axis_index("y"), lax.axis_index("c")
    me = 4 * x + 2 * y + c
    for a, gather in enumerate(modes):
        cp = pltpu.make_async_copy(cin[a] if gather else cin[a].at[me], cout[a].at[me], loc.at[a])
        if start:
            cp.start()
        else:
            cp.wait()
    peers = []
    for kk in PEER_ORDER:
        px, py, pc = lax.rem(x + (kk >> 2 & 1), 2), lax.rem(y + (kk >> 1 & 1), 2), lax.rem(c + (kk & 1), 2)
        peers.append((kk, (px, py, pc), 4 * px + 2 * py + pc))

    def outgoing(a, gather, kk, pid, peer):
        return pltpu.make_async_remote_copy(
            src_ref=cin[a] if gather else cin[a].at[peer], dst_ref=cout[a].at[me], send_sem=send.at[a, kk],
            recv_sem=recv.at[a, kk], device_id=pid, device_id_type=pl.DeviceIdType.MESH)

    for kk, pid, peer in peers:
        for a, gather in enumerate(modes):
            if start:
                outgoing(a, gather, kk, pid, peer).start()
            else:
                outgoing(a, gather, kk, pid, peer).wait_send()
    if not start:
        for kk, pid, peer in peers:
            for a in range(len(modes)):
                pltpu.make_async_remote_copy(
                    src_ref=cout[a].at[peer], dst_ref=cout[a].at[peer], send_sem=send.at[a, kk], recv_sem=recv.at[a, kk],
                    device_id=(x, y, c), device_id_type=pl.DeviceIdType.MESH).wait_recv()


def _xchg_shapes(comm):
    return [jax.ShapeDtypeStruct((N_DEV,) + tuple(a.shape[(0 if gather else 1):]), a.dtype) for a, gather in comm]


def _xchg_sems(n):
    return [pltpu.SemaphoreType.DMA((n, N_DEV)), pltpu.SemaphoreType.DMA((n, N_DEV)), pltpu.SemaphoreType.DMA((n,))]


def _exchange(comm, name):
    n = len(comm)
    modes = [g for _, g in comm]

    def body(*refs):
        cin, cout, (send, recv, loc) = refs[:n], refs[n:2 * n], refs[2 * n:]
        _xchg(cin, cout, send, recv, loc, modes, True)
        _xchg(cin, cout, send, recv, loc, modes, False)

    anyspec = pl.BlockSpec(memory_space=pl.ANY)
    return pl.pallas_call(
        body, in_specs=[anyspec] * n, out_specs=[anyspec] * n, out_shape=_xchg_shapes(comm),
        scratch_shapes=_xchg_sems(n), name=name)(*[a for a, _ in comm])


def _pcall(body, *, grid, in_specs, out_specs, out_shape, operands, name, scratch_shapes=(), comm=None):
    if not comm:
        outs = pl.pallas_call(
            body, grid=grid, in_specs=in_specs, out_specs=out_specs, out_shape=out_shape, scratch_shapes=list(scratch_shapes),
            compiler_params=_cp(("arbitrary",) * len(grid)), name=name)(*operands)
        return list(outs), []
    n, nin, nout, nsc = len(comm), len(operands), len(out_shape), len(scratch_shapes)
    modes = [g for _, g in comm]

    def wrapped(*refs):
        ins, cin = refs[:nin], refs[nin:nin + n]
        outs, cout = refs[nin + n:nin + n + nout], refs[nin + n + nout:nin + 2 * n + nout]
        scratch = refs[nin + 2 * n + nout:]
        own, (send, recv, loc) = scratch[:nsc], scratch[nsc:]
        ids = [pl.program_id(d) for d in range(len(grid))]
        first = functools.reduce(jnp.logical_and, [i == 0 for i in ids])
        last = functools.reduce(jnp.logical_and, [i == g - 1 for i, g in zip(ids, grid)])

        @pl.when(first)
        def _():
            _xchg(cin, cout, send, recv, loc, modes, True)

        body(*ins, *outs, *own)

        @pl.when(last)
        def _():
            _xchg(cin, cout, send, recv, loc, modes, False)

    anyspec = pl.BlockSpec(memory_space=pl.ANY)
    res = pl.pallas_call(
        wrapped, grid=grid, in_specs=list(in_specs) + [anyspec] * n, out_specs=list(out_specs) + [anyspec] * n,
        out_shape=list(out_shape) + _xchg_shapes(comm), scratch_shapes=list(scratch_shapes) + _xchg_sems(n),
        compiler_params=_cp(("arbitrary",) * len(grid)), name=name)(*operands, *[a for a, _ in comm])
    return list(res[:nout]), list(res[nout:])


def _sig(x):
    return 1.0 / (1.0 + jnp.exp(-x))


def _silu(x):
    return x * _sig(x)


def _dsilu(x):
    s = _sig(x)
    return s * (1.0 + x * (1.0 - s))


def _gelu(x):
    return 0.5 * x * (1.0 + jnp.tanh(GELU_K * (x + 0.044715 * x * x * x)))


def _dgelu(x):
    t = jnp.tanh(GELU_K * (x + 0.044715 * x * x * x))
    return 0.5 * (1.0 + t) + 0.5 * x * (1.0 - t * t) * GELU_K * (1.0 + 3.0 * 0.044715 * x * x)


def _ln_hat(x):
    mu = jnp.mean(x, axis=-1, keepdims=True)
    xc = x - mu
    rs = lax.rsqrt(jnp.mean(xc * xc, axis=-1, keepdims=True) + EPS)
    return xc * rs, rs


def _ln_bwd(dhat, hat, rs):
    return rs * (dhat - jnp.mean(dhat, axis=-1, keepdims=True) - hat * jnp.mean(dhat * hat, axis=-1, keepdims=True))


def _dot(a, b):
    return jnp.dot(a, b, preferred_element_type=F32)


def _dot_nt(a, b):
    return lax.dot_general(a, b, (((1,), (1,)), ((), ())), preferred_element_type=F32)


def _dot_tn(a, b):
    return lax.dot_general(a, b, (((0,), (0,)), ((), ())), preferred_element_type=F32)


def _fold8(x):
    acc = x[0:8]
    for r in range(1, x.shape[0] // 8):
        acc = acc + x[8 * r:8 * r + 8]
    return acc


def _rms_fwd(x, g, name):
    s = x.shape[0]

    def body(x_ref, g_ref, h_ref):
        xv = x_ref[...]
        r = lax.rsqrt(jnp.mean(xv * xv, axis=-1, keepdims=True) + EPS)
        h_ref[...] = (xv * r * g_ref[...]).astype(BF16)

    return pl.pallas_call(
        body, grid=(s // TM,),
        in_specs=[pl.BlockSpec((TM, D_MODEL), lambda i: (i, 0)), pl.BlockSpec((1, D_MODEL), lambda i: (0, 0))],
        out_specs=pl.BlockSpec((TM, D_MODEL), lambda i: (i, 0)),
        out_shape=jax.ShapeDtypeStruct((s, D_MODEL), BF16), compiler_params=_cp(("parallel",)), name=name)(x, g)


def _rms_bwd(dh, x, g, dx_next, name):
    s = x.shape[0]

    def body(dh_ref, x_ref, g_ref, dxn_ref, dx_ref, dg_ref):
        i = pl.program_id(0)
        xv = x_ref[...]
        r = lax.rsqrt(jnp.mean(xv * xv, axis=-1, keepdims=True) + EPS)
        xn = xv * r
        dhv = dh_ref[...]
        dxn = dhv * g_ref[...]
        dx_ref[...] = dxn_ref[...] + r * (dxn - xn * jnp.mean(dxn * xn, axis=-1, keepdims=True))

        @pl.when(i == 0)
        def _():
            dg_ref[...] = jnp.zeros_like(dg_ref)

        dg_ref[...] += _fold8(dhv * xn)

    tile = pl.BlockSpec((TM, D_MODEL), lambda i: (i, 0))
    return pl.pallas_call(
        body, grid=(s // TM,),
        in_specs=[tile, tile, pl.BlockSpec((1, D_MODEL), lambda i: (0, 0)), tile],
        out_specs=[tile, pl.BlockSpec((8, D_MODEL), lambda i: (0, 0))],
        out_shape=[jax.ShapeDtypeStruct((s, D_MODEL), F32), jax.ShapeDtypeStruct((8, D_MODEL), F32)],
        compiler_params=_cp(("arbitrary",)), name=name)(dh, x, g, dx_next)


def _loss_head(x, g, target):
    s = x.shape[0]

    def body(x_ref, g_ref, t_ref, loss_ref, dx_ref, dg_ref):
        i = pl.program_id(0)
        xv = x_ref[...]
        r = lax.rsqrt(jnp.mean(xv * xv, axis=-1, keepdims=True) + EPS)
        xn = xv * r
        err = xn * g_ref[...] - t_ref[...]
        dy = err * (1.0 / D_MODEL)
        dxn = dy * g_ref[...]
        dx_ref[...] = r * (dxn - xn * jnp.mean(dxn * xn, axis=-1, keepdims=True))

        @pl.when(i == 0)
        def _():
            dg_ref[...] = jnp.zeros_like(dg_ref)
            loss_ref[...] = jnp.zeros_like(loss_ref)

        dg_ref[...] += _fold8(dy * xn)
        loss_ref[...] += _fold8(err * err)

    tile = pl.BlockSpec((TM, D_MODEL), lambda i: (i, 0))
    acc = pl.BlockSpec((8, D_MODEL), lambda i: (0, 0))
    return pl.pallas_call(
        body, grid=(s // TM,),
        in_specs=[tile, pl.BlockSpec((1, D_MODEL), lambda i: (0, 0)), tile],
        out_specs=[acc, tile, acc],
        out_shape=[jax.ShapeDtypeStruct((8, D_MODEL), F32), jax.ShapeDtypeStruct((s, D_MODEL), F32),
                   jax.ShapeDtypeStruct((8, D_MODEL), F32)],
        compiler_params=_cp(("arbitrary",)), name="loss_head")(x, g, target)


def _mm_nn(a, b, tn, name, comm=None):
    m, k = a.shape
    n = b.shape[1]
    tm = 512

    def body(a_ref, b_ref, o_ref):
        o_ref[...] = _dot(a_ref[...], b_ref[...])

    (out,), couts = _pcall(
        body, grid=(n // tn, m // tm),
        in_specs=[pl.BlockSpec((tm, k), lambda j, i: (i, 0)), pl.BlockSpec((k, tn), lambda j, i: (0, j))],
        out_specs=[pl.BlockSpec((tm, tn), lambda j, i: (i, j))],
        out_shape=[jax.ShapeDtypeStruct((m, n), F32)], operands=(a, b), name=name, comm=comm)
    return out, couts


def _dh(dpm, dpg, dpf, w_mix, w_merge, w_f, name):
    s = dpm.shape[0]
    tm = 1024 if s % 1024 == 0 else 512
    tk = 512
    n1, n2 = N_MIX // tk, N_MERGE // tk

    def body(dpm_ref, dpg_ref, dpf_ref, wm_ref, wg_ref, wf_ref, o_ref):
        j = pl.program_id(1)

        @pl.when(j == 0)
        def _():
            o_ref[...] = _dot_nt(dpf_ref[...], wf_ref[...])

        @pl.when(j < n1)
        def _():
            o_ref[...] += _dot_nt(dpm_ref[...], wm_ref[...])

        @pl.when(j >= n1)
        def _():
            o_ref[...] += _dot_nt(dpg_ref[...], wg_ref[...])

    mix_j = lambda j: jnp.minimum(j, n1 - 1)
    merge_j = lambda j: jnp.maximum(j - n1, 0)
    return pl.pallas_call(
        body, grid=(s // tm, n1 + n2),
        in_specs=[pl.BlockSpec((tm, tk), lambda i, j: (i, mix_j(j))), pl.BlockSpec((tm, tk), lambda i, j: (i, merge_j(j))),
                  pl.BlockSpec((tm, N_F), lambda i, j: (i, 0)),
                  pl.BlockSpec((D_MODEL, tk), lambda i, j: (0, mix_j(j))), pl.BlockSpec((D_MODEL, tk), lambda i, j: (0, merge_j(j))),
                  pl.BlockSpec((D_MODEL, N_F), lambda i, j: (0, 0))],
        out_specs=pl.BlockSpec((tm, D_MODEL), lambda i, j: (i, 0)),
        out_shape=jax.ShapeDtypeStruct((s, D_MODEL), F32), compiler_params=_cp(("parallel", "arbitrary")), name=name)(
            dpm, dpg, dpf, w_mix, w_merge, w_f)


def _mm_tn(a, d, tn, name):
    m, k = a.shape
    n = d.shape[1]
    tm = 512
    nm = m // tm

    def body(a_ref, d_ref, o_ref, acc):
        i = pl.program_id(1)

        @pl.when(i == 0)
        def _():
            acc[...] = jnp.zeros_like(acc)

        acc[...] += _dot_tn(a_ref[...], d_ref[...])

        @pl.when(i == nm - 1)
        def _():
            o_ref[...] = acc[...].astype(BF16)

    return pl.pallas_call(
        body, grid=(n // tn, nm),
        in_specs=[pl.BlockSpec((tm, k), lambda j, i: (i, 0)), pl.BlockSpec((tm, tn), lambda j, i: (i, j))],
        out_specs=pl.BlockSpec((k, tn), lambda j, i: (0, j)),
        out_shape=jax.ShapeDtypeStruct((k, n), BF16), scratch_shapes=[pltpu.VMEM((k, tn), F32)],
        compiler_params=_cp(("parallel", "arbitrary")), name=name)(a, d)


def _lane_head():
    return lax.broadcasted_iota(jnp.int32, (1, BR), 1) // HEAD_DIM


def _gmlp_chunk_fwd(p_ref, r0, gw_ref, gb_ref, lg, lb):
    uv = p_ref[r0:r0 + CHUNK, O_UV:O_UV + 2 * BR]
    u = _gelu(uv[:, :BR])
    vhat, rs = _ln_hat(_gelu(uv[:, BR:]))
    vn = (vhat * lg + lb).astype(BF16)
    head = _lane_head()
    mixed = gb_ref[...]
    for h in range(N_HEADS):
        mixed = mixed + jnp.where(head == h, _dot(gw_ref[h], vn), 0.0)
    return uv, u, vhat, rs, vn, mixed


def _conv_sub_blocks(rows):
    out = [(r, SUB) for r in range(0, rows - rows % SUB, SUB)]
    if rows % SUB:
        out.append((rows - rows % SUB, rows % SUB))
    return out


def _mix_fwd(pm, wl, name, comm=None):
    s = pm.shape[0]
    nt = s // TM

    def body(p_ref, ph_ref, gw_ref, gb_ref, lg_ref, lb_ref, scw_ref, dww_ref, dwb_ref, clg_ref, clb_ref,
             ya_ref, yc_ref, yd_ref, zbuf, hbuf):
        i = pl.program_id(0)
        lg = lg_ref[...]
        lb = lb_ref[...]
        for c in range(TM // CHUNK):
            r0 = c * CHUNK
            _, u, _, _, _, mixed = _gmlp_chunk_fwd(p_ref, r0, gw_ref, gb_ref, lg, lb)
            ag = p_ref[r0:r0 + CHUNK, O_AG:O_AG + BR]
            ya_ref[r0:r0 + CHUNK, :] = (u * mixed * _silu(ag)).astype(BF16)

        first = i > 0
        zbuf[0:HALO, :] = jnp.where(first, ph_ref[:, O_CIN + BR:O_CIN + 2 * BR] * ph_ref[:, O_CIN + 2 * BR:O_CIN + 3 * BR], 0.0)
        zbuf[HALO:HALO + TM, :] = p_ref[:, O_CIN + BR:O_CIN + 2 * BR] * p_ref[:, O_CIN + 2 * BR:O_CIN + 3 * BR]
        hbuf[0:HALO, :] = jnp.where(first, ph_ref[:, O_GLU:O_GLU + BR] * _sig(ph_ref[:, O_GLU + BR:O_GLU + 2 * BR]), 0.0)
        hbuf[HALO:HALO + TM, :] = p_ref[:, O_GLU:O_GLU + BR] * _sig(p_ref[:, O_GLU + BR:O_GLU + 2 * BR])

        clg = clg_ref[...]
        clb = clb_ref[...]
        for r0, nr in _conv_sub_blocks(TM):
            yc = jnp.zeros((nr, BR), F32)
            for k in range(SHORT_CONV):
                yc = yc + scw_ref[k:k + 1, :] * zbuf[HALO - (SHORT_CONV - 1) + k + r0:HALO - (SHORT_CONV - 1) + k + r0 + nr, :]
            bgate = p_ref[r0:r0 + nr, O_CIN:O_CIN + BR]
            cg = p_ref[r0:r0 + nr, O_CG:O_CG + BR]
            yc_ref[r0:r0 + nr, :] = (bgate * yc * _silu(cg)).astype(BF16)

            cc = jnp.zeros((nr, BR), F32) + dwb_ref[...]
            for k in range(CONF_CONV):
                cc = cc + dww_ref[k:k + 1, :] * hbuf[HALO - (CONF_CONV - 1) + k + r0:HALO - (CONF_CONV - 1) + k + r0 + nr, :]
            chat, _ = _ln_hat(cc)
            dg = p_ref[r0:r0 + nr, O_DG:O_DG + BR]
            yd_ref[r0:r0 + nr, :] = (_silu(chat * clg + clb) * _silu(dg)).astype(BF16)

    full = lambda shape: pl.BlockSpec(shape, lambda i: tuple(0 for _ in shape))
    ytile = pl.BlockSpec((TM, BR), lambda i: (i, 0))
    yshape = jax.ShapeDtypeStruct((s, BR), BF16)
    return _pcall(
        body, grid=(nt,),
        in_specs=[pl.BlockSpec((TM, N_MIX), lambda i: (i, 0)),
                  pl.BlockSpec((HALO, N_MIX), lambda i: (jnp.maximum(i * (TM // HALO) - 1, 0), 0)),
                  full((N_HEADS, CHUNK, CHUNK)), full((CHUNK, BR)), full((1, BR)), full((1, BR)),
                  full((8, BR)), full((32, BR)), full((1, BR)), full((1, BR)), full((1, BR))],
        out_specs=[ytile, ytile, ytile],
        out_shape=[yshape, yshape, yshape],
        scratch_shapes=[pltpu.VMEM((HALO + TM, BR), F32), pltpu.VMEM((HALO + TM, BR), F32)],
        name=name, comm=comm, operands=(
            pm, pm, wl["gw"], wl["gb"], wl["sgu_ln_g"], wl["sgu_ln_b"], wl["scw"], wl["dww"], wl["conf_dw_b"],
            wl["conf_ln_g"], wl["conf_ln_b"]))


def _mix_bwd(pm, dya, dyc, dyd, dqkv, dbg, wl, name):
    s = pm.shape[0]
    nt = s // TM
    ext = TM + HALO

    def body(p_ref, ph_ref, pn_ref, dya_ref, dyc_ref, dycn_ref, dyd_ref, dydn_ref, dqkv_ref, dbg_ref,
             gw_ref, gwt_ref, gb_ref, lg_ref, lb_ref, scw_ref, dww_ref, dwb_ref, clg_ref, clb_ref,
             dp_ref, dgw_ref, dgb_ref, vec_ref, dscw_ref, ddww_ref, zbuf, dcb, hbuf, dcc):
        i = pl.program_id(0)

        @pl.when(i == 0)
        def _():
            dgw_ref[...] = jnp.zeros_like(dgw_ref)
            dgb_ref[...] = jnp.zeros_like(dgb_ref)
            vec_ref[...] = jnp.zeros_like(vec_ref)
            dscw_ref[...] = jnp.zeros_like(dscw_ref)
            ddww_ref[...] = jnp.zeros_like(ddww_ref)

        lg = lg_ref[...]
        lb = lb_ref[...]
        head = _lane_head()
        d_lg = jnp.zeros((1, BR), F32)
        d_lb = jnp.zeros((1, BR), F32)
        for c in range(TM // CHUNK):
            r0 = c * CHUNK
            uv, u, vhat, rs, vn, mixed = _gmlp_chunk_fwd(p_ref, r0, gw_ref, gb_ref, lg, lb)
            ag = p_ref[r0:r0 + CHUNK, O_AG:O_AG + BR]
            dy = dya_ref[r0:r0 + CHUNK, :]
            sa = _silu(ag)
            du = dy * mixed * sa
            dmx = dy * u * sa
            dp_ref[r0:r0 + CHUNK, O_AG:O_AG + BR] = (dy * u * mixed * _dsilu(ag)).astype(BF16)
            dgb_ref[...] += dmx
            dmx_b = dmx.astype(BF16)
            dvn = jnp.zeros((CHUNK, BR), F32)
            for h in range(N_HEADS):
                sel = head == h
                dgw_ref[h] += _dot_nt(jnp.where(sel, dmx, 0.0).astype(BF16), vn)
                dvn = dvn + jnp.where(sel, _dot(gwt_ref[h], dmx_b), 0.0)
            d_lg = d_lg + jnp.sum(dvn * vhat, axis=0, keepdims=True)
            d_lb = d_lb + jnp.sum(dvn, axis=0, keepdims=True)
            dv0 = _ln_bwd(dvn * lg, vhat, rs)
            dp_ref[r0:r0 + CHUNK, O_UV:O_UV + BR] = (du * _dgelu(uv[:, :BR])).astype(BF16)
            dp_ref[r0:r0 + CHUNK, O_UV + BR:O_UV + 2 * BR] = (dv0 * _dgelu(uv[:, BR:])).astype(BF16)
        vec_ref[0:1, :] += d_lg
        vec_ref[1:2, :] += d_lb

        dp_ref[:, O_QKV:O_QKV + 3 * BR] = dqkv_ref[...]
        dp_ref[:, O_BG:O_BG + BR] = dbg_ref[...].astype(BF16)

        first = i > 0
        last = i < nt - 1
        zbuf[0:HALO, :] = jnp.where(first, ph_ref[:, O_CIN + BR:O_CIN + 2 * BR] * ph_ref[:, O_CIN + 2 * BR:O_CIN + 3 * BR], 0.0)
        zbuf[HALO:HALO + TM, :] = p_ref[:, O_CIN + BR:O_CIN + 2 * BR] * p_ref[:, O_CIN + 2 * BR:O_CIN + 3 * BR]
        dcb[0:TM, :] = dyc_ref[...] * p_ref[:, O_CIN:O_CIN + BR] * _silu(p_ref[:, O_CG:O_CG + BR])
        dcb[TM:ext, :] = jnp.where(last, dycn_ref[...] * pn_ref[:, O_CIN:O_CIN + BR] * _silu(pn_ref[:, O_CG:O_CG + BR]), 0.0)
        for r0, nr in _conv_sub_blocks(TM):
            yc = jnp.zeros((nr, BR), F32)
            dz = jnp.zeros((nr, BR), F32)
            dcur = dcb[r0:r0 + nr, :]
            for k in range(SHORT_CONV):
                zk = zbuf[HALO - (SHORT_CONV - 1) + k + r0:HALO - (SHORT_CONV - 1) + k + r0 + nr, :]
                yc = yc + scw_ref[k:k + 1, :] * zk
                dscw_ref[8 * k:8 * k + 8, :] += _fold8(dcur * zk)
                dz = dz + scw_ref[k:k + 1, :] * dcb[(SHORT_CONV - 1) - k + r0:(SHORT_CONV - 1) - k + r0 + nr, :]
            dy = dyc_ref[r0:r0 + nr, :]
            bgate = p_ref[r0:r0 + nr, O_CIN:O_CIN + BR]
            cg = p_ref[r0:r0 + nr, O_CG:O_CG + BR]
            dp_ref[r0:r0 + nr, O_CIN:O_CIN + BR] = (dy * yc * _silu(cg)).astype(BF16)
            dp_ref[r0:r0 + nr, O_CIN + BR:O_CIN + 2 * BR] = (dz * p_ref[r0:r0 + nr, O_CIN + 2 * BR:O_CIN + 3 * BR]).astype(BF16)
            dp_ref[r0:r0 + nr, O_CIN + 2 * BR:O_CIN + 3 * BR] = (dz * p_ref[r0:r0 + nr, O_CIN + BR:O_CIN + 2 * BR]).astype(BF16)
            dp_ref[r0:r0 + nr, O_CG:O_CG + BR] = (dy * bgate * yc * _dsilu(cg)).astype(BF16)

        hbuf[0:HALO, :] = jnp.where(first, ph_ref[:, O_GLU:O_GLU + BR] * _sig(ph_ref[:, O_GLU + BR:O_GLU + 2 * BR]), 0.0)
        hbuf[HALO:HALO + TM, :] = p_ref[:, O_GLU:O_GLU + BR] * _sig(p_ref[:, O_GLU + BR:O_GLU + 2 * BR])
        hbuf[HALO + TM:HALO + ext, :] = jnp.where(last, pn_ref[:, O_GLU:O_GLU + BR] * _sig(pn_ref[:, O_GLU + BR:O_GLU + 2 * BR]), 0.0)
        clg = clg_ref[...]
        clb = clb_ref[...]
        d_clg = jnp.zeros((1, BR), F32)
        d_clb = jnp.zeros((1, BR), F32)
        d_dwb = jnp.zeros((1, BR), F32)
        for r0, nr in _conv_sub_blocks(ext):
            in_tile = r0 < TM
            cc = jnp.zeros((nr, BR), F32) + dwb_ref[...]
            for k in range(CONF_CONV):
                cc = cc + dww_ref[k:k + 1, :] * hbuf[HALO - (CONF_CONV - 1) + k + r0:HALO - (CONF_CONV - 1) + k + r0 + nr, :]
            chat, rs = _ln_hat(cc)
            ln = chat * clg + clb
            if in_tile:
                dy = dyd_ref[r0:r0 + nr, :]
                dg = p_ref[r0:r0 + nr, O_DG:O_DG + BR]
            else:
                dy = jnp.where(last, dydn_ref[...], 0.0)
                dg = pn_ref[:, O_DG:O_DG + BR]
            dln = dy * _silu(dg) * _dsilu(ln)
            dc = _ln_bwd(dln * clg, chat, rs)
            dcc[r0:r0 + nr, :] = dc
            if in_tile:
                dp_ref[r0:r0 + nr, O_DG:O_DG + BR] = (dy * _silu(ln) * _dsilu(dg)).astype(BF16)
                d_clg = d_clg + jnp.sum(dln * chat, axis=0, keepdims=True)
                d_clb = d_clb + jnp.sum(dln, axis=0, keepdims=True)
                d_dwb = d_dwb + jnp.sum(dc, axis=0, keepdims=True)
        vec_ref[2:3, :] += d_dwb
        vec_ref[3:4, :] += d_clg
        vec_ref[4:5, :] += d_clb
        for r0, nr in _conv_sub_blocks(TM):
            dcur = dcc[r0:r0 + nr, :]
            dhh = jnp.zeros((nr, BR), F32)
            for k in range(CONF_CONV):
                hk = hbuf[HALO - (CONF_CONV - 1) + k + r0:HALO - (CONF_CONV - 1) + k + r0 + nr, :]
                ddww_ref[8 * k:8 * k + 8, :] += _fold8(dcur * hk)
                dhh = dhh + dww_ref[k:k + 1, :] * dcc[(CONF_CONV - 1) - k + r0:(CONF_CONV - 1) - k + r0 + nr, :]
            a = p_ref[r0:r0 + nr, O_GLU:O_GLU + BR]
            sg = _sig(p_ref[r0:r0 + nr, O_GLU + BR:O_GLU + 2 * BR])
            dp_ref[r0:r0 + nr, O_GLU:O_GLU + BR] = (dhh * sg).astype(BF16)
            dp_ref[r0:r0 + nr, O_GLU + BR:O_GLU + 2 * BR] = (dhh * a * sg * (1.0 - sg)).astype(BF16)

    full = lambda shape: pl.BlockSpec(shape, lambda i: tuple(0 for _ in shape))
    rpt = TM // HALO
    prev_map = lambda i: (jnp.maximum(i * rpt - 1, 0), 0)
    next_map = lambda i: (jnp.minimum((i + 1) * rpt, nt * rpt - 1), 0)
    ytile = pl.BlockSpec((TM, BR), lambda i: (i, 0))
    return pl.pallas_call(
        body, grid=(nt,),
        in_specs=[pl.BlockSpec((TM, N_MIX), lambda i: (i, 0)), pl.BlockSpec((HALO, N_MIX), prev_map),
                  pl.BlockSpec((HALO, N_MIX), next_map),
                  ytile, ytile, pl.BlockSpec((HALO, BR), next_map), ytile, pl.BlockSpec((HALO, BR), next_map),
                  pl.BlockSpec((TM, 3 * BR), lambda i: (i, 0)), ytile,
                  full((N_HEADS, CHUNK, CHUNK)), full((N_HEADS, CHUNK, CHUNK)), full((CHUNK, BR)), full((1, BR)), full((1, BR)),
                  full((8, BR)), full((32, BR)), full((1, BR)), full((1, BR)), full((1, BR))],
        out_specs=[pl.BlockSpec((TM, N_MIX), lambda i: (i, 0)), full((N_HEADS, CHUNK, CHUNK)), full((CHUNK, BR)),
                   full((16, BR)), full((64, BR)), full((256, BR))],
        out_shape=[jax.ShapeDtypeStruct((s, N_MIX), BF16), jax.ShapeDtypeStruct((N_HEADS, CHUNK, CHUNK), F32),
                   jax.ShapeDtypeStruct((CHUNK, BR), F32), jax.ShapeDtypeStruct((16, BR), F32),
                   jax.ShapeDtypeStruct((64, BR), F32), jax.ShapeDtypeStruct((256, BR), F32)],
        scratch_shapes=[pltpu.VMEM((HALO + TM, BR), F32), pltpu.VMEM((ext, BR), F32),
                        pltpu.VMEM((HALO + ext, BR), F32), pltpu.VMEM((ext, BR), F32)],
        compiler_params=_cp(("arbitrary",)), name=name)(
            pm, pm, pm, dya, dyc, dyc, dyd, dyd, dqkv, dbg,
            wl["gw"], wl["gwt"], wl["gb"], wl["sgu_ln_g"], wl["sgu_ln_b"], wl["scw"], wl["dww"], wl["conf_dw_b"],
            wl["conf_ln_g"], wl["conf_ln_b"])


def _tri(lower):
    r = lax.broadcasted_iota(jnp.int32, (CHUNK, CHUNK), 0)
    c = lax.broadcasted_iota(jnp.int32, (CHUNK, CHUNK), 1)
    return jnp.where((r >= c) if lower else (r <= c), 1.0, 0.0).astype(F32)


def _dot_hi(a, b):
    return jnp.dot(a, b, preferred_element_type=F32, precision=lax.Precision.HIGHEST)


def _cum_fwd(pf, fb, name):
    s = pf.shape[0]

    def body(pf_ref, fb_ref, cum_ref, carry):
        i = pl.program_id(0)

        @pl.when(i == 0)
        def _():
            carry[...] = jnp.zeros_like(carry)

        z = pf_ref[...] + fb_ref[...]
        logf = jnp.minimum(z, 0.0) - jnp.log(1.0 + jnp.exp(-jnp.abs(z)))
        cum_ref[...] = _dot_hi(_tri(True), logf) + carry[...]
        carry[...] += jnp.sum(logf, axis=0, keepdims=True)

    return pl.pallas_call(
        body, grid=(s // CHUNK,),
        in_specs=[pl.BlockSpec((CHUNK, N_F), lambda i: (i, 0)), pl.BlockSpec((1, N_F), lambda i: (0, 0))],
        out_specs=pl.BlockSpec((CHUNK, N_F), lambda i: (i, 0)),
        out_shape=jax.ShapeDtypeStruct((s, N_F), F32),
        scratch_shapes=[pltpu.VMEM((1, N_F), F32)],
        compiler_params=_cp(("arbitrary",)), name=name)(pf, fb)


def _cum_bwd(dcum, pf, fb, name):
    s = pf.shape[0]
    nb = s // CHUNK

    def body(dc_ref, pf_ref, fb_ref, dpf_ref, dfb_ref, carry):
        i = pl.program_id(0)

        @pl.when(i == 0)
        def _():
            carry[...] = jnp.zeros_like(carry)
            dfb_ref[...] = jnp.zeros_like(dfb_ref)

        dc = dc_ref[...]
        dlogf = _dot_hi(_tri(False), dc) + carry[...]
        carry[...] += jnp.sum(dc, axis=0, keepdims=True)
        z = pf_ref[...] + fb_ref[...]
        dz = dlogf * (1.0 - _sig(z))
        dpf_ref[...] = dz.astype(BF16)
        dfb_ref[...] += _fold8(dz)

    rev = lambda i: (nb - 1 - i, 0)
    return pl.pallas_call(
        body, grid=(nb,),
        in_specs=[pl.BlockSpec((CHUNK, N_F), rev), pl.BlockSpec((CHUNK, N_F), rev), pl.BlockSpec((1, N_F), lambda i: (0, 0))],
        out_specs=[pl.BlockSpec((CHUNK, N_F), rev), pl.BlockSpec((8, N_F), lambda i: (0, 0))],
        out_shape=[jax.ShapeDtypeStruct((s, N_F), BF16), jax.ShapeDtypeStruct((8, N_F), F32)],
        scratch_shapes=[pltpu.VMEM((1, N_F), F32)],
        compiler_params=_cp(("arbitrary",)), name=name)(dcum, pf, fb)


def _causal_mask():
    r = lax.broadcasted_iota(jnp.int32, (BQ, BQ), 0)
    c = lax.broadcasted_iota(jnp.int32, (BQ, BQ), 1)
    return r >= c


def _attn_fwd(q, k, v, cq, ck, name, comm=None):
    s = q.shape[1]
    nb = s // BQ

    def body(q_ref, k_ref, v_ref, cq_ref, ck_ref, o_ref, lse_ref):
        for qi in range(nb):
            qs = qi * BQ
            qb = q_ref[0, qs:qs + BQ, :]
            cqb = cq_ref[0, qs:qs + BQ, :]

            def block(kj, carry, masked):
                m, l, acc = carry
                ks = pl.multiple_of(kj * BQ, BQ)
                kb = k_ref[0, pl.ds(ks, BQ), :]
                vb = v_ref[0, pl.ds(ks, BQ), :]
                sc = _dot_nt(qb, kb) * SCALE + (cqb - ck_ref[0, kj])
                if masked:
                    sc = jnp.where(_causal_mask(), sc, NEG)
                m_new = jnp.maximum(m, jnp.max(sc, axis=-1, keepdims=True))
                alpha = jnp.exp(m - m_new)
                p = jnp.exp(sc - m_new)
                l = alpha * l + jnp.sum(p, axis=-1, keepdims=True)
                acc = alpha * acc + _dot(p.astype(BF16), vb)
                return m_new, l, acc

            carry = (jnp.full((BQ, 1), NEG, F32), jnp.zeros((BQ, 1), F32), jnp.zeros((BQ, HEAD_DIM), F32))
            if qi > 0:
                carry = lax.fori_loop(0, qi, lambda kj, cr: block(kj, cr, False), carry)
            m, l, acc = block(qi, carry, True)
            o_ref[0, qs:qs + BQ, :] = acc / l
            lse_ref[0, qs:qs + BQ, :] = m + jnp.log(l)

    hblk = pl.BlockSpec((1, s, HEAD_DIM), lambda h: (h, 0, 0))
    cblk = pl.BlockSpec((1, s, 1), lambda h: (h, 0, 0))
    return _pcall(
        body, grid=(N_HEADS,),
        in_specs=[hblk, hblk, hblk, cblk, pl.BlockSpec((1, nb, 1, BQ), lambda h: (h, 0, 0, 0))],
        out_specs=[hblk, cblk],
        out_shape=[jax.ShapeDtypeStruct((N_HEADS, s, HEAD_DIM), F32), jax.ShapeDtypeStruct((N_HEADS, s, 1), F32)],
        operands=(q, k, v, cq, ck), name=name, comm=comm)


def _attn_bwd(q, k, v, cq, ck, o, lse, do, name, comm=None):
    s = q.shape[1]
    nb = s // BQ

    def body(q_ref, k_ref, v_ref, cq_ref, ck_ref, o_ref, lse_ref, do_ref, dq_ref, dk_ref, dv_ref, dcq_ref, dck_ref, delta):
        delta[...] = jnp.sum(do_ref[0] * o_ref[0], axis=-1, keepdims=True)
        dq_ref[...] = jnp.zeros_like(dq_ref)
        dcq_ref[...] = jnp.zeros_like(dcq_ref)
        for kj in range(nb):
            ks = kj * BQ
            kb = k_ref[0, ks:ks + BQ, :]
            vb = v_ref[0, ks:ks + BQ, :]
            ckb = ck_ref[0, kj]

            def block(qi, carry, masked):
                dk_acc, dv_acc, dck_acc = carry
                qs = pl.multiple_of(qi * BQ, BQ)
                qb = q_ref[0, pl.ds(qs, BQ), :]
                dob = do_ref[0, pl.ds(qs, BQ), :].astype(BF16)
                sc = _dot_nt(qb, kb) * SCALE + (cq_ref[0, pl.ds(qs, BQ), :] - ckb)
                p = jnp.exp(sc - lse_ref[0, pl.ds(qs, BQ), :])
                if masked:
                    p = jnp.where(_causal_mask(), p, 0.0)
                dp = _dot_nt(dob, vb)
                ds = p * (dp - delta[pl.ds(qs, BQ), :])
                ds_b = ds.astype(BF16)
                dv_acc = dv_acc + _dot_tn(p.astype(BF16), dob)
                dk_acc = dk_acc + _dot_tn(ds_b, qb) * SCALE
                dq_ref[0, pl.ds(qs, BQ), :] += _dot(ds_b, kb) * SCALE
                dcq_ref[0, pl.ds(qs, BQ), :] += jnp.sum(ds, axis=-1, keepdims=True)
                dck_acc = dck_acc - jnp.sum(ds, axis=0, keepdims=True)
                return dk_acc, dv_acc, dck_acc

            carry = (jnp.zeros((BQ, HEAD_DIM), F32), jnp.zeros((BQ, HEAD_DIM), F32), jnp.zeros((1, BQ), F32))
            carry = block(kj, carry, True)
            if kj < nb - 1:
                carry = lax.fori_loop(kj + 1, nb, lambda qi, cr: block(qi, cr, False), carry)
            dk_ref[0, ks:ks + BQ, :] = carry[0]
            dv_ref[0, ks:ks + BQ, :] = carry[1]
            dck_ref[0, kj] = carry[2]

    hblk = pl.BlockSpec((1, s, HEAD_DIM), lambda h: (h, 0, 0))
    cblk = pl.BlockSpec((1, s, 1), lambda h: (h, 0, 0))
    kblk = pl.BlockSpec((1, nb, 1, BQ), lambda h: (h, 0, 0, 0))
    hshape = jax.ShapeDtypeStruct((N_HEADS, s, HEAD_DIM), F32)
    return _pcall(
        body, grid=(N_HEADS,),
        in_specs=[hblk, hblk, hblk, cblk, kblk, hblk, cblk, hblk],
        out_specs=[hblk, hblk, hblk, cblk, kblk],
        out_shape=[hshape, hshape, hshape, jax.ShapeDtypeStruct((N_HEADS, s, 1), F32),
                   jax.ShapeDtypeStruct((N_HEADS, nb, 1, BQ), F32)],
        scratch_shapes=[pltpu.VMEM((s, 1), F32)],
        operands=(q, k, v, cq, ck, o, lse, do), name=name, comm=comm)


def _merge_fwd(x, ya, yc, yd, o, pm, pg, wb, wo, name, comm=None):
    s = x.shape[0]

    def body(x_ref, ya_ref, yc_ref, yd_ref, o_ref, bg_ref, pg_ref, wb_ref, wo_ref, xo_ref, yb_ref):
        yb = (o_ref[...] * _silu(bg_ref[...])).astype(BF16)
        yb_ref[...] = yb
        ys = (ya_ref[...], yb, yc_ref[...], yd_ref[...])
        merged = jnp.zeros((TMG, D_MODEL), F32)
        for n in range(N_BRANCH):
            merged = merged + _sig(pg_ref[:, n * D_MODEL:(n + 1) * D_MODEL]) * _dot(ys[n], wb_ref[n])
        xo_ref[...] = x_ref[...] + _dot(merged.astype(BF16), wo_ref[...])

    xt = pl.BlockSpec((TMG, D_MODEL), lambda i: (i, 0))
    yt = pl.BlockSpec((TMG, BR), lambda i: (i, 0))
    return _pcall(
        body, grid=(s // TMG,),
        in_specs=[xt, yt, yt, yt, yt, pl.BlockSpec((TMG, BR), lambda i: (i, O_BG // BR)),
                  pl.BlockSpec((TMG, N_MERGE), lambda i: (i, 0)),
                  pl.BlockSpec((N_BRANCH, BR, D_MODEL), lambda i: (0, 0, 0)), pl.BlockSpec((D_MODEL, D_MODEL), lambda i: (0, 0))],
        out_specs=[xt, yt],
        out_shape=[jax.ShapeDtypeStruct((s, D_MODEL), F32), jax.ShapeDtypeStruct((s, BR), BF16)],
        operands=(x, ya, yc, yd, o, pm, pg, wb, wo), name=name, comm=comm)


def _merge_bwd(dx, ya, yb, yc, yd, o, pm, pg, wb, wo, name, comm=None):
    s = dx.shape[0]
    nt = s // TMG

    def body(dx_ref, ya_ref, yb_ref, yc_ref, yd_ref, o_ref, bg_ref, pg_ref, wb_ref, wo_ref,
             dpg_ref, dya_ref, do_ref, dbg_ref, dyc_ref, dyd_ref, dwb_ref, dwo_ref, dwb_acc, dwo_acc):
        i = pl.program_id(0)

        @pl.when(i == 0)
        def _():
            dwb_acc[...] = jnp.zeros_like(dwb_acc)
            dwo_acc[...] = jnp.zeros_like(dwo_acc)

        dxb = dx_ref[...].astype(BF16)
        dmerged = _dot_nt(dxb, wo_ref[...])
        ys = (ya_ref[...], yb_ref[...], yc_ref[...], yd_ref[...])
        dys = (dya_ref, None, dyc_ref, dyd_ref)
        merged = jnp.zeros((TMG, D_MODEL), F32)
        for n in range(N_BRANCH):
            gate = _sig(pg_ref[:, n * D_MODEL:(n + 1) * D_MODEL])
            pr = _dot(ys[n], wb_ref[n])
            merged = merged + gate * pr
            dpg_ref[:, n * D_MODEL:(n + 1) * D_MODEL] = (dmerged * pr * gate * (1.0 - gate)).astype(BF16)
            dpr = (gate * dmerged).astype(BF16)
            dwb_acc[n] += _dot_tn(ys[n], dpr)
            dyn = _dot_nt(dpr, wb_ref[n])
            if n == 1:
                bg = bg_ref[...]
                do_ref[...] = dyn * _silu(bg)
                dbg_ref[...] = dyn * o_ref[...] * _dsilu(bg)
            else:
                dys[n][...] = dyn
        dwo_acc[...] += _dot_tn(merged.astype(BF16), dxb)

        @pl.when(i == nt - 1)
        def _():
            dwb_ref[...] = dwb_acc[...].astype(BF16)
            dwo_ref[...] = dwo_acc[...].astype(BF16)

    xt = pl.BlockSpec((TMG, D_MODEL), lambda i: (i, 0))
    yt = pl.BlockSpec((TMG, BR), lambda i: (i, 0))
    gt = pl.BlockSpec((TMG, N_MERGE), lambda i: (i, 0))
    wbs = pl.BlockSpec((N_BRANCH, BR, D_MODEL), lambda i: (0, 0, 0))
    wos = pl.BlockSpec((D_MODEL, D_MODEL), lambda i: (0, 0))
    yf = jax.ShapeDtypeStruct((s, BR), F32)
    return _pcall(
        body, grid=(nt,),
        in_specs=[xt, yt, yt, yt, yt, yt, pl.BlockSpec((TMG, BR), lambda i: (i, O_BG // BR)), gt, wbs, wos],
        out_specs=[gt, yt, yt, yt, yt, yt, wbs, wos],
        out_shape=[jax.ShapeDtypeStruct((s, N_MERGE), BF16), yf, yf, yf, yf, yf,
                   jax.ShapeDtypeStruct((N_BRANCH, BR, D_MODEL), BF16), jax.ShapeDtypeStruct((D_MODEL, D_MODEL), BF16)],
        scratch_shapes=[pltpu.VMEM((N_BRANCH, BR, D_MODEL), F32), pltpu.VMEM((D_MODEL, D_MODEL), F32)],
        operands=(dx, ya, yb, yc, yd, o, pm, pg, wb, wo), name=name, comm=comm)


def _heads(a):
    s = a.shape[0]
    return a.reshape(s, N_HEADS, HEAD_DIM).transpose(1, 0, 2)


def _unheads(a):
    s = a.shape[1]
    return a.transpose(1, 0, 2).reshape(s, N_HEADS * HEAD_DIM)


def _layer_fwd(x, wl, tag, attach=None):
    attach = attach or {}

    def riding(stage):
        comm, sink = attach.get(stage, (None, None))
        return comm, (sink or (lambda res: None))

    s = x.shape[0]
    h = _rms_fwd(x, wl["norm_g"], "rms_fwd" + tag)
    comm, sink = riding("proj_mix")
    pm, res = _mm_nn(h, wl["w_mix"], 1792, "proj_mix" + tag, comm)
    sink(res)
    pg, _ = _mm_nn(h, wl["w_merge"], 2048, "proj_merge" + tag)
    pf, _ = _mm_nn(h, wl["w_f"], N_F, "proj_f" + tag)
    comm, sink = riding("mix_fwd")
    (ya, yc, yd), res = _mix_fwd(pm, wl, "mix_fwd" + tag, comm)
    sink(res)
    cum = _cum_fwd(pf, wl["f_bias"], "cum_fwd" + tag)
    cum_t = cum[:, :N_HEADS].T
    cq = cum_t.reshape(N_HEADS, s, 1)
    ck = cum_t.reshape(N_HEADS, s // BQ, 1, BQ)
    qkv = pm[:, O_QKV:O_QKV + 3 * BR].astype(BF16)
    q, k, v = (_heads(qkv[:, j * BR:(j + 1) * BR]) for j in range(3))
    comm, sink = riding("attn_fwd")
    (o_h, lse), res = _attn_fwd(q, k, v, cq, ck, "attn_fwd" + tag, comm)
    sink(res)
    o = _unheads(o_h)
    comm, sink = riding("merge_fwd")
    (x_next, yb), res = _merge_fwd(x, ya, yc, yd, o, pm, pg, wl["wb"], wl["wo"], "merge_fwd" + tag, comm)
    sink(res)
    saved = dict(x=x, h=h, pm=pm, pg=pg, pf=pf, ya=ya, yb=yb, yc=yc, yd=yd, q=q, k=k, v=v, cq=cq, ck=ck,
                 o_h=o_h, o=o, lse=lse)
    return x_next, saved


def _blocks_rows(g):
    return g.reshape(N_DEV, g.shape[0] // N_DEV, g.shape[1])


def _blocks_cols(g):
    return g.reshape(N_BRANCH * BR, N_DEV, D_MODEL // N_DEV).transpose(1, 0, 2)


def _layer_bwd(dx_next, sv, wl, tag, dist, riding, extra_small):
    s = dx_next.shape[0]
    (dpg, dya, do, dbg, dyc, dyd, dwb, dwo), rode = _merge_bwd(
        dx_next, sv["ya"], sv["yb"], sv["yc"], sv["yd"], sv["o"], sv["pm"], sv["pg"], wl["wb"], wl["wo"], "merge_bwd" + tag,
        riding)
    dw_merge = _mm_tn(sv["h"], dpg, 2048, "dw_merge" + tag)
    early = [(_blocks_rows(dw_merge), False), (_blocks_cols(dwb), False), (_blocks_rows(dwo), False)] if dist else None
    (dq, dk, dv, dcq, dck), early_out = _attn_bwd(sv["q"], sv["k"], sv["v"], sv["cq"], sv["ck"], sv["o_h"], sv["lse"],
                                                  _heads(do), "attn_bwd" + tag, early)
    dqkv = jnp.concatenate([_unheads(dq), _unheads(dk), _unheads(dv)], axis=1).astype(BF16)
    dcum_t = dcq.reshape(N_HEADS, s) + dck.reshape(N_HEADS, s)
    dcum = jnp.pad(dcum_t.T, ((0, 0), (0, N_F - N_HEADS)))
    dpf, dfb = _cum_bwd(dcum, sv["pf"], wl["f_bias"], "cum_bwd" + tag)
    dpm, dgw, dgb, vec, dscw, ddww = _mix_bwd(sv["pm"], dya, dyc, dyd, dqkv, dbg, wl, "mix_bwd" + tag)
    dh = _dh(dpm, dpg, dpf, wl["w_mix"], wl["w_merge"], wl["w_f"], "dh" + tag)
    dw_mix = _mm_tn(sv["h"], dpm, 1792, "dw_mix" + tag)
    dw_f = _mm_tn(sv["h"], dpf, N_F, "dw_f" + tag)
    dx, dng = _rms_bwd(dh, sv["x"], wl["norm_g"], dx_next, "rms_bwd" + tag)
    causal = jnp.tril(jnp.ones((CHUNK, CHUNK), bool))
    small = dict(
        norm_g=dng.sum(0),
        f_bias=dfb.sum(0)[:N_HEADS],
        sgu_w=jnp.where(causal[None], dgw, 0.0),
        sgu_b=dgb.reshape(CHUNK, N_HEADS, HEAD_DIM).sum(-1).T,
        sgu_ln_g=vec[0], sgu_ln_b=vec[1], conf_dw_b=vec[2], conf_ln_g=vec[3], conf_ln_b=vec[4],
        short_conv_w=dscw.reshape(8, 8, BR).sum(1)[:SHORT_CONV],
        conf_dw_w=ddww.reshape(32, 8, BR).sum(1)[:CONF_CONV],
    )
    grads = dict(small, w_mix=dw_mix, w_merge=dw_merge, w_f=dw_f, wb=dwb, wo=dwo)
    slab, spans = _pack([small[nm] for nm in SMALL] + list(extra_small))
    late = [(_blocks_rows(dw_mix), False), (_blocks_rows(dw_f), False), (slab, True)]
    return dx, grads, early_out, (late, spans), rode


def _prep_layer_small(norm_g, f_bias, sgu_w, sgu_b, sgu_ln_g, sgu_ln_b, scw, dww, conf_dw_b, conf_ln_g, conf_ln_b):
    causal = jnp.tril(jnp.ones((CHUNK, CHUNK), bool))
    gw = jnp.where(causal[None], sgu_w, 0.0)
    row = lambda a: a.reshape(1, -1)
    return dict(
        norm_g=row(norm_g),
        f_bias=jnp.pad(row(f_bias), ((0, 0), (0, N_F - N_HEADS))),
        gw=gw.astype(BF16), gwt=gw.transpose(0, 2, 1).astype(BF16),
        gb=jnp.repeat(sgu_b.T, HEAD_DIM, axis=1),
        sgu_ln_g=row(sgu_ln_g), sgu_ln_b=row(sgu_ln_b),
        scw=jnp.pad(scw, ((0, 8 - SHORT_CONV), (0, 0))), dww=jnp.pad(dww, ((0, 32 - CONF_CONV), (0, 0))),
        conf_dw_b=row(conf_dw_b), conf_ln_g=row(conf_ln_g), conf_ln_b=row(conf_ln_b))


def _local_step(x, target, layers, final_g):
    saved = []
    for l in range(DEPTH):
        x, sv = _layer_fwd(x, layers[l], str(l))
        saved.append(sv)
    loss_p, dx, dfg = _loss_head(x, final_g.reshape(1, D_MODEL), target)
    grads = [None] * DEPTH
    for l in reversed(range(DEPTH)):
        dx, grads[l], _, _, _ = _layer_bwd(dx, saved[l], layers[l], str(l), False, None, [])
    return 0.5 / D_MODEL * jnp.sum(loss_p), dx, grads, dfg.sum(0)


def _sum8(a, name):
    _, r, c = a.shape
    tr = r
    while tr * c * a.dtype.itemsize * N_DEV > 4 * 1024 * 1024 and tr % 32 == 0:
        tr //= 2

    def body(a_ref, o_ref):
        acc = a_ref[0].astype(F32)
        for d in range(1, N_DEV):
            acc = acc + a_ref[d].astype(F32)
        o_ref[...] = acc

    return pl.pallas_call(
        body, grid=(r // tr,),
        in_specs=[pl.BlockSpec((N_DEV, tr, c), lambda i: (0, i, 0))],
        out_specs=pl.BlockSpec((tr, c), lambda i: (i, 0)),
        out_shape=jax.ShapeDtypeStruct((r, c), F32), compiler_params=_cp(("parallel",)), name=name)(a)


def _adamw(w, g, m, v, name):
    l, r, c = w.shape
    tr = r
    while tr * c * 4 > 1024 * 1024 and tr % 16 == 0:
        tr //= 2
    c1 = 1.0 - ADAM_B1 ** ADAM_STEP
    c2 = 1.0 - ADAM_B2 ** ADAM_STEP

    def body(w_ref, g_ref, m_ref, v_ref, d_ref, mo_ref, vo_ref):
        gv = g_ref[...]
        mn = ADAM_B1 * m_ref[...] + (1.0 - ADAM_B1) * gv
        vn = ADAM_B2 * v_ref[...] + (1.0 - ADAM_B2) * (gv * gv)
        mo_ref[...] = mn
        vo_ref[...] = vn
        d_ref[...] = -ADAM_LR * ((mn / c1) / (jnp.sqrt(vn / c2) + ADAM_EPS) + ADAM_WD * w_ref[...])

    blk = pl.BlockSpec((1, tr, c), lambda a, i: (a, i, 0))
    shp = jax.ShapeDtypeStruct((l, r, c), F32)
    return pl.pallas_call(
        body, grid=(l, r // tr), in_specs=[blk] * 4, out_specs=[blk] * 3, out_shape=[shp] * 3,
        compiler_params=_cp(("parallel", "parallel")), name=name)(w, g, m, v)


def _pack(parts):
    rows, spans, r = [], [], 0
    for p in parts:
        flat = p.reshape(-1)
        nr = -(-flat.shape[0] // 1024) * 8
        rows.append(jnp.pad(flat, (0, nr * 128 - flat.shape[0])).reshape(nr, 128))
        spans.append((r, nr, p.shape))
        r += nr
    return jnp.concatenate(rows, axis=0), spans


def _unpack(slab, spans):
    out = []
    for r, nr, shape in spans:
        size = math.prod(shape)
        out.append(slab[r:r + nr].reshape(-1)[:size].reshape(shape))
    return out


def _split_w_in(w):
    mix = jnp.concatenate([w[..., 0:1536], w[..., 1540:1796], w[..., 3332:3588], w[..., 1796:2820], w[..., 2820:3332]], axis=-1)
    return mix, w[..., 3588:7684], w[..., 1536:1540]


def _join_w_in(mix, merge, f):
    return jnp.concatenate([mix[..., 0:1536], f, mix[..., 1536:1792], mix[..., 2048:3072], mix[..., 3072:3584],
                            mix[..., 1792:2048], merge], axis=-1)


SMALL = ("norm_g", "f_bias", "sgu_w", "sgu_b", "sgu_ln_g", "sgu_ln_b", "short_conv_w", "conf_dw_w", "conf_dw_b",
         "conf_ln_g", "conf_ln_b")


def kernel(x, norm_g, w_in, f_bias, sgu_w, sgu_b, sgu_ln_g, sgu_ln_b, short_conv_w, conf_dw_w, conf_dw_b, conf_ln_g, conf_ln_b, w_branch, w_out, final_g, loss_target, m_norm_g, m_w_in, m_f_bias, m_sgu_w, m_sgu_b, m_sgu_ln_g, m_sgu_ln_b, m_short_conv_w, m_conf_dw_w, m_conf_dw_b, m_conf_ln_g, m_conf_ln_b, m_w_branch, m_w_out, m_final_g, v_norm_g, v_w_in, v_f_bias, v_sgu_w, v_sgu_b, v_sgu_ln_g, v_sgu_ln_b, v_short_conv_w, v_conf_dw_w, v_conf_dw_b, v_conf_ln_g, v_conf_ln_b, v_w_branch, v_w_out, v_final_g):
    me = 4 * lax.axis_index("x") + 2 * lax.axis_index("y") + lax.axis_index("c")
    rows = D_MODEL // N_DEV
    cshard = BR // N_DEV

    sh = []
    for l in range(DEPTH):
        mix, merge, f = _split_w_in(w_in[l])
        sh.append(dict(mix=mix.astype(BF16), merge=merge.astype(BF16),
                       f=jnp.pad(f, ((0, 0), (0, N_F - N_HEADS))).astype(BF16),
                       wb=w_branch[l].astype(BF16), wo=w_out[l].astype(BF16)))
    conv_slab, conv_spans = _pack([short_conv_w, conf_dw_w])
    g_mix, g_f, g_conv = _exchange([(sh[0]["mix"], True), (sh[0]["f"], True), (conv_slab, True)], "gather_first")
    conv_full = [_unpack(g_conv[d], conv_spans) for d in range(N_DEV)]
    scw_full = jnp.concatenate([cf[0] for cf in conv_full], axis=-1)
    dww_full = jnp.concatenate([cf[1] for cf in conv_full], axis=-1)
    layers = [_prep_layer_small(norm_g[l], f_bias[l], sgu_w[l], sgu_b[l], sgu_ln_g[l], sgu_ln_b[l], scw_full[l], dww_full[l],
                                conf_dw_b[l], conf_ln_g[l], conf_ln_b[l]) for l in range(DEPTH)]
    layers[0].update(w_mix=g_mix.reshape(D_MODEL, N_MIX), w_f=g_f.reshape(D_MODEL, N_F))

    def put_in(l):
        def sink(res):
            layers[l].update(w_mix=res[0].reshape(D_MODEL, N_MIX), w_f=res[1].reshape(D_MODEL, N_F),
                             w_merge=res[2].reshape(D_MODEL, N_MERGE))
        return sink

    def put_merge(l):
        def sink(res):
            layers[l].update(w_merge=res[0].reshape(D_MODEL, N_MERGE))
        return sink

    def put_out(l):
        def sink(res):
            layers[l].update(wb=res[0].transpose(1, 2, 0, 3).reshape(N_BRANCH, BR, D_MODEL), wo=res[1].reshape(D_MODEL, D_MODEL))
        return sink

    attach0 = {
        "proj_mix": ([(sh[0]["merge"], True)], put_merge(0)),
        "mix_fwd": ([(sh[0]["wb"], True), (sh[0]["wo"], True)], put_out(0)),
        "attn_fwd": ([(sh[1]["mix"], True), (sh[1]["f"], True), (sh[1]["merge"], True)], put_in(1)),
        "merge_fwd": ([(sh[1]["wb"], True), (sh[1]["wo"], True)], put_out(1)),
    }

    xs = x[0]
    xs, sv0 = _layer_fwd(xs, layers[0], "0", attach0)
    xs, sv1 = _layer_fwd(xs, layers[1], "1")
    loss_p, dx, dfg = _loss_head(xs, final_g.reshape(1, D_MODEL), loss_target[0])
    loss = lax.psum(0.5 / D_MODEL * jnp.sum(loss_p), ("x", "y", "c"))
    dx, g1, early1, (late1, spans1), _ = _layer_bwd(dx, sv1, layers[1], "1", True, None, [dfg.sum(0)])
    dx, g0, early0, (late0, spans0), late1_out = _layer_bwd(dx, sv0, layers[0], "0", True, late1, [])
    late0_out = _exchange(late0, "trade_last")

    red, small = [], []
    for l, (early, late, spans) in enumerate(((early0, late0_out, spans0), (early1, late1_out, spans1))):
        t = str(l)
        red.append(dict(merge=_sum8(early[0], "sum_merge" + t), wb=_sum8(early[1], "sum_wb" + t), wo=_sum8(early[2], "sum_wo" + t),
                        mix=_sum8(late[0], "sum_mix" + t), f=_sum8(late[1], "sum_f" + t)))
        small.append(_unpack(_sum8(late[2], "sum_small" + t), spans))
    gs = {nm: jnp.stack([small[l][j] for l in range(DEPTH)]) for j, nm in enumerate(SMALL)}
    gs["final_g"] = small[1][len(SMALL)]
    gs["short_conv_w"] = lax.dynamic_slice_in_dim(gs["short_conv_w"], me * cshard, cshard, axis=2)
    gs["conf_dw_w"] = lax.dynamic_slice_in_dim(gs["conf_dw_w"], me * cshard, cshard, axis=2)
    g_w_in = jnp.stack([_join_w_in(red[l]["mix"], red[l]["merge"], red[l]["f"][:, :N_HEADS]) for l in range(DEPTH)])
    g_w_branch = jnp.stack([red[l]["wb"].reshape(N_BRANCH, BR, rows) for l in range(DEPTH)])
    g_w_out = jnp.stack([red[l]["wo"] for l in range(DEPTH)])

    d_w_in, nm_w_in, nv_w_in = _adamw(w_in, g_w_in, m_w_in, v_w_in, "adamw_w_in")
    flat = lambda a: a.reshape(DEPTH, N_BRANCH * BR, rows)
    d_w_branch, nm_w_branch, nv_w_branch = (a.reshape(w_branch.shape) for a in _adamw(
        flat(w_branch), flat(g_w_branch), flat(m_w_branch), flat(v_w_branch), "adamw_w_branch"))
    d_w_out, nm_w_out, nv_w_out = _adamw(w_out, g_w_out, m_w_out, v_w_out, "adamw_w_out")
    names = SMALL + ("final_g",)
    ws = dict(zip(names, (norm_g, f_bias, sgu_w, sgu_b, sgu_ln_g, sgu_ln_b, short_conv_w, conf_dw_w, conf_dw_b, conf_ln_g,
                          conf_ln_b, final_g)))
    ms = dict(zip(names, (m_norm_g, m_f_bias, m_sgu_w, m_sgu_b, m_sgu_ln_g, m_sgu_ln_b, m_short_conv_w, m_conf_dw_w,
                          m_conf_dw_b, m_conf_ln_g, m_conf_ln_b, m_final_g)))
    vs = dict(zip(names, (v_norm_g, v_f_bias, v_sgu_w, v_sgu_b, v_sgu_ln_g, v_sgu_ln_b, v_short_conv_w, v_conf_dw_w,
                          v_conf_dw_b, v_conf_ln_g, v_conf_ln_b, v_final_g)))
    w_slab, spans = _pack([ws[nm] for nm in names])
    g_slab, _ = _pack([gs[nm] for nm in names])
    m_slab, _ = _pack([ms[nm] for nm in names])
    v_slab, _ = _pack([vs[nm] for nm in names])
    d_s, nm_s, nv_s = (dict(zip(names, _unpack(a[0], spans))) for a in _adamw(w_slab[None], g_slab[None], m_slab[None],
                                                                              v_slab[None], "adamw_small"))

    def ordered(small, w_in_v, w_branch_v, w_out_v):
        return [small["norm_g"], w_in_v, small["f_bias"], small["sgu_w"], small["sgu_b"], small["sgu_ln_g"],
                small["sgu_ln_b"], small["short_conv_w"], small["conf_dw_w"], small["conf_dw_b"], small["conf_ln_g"],
                small["conf_ln_b"], w_branch_v, w_out_v, small["final_g"]]

    return (loss, dx[None], *ordered(gs, g_w_in, g_w_branch, g_w_out), *ordered(d_s, d_w_in, d_w_branch, d_w_out),
            *ordered(nm_s, nm_w_in, nm_w_branch, nm_w_out), *ordered(nv_s, nv_w_in, nv_w_branch, nv_w_out))
```

```python
import functools
import math

import jax
import jax.numpy as jnp
from jax import lax
from jax.experimental import pallas as pl
from jax.experimental.pallas import tpu as pltpu

F32 = jnp.float32
BF16 = jnp.bfloat16

D_MODEL = 1024
DEPTH = 2
N_BRANCH = 4
BR = 256
N_HEADS = 4
HEAD_DIM = 64
CHUNK = 128
SHORT_CONV = 3
CONF_CONV = 31
EPS = 1e-6
N_DEV = 8

ADAM_LR = 0.001
ADAM_B1 = 0.9
ADAM_B2 = 0.999
ADAM_EPS = 1e-08
ADAM_WD = 0.01
ADAM_STEP = 10

O_UV, O_AG, O_QKV, O_BG, O_DG, O_CIN, O_CG, O_GLU = 0, 512, 768, 1536, 1792, 2048, 2816, 3072
N_MIX = 3584
N_MERGE = N_BRANCH * D_MODEL
N_F = 128
IN_COLS = 7684
HALO = 32
TM = 512
TMG = 256
BQ = 512
SUB = 64
VMEM_LIMIT = 56 * 1024 * 1024
NEG = -1e30
SCALE = 1.0 / math.sqrt(HEAD_DIM)
GELU_K = math.sqrt(2.0 / math.pi)


def _cp(sem=None):
    return pltpu.CompilerParams(dimension_semantics=sem, vmem_limit_bytes=VMEM_LIMIT)


PEER_ORDER = (6, 4, 2, 7, 5, 3, 1)
RELAYED = (3, 5, 7)


def _xchg(cin, cout, send, recv, loc, modes, start):
    x, y, c = lax.axis_index("x"), lax.axis_index("y"), lax.axis_index("c")
    me = 4 * x + 2 * y + c

    def peer_of(kk):
        px, py, pc = lax.rem(x + (kk >> 2 & 1), 2), lax.rem(y + (kk >> 1 & 1), 2), lax.rem(c + (kk & 1), 2)
        return (px, py, pc), 4 * px + 2 * py + pc

    def remote(src, dst, a, kk, pid):
        return pltpu.make_async_remote_copy(src_ref=src, dst_ref=dst, send_sem=send.at[a, kk], recv_sem=recv.at[a, kk],
                                            device_id=pid, device_id_type=pl.DeviceIdType.MESH)

    def outgoing(a, kk):
        if modes[a] and kk in RELAYED:
            _, origin = peer_of(kk - 1)
            return remote(cout[a].at[origin], cout[a].at[origin], a, kk, peer_of(1)[0])
        pid, peer = peer_of(kk)
        return remote(cin[a] if modes[a] else cin[a].at[peer], cout[a].at[me], a, kk, pid)

    def arrival(a, kk):
        _, peer = peer_of(kk)
        return remote(cout[a].at[peer], cout[a].at[peer], a, kk, (x, y, c))

    for a, gather in enumerate(modes):
        cp = pltpu.make_async_copy(cin[a] if gather else cin[a].at[me], cout[a].at[me], loc.at[a])
        if start:
            cp.start()
        else:
            cp.wait()
    if start:
        for kk in PEER_ORDER:
            for a, gather in enumerate(modes):
                if not (gather and kk in RELAYED):
                    outgoing(a, kk).start()
        return
    for kk in RELAYED:
        for a, gather in enumerate(modes):
            if gather:
                arrival(a, kk - 1).wait_recv()
                outgoing(a, kk).start()
    for kk in PEER_ORDER:
        for a in range(len(modes)):
            outgoing(a, kk).wait_send()
    for kk in PEER_ORDER:
        for a, gather in enumerate(modes):
            if not (gather and kk + 1 in RELAYED):
                arrival(a, kk).wait_recv()


def _xchg_shapes(comm):
    return [jax.ShapeDtypeStruct((N_DEV,) + tuple(a.shape[(0 if gather else 1):]), a.dtype) for a, gather in comm]


def _xchg_sems(n):
    return [pltpu.SemaphoreType.DMA((n, N_DEV)), pltpu.SemaphoreType.DMA((n, N_DEV)), pltpu.SemaphoreType.DMA((n,))]


def _exchange(comm, name):
    n = len(comm)
    modes = [g for _, g in comm]

    def body(*refs):
        cin, cout, (send, recv, loc) = refs[:n], refs[n:2 * n], refs[2 * n:]
        _xchg(cin, cout, send, recv, loc, modes, True)
        _xchg(cin, cout, send, recv, loc, modes, False)

    anyspec = pl.BlockSpec(memory_space=pl.ANY)
    return pl.pallas_call(
        body, in_specs=[anyspec] * n, out_specs=[anyspec] * n, out_shape=_xchg_shapes(comm),
        scratch_shapes=_xchg_sems(n), name=name)(*[a for a, _ in comm])


def _pcall(body, *, grid, in_specs, out_specs, out_shape, operands, name, scratch_shapes=(), comm=None):
    if not comm:
        outs = pl.pallas_call(
            body, grid=grid, in_specs=in_specs, out_specs=out_specs, out_shape=out_shape, scratch_shapes=list(scratch_shapes),
            compiler_params=_cp(("arbitrary",) * len(grid)), name=name)(*operands)
        return list(outs), []
    n, nin, nout, nsc = len(comm), len(operands), len(out_shape), len(scratch_shapes)
    modes = [g for _, g in comm]

    def wrapped(*refs):
        ins, cin = refs[:nin], refs[nin:nin + n]
        outs, cout = refs[nin + n:nin + n + nout], refs[nin + n + nout:nin + 2 * n + nout]
        scratch = refs[nin + 2 * n + nout:]
        own, (send, recv, loc) = scratch[:nsc], scratch[nsc:]
        ids = [pl.program_id(d) for d in range(len(grid))]
        first = functools.reduce(jnp.logical_and, [i == 0 for i in ids])
        last = functools.reduce(jnp.logical_and, [i == g - 1 for i, g in zip(ids, grid)])

        @pl.when(first)
        def _():
            _xchg(cin, cout, send, recv, loc, modes, True)

        body(*ins, *outs, *own)

        @pl.when(last)
        def _():
            _xchg(cin, cout, send, recv, loc, modes, False)

    anyspec = pl.BlockSpec(memory_space=pl.ANY)
    res = pl.pallas_call(
        wrapped, grid=grid, in_specs=list(in_specs) + [anyspec] * n, out_specs=list(out_specs) + [anyspec] * n,
        out_shape=list(out_shape) + _xchg_shapes(comm), scratch_shapes=list(scratch_shapes) + _xchg_sems(n),
        compiler_params=_cp(("arbitrary",) * len(grid)), name=name)(*operands, *[a for a, _ in comm])
    return list(res[:nout]), list(res[nout:])


def _sig(x):
    return 1.0 / (1.0 + jnp.exp(-x))


def _silu(x):
    return x * _sig(x)


def _dsilu(x):
    s = _sig(x)
    return s * (1.0 + x * (1.0 - s))


def _gelu(x):
    return 0.5 * x * (1.0 + jnp.tanh(GELU_K * (x + 0.044715 * x * x * x)))


def _dgelu(x):
    t = jnp.tanh(GELU_K * (x + 0.044715 * x * x * x))
    return 0.5 * (1.0 + t) + 0.5 * x * (1.0 - t * t) * GELU_K * (1.0 + 3.0 * 0.044715 * x * x)


def _ln_hat(x):
    mu = jnp.mean(x, axis=-1, keepdims=True)
    xc = x - mu
    rs = lax.rsqrt(jnp.mean(xc * xc, axis=-1, keepdims=True) + EPS)
    return xc * rs, rs


def _ln_bwd(dhat, hat, rs):
    return rs * (dhat - jnp.mean(dhat, axis=-1, keepdims=True) - hat * jnp.mean(dhat * hat, axis=-1, keepdims=True))


def _dot(a, b):
    return jnp.dot(a, b, preferred_element_type=F32)


def _dot_nt(a, b):
    return lax.dot_general(a, b, (((1,), (1,)), ((), ())), preferred_element_type=F32)


def _dot_tn(a, b):
    return lax.dot_general(a, b, (((0,), (0,)), ((), ())), preferred_element_type=F32)


def _fold8(x):
    acc = x[0:8]
    for r in range(1, x.shape[0] // 8):
        acc = acc + x[8 * r:8 * r + 8]
    return acc


def _rms_fwd(x, g, name):
    s = x.shape[0]

    def body(x_ref, g_ref, h_ref):
        xv = x_ref[...]
        r = lax.rsqrt(jnp.mean(xv * xv, axis=-1, keepdims=True) + EPS)
        h_ref[...] = (xv * r * g_ref[...]).astype(BF16)

    return pl.pallas_call(
        body, grid=(s // TM,),
        in_specs=[pl.BlockSpec((TM, D_MODEL), lambda i: (i, 0)), pl.BlockSpec((1, D_MODEL), lambda i: (0, 0))],
        out_specs=pl.BlockSpec((TM, D_MODEL), lambda i: (i, 0)),
        out_shape=jax.ShapeDtypeStruct((s, D_MODEL), BF16), compiler_params=_cp(("parallel",)), name=name)(x, g)


def _rms_bwd(dh, x, g, dx_next, name):
    s = x.shape[0]

    def body(dh_ref, x_ref, g_ref, dxn_ref, dx_ref, dg_ref):
        i = pl.program_id(0)
        xv = x_ref[...]
        r = lax.rsqrt(jnp.mean(xv * xv, axis=-1, keepdims=True) + EPS)
        xn = xv * r
        dhv = dh_ref[...]
        dxn = dhv * g_ref[...]
        dx_ref[...] = dxn_ref[...] + r * (dxn - xn * jnp.mean(dxn * xn, axis=-1, keepdims=True))

        @pl.when(i == 0)
        def _():
            dg_ref[...] = jnp.zeros_like(dg_ref)

        dg_ref[...] += _fold8(dhv * xn)

    tile = pl.BlockSpec((TM, D_MODEL), lambda i: (i, 0))
    return pl.pallas_call(
        body, grid=(s // TM,),
        in_specs=[tile, tile, pl.BlockSpec((1, D_MODEL), lambda i: (0, 0)), tile],
        out_specs=[tile, pl.BlockSpec((8, D_MODEL), lambda i: (0, 0))],
        out_shape=[jax.ShapeDtypeStruct((s, D_MODEL), F32), jax.ShapeDtypeStruct((8, D_MODEL), F32)],
        compiler_params=_cp(("arbitrary",)), name=name)(dh, x, g, dx_next)


def _loss_head(x, g, target):
    s = x.shape[0]

    def body(x_ref, g_ref, t_ref, loss_ref, dx_ref, dg_ref):
        i = pl.program_id(0)
        xv = x_ref[...]
        r = lax.rsqrt(jnp.mean(xv * xv, axis=-1, keepdims=True) + EPS)
        xn = xv * r
        err = xn * g_ref[...] - t_ref[...]
        dy = err * (1.0 / D_MODEL)
        dxn = dy * g_ref[...]
        dx_ref[...] = r * (dxn - xn * jnp.mean(dxn * xn, axis=-1, keepdims=True))

        @pl.when(i == 0)
        def _():
            dg_ref[...] = jnp.zeros_like(dg_ref)
            loss_ref[...] = jnp.zeros_like(loss_ref)

        dg_ref[...] += _fold8(dy * xn)
        loss_ref[...] += _fold8(err * err)

    tile = pl.BlockSpec((TM, D_MODEL), lambda i: (i, 0))
    acc = pl.BlockSpec((8, D_MODEL), lambda i: (0, 0))
    return pl.pallas_call(
        body, grid=(s // TM,),
        in_specs=[tile, pl.BlockSpec((1, D_MODEL), lambda i: (0, 0)), tile],
        out_specs=[acc, tile, acc],
        out_shape=[jax.ShapeDtypeStruct((8, D_MODEL), F32), jax.ShapeDtypeStruct((s, D_MODEL), F32),
                   jax.ShapeDtypeStruct((8, D_MODEL), F32)],
        compiler_params=_cp(("arbitrary",)), name="loss_head")(x, g, target)


def _mm_nn(a, b, tn, name, comm=None):
    m, k = a.shape
    n = b.shape[1]
    tm = 512

    def body(a_ref, b_ref, o_ref):
        o_ref[...] = _dot(a_ref[...], b_ref[...])

    (out,), couts = _pcall(
        body, grid=(n // tn, m // tm),
        in_specs=[pl.BlockSpec((tm, k), lambda j, i: (i, 0)), pl.BlockSpec((k, tn), lambda j, i: (0, j))],
        out_specs=[pl.BlockSpec((tm, tn), lambda j, i: (i, j))],
        out_shape=[jax.ShapeDtypeStruct((m, n), F32)], operands=(a, b), name=name, comm=comm)
    return out, couts


def _dh(dpm, dpg, dpf, w_mix, w_merge, w_f, name, comm=None):
    s = dpm.shape[0]
    tm = 1024 if s % 1024 == 0 else 512
    tk = 512
    n1, n2 = N_MIX // tk, N_MERGE // tk

    def body(dpm_ref, dpg_ref, dpf_ref, wm_ref, wg_ref, wf_ref, o_ref):
        j = pl.program_id(1)

        @pl.when(j == 0)
        def _():
            o_ref[...] = _dot_nt(dpf_ref[...], wf_ref[...])

        @pl.when(j < n1)
        def _():
            o_ref[...] += _dot_nt(dpm_ref[...], wm_ref[...])

        @pl.when(j >= n1)
        def _():
            o_ref[...] += _dot_nt(dpg_ref[...], wg_ref[...])

    mix_j = lambda j: jnp.minimum(j, n1 - 1)
    merge_j = lambda j: jnp.maximum(j - n1, 0)
    (out,), couts = _pcall(
        body, grid=(s // tm, n1 + n2),
        in_specs=[pl.BlockSpec((tm, tk), lambda i, j: (i, mix_j(j))), pl.BlockSpec((tm, tk), lambda i, j: (i, merge_j(j))),
                  pl.BlockSpec((tm, N_F), lambda i, j: (i, 0)),
                  pl.BlockSpec((D_MODEL, tk), lambda i, j: (0, mix_j(j))), pl.BlockSpec((D_MODEL, tk), lambda i, j: (0, merge_j(j))),
                  pl.BlockSpec((D_MODEL, N_F), lambda i, j: (0, 0))],
        out_specs=[pl.BlockSpec((tm, D_MODEL), lambda i, j: (i, 0))],
        out_shape=[jax.ShapeDtypeStruct((s, D_MODEL), F32)], operands=(dpm, dpg, dpf, w_mix, w_merge, w_f), name=name, comm=comm)
    return out, couts


def _mm_tn(a, d, tn, name):
    m, k = a.shape
    n = d.shape[1]
    tm = 512
    nm = m // tm

    def body(a_ref, d_ref, o_ref, acc):
        i = pl.program_id(1)

        @pl.when(i == 0)
        def _():
            acc[...] = jnp.zeros_like(acc)

        acc[...] += _dot_tn(a_ref[...], d_ref[...])

        @pl.when(i == nm - 1)
        def _():
            o_ref[...] = acc[...].astype(BF16)

    return pl.pallas_call(
        body, grid=(n // tn, nm),
        in_specs=[pl.BlockSpec((tm, k), lambda j, i: (i, 0)), pl.BlockSpec((tm, tn), lambda j, i: (i, j))],
        out_specs=pl.BlockSpec((k, tn), lambda j, i: (0, j)),
        out_shape=jax.ShapeDtypeStruct((k, n), BF16), scratch_shapes=[pltpu.VMEM((k, tn), F32)],
        compiler_params=_cp(("parallel", "arbitrary")), name=name)(a, d)


def _lane_head():
    return lax.broadcasted_iota(jnp.int32, (1, BR), 1) // HEAD_DIM


def _gmlp_chunk_fwd(p_ref, r0, gw_ref, gb_ref, lg, lb):
    uv = p_ref[r0:r0 + CHUNK, O_UV:O_UV + 2 * BR]
    u = _gelu(uv[:, :BR])
    vhat, rs = _ln_hat(_gelu(uv[:, BR:]))
    vn = (vhat * lg + lb).astype(BF16)
    head = _lane_head()
    mixed = gb_ref[...]
    for h in range(N_HEADS):
        mixed = mixed + jnp.where(head == h, _dot(gw_ref[h], vn), 0.0)
    return uv, u, vhat, rs, vn, mixed


def _conv_sub_blocks(rows):
    out = [(r, SUB) for r in range(0, rows - rows % SUB, SUB)]
    if rows % SUB:
        out.append((rows - rows % SUB, rows % SUB))
    return out


def _mix_fwd(pm, wl, name, comm=None):
    s = pm.shape[0]
    nt = s // TM

    def body(p_ref, ph_ref, gw_ref, gb_ref, lg_ref, lb_ref, scw_ref, dww_ref, dwb_ref, clg_ref, clb_ref,
             ya_ref, yc_ref, yd_ref, qkv_ref, zbuf, hbuf):
        i = pl.program_id(0)
        qkv_ref[...] = p_ref[:, O_QKV:O_QKV + 3 * BR].astype(BF16)
        lg = lg_ref[...]
        lb = lb_ref[...]
        for c in range(TM // CHUNK):
            r0 = c * CHUNK
            _, u, _, _, _, mixed = _gmlp_chunk_fwd(p_ref, r0, gw_ref, gb_ref, lg, lb)
            ag = p_ref[r0:r0 + CHUNK, O_AG:O_AG + BR]
            ya_ref[r0:r0 + CHUNK, :] = (u * mixed * _silu(ag)).astype(BF16)

        first = i > 0
        zbuf[0:HALO, :] = jnp.where(first, ph_ref[:, O_CIN + BR:O_CIN + 2 * BR] * ph_ref[:, O_CIN + 2 * BR:O_CIN + 3 * BR], 0.0)
        zbuf[HALO:HALO + TM, :] = p_ref[:, O_CIN + BR:O_CIN + 2 * BR] * p_ref[:, O_CIN + 2 * BR:O_CIN + 3 * BR]
        hbuf[0:HALO, :] = jnp.where(first, ph_ref[:, O_GLU:O_GLU + BR] * _sig(ph_ref[:, O_GLU + BR:O_GLU + 2 * BR]), 0.0)
        hbuf[HALO:HALO + TM, :] = p_ref[:, O_GLU:O_GLU + BR] * _sig(p_ref[:, O_GLU + BR:O_GLU + 2 * BR])

        clg = clg_ref[...]
        clb = clb_ref[...]
        for r0, nr in _conv_sub_blocks(TM):
            yc = jnp.zeros((nr, BR), F32)
            for k in range(SHORT_CONV):
                yc = yc + scw_ref[k:k + 1, :] * zbuf[HALO - (SHORT_CONV - 1) + k + r0:HALO - (SHORT_CONV - 1) + k + r0 + nr, :]
            bgate = p_ref[r0:r0 + nr, O_CIN:O_CIN + BR]
            cg = p_ref[r0:r0 + nr, O_CG:O_CG + BR]
            yc_ref[r0:r0 + nr, :] = (bgate * yc * _silu(cg)).astype(BF16)

            cc = jnp.zeros((nr, BR), F32) + dwb_ref[...]
            for k in range(CONF_CONV):
                cc = cc + dww_ref[k:k + 1, :] * hbuf[HALO - (CONF_CONV - 1) + k + r0:HALO - (CONF_CONV - 1) + k + r0 + nr, :]
            chat, _ = _ln_hat(cc)
            dg = p_ref[r0:r0 + nr, O_DG:O_DG + BR]
            yd_ref[r0:r0 + nr, :] = (_silu(chat * clg + clb) * _silu(dg)).astype(BF16)

    full = lambda shape: pl.BlockSpec(shape, lambda i: tuple(0 for _ in shape))
    ytile = pl.BlockSpec((TM, BR), lambda i: (i, 0))
    yshape = jax.ShapeDtypeStruct((s, BR), BF16)
    return _pcall(
        body, grid=(nt,),
        in_specs=[pl.BlockSpec((TM, N_MIX), lambda i: (i, 0)),
                  pl.BlockSpec((HALO, N_MIX), lambda i: (jnp.maximum(i * (TM // HALO) - 1, 0), 0)),
                  full((N_HEADS, CHUNK, CHUNK)), full((CHUNK, BR)), full((1, BR)), full((1, BR)),
                  full((8, BR)), full((32, BR)), full((1, BR)), full((1, BR)), full((1, BR))],
        out_specs=[ytile, ytile, ytile, pl.BlockSpec((TM, 3 * BR), lambda i: (i, 0))],
        out_shape=[yshape, yshape, yshape, jax.ShapeDtypeStruct((s, 3 * BR), BF16)],
        scratch_shapes=[pltpu.VMEM((HALO + TM, BR), F32), pltpu.VMEM((HALO + TM, BR), F32)],
        name=name, comm=comm, operands=(
            pm, pm, wl["gw"], wl["gb"], wl["sgu_ln_g"], wl["sgu_ln_b"], wl["scw"], wl["dww"], wl["conf_dw_b"],
            wl["conf_ln_g"], wl["conf_ln_b"]))


def _mix_bwd(pm, dya, dyc, dyd, dqkv, dbg, wl, name):
    s = pm.shape[0]
    nt = s // TM
    ext = TM + HALO

    def body(p_ref, ph_ref, pn_ref, dya_ref, dyc_ref, dycn_ref, dyd_ref, dydn_ref, dqkv_ref, dbg_ref,
             gw_ref, gwt_ref, gb_ref, lg_ref, lb_ref, scw_ref, dww_ref, dwb_ref, clg_ref, clb_ref,
             dp_ref, dgw_ref, dgb_ref, vec_ref, dscw_ref, ddww_ref, zbuf, dcb, hbuf, dcc):
        i = pl.program_id(0)

        @pl.when(i == 0)
        def _():
            dgw_ref[...] = jnp.zeros_like(dgw_ref)
            dgb_ref[...] = jnp.zeros_like(dgb_ref)
            vec_ref[...] = jnp.zeros_like(vec_ref)
            dscw_ref[...] = jnp.zeros_like(dscw_ref)
            ddww_ref[...] = jnp.zeros_like(ddww_ref)

        lg = lg_ref[...]
        lb = lb_ref[...]
        head = _lane_head()
        d_lg = jnp.zeros((1, BR), F32)
        d_lb = jnp.zeros((1, BR), F32)
        for c in range(TM // CHUNK):
            r0 = c * CHUNK
            uv, u, vhat, rs, vn, mixed = _gmlp_chunk_fwd(p_ref, r0, gw_ref, gb_ref, lg, lb)
            ag = p_ref[r0:r0 + CHUNK, O_AG:O_AG + BR]
            dy = dya_ref[r0:r0 + CHUNK, :]
            sa = _silu(ag)
            du = dy * mixed * sa
            dmx = dy * u * sa
            dp_ref[r0:r0 + CHUNK, O_AG:O_AG + BR] = (dy * u * mixed * _dsilu(ag)).astype(BF16)
            dgb_ref[...] += dmx
            dmx_b = dmx.astype(BF16)
            dvn = jnp.zeros((CHUNK, BR), F32)
            for h in range(N_HEADS):
                sel = head == h
                dgw_ref[h] += _dot_nt(jnp.where(sel, dmx, 0.0).astype(BF16), vn)
                dvn = dvn + jnp.where(sel, _dot(gwt_ref[h], dmx_b), 0.0)
            d_lg = d_lg + jnp.sum(dvn * vhat, axis=0, keepdims=True)
            d_lb = d_lb + jnp.sum(dvn, axis=0, keepdims=True)
            dv0 = _ln_bwd(dvn * lg, vhat, rs)
            dp_ref[r0:r0 + CHUNK, O_UV:O_UV + BR] = (du * _dgelu(uv[:, :BR])).astype(BF16)
            dp_ref[r0:r0 + CHUNK, O_UV + BR:O_UV + 2 * BR] = (dv0 * _dgelu(uv[:, BR:])).astype(BF16)
        vec_ref[0:1, :] += d_lg
        vec_ref[1:2, :] += d_lb

        dp_ref[:, O_QKV:O_QKV + 3 * BR] = dqkv_ref[...]
        dp_ref[:, O_BG:O_BG + BR] = dbg_ref[...].astype(BF16)

        first = i > 0
        last = i < nt - 1
        zbuf[0:HALO, :] = jnp.where(first, ph_ref[:, O_CIN + BR:O_CIN + 2 * BR] * ph_ref[:, O_CIN + 2 * BR:O_CIN + 3 * BR], 0.0)
        zbuf[HALO:HALO + TM, :] = p_ref[:, O_CIN + BR:O_CIN + 2 * BR] * p_ref[:, O_CIN + 2 * BR:O_CIN + 3 * BR]
        dcb[0:TM, :] = dyc_ref[...] * p_ref[:, O_CIN:O_CIN + BR] * _silu(p_ref[:, O_CG:O_CG + BR])
        dcb[TM:ext, :] = jnp.where(last, dycn_ref[...] * pn_ref[:, O_CIN:O_CIN + BR] * _silu(pn_ref[:, O_CG:O_CG + BR]), 0.0)
        for r0, nr in _conv_sub_blocks(TM):
            yc = jnp.zeros((nr, BR), F32)
            dz = jnp.zeros((nr, BR), F32)
            dcur = dcb[r0:r0 + nr, :]
            for k in range(SHORT_CONV):
                zk = zbuf[HALO - (SHORT_CONV - 1) + k + r0:HALO - (SHORT_CONV - 1) + k + r0 + nr, :]
                yc = yc + scw_ref[k:k + 1, :] * zk
                dscw_ref[8 * k:8 * k + 8, :] += _fold8(dcur * zk)
                dz = dz + scw_ref[k:k + 1, :] * dcb[(SHORT_CONV - 1) - k + r0:(SHORT_CONV - 1) - k + r0 + nr, :]
            dy = dyc_ref[r0:r0 + nr, :]
            bgate = p_ref[r0:r0 + nr, O_CIN:O_CIN + BR]
            cg = p_ref[r0:r0 + nr, O_CG:O_CG + BR]
            dp_ref[r0:r0 + nr, O_CIN:O_CIN + BR] = (dy * yc * _silu(cg)).astype(BF16)
            dp_ref[r0:r0 + nr, O_CIN + BR:O_CIN + 2 * BR] = (dz * p_ref[r0:r0 + nr, O_CIN + 2 * BR:O_CIN + 3 * BR]).astype(BF16)
            dp_ref[r0:r0 + nr, O_CIN + 2 * BR:O_CIN + 3 * BR] = (dz * p_ref[r0:r0 + nr, O_CIN + BR:O_CIN + 2 * BR]).astype(BF16)
            dp_ref[r0:r0 + nr, O_CG:O_CG + BR] = (dy * bgate * yc * _dsilu(cg)).astype(BF16)

        hbuf[0:HALO, :] = jnp.where(first, ph_ref[:, O_GLU:O_GLU + BR] * _sig(ph_ref[:, O_GLU + BR:O_GLU + 2 * BR]), 0.0)
        hbuf[HALO:HALO + TM, :] = p_ref[:, O_GLU:O_GLU + BR] * _sig(p_ref[:, O_GLU + BR:O_GLU + 2 * BR])
        hbuf[HALO + TM:HALO + ext, :] = jnp.where(last, pn_ref[:, O_GLU:O_GLU + BR] * _sig(pn_ref[:, O_GLU + BR:O_GLU + 2 * BR]), 0.0)
        clg = clg_ref[...]
        clb = clb_ref[...]
        d_clg = jnp.zeros((1, BR), F32)
        d_clb = jnp.zeros((1, BR), F32)
        d_dwb = jnp.zeros((1, BR), F32)
        for r0, nr in _conv_sub_blocks(ext):
            in_tile = r0 < TM
            cc = jnp.zeros((nr, BR), F32) + dwb_ref[...]
            for k in range(CONF_CONV):
                cc = cc + dww_ref[k:k + 1, :] * hbuf[HALO - (CONF_CONV - 1) + k + r0:HALO - (CONF_CONV - 1) + k + r0 + nr, :]
            chat, rs = _ln_hat(cc)
            ln = chat * clg + clb
            if in_tile:
                dy = dyd_ref[r0:r0 + nr, :]
                dg = p_ref[r0:r0 + nr, O_DG:O_DG + BR]
            else:
                dy = jnp.where(last, dydn_ref[...], 0.0)
                dg = pn_ref[:, O_DG:O_DG + BR]
            dln = dy * _silu(dg) * _dsilu(ln)
            dc = _ln_bwd(dln * clg, chat, rs)
            dcc[r0:r0 + nr, :] = dc
            if in_tile:
                dp_ref[r0:r0 + nr, O_DG:O_DG + BR] = (dy * _silu(ln) * _dsilu(dg)).astype(BF16)
                d_clg = d_clg + jnp.sum(dln * chat, axis=0, keepdims=True)
                d_clb = d_clb + jnp.sum(dln, axis=0, keepdims=True)
                d_dwb = d_dwb + jnp.sum(dc, axis=0, keepdims=True)
        vec_ref[2:3, :] += d_dwb
        vec_ref[3:4, :] += d_clg
        vec_ref[4:5, :] += d_clb
        for r0, nr in _conv_sub_blocks(TM):
            dcur = dcc[r0:r0 + nr, :]
            dhh = jnp.zeros((nr, BR), F32)
            for k in range(CONF_CONV):
                hk = hbuf[HALO - (CONF_CONV - 1) + k + r0:HALO - (CONF_CONV - 1) + k + r0 + nr, :]
                ddww_ref[8 * k:8 * k + 8, :] += _fold8(dcur * hk)
                dhh = dhh + dww_ref[k:k + 1, :] * dcc[(CONF_CONV - 1) - k + r0:(CONF_CONV - 1) - k + r0 + nr, :]
            a = p_ref[r0:r0 + nr, O_GLU:O_GLU + BR]
            sg = _sig(p_ref[r0:r0 + nr, O_GLU + BR:O_GLU + 2 * BR])
            dp_ref[r0:r0 + nr, O_GLU:O_GLU + BR] = (dhh * sg).astype(BF16)
            dp_ref[r0:r0 + nr, O_GLU + BR:O_GLU + 2 * BR] = (dhh * a * sg * (1.0 - sg)).astype(BF16)

    full = lambda shape: pl.BlockSpec(shape, lambda i: tuple(0 for _ in shape))
    rpt = TM // HALO
    prev_map = lambda i: (jnp.maximum(i * rpt - 1, 0), 0)
    next_map = lambda i: (jnp.minimum((i + 1) * rpt, nt * rpt - 1), 0)
    ytile = pl.BlockSpec((TM, BR), lambda i: (i, 0))
    return pl.pallas_call(
        body, grid=(nt,),
        in_specs=[pl.BlockSpec((TM, N_MIX), lambda i: (i, 0)), pl.BlockSpec((HALO, N_MIX), prev_map),
                  pl.BlockSpec((HALO, N_MIX), next_map),
                  ytile, ytile, pl.BlockSpec((HALO, BR), next_map), ytile, pl.BlockSpec((HALO, BR), next_map),
                  pl.BlockSpec((TM, 3 * BR), lambda i: (i, 0)), ytile,
                  full((N_HEADS, CHUNK, CHUNK)), full((N_HEADS, CHUNK, CHUNK)), full((CHUNK, BR)), full((1, BR)), full((1, BR)),
                  full((8, BR)), full((32, BR)), full((1, BR)), full((1, BR)), full((1, BR))],
        out_specs=[pl.BlockSpec((TM, N_MIX), lambda i: (i, 0)), full((N_HEADS, CHUNK, CHUNK)), full((CHUNK, BR)),
                   full((16, BR)), full((64, BR)), full((256, BR))],
        out_shape=[jax.ShapeDtypeStruct((s, N_MIX), BF16), jax.ShapeDtypeStruct((N_HEADS, CHUNK, CHUNK), F32),
                   jax.ShapeDtypeStruct((CHUNK, BR), F32), jax.ShapeDtypeStruct((16, BR), F32),
                   jax.ShapeDtypeStruct((64, BR), F32), jax.ShapeDtypeStruct((256, BR), F32)],
        scratch_shapes=[pltpu.VMEM((HALO + TM, BR), F32), pltpu.VMEM((ext, BR), F32),
                        pltpu.VMEM((HALO + ext, BR), F32), pltpu.VMEM((ext, BR), F32)],
        compiler_params=_cp(("arbitrary",)), name=name)(
            pm, pm, pm, dya, dyc, dyc, dyd, dyd, dqkv, dbg,
            wl["gw"], wl["gwt"], wl["gb"], wl["sgu_ln_g"], wl["sgu_ln_b"], wl["scw"], wl["dww"], wl["conf_dw_b"],
            wl["conf_ln_g"], wl["conf_ln_b"])


def _tri(lower):
    r = lax.broadcasted_iota(jnp.int32, (CHUNK, CHUNK), 0)
    c = lax.broadcasted_iota(jnp.int32, (CHUNK, CHUNK), 1)
    return jnp.where((r >= c) if lower else (r <= c), 1.0, 0.0).astype(F32)


def _dot_hi(a, b):
    return jnp.dot(a, b, preferred_element_type=F32, precision=lax.Precision.HIGHEST)


def _cum_fwd(pf, fb, name):
    s = pf.shape[0]

    def body(pf_ref, fb_ref, cum_ref, carry):
        i = pl.program_id(0)

        @pl.when(i == 0)
        def _():
            carry[...] = jnp.zeros_like(carry)

        z = pf_ref[...] + fb_ref[...]
        logf = jnp.minimum(z, 0.0) - jnp.log(1.0 + jnp.exp(-jnp.abs(z)))
        cum_ref[...] = _dot_hi(_tri(True), logf) + carry[...]
        carry[...] += jnp.sum(logf, axis=0, keepdims=True)

    return pl.pallas_call(
        body, grid=(s // CHUNK,),
        in_specs=[pl.BlockSpec((CHUNK, N_F), lambda i: (i, 0)), pl.BlockSpec((1, N_F), lambda i: (0, 0))],
        out_specs=pl.BlockSpec((CHUNK, N_F), lambda i: (i, 0)),
        out_shape=jax.ShapeDtypeStruct((s, N_F), F32),
        scratch_shapes=[pltpu.VMEM((1, N_F), F32)],
        compiler_params=_cp(("arbitrary",)), name=name)(pf, fb)


def _cum_bwd(dcq, dck, pf, fb, name):
    s = pf.shape[0]
    nb = s // CHUNK

    def body(dcq_ref, dck_ref, pf_ref, fb_ref, dpf_ref, dfb_ref, carry):
        i = pl.program_id(0)

        @pl.when(i == 0)
        def _():
            carry[...] = jnp.zeros_like(carry)
            dfb_ref[...] = jnp.zeros_like(dfb_ref)

        lane = lax.broadcasted_iota(jnp.int32, (1, N_F), 1)
        dc = dck_ref[...]
        for h in range(N_HEADS):
            dc = dc + jnp.where(lane == h, dcq_ref[h], 0.0)
        dlogf = _dot_hi(_tri(False), dc) + carry[...]
        carry[...] += jnp.sum(dc, axis=0, keepdims=True)
        z = pf_ref[...] + fb_ref[...]
        dz = dlogf * (1.0 - _sig(z))
        dpf_ref[...] = dz.astype(BF16)
        dfb_ref[...] += _fold8(dz)

    rev = lambda i: (nb - 1 - i, 0)
    return pl.pallas_call(
        body, grid=(nb,),
        in_specs=[pl.BlockSpec((N_HEADS, CHUNK, 1), lambda i: (0, nb - 1 - i, 0)), pl.BlockSpec((CHUNK, N_F), rev),
                  pl.BlockSpec((CHUNK, N_F), rev), pl.BlockSpec((1, N_F), lambda i: (0, 0))],
        out_specs=[pl.BlockSpec((CHUNK, N_F), rev), pl.BlockSpec((8, N_F), lambda i: (0, 0))],
        out_shape=[jax.ShapeDtypeStruct((s, N_F), BF16), jax.ShapeDtypeStruct((8, N_F), F32)],
        scratch_shapes=[pltpu.VMEM((1, N_F), F32)],
        compiler_params=_cp(("arbitrary",)), name=name)(dcq, dck, pf, fb)


def _causal_mask():
    r = lax.broadcasted_iota(jnp.int32, (BQ, BQ), 0)
    c = lax.broadcasted_iota(jnp.int32, (BQ, BQ), 1)
    return r >= c


def _attn_fwd(q, k, v, cq, ck, name, comm=None):
    s = q.shape[1]
    nb = s // BQ

    def body(q_ref, k_ref, v_ref, cq_ref, ck_ref, o_ref, lse_ref):
        for qi in range(nb):
            qs = qi * BQ
            qb = q_ref[0, qs:qs + BQ, :]
            cqb = cq_ref[0, qs:qs + BQ, :]

            def block(kj, carry, masked):
                m, l, acc = carry
                ks = pl.multiple_of(kj * BQ, BQ)
                kb = k_ref[0, pl.ds(ks, BQ), :]
                vb = v_ref[0, pl.ds(ks, BQ), :]
                sc = _dot_nt(qb, kb) + (cqb - ck_ref[0, kj])
                if masked:
                    sc = jnp.where(_causal_mask(), sc, NEG)
                m_new = jnp.maximum(m, jnp.max(sc, axis=-1, keepdims=True))
                alpha = jnp.exp(m - m_new)
                p = jnp.exp(sc - m_new)
                l = alpha * l + jnp.sum(p, axis=-1, keepdims=True)
                acc = alpha * acc + _dot(p.astype(BF16), vb)
                return m_new, l, acc

            carry = (jnp.full((BQ, 1), NEG, F32), jnp.zeros((BQ, 1), F32), jnp.zeros((BQ, HEAD_DIM), F32))
            if qi > 0:
                carry = lax.fori_loop(0, qi, lambda kj, cr: block(kj, cr, False), carry, unroll=2)
            m, l, acc = block(qi, carry, True)
            o_ref[0, qs:qs + BQ, :] = acc / l
            lse_ref[0, qs:qs + BQ, :] = m + jnp.log(l)

    hblk = pl.BlockSpec((1, s, HEAD_DIM), lambda h: (h, 0, 0))
    cblk = pl.BlockSpec((1, s, 1), lambda h: (h, 0, 0))
    return _pcall(
        body, grid=(N_HEADS,),
        in_specs=[hblk, hblk, hblk, cblk, pl.BlockSpec((1, nb, 1, BQ), lambda h: (h, 0, 0, 0))],
        out_specs=[hblk, cblk],
        out_shape=[jax.ShapeDtypeStruct((N_HEADS, s, HEAD_DIM), F32), jax.ShapeDtypeStruct((N_HEADS, s, 1), F32)],
        operands=(q, k, v, cq, ck), name=name, comm=comm)


def _attn_bwd(q, k, v, cq, ck, o, lse, do, name, comm=None):
    s = q.shape[1]
    nb = s // BQ

    def body(q_ref, k_ref, v_ref, cq_ref, ck_ref, o_ref, lse_ref, do_ref, dq_ref, dk_ref, dv_ref, dcq_ref, dck_ref, delta):
        delta[...] = jnp.sum(do_ref[0] * o_ref[0], axis=-1, keepdims=True)
        dq_ref[...] = jnp.zeros_like(dq_ref)
        dcq_ref[...] = jnp.zeros_like(dcq_ref)
        for kj in range(nb):
            ks = kj * BQ
            kb = k_ref[0, ks:ks + BQ, :]
            vb = v_ref[0, ks:ks + BQ, :]
            ckb = ck_ref[0, kj]

            def block(qi, carry, masked):
                dk_acc, dv_acc, dck_acc = carry
                qs = pl.multiple_of(qi * BQ, BQ)
                qb = q_ref[0, pl.ds(qs, BQ), :]
                dob = do_ref[0, pl.ds(qs, BQ), :].astype(BF16)
                sc = _dot_nt(qb, kb) + (cq_ref[0, pl.ds(qs, BQ), :] - ckb)
                p = jnp.exp(sc - lse_ref[0, pl.ds(qs, BQ), :])
                if masked:
                    p = jnp.where(_causal_mask(), p, 0.0)
                dp = _dot_nt(dob, vb)
                ds = p * (dp - delta[pl.ds(qs, BQ), :])
                ds_b = ds.astype(BF16)
                dv_acc = dv_acc + _dot_tn(p.astype(BF16), dob)
                dk_acc = dk_acc + _dot_tn(ds_b, qb)
                dq_ref[0, pl.ds(qs, BQ), :] += _dot(ds_b, kb) * SCALE
                dcq_ref[0, pl.ds(qs, BQ), :] += jnp.sum(ds, axis=-1, keepdims=True)
                dck_acc = dck_acc - jnp.sum(ds, axis=0, keepdims=True)
                return dk_acc, dv_acc, dck_acc

            carry = (jnp.zeros((BQ, HEAD_DIM), F32), jnp.zeros((BQ, HEAD_DIM), F32), jnp.zeros((1, BQ), F32))
            carry = block(kj, carry, True)
            if kj < nb - 1:
                carry = lax.fori_loop(kj + 1, nb, lambda qi, cr: block(qi, cr, False), carry, unroll=2)
            dk_ref[0, ks:ks + BQ, :] = carry[0]
            dv_ref[0, ks:ks + BQ, :] = carry[1]
            dck_ref[0, kj] = carry[2]

    hblk = pl.BlockSpec((1, s, HEAD_DIM), lambda h: (h, 0, 0))
    cblk = pl.BlockSpec((1, s, 1), lambda h: (h, 0, 0))
    kblk = pl.BlockSpec((1, nb, 1, BQ), lambda h: (h, 0, 0, 0))
    hshape = jax.ShapeDtypeStruct((N_HEADS, s, HEAD_DIM), F32)
    return _pcall(
        body, grid=(N_HEADS,),
        in_specs=[hblk, hblk, hblk, cblk, kblk, hblk, cblk, hblk],
        out_specs=[hblk, hblk, hblk, cblk, kblk],
        out_shape=[hshape, hshape, hshape, jax.ShapeDtypeStruct((N_HEADS, s, 1), F32),
                   jax.ShapeDtypeStruct((N_HEADS, nb, 1, BQ), F32)],
        scratch_shapes=[pltpu.VMEM((s, 1), F32)],
        operands=(q, k, v, cq, ck, o, lse, do), name=name, comm=comm)


def _merge_fwd(x, ya, yc, yd, o, pm, pg, wb, wo, name, comm=None):
    s = x.shape[0]

    def body(x_ref, ya_ref, yc_ref, yd_ref, o_ref, bg_ref, pg_ref, wb_ref, wo_ref, xo_ref, yb_ref):
        yb = (o_ref[...] * _silu(bg_ref[...])).astype(BF16)
        yb_ref[...] = yb
        ys = (ya_ref[...], yb, yc_ref[...], yd_ref[...])
        merged = jnp.zeros((TMG, D_MODEL), F32)
        for n in range(N_BRANCH):
            merged = merged + _sig(pg_ref[:, n * D_MODEL:(n + 1) * D_MODEL]) * _dot(ys[n], wb_ref[n])
        xo_ref[...] = x_ref[...] + _dot(merged.astype(BF16), wo_ref[...])

    xt = pl.BlockSpec((TMG, D_MODEL), lambda i: (i, 0))
    yt = pl.BlockSpec((TMG, BR), lambda i: (i, 0))
    return _pcall(
        body, grid=(s // TMG,),
        in_specs=[xt, yt, yt, yt, yt, pl.BlockSpec((TMG, BR), lambda i: (i, O_BG // BR)),
                  pl.BlockSpec((TMG, N_MERGE), lambda i: (i, 0)),
                  pl.BlockSpec((N_BRANCH, BR, D_MODEL), lambda i: (0, 0, 0)), pl.BlockSpec((D_MODEL, D_MODEL), lambda i: (0, 0))],
        out_specs=[xt, yt],
        out_shape=[jax.ShapeDtypeStruct((s, D_MODEL), F32), jax.ShapeDtypeStruct((s, BR), BF16)],
        operands=(x, ya, yc, yd, o, pm, pg, wb, wo), name=name, comm=comm)


def _merge_bwd(dx, ya, yb, yc, yd, o, pm, pg, wb, wo, name, comm=None):
    s = dx.shape[0]
    nt = s // TMG

    def body(dx_ref, ya_ref, yb_ref, yc_ref, yd_ref, o_ref, bg_ref, pg_ref, wb_ref, wo_ref,
             dpg_ref, dya_ref, do_ref, dbg_ref, dyc_ref, dyd_ref, dwb_ref, dwo_ref, dwb_acc, dwo_acc):
        i = pl.program_id(0)

        @pl.when(i == 0)
        def _():
            dwb_acc[...] = jnp.zeros_like(dwb_acc)
            dwo_acc[...] = jnp.zeros_like(dwo_acc)

        dxb = dx_ref[...].astype(BF16)
        dmerged = _dot_nt(dxb, wo_ref[...])
        ys = (ya_ref[...], yb_ref[...], yc_ref[...], yd_ref[...])
        dys = (dya_ref, None, dyc_ref, dyd_ref)
        merged = jnp.zeros((TMG, D_MODEL), F32)
        for n in range(N_BRANCH):
            gate = _sig(pg_ref[:, n * D_MODEL:(n + 1) * D_MODEL])
            pr = _dot(ys[n], wb_ref[n])
            merged = merged + gate * pr
            dpg_ref[:, n * D_MODEL:(n + 1) * D_MODEL] = (dmerged * pr * gate * (1.0 - gate)).astype(BF16)
            dpr = (gate * dmerged).astype(BF16)
            dwb_acc[n] += _dot_tn(ys[n], dpr)
            dyn = _dot_nt(dpr, wb_ref[n])
            if n == 1:
                bg = bg_ref[...]
                do_ref[...] = dyn * _silu(bg)
                dbg_ref[...] = dyn * o_ref[...] * _dsilu(bg)
            else:
                dys[n][...] = dyn
        dwo_acc[...] += _dot_tn(merged.astype(BF16), dxb)

        @pl.when(i == nt - 1)
        def _():
            dwb_ref[...] = dwb_acc[...].astype(BF16)
            dwo_ref[...] = dwo_acc[...].astype(BF16)

    xt = pl.BlockSpec((TMG, D_MODEL), lambda i: (i, 0))
    yt = pl.BlockSpec((TMG, BR), lambda i: (i, 0))
    gt = pl.BlockSpec((TMG, N_MERGE), lambda i: (i, 0))
    wbs = pl.BlockSpec((N_BRANCH, BR, D_MODEL), lambda i: (0, 0, 0))
    wos = pl.BlockSpec((D_MODEL, D_MODEL), lambda i: (0, 0))
    yf = jax.ShapeDtypeStruct((s, BR), F32)
    return _pcall(
        body, grid=(nt,),
        in_specs=[xt, yt, yt, yt, yt, yt, pl.BlockSpec((TMG, BR), lambda i: (i, O_BG // BR)), gt, wbs, wos],
        out_specs=[gt, yt, yt, yt, yt, yt, wbs, wos],
        out_shape=[jax.ShapeDtypeStruct((s, N_MERGE), BF16), yf, yf, yf, yf, yf,
                   jax.ShapeDtypeStruct((N_BRANCH, BR, D_MODEL), BF16), jax.ShapeDtypeStruct((D_MODEL, D_MODEL), BF16)],
        scratch_shapes=[pltpu.VMEM((N_BRANCH, BR, D_MODEL), F32), pltpu.VMEM((D_MODEL, D_MODEL), F32)],
        operands=(dx, ya, yb, yc, yd, o, pm, pg, wb, wo), name=name, comm=comm)


def _heads(a):
    s = a.shape[0]
    return a.reshape(s, N_HEADS, HEAD_DIM).transpose(1, 0, 2)


def _unheads(a):
    s = a.shape[1]
    return a.transpose(1, 0, 2).reshape(s, N_HEADS * HEAD_DIM)


def _layer_fwd(x, wl, tag, attach=None):
    attach = attach or {}

    def riding(stage):
        comm, sink = attach.get(stage, (None, None))
        return comm, (sink or (lambda res: None))

    s = x.shape[0]
    h = _rms_fwd(x, wl["norm_g"], "rms_fwd" + tag)
    comm, sink = riding("proj_mix")
    pm, res = _mm_nn(h, wl["w_mix"], 1792, "proj_mix" + tag, comm)
    sink(res)
    pg, _ = _mm_nn(h, wl["w_merge"], 2048, "proj_merge" + tag)
    pf, _ = _mm_nn(h, wl["w_f"], N_F, "proj_f" + tag)
    comm, sink = riding("mix_fwd")
    (ya, yc, yd, qkv), res = _mix_fwd(pm, wl, "mix_fwd" + tag, comm)
    sink(res)
    cum = _cum_fwd(pf, wl["f_bias"], "cum_fwd" + tag)
    cum_t = cum[:, :N_HEADS].T
    cq = cum_t.reshape(N_HEADS, s, 1)
    ck = cum_t.reshape(N_HEADS, s // BQ, 1, BQ)
    q = _heads(qkv[:, :BR] * SCALE)
    k, v = _heads(qkv[:, BR:2 * BR]), _heads(qkv[:, 2 * BR:])
    comm, sink = riding("attn_fwd")
    (o_h, lse), res = _attn_fwd(q, k, v, cq, ck, "attn_fwd" + tag, comm)
    sink(res)
    o = _unheads(o_h)
    comm, sink = riding("merge_fwd")
    (x_next, yb), res = _merge_fwd(x, ya, yc, yd, o, pm, pg, wl["wb"], wl["wo"], "merge_fwd" + tag, comm)
    sink(res)
    saved = dict(x=x, h=h, pm=pm, pg=pg, pf=pf, ya=ya, yb=yb, yc=yc, yd=yd, q=q, k=k, v=v, cq=cq, ck=ck,
                 o_h=o_h, o=o, lse=lse)
    return x_next, saved


def _blocks_rows(g):
    return g.reshape(N_DEV, g.shape[0] // N_DEV, g.shape[1])


def _blocks_cols(g):
    return g.reshape(N_BRANCH * BR, N_DEV, D_MODEL // N_DEV).transpose(1, 0, 2)


def _layer_bwd(dx_next, sv, wl, tag, dist, riding, extra_small):
    s = dx_next.shape[0]
    (dpg, dya, do, dbg, dyc, dyd, dwb, dwo), rode = _merge_bwd(
        dx_next, sv["ya"], sv["yb"], sv["yc"], sv["yd"], sv["o"], sv["pm"], sv["pg"], wl["wb"], wl["wo"], "merge_bwd" + tag,
        riding)
    dw_merge = _mm_tn(sv["h"], dpg, 2048, "dw_merge" + tag)
    early = [(_blocks_rows(dw_merge), False), (_blocks_cols(dwb), False), (_blocks_rows(dwo), False)] if dist else None
    (dq, dk, dv, dcq, dck), early_out = _attn_bwd(sv["q"], sv["k"], sv["v"], sv["cq"], sv["ck"], sv["o_h"], sv["lse"],
                                                  _heads(do), "attn_bwd" + tag, early)
    dqkv = jnp.concatenate([_unheads(dq), _unheads(dk), _unheads(dv)], axis=1).astype(BF16)
    dck_cols = jnp.pad(dck.reshape(N_HEADS, s).T, ((0, 0), (0, N_F - N_HEADS)))
    dpf, dfb = _cum_bwd(dcq, dck_cols, sv["pf"], wl["f_bias"], "cum_bwd" + tag)
    dpm, dgw, dgb, vec, dscw, ddww = _mix_bwd(sv["pm"], dya, dyc, dyd, dqkv, dbg, wl, "mix_bwd" + tag)
    dw_mix = _mm_tn(sv["h"], dpm, 1792, "dw_mix" + tag)
    dw_f = _mm_tn(sv["h"], dpf, N_F, "dw_f" + tag)
    causal = jnp.tril(jnp.ones((CHUNK, CHUNK), bool))
    small = dict(
        f_bias=dfb.sum(0)[:N_HEADS],
        sgu_w=jnp.where(causal[None], dgw, 0.0),
        sgu_b=dgb.reshape(CHUNK, N_HEADS, HEAD_DIM).sum(-1).T,
        sgu_ln_g=vec[0], sgu_ln_b=vec[1], conf_dw_b=vec[2], conf_ln_g=vec[3], conf_ln_b=vec[4],
        short_conv_w=dscw.reshape(8, 8, BR).sum(1)[:SHORT_CONV],
        conf_dw_w=ddww.reshape(32, 8, BR).sum(1)[:CONF_CONV],
    )
    slab, spans = _pack([small[nm] for nm in SMALL[1:]])
    late = [(_blocks_rows(dw_mix), False), (_blocks_rows(dw_f), False), (slab, True)] if dist else None
    dh, late_out = _dh(dpm, dpg, dpf, wl["w_mix"], wl["w_merge"], wl["w_f"], "dh" + tag, late)
    dx, dng = _rms_bwd(dh, sv["x"], wl["norm_g"], dx_next, "rms_bwd" + tag)
    small["norm_g"] = dng.sum(0)
    grads = dict(small, w_mix=dw_mix, w_merge=dw_merge, w_f=dw_f, wb=dwb, wo=dwo)
    last_slab, last_spans = _pack([small["norm_g"]] + list(extra_small))
    return dx, grads, early_out, (late_out, spans), ([(last_slab, True)], last_spans), rode


def _prep_layer_small(norm_g, f_bias, sgu_w, sgu_b, sgu_ln_g, sgu_ln_b, scw, dww, conf_dw_b, conf_ln_g, conf_ln_b):
    causal = jnp.tril(jnp.ones((CHUNK, CHUNK), bool))
    gw = jnp.where(causal[None], sgu_w, 0.0)
    row = lambda a: a.reshape(1, -1)
    return dict(
        norm_g=row(norm_g),
        f_bias=jnp.pad(row(f_bias), ((0, 0), (0, N_F - N_HEADS))),
        gw=gw.astype(BF16), gwt=gw.transpose(0, 2, 1).astype(BF16),
        gb=jnp.repeat(sgu_b.T, HEAD_DIM, axis=1),
        sgu_ln_g=row(sgu_ln_g), sgu_ln_b=row(sgu_ln_b),
        scw=jnp.pad(scw, ((0, 8 - SHORT_CONV), (0, 0))), dww=jnp.pad(dww, ((0, 32 - CONF_CONV), (0, 0))),
        conf_dw_b=row(conf_dw_b), conf_ln_g=row(conf_ln_g), conf_ln_b=row(conf_ln_b))


def _local_step(x, target, layers, final_g):
    saved = []
    for l in range(DEPTH):
        x, sv = _layer_fwd(x, layers[l], str(l))
        saved.append(sv)
    loss_p, dx, dfg = _loss_head(x, final_g.reshape(1, D_MODEL), target)
    grads = [None] * DEPTH
    for l in reversed(range(DEPTH)):
        dx, grads[l], _, _, _, _ = _layer_bwd(dx, saved[l], layers[l], str(l), False, None, [])
    return 0.5 / D_MODEL * jnp.sum(loss_p), dx, grads, dfg.sum(0)


def _sum8(a, name):
    _, r, c = a.shape
    tr = r
    while tr * c * a.dtype.itemsize * N_DEV > 4 * 1024 * 1024 and tr % 32 == 0:
        tr //= 2

    def body(a_ref, o_ref):
        acc = a_ref[0].astype(F32)
        for d in range(1, N_DEV):
            acc = acc + a_ref[d].astype(F32)
        o_ref[...] = acc

    return pl.pallas_call(
        body, grid=(r // tr,),
        in_specs=[pl.BlockSpec((N_DEV, tr, c), lambda i: (0, i, 0))],
        out_specs=pl.BlockSpec((tr, c), lambda i: (i, 0)),
        out_shape=jax.ShapeDtypeStruct((r, c), F32), compiler_params=_cp(("parallel",)), name=name)(a)


def _adamw(w, g, m, v, name):
    l, r, c = w.shape
    tr = r
    while tr * c * 4 > 1024 * 1024 and tr % 16 == 0:
        tr //= 2
    c1 = 1.0 - ADAM_B1 ** ADAM_STEP
    c2 = 1.0 - ADAM_B2 ** ADAM_STEP

    def body(w_ref, g_ref, m_ref, v_ref, d_ref, mo_ref, vo_ref):
        gv = g_ref[...]
        mn = ADAM_B1 * m_ref[...] + (1.0 - ADAM_B1) * gv
        vn = ADAM_B2 * v_ref[...] + (1.0 - ADAM_B2) * (gv * gv)
        mo_ref[...] = mn
        vo_ref[...] = vn
        d_ref[...] = -ADAM_LR * ((mn / c1) / (jnp.sqrt(vn / c2) + ADAM_EPS) + ADAM_WD * w_ref[...])

    blk = pl.BlockSpec((1, tr, c), lambda a, i: (a, i, 0))
    shp = jax.ShapeDtypeStruct((l, r, c), F32)
    return pl.pallas_call(
        body, grid=(l, r // tr), in_specs=[blk] * 4, out_specs=[blk] * 3, out_shape=[shp] * 3,
        compiler_params=_cp(("parallel", "parallel")), name=name)(w, g, m, v)


def _pack(parts):
    rows, spans, r = [], [], 0
    for p in parts:
        flat = p.reshape(-1)
        nr = -(-flat.shape[0] // 1024) * 8
        rows.append(jnp.pad(flat, (0, nr * 128 - flat.shape[0])).reshape(nr, 128))
        spans.append((r, nr, p.shape))
        r += nr
    return jnp.concatenate(rows, axis=0), spans


def _unpack(slab, spans):
    out = []
    for r, nr, shape in spans:
        size = math.prod(shape)
        out.append(slab[r:r + nr].reshape(-1)[:size].reshape(shape))
    return out


def _split_w_in(w):
    mix = jnp.concatenate([w[..., 0:1536], w[..., 1540:1796], w[..., 3332:3588], w[..., 1796:2820], w[..., 2820:3332]], axis=-1)
    return mix, w[..., 3588:7684], w[..., 1536:1540]


def _join_w_in(mix, merge, f):
    return jnp.concatenate([mix[..., 0:1536], f, mix[..., 1536:1792], mix[..., 2048:3072], mix[..., 3072:3584],
                            mix[..., 1792:2048], merge], axis=-1)


SMALL = ("norm_g", "f_bias", "sgu_w", "sgu_b", "sgu_ln_g", "sgu_ln_b", "short_conv_w", "conf_dw_w", "conf_dw_b",
         "conf_ln_g", "conf_ln_b")


def kernel(x, norm_g, w_in, f_bias, sgu_w, sgu_b, sgu_ln_g, sgu_ln_b, short_conv_w, conf_dw_w, conf_dw_b, conf_ln_g, conf_ln_b, w_branch, w_out, final_g, loss_target, m_norm_g, m_w_in, m_f_bias, m_sgu_w, m_sgu_b, m_sgu_ln_g, m_sgu_ln_b, m_short_conv_w, m_conf_dw_w, m_conf_dw_b, m_conf_ln_g, m_conf_ln_b, m_w_branch, m_w_out, m_final_g, v_norm_g, v_w_in, v_f_bias, v_sgu_w, v_sgu_b, v_sgu_ln_g, v_sgu_ln_b, v_short_conv_w, v_conf_dw_w, v_conf_dw_b, v_conf_ln_g, v_conf_ln_b, v_w_branch, v_w_out, v_final_g):
    me = 4 * lax.axis_index("x") + 2 * lax.axis_index("y") + lax.axis_index("c")
    rows = D_MODEL // N_DEV
    cshard = BR // N_DEV

    sh = []
    for l in range(DEPTH):
        mix, merge, f = _split_w_in(w_in[l])
        sh.append(dict(mix=mix.astype(BF16), merge=merge.astype(BF16),
                       f=jnp.pad(f, ((0, 0), (0, N_F - N_HEADS))).astype(BF16),
                       wb=w_branch[l].astype(BF16), wo=w_out[l].astype(BF16)))
    conv_slab, conv_spans = _pack([short_conv_w, conf_dw_w])
    g_mix, g_f, g_conv = _exchange([(sh[0]["mix"], True), (sh[0]["f"], True), (conv_slab, True)], "gather_first")
    conv_full = [_unpack(g_conv[d], conv_spans) for d in range(N_DEV)]
    scw_full = jnp.concatenate([cf[0] for cf in conv_full], axis=-1)
    dww_full = jnp.concatenate([cf[1] for cf in conv_full], axis=-1)
    layers = [_prep_layer_small(norm_g[l], f_bias[l], sgu_w[l], sgu_b[l], sgu_ln_g[l], sgu_ln_b[l], scw_full[l], dww_full[l],
                                conf_dw_b[l], conf_ln_g[l], conf_ln_b[l]) for l in range(DEPTH)]
    layers[0].update(w_mix=g_mix.reshape(D_MODEL, N_MIX), w_f=g_f.reshape(D_MODEL, N_F))

    def put_in(l):
        def sink(res):
            layers[l].update(w_mix=res[0].reshape(D_MODEL, N_MIX), w_f=res[1].reshape(D_MODEL, N_F),
                             w_merge=res[2].reshape(D_MODEL, N_MERGE))
        return sink

    def put_merge(l):
        def sink(res):
            layers[l].update(w_merge=res[0].reshape(D_MODEL, N_MERGE))
        return sink

    def put_out(l):
        def sink(res):
            layers[l].update(wb=res[0].transpose(1, 2, 0, 3).reshape(N_BRANCH, BR, D_MODEL), wo=res[1].reshape(D_MODEL, D_MODEL))
        return sink

    attach0 = {
        "proj_mix": ([(sh[0]["merge"], True)], put_merge(0)),
        "mix_fwd": ([(sh[0]["wb"], True), (sh[0]["wo"], True)], put_out(0)),
        "attn_fwd": ([(sh[1]["mix"], True), (sh[1]["f"], True), (sh[1]["merge"], True)], put_in(1)),
        "merge_fwd": ([(sh[1]["wb"], True), (sh[1]["wo"], True)], put_out(1)),
    }

    xs = x[0]
    xs, sv0 = _layer_fwd(xs, layers[0], "0", attach0)
    xs, sv1 = _layer_fwd(xs, layers[1], "1")
    loss_p, dx, dfg = _loss_head(xs, final_g.reshape(1, D_MODEL), loss_target[0])
    loss = lax.psum(0.5 / D_MODEL * jnp.sum(loss_p), ("x", "y", "c"))
    dx, g1, early1, (late1, spans1), (last1, lspans1), _ = _layer_bwd(dx, sv1, layers[1], "1", True, None, [dfg.sum(0)])
    dx, g0, early0, (late0, spans0), (last0, lspans0), last1_out = _layer_bwd(dx, sv0, layers[0], "0", True, last1, [])
    last0_out = _exchange(last0, "gather_last")

    red, small = [], []
    for l, (early, late, spans, last, lspans) in enumerate(((early0, late0, spans0, last0_out, lspans0),
                                                            (early1, late1, spans1, last1_out, lspans1))):
        t = str(l)
        red.append(dict(merge=_sum8(early[0], "sum_merge" + t), wb=_sum8(early[1], "sum_wb" + t), wo=_sum8(early[2], "sum_wo" + t),
                        mix=_sum8(late[0], "sum_mix" + t), f=_sum8(late[1], "sum_f" + t)))
        keys = ["norm_g"] + (["final_g"] if l == DEPTH - 1 else []) + list(SMALL[1:])
        small.append(dict(zip(keys, _unpack(_sum8(last[0], "sum_last" + t), lspans)
                              + _unpack(_sum8(late[2], "sum_small" + t), spans))))
    gs = {nm: jnp.stack([small[l][nm] for l in range(DEPTH)]) for nm in SMALL}
    gs["final_g"] = small[DEPTH - 1]["final_g"]
    gs["short_conv_w"] = lax.dynamic_slice_in_dim(gs["short_conv_w"], me * cshard, cshard, axis=2)
    gs["conf_dw_w"] = lax.dynamic_slice_in_dim(gs["conf_dw_w"], me * cshard, cshard, axis=2)
    g_w_in = jnp.stack([_join_w_in(red[l]["mix"], red[l]["merge"], red[l]["f"][:, :N_HEADS]) for l in range(DEPTH)])
    g_w_branch = jnp.stack([red[l]["wb"].reshape(N_BRANCH, BR, rows) for l in range(DEPTH)])
    g_w_out = jnp.stack([red[l]["wo"] for l in range(DEPTH)])

    d_w_in, nm_w_in, nv_w_in = _adamw(w_in, g_w_in, m_w_in, v_w_in, "adamw_w_in")
    flat = lambda a: a.reshape(DEPTH, N_BRANCH * BR, rows)
    d_w_branch, nm_w_branch, nv_w_branch = (a.reshape(w_branch.shape) for a in _adamw(
        flat(w_branch), flat(g_w_branch), flat(m_w_branch), flat(v_w_branch), "adamw_w_branch"))
    d_w_out, nm_w_out, nv_w_out = _adamw(w_out, g_w_out, m_w_out, v_w_out, "adamw_w_out")
    names = SMALL + ("final_g",)
    ws = dict(zip(names, (norm_g, f_bias, sgu_w, sgu_b, sgu_ln_g, sgu_ln_b, short_conv_w, conf_dw_w, conf_dw_b, conf_ln_g,
                          conf_ln_b, final_g)))
    ms = dict(zip(names, (m_norm_g, m_f_bias, m_sgu_w, m_sgu_b, m_sgu_ln_g, m_sgu_ln_b, m_short_conv_w, m_conf_dw_w,
                          m_conf_dw_b, m_conf_ln_g, m_conf_ln_b, m_final_g)))
    vs = dict(zip(names, (v_norm_g, v_f_bias, v_sgu_w, v_sgu_b, v_sgu_ln_g, v_sgu_ln_b, v_short_conv_w, v_conf_dw_w,
                          v_conf_dw_b, v_conf_ln_g, v_conf_ln_b, v_final_g)))
    w_slab, spans = _pack([ws[nm] for nm in names])
    g_slab, _ = _pack([gs[nm] for nm in names])
    m_slab, _ = _pack([ms[nm] for nm in names])
    v_slab, _ = _pack([vs[nm] for nm in names])
    d_s, nm_s, nv_s = (dict(zip(names, _unpack(a[0], spans))) for a in _adamw(w_slab[None], g_slab[None], m_slab[None],
                                                                              v_slab[None], "adamw_small"))

    def ordered(small, w_in_v, w_branch_v, w_out_v):
        return [small["norm_g"], w_in_v, small["f_bias"], small["sgu_w"], small["sgu_b"], small["sgu_ln_g"],
                small["sgu_ln_b"], small["short_conv_w"], small["conf_dw_w"], small["conf_dw_b"], small["conf_ln_g"],
                small["conf_ln_b"], w_branch_v, w_out_v, small["final_g"]]

    return (loss, dx[None], *ordered(gs, g_w_in, g_w_branch, g_w_out), *ordered(d_s, d_w_in, d_w_branch, d_w_out),
            *ordered(nm_s, nm_w_in, nm_w_branch, nm_w_out), *ordered(nv_s, nv_w_in, nv_w_branch, nv_w_out))
```

```python
import functools
import math

import jax
import jax.numpy as jnp
from jax import lax
from jax.experimental import pallas as pl
from jax.experimental.pallas import tpu as pltpu

F32 = jnp.float32
BF16 = jnp.bfloat16

D_MODEL = 1024
DEPTH = 2
N_BRANCH = 4
BR = 256
N_HEADS = 4
HEAD_DIM = 64
CHUNK = 128
SHORT_CONV = 3
CONF_CONV = 31
EPS = 1e-6
N_DEV = 8

ADAM_LR = 0.001
ADAM_B1 = 0.9
ADAM_B2 = 0.999
ADAM_EPS = 1e-08
ADAM_WD = 0.01
ADAM_STEP = 10

O_UV, O_AG, O_QKV, O_BG, O_DG, O_CIN, O_CG, O_GLU = 0, 512, 768, 1536, 1792, 2048, 2816, 3072
N_MIX = 3584
N_MERGE = N_BRANCH * D_MODEL
N_F = 128
IN_COLS = 7684
HALO = 32
TM = 512
TMG = 256
BQ = 512
SUB = 64
CUMB = 512
VMEM_LIMIT = 56 * 1024 * 1024
NEG = -1e30
SCALE = 1.0 / math.sqrt(HEAD_DIM)
GELU_K = math.sqrt(2.0 / math.pi)


def _cp(sem=None):
    return pltpu.CompilerParams(dimension_semantics=sem, vmem_limit_bytes=VMEM_LIMIT)


PEER_ORDER = (6, 4, 2, 7, 5, 3, 1)
RELAYED = (3, 5, 7)


def _xchg(cin, cout, send, recv, loc, modes, start):
    x, y, c = lax.axis_index("x"), lax.axis_index("y"), lax.axis_index("c")
    me = 4 * x + 2 * y + c

    def peer_of(kk):
        px, py, pc = lax.rem(x + (kk >> 2 & 1), 2), lax.rem(y + (kk >> 1 & 1), 2), lax.rem(c + (kk & 1), 2)
        return (px, py, pc), 4 * px + 2 * py + pc

    def remote(src, dst, a, kk, pid):
        return pltpu.make_async_remote_copy(src_ref=src, dst_ref=dst, send_sem=send.at[a, kk], recv_sem=recv.at[a, kk],
                                            device_id=pid, device_id_type=pl.DeviceIdType.MESH)

    def outgoing(a, kk):
        if modes[a] and kk in RELAYED:
            _, origin = peer_of(kk - 1)
            return remote(cout[a].at[origin], cout[a].at[origin], a, kk, peer_of(1)[0])
        pid, peer = peer_of(kk)
        return remote(cin[a] if modes[a] else cin[a].at[peer], cout[a].at[me], a, kk, pid)

    def arrival(a, kk):
        _, peer = peer_of(kk)
        return remote(cout[a].at[peer], cout[a].at[peer], a, kk, (x, y, c))

    for a, gather in enumerate(modes):
        cp = pltpu.make_async_copy(cin[a] if gather else cin[a].at[me], cout[a].at[me], loc.at[a])
        if start:
            cp.start()
        else:
            cp.wait()
    if start:
        for kk in PEER_ORDER:
            for a, gather in enumerate(modes):
                if not (gather and kk in RELAYED):
                    outgoing(a, kk).start()
        return
    for kk in RELAYED:
        for a, gather in enumerate(modes):
            if gather:
                arrival(a, kk - 1).wait_recv()
                outgoing(a, kk).start()
    for kk in PEER_ORDER:
        for a in range(len(modes)):
            outgoing(a, kk).wait_send()
    for kk in PEER_ORDER:
        for a, gather in enumerate(modes):
            if not (gather and kk + 1 in RELAYED):
                arrival(a, kk).wait_recv()


def _xchg_shapes(comm):
    return [jax.ShapeDtypeStruct((N_DEV,) + tuple(a.shape[(0 if gather else 1):]), a.dtype) for a, gather in comm]


def _xchg_sems(n):
    return [pltpu.SemaphoreType.DMA((n, N_DEV)), pltpu.SemaphoreType.DMA((n, N_DEV)), pltpu.SemaphoreType.DMA((n,))]


def _exchange(comm, name):
    n = len(comm)
    modes = [g for _, g in comm]

    def body(*refs):
        cin, cout, (send, recv, loc) = refs[:n], refs[n:2 * n], refs[2 * n:]
        _xchg(cin, cout, send, recv, loc, modes, True)
        _xchg(cin, cout, send, recv, loc, modes, False)

    anyspec = pl.BlockSpec(memory_space=pl.ANY)
    return pl.pallas_call(
        body, in_specs=[anyspec] * n, out_specs=[anyspec] * n, out_shape=_xchg_shapes(comm),
        scratch_shapes=_xchg_sems(n), name=name)(*[a for a, _ in comm])


def _pcall(body, *, grid, in_specs, out_specs, out_shape, operands, name, scratch_shapes=(), comm=None):
    if not comm:
        outs = pl.pallas_call(
            body, grid=grid, in_specs=in_specs, out_specs=out_specs, out_shape=out_shape, scratch_shapes=list(scratch_shapes),
            compiler_params=_cp(("arbitrary",) * len(grid)), name=name)(*operands)
        return list(outs), []
    n, nin, nout, nsc = len(comm), len(operands), len(out_shape), len(scratch_shapes)
    modes = [g for _, g in comm]

    def wrapped(*refs):
        ins, cin = refs[:nin], refs[nin:nin + n]
        outs, cout = refs[nin + n:nin + n + nout], refs[nin + n + nout:nin + 2 * n + nout]
        scratch = refs[nin + 2 * n + nout:]
        own, (send, recv, loc) = scratch[:nsc], scratch[nsc:]
        ids = [pl.program_id(d) for d in range(len(grid))]
        first = functools.reduce(jnp.logical_and, [i == 0 for i in ids])
        last = functools.reduce(jnp.logical_and, [i == g - 1 for i, g in zip(ids, grid)])

        @pl.when(first)
        def _():
            _xchg(cin, cout, send, recv, loc, modes, True)

        body(*ins, *outs, *own)

        @pl.when(last)
        def _():
            _xchg(cin, cout, send, recv, loc, modes, False)

    anyspec = pl.BlockSpec(memory_space=pl.ANY)
    res = pl.pallas_call(
        wrapped, grid=grid, in_specs=list(in_specs) + [anyspec] * n, out_specs=list(out_specs) + [anyspec] * n,
        out_shape=list(out_shape) + _xchg_shapes(comm), scratch_shapes=list(scratch_shapes) + _xchg_sems(n),
        compiler_params=_cp(("arbitrary",) * len(grid)), name=name)(*operands, *[a for a, _ in comm])
    return list(res[:nout]), list(res[nout:])


def _sig(x):
    return 1.0 / (1.0 + jnp.exp(-x))


def _silu(x):
    return x * _sig(x)


def _dsilu(x):
    s = _sig(x)
    return s * (1.0 + x * (1.0 - s))


def _gelu(x):
    return 0.5 * x * (1.0 + jnp.tanh(GELU_K * (x + 0.044715 * x * x * x)))


def _dgelu(x):
    t = jnp.tanh(GELU_K * (x + 0.044715 * x * x * x))
    return 0.5 * (1.0 + t) + 0.5 * x * (1.0 - t * t) * GELU_K * (1.0 + 3.0 * 0.044715 * x * x)


def _ln_hat(x):
    mu = jnp.mean(x, axis=-1, keepdims=True)
    xc = x - mu
    rs = lax.rsqrt(jnp.mean(xc * xc, axis=-1, keepdims=True) + EPS)
    return xc * rs, rs


def _ln_bwd(dhat, hat, rs):
    return rs * (dhat - jnp.mean(dhat, axis=-1, keepdims=True) - hat * jnp.mean(dhat * hat, axis=-1, keepdims=True))


def _dot(a, b):
    return jnp.dot(a, b, preferred_element_type=F32)


def _dot_nt(a, b):
    return lax.dot_general(a, b, (((1,), (1,)), ((), ())), preferred_element_type=F32)


def _dot_tn(a, b):
    return lax.dot_general(a, b, (((0,), (0,)), ((), ())), preferred_element_type=F32)


def _fold8(x):
    acc = x[0:8]
    for r in range(1, x.shape[0] // 8):
        acc = acc + x[8 * r:8 * r + 8]
    return acc


def _rms_fwd(x, g, name):
    s = x.shape[0]

    def body(x_ref, g_ref, h_ref):
        xv = x_ref[...]
        r = lax.rsqrt(jnp.mean(xv * xv, axis=-1, keepdims=True) + EPS)
        h_ref[...] = (xv * r * g_ref[...]).astype(BF16)

    return pl.pallas_call(
        body, grid=(s // TM,),
        in_specs=[pl.BlockSpec((TM, D_MODEL), lambda i: (i, 0)), pl.BlockSpec((1, D_MODEL), lambda i: (0, 0))],
        out_specs=pl.BlockSpec((TM, D_MODEL), lambda i: (i, 0)),
        out_shape=jax.ShapeDtypeStruct((s, D_MODEL), BF16), compiler_params=_cp(("parallel",)), name=name)(x, g)


def _rms_bwd(dh, x, g, dx_next, name):
    s = x.shape[0]

    def body(dh_ref, x_ref, g_ref, dxn_ref, dx_ref, dg_ref):
        i = pl.program_id(0)
        xv = x_ref[...]
        r = lax.rsqrt(jnp.mean(xv * xv, axis=-1, keepdims=True) + EPS)
        xn = xv * r
        dhv = dh_ref[...]
        dxn = dhv * g_ref[...]
        dx_ref[...] = dxn_ref[...] + r * (dxn - xn * jnp.mean(dxn * xn, axis=-1, keepdims=True))

        @pl.when(i == 0)
        def _():
            dg_ref[...] = jnp.zeros_like(dg_ref)

        dg_ref[...] += _fold8(dhv * xn)

    tile = pl.BlockSpec((TM, D_MODEL), lambda i: (i, 0))
    return pl.pallas_call(
        body, grid=(s // TM,),
        in_specs=[tile, tile, pl.BlockSpec((1, D_MODEL), lambda i: (0, 0)), tile],
        out_specs=[tile, pl.BlockSpec((8, D_MODEL), lambda i: (0, 0))],
        out_shape=[jax.ShapeDtypeStruct((s, D_MODEL), F32), jax.ShapeDtypeStruct((8, D_MODEL), F32)],
        compiler_params=_cp(("arbitrary",)), name=name)(dh, x, g, dx_next)


def _loss_head(x, g, target):
    s = x.shape[0]

    def body(x_ref, g_ref, t_ref, loss_ref, dx_ref, dg_ref):
        i = pl.program_id(0)
        xv = x_ref[...]
        r = lax.rsqrt(jnp.mean(xv * xv, axis=-1, keepdims=True) + EPS)
        xn = xv * r
        err = xn * g_ref[...] - t_ref[...]
        dy = err * (1.0 / D_MODEL)
        dxn = dy * g_ref[...]
        dx_ref[...] = r * (dxn - xn * jnp.mean(dxn * xn, axis=-1, keepdims=True))

        @pl.when(i == 0)
        def _():
            dg_ref[...] = jnp.zeros_like(dg_ref)
            loss_ref[...] = jnp.zeros_like(loss_ref)

        dg_ref[...] += _fold8(dy * xn)
        loss_ref[...] += _fold8(err * err)

    tile = pl.BlockSpec((TM, D_MODEL), lambda i: (i, 0))
    acc = pl.BlockSpec((8, D_MODEL), lambda i: (0, 0))
    return pl.pallas_call(
        body, grid=(s // TM,),
        in_specs=[tile, pl.BlockSpec((1, D_MODEL), lambda i: (0, 0)), tile],
        out_specs=[acc, tile, acc],
        out_shape=[jax.ShapeDtypeStruct((8, D_MODEL), F32), jax.ShapeDtypeStruct((s, D_MODEL), F32),
                   jax.ShapeDtypeStruct((8, D_MODEL), F32)],
        compiler_params=_cp(("arbitrary",)), name="loss_head")(x, g, target)


def _mm_nn(a, b, tn, name, comm=None):
    m, k = a.shape
    n = b.shape[1]
    tm = 512

    def body(a_ref, b_ref, o_ref):
        o_ref[...] = _dot(a_ref[...], b_ref[...])

    (out,), couts = _pcall(
        body, grid=(n // tn, m // tm),
        in_specs=[pl.BlockSpec((tm, k), lambda j, i: (i, 0)), pl.BlockSpec((k, tn), lambda j, i: (0, j))],
        out_specs=[pl.BlockSpec((tm, tn), lambda j, i: (i, j))],
        out_shape=[jax.ShapeDtypeStruct((m, n), F32)], operands=(a, b), name=name, comm=comm)
    return out, couts


def _dh(dpm, dpg, dpf, w_mix, w_merge, w_f, name, comm=None):
    s = dpm.shape[0]
    tm = 1024 if s % 1024 == 0 else 512
    tk = 512
    n1, n2 = N_MIX // tk, N_MERGE // tk

    def body(dpm_ref, dpg_ref, dpf_ref, wm_ref, wg_ref, wf_ref, o_ref):
        j = pl.program_id(1)

        @pl.when(j == 0)
        def _():
            o_ref[...] = _dot_nt(dpf_ref[...], wf_ref[...])

        @pl.when(j < n1)
        def _():
            o_ref[...] += _dot_nt(dpm_ref[...], wm_ref[...])

        @pl.when(j >= n1)
        def _():
            o_ref[...] += _dot_nt(dpg_ref[...], wg_ref[...])

    mix_j = lambda j: jnp.minimum(j, n1 - 1)
    merge_j = lambda j: jnp.maximum(j - n1, 0)
    (out,), couts = _pcall(
        body, grid=(s // tm, n1 + n2),
        in_specs=[pl.BlockSpec((tm, tk), lambda i, j: (i, mix_j(j))), pl.BlockSpec((tm, tk), lambda i, j: (i, merge_j(j))),
                  pl.BlockSpec((tm, N_F), lambda i, j: (i, 0)),
                  pl.BlockSpec((D_MODEL, tk), lambda i, j: (0, mix_j(j))), pl.BlockSpec((D_MODEL, tk), lambda i, j: (0, merge_j(j))),
                  pl.BlockSpec((D_MODEL, N_F), lambda i, j: (0, 0))],
        out_specs=[pl.BlockSpec((tm, D_MODEL), lambda i, j: (i, 0))],
        out_shape=[jax.ShapeDtypeStruct((s, D_MODEL), F32)], operands=(dpm, dpg, dpf, w_mix, w_merge, w_f), name=name, comm=comm)
    return out, couts


def _mm_tn(a, d, tn, name):
    m, k = a.shape
    n = d.shape[1]
    tm = 512
    nm = m // tm

    def body(a_ref, d_ref, o_ref, acc):
        i = pl.program_id(1)

        @pl.when(i == 0)
        def _():
            acc[...] = jnp.zeros_like(acc)

        acc[...] += _dot_tn(a_ref[...], d_ref[...])

        @pl.when(i == nm - 1)
        def _():
            o_ref[...] = acc[...].astype(BF16)

    return pl.pallas_call(
        body, grid=(n // tn, nm),
        in_specs=[pl.BlockSpec((tm, k), lambda j, i: (i, 0)), pl.BlockSpec((tm, tn), lambda j, i: (i, j))],
        out_specs=pl.BlockSpec((k, tn), lambda j, i: (0, j)),
        out_shape=jax.ShapeDtypeStruct((k, n), BF16), scratch_shapes=[pltpu.VMEM((k, tn), F32)],
        compiler_params=_cp(("parallel", "arbitrary")), name=name)(a, d)


def _lane_head():
    return lax.broadcasted_iota(jnp.int32, (1, BR), 1) // HEAD_DIM


def _gmlp_chunk_fwd(p_ref, r0, gw_ref, gb_ref, lg, lb):
    uv = p_ref[r0:r0 + CHUNK, O_UV:O_UV + 2 * BR]
    u = _gelu(uv[:, :BR])
    vhat, rs = _ln_hat(_gelu(uv[:, BR:]))
    vn = (vhat * lg + lb).astype(BF16)
    head = _lane_head()
    mixed = gb_ref[...]
    for h in range(N_HEADS):
        mixed = mixed + jnp.where(head == h, _dot(gw_ref[h], vn), 0.0)
    return uv, u, vhat, rs, vn, mixed


def _tap_groups(k_width):
    groups = []
    for b in range(8):
        taps = [(d // 8, k_width - 1 - d) for d in range(b, k_width, 8)]
        if taps:
            groups.append((b, taps))
    return groups


def _causal_taps(buf, off, r0, nr, k_width):
    lead = 8 * ((k_width - 1) // 8 + 1)
    win = buf[off + r0 - lead:off + r0 + nr, :]
    for b, taps in _tap_groups(k_width):
        shifted = win if b == 0 else pltpu.roll(win, b, 0)
        for a, k in taps:
            yield k, shifted[lead - 8 * a:lead - 8 * a + nr]


def _anticausal_taps(buf, r0, nr, k_width):
    lead = 8 * ((k_width - 1) // 8 + 1)
    win = buf[r0:r0 + nr + lead, :]
    for b, taps in _tap_groups(k_width):
        shifted = win if b == 0 else pltpu.roll(win, nr + lead - b, 0)
        for a, k in taps:
            yield k, shifted[8 * a:8 * a + nr]


def _conv_sub_blocks(rows):
    out = [(r, SUB) for r in range(0, rows - rows % SUB, SUB)]
    if rows % SUB:
        out.append((rows - rows % SUB, rows % SUB))
    return out


def _mix_fwd(pm, wl, name, comm=None):
    s = pm.shape[0]
    nt = s // TM

    def body(p_ref, ph_ref, gw_ref, gb_ref, lg_ref, lb_ref, scw_ref, dww_ref, dwb_ref, clg_ref, clb_ref,
             ya_ref, yc_ref, yd_ref, qkv_ref, zbuf, hbuf):
        i = pl.program_id(0)
        qkv_ref[...] = p_ref[:, O_QKV:O_QKV + 3 * BR].astype(BF16)
        lg = lg_ref[...]
        lb = lb_ref[...]
        for c in range(TM // CHUNK):
            r0 = c * CHUNK
            _, u, _, _, _, mixed = _gmlp_chunk_fwd(p_ref, r0, gw_ref, gb_ref, lg, lb)
            ag = p_ref[r0:r0 + CHUNK, O_AG:O_AG + BR]
            ya_ref[r0:r0 + CHUNK, :] = (u * mixed * _silu(ag)).astype(BF16)

        first = i > 0
        zbuf[0:HALO, :] = jnp.where(first, ph_ref[:, O_CIN + BR:O_CIN + 2 * BR] * ph_ref[:, O_CIN + 2 * BR:O_CIN + 3 * BR], 0.0)
        zbuf[HALO:HALO + TM, :] = p_ref[:, O_CIN + BR:O_CIN + 2 * BR] * p_ref[:, O_CIN + 2 * BR:O_CIN + 3 * BR]
        hbuf[0:HALO, :] = jnp.where(first, ph_ref[:, O_GLU:O_GLU + BR] * _sig(ph_ref[:, O_GLU + BR:O_GLU + 2 * BR]), 0.0)
        hbuf[HALO:HALO + TM, :] = p_ref[:, O_GLU:O_GLU + BR] * _sig(p_ref[:, O_GLU + BR:O_GLU + 2 * BR])

        clg = clg_ref[...]
        clb = clb_ref[...]
        for r0, nr in _conv_sub_blocks(TM):
            yc = jnp.zeros((nr, BR), F32)
            for k, zk in _causal_taps(zbuf, HALO, r0, nr, SHORT_CONV):
                yc = yc + scw_ref[k:k + 1, :] * zk
            bgate = p_ref[r0:r0 + nr, O_CIN:O_CIN + BR]
            cg = p_ref[r0:r0 + nr, O_CG:O_CG + BR]
            yc_ref[r0:r0 + nr, :] = (bgate * yc * _silu(cg)).astype(BF16)

            cc = jnp.zeros((nr, BR), F32) + dwb_ref[...]
            for k, hk in _causal_taps(hbuf, HALO, r0, nr, CONF_CONV):
                cc = cc + dww_ref[k:k + 1, :] * hk
            chat, _ = _ln_hat(cc)
            dg = p_ref[r0:r0 + nr, O_DG:O_DG + BR]
            yd_ref[r0:r0 + nr, :] = (_silu(chat * clg + clb) * _silu(dg)).astype(BF16)

    full = lambda shape: pl.BlockSpec(shape, lambda i: tuple(0 for _ in shape))
    ytile = pl.BlockSpec((TM, BR), lambda i: (i, 0))
    yshape = jax.ShapeDtypeStruct((s, BR), BF16)
    return _pcall(
        body, grid=(nt,),
        in_specs=[pl.BlockSpec((TM, N_MIX), lambda i: (i, 0)),
                  pl.BlockSpec((HALO, N_MIX), lambda i: (jnp.maximum(i * (TM // HALO) - 1, 0), 0)),
                  full((N_HEADS, CHUNK, CHUNK)), full((CHUNK, BR)), full((1, BR)), full((1, BR)),
                  full((8, BR)), full((32, BR)), full((1, BR)), full((1, BR)), full((1, BR))],
        out_specs=[ytile, ytile, ytile, pl.BlockSpec((TM, 3 * BR), lambda i: (i, 0))],
        out_shape=[yshape, yshape, yshape, jax.ShapeDtypeStruct((s, 3 * BR), BF16)],
        scratch_shapes=[pltpu.VMEM((HALO + TM, BR), F32), pltpu.VMEM((HALO + TM, BR), F32)],
        name=name, comm=comm, operands=(
            pm, pm, wl["gw"], wl["gb"], wl["sgu_ln_g"], wl["sgu_ln_b"], wl["scw"], wl["dww"], wl["conf_dw_b"],
            wl["conf_ln_g"], wl["conf_ln_b"]))


def _mix_bwd(pm, dya, dyc, dyd, dqkv, dbg, wl, name):
    s = pm.shape[0]
    nt = s // TM
    ext = TM + HALO

    def body(p_ref, ph_ref, pn_ref, dya_ref, dyc_ref, dycn_ref, dyd_ref, dydn_ref, dq_ref, dk_ref, dv_ref, dbg_ref,
             gw_ref, gwt_ref, gb_ref, lg_ref, lb_ref, scw_ref, dww_ref, dwb_ref, clg_ref, clb_ref,
             dp_ref, dgw_ref, dgb_ref, vec_ref, dscw_ref, ddww_ref, zbuf, dcb, hbuf, dcc):
        i = pl.program_id(0)

        @pl.when(i == 0)
        def _():
            dgw_ref[...] = jnp.zeros_like(dgw_ref)
            dgb_ref[...] = jnp.zeros_like(dgb_ref)
            vec_ref[...] = jnp.zeros_like(vec_ref)
            dscw_ref[...] = jnp.zeros_like(dscw_ref)
            ddww_ref[...] = jnp.zeros_like(ddww_ref)

        lg = lg_ref[...]
        lb = lb_ref[...]
        head = _lane_head()
        d_lg = jnp.zeros((1, BR), F32)
        d_lb = jnp.zeros((1, BR), F32)
        for c in range(TM // CHUNK):
            r0 = c * CHUNK
            uv, u, vhat, rs, vn, mixed = _gmlp_chunk_fwd(p_ref, r0, gw_ref, gb_ref, lg, lb)
            ag = p_ref[r0:r0 + CHUNK, O_AG:O_AG + BR]
            dy = dya_ref[r0:r0 + CHUNK, :]
            sa = _silu(ag)
            du = dy * mixed * sa
            dmx = dy * u * sa
            dp_ref[r0:r0 + CHUNK, O_AG:O_AG + BR] = (dy * u * mixed * _dsilu(ag)).astype(BF16)
            dgb_ref[...] += dmx
            dmx_b = dmx.astype(BF16)
            dvn = jnp.zeros((CHUNK, BR), F32)
            for h in range(N_HEADS):
                sel = head == h
                dgw_ref[h] += _dot_nt(jnp.where(sel, dmx, 0.0).astype(BF16), vn)
                dvn = dvn + jnp.where(sel, _dot(gwt_ref[h], dmx_b), 0.0)
            d_lg = d_lg + jnp.sum(dvn * vhat, axis=0, keepdims=True)
            d_lb = d_lb + jnp.sum(dvn, axis=0, keepdims=True)
            dv0 = _ln_bwd(dvn * lg, vhat, rs)
            dp_ref[r0:r0 + CHUNK, O_UV:O_UV + BR] = (du * _dgelu(uv[:, :BR])).astype(BF16)
            dp_ref[r0:r0 + CHUNK, O_UV + BR:O_UV + 2 * BR] = (dv0 * _dgelu(uv[:, BR:])).astype(BF16)
        vec_ref[0:1, :] += d_lg
        vec_ref[1:2, :] += d_lb

        dp_ref[:, O_QKV:O_QKV + BR] = dq_ref[...]
        dp_ref[:, O_QKV + BR:O_QKV + 2 * BR] = dk_ref[...]
        dp_ref[:, O_QKV + 2 * BR:O_QKV + 3 * BR] = dv_ref[...]
        dp_ref[:, O_BG:O_BG + BR] = dbg_ref[...].astype(BF16)

        first = i > 0
        last = i < nt - 1
        zbuf[0:HALO, :] = jnp.where(first, ph_ref[:, O_CIN + BR:O_CIN + 2 * BR] * ph_ref[:, O_CIN + 2 * BR:O_CIN + 3 * BR], 0.0)
        zbuf[HALO:HALO + TM, :] = p_ref[:, O_CIN + BR:O_CIN + 2 * BR] * p_ref[:, O_CIN + 2 * BR:O_CIN + 3 * BR]
        dcb[0:TM, :] = dyc_ref[...] * p_ref[:, O_CIN:O_CIN + BR] * _silu(p_ref[:, O_CG:O_CG + BR])
        dcb[TM:ext, :] = jnp.where(last, dycn_ref[...] * pn_ref[:, O_CIN:O_CIN + BR] * _silu(pn_ref[:, O_CG:O_CG + BR]), 0.0)
        for r0, nr in _conv_sub_blocks(TM):
            yc = jnp.zeros((nr, BR), F32)
            dz = jnp.zeros((nr, BR), F32)
            dcur = dcb[r0:r0 + nr, :]
            for k, zk in _causal_taps(zbuf, HALO, r0, nr, SHORT_CONV):
                yc = yc + scw_ref[k:k + 1, :] * zk
                dscw_ref[8 * k:8 * k + 8, :] += _fold8(dcur * zk)
            for k, dk in _anticausal_taps(dcb, r0, nr, SHORT_CONV):
                dz = dz + scw_ref[k:k + 1, :] * dk
            dy = dyc_ref[r0:r0 + nr, :]
            bgate = p_ref[r0:r0 + nr, O_CIN:O_CIN + BR]
            cg = p_ref[r0:r0 + nr, O_CG:O_CG + BR]
            dp_ref[r0:r0 + nr, O_CIN:O_CIN + BR] = (dy * yc * _silu(cg)).astype(BF16)
            dp_ref[r0:r0 + nr, O_CIN + BR:O_CIN + 2 * BR] = (dz * p_ref[r0:r0 + nr, O_CIN + 2 * BR:O_CIN + 3 * BR]).astype(BF16)
            dp_ref[r0:r0 + nr, O_CIN + 2 * BR:O_CIN + 3 * BR] = (dz * p_ref[r0:r0 + nr, O_CIN + BR:O_CIN + 2 * BR]).astype(BF16)
            dp_ref[r0:r0 + nr, O_CG:O_CG + BR] = (dy * bgate * yc * _dsilu(cg)).astype(BF16)

        hbuf[0:HALO, :] = jnp.where(first, ph_ref[:, O_GLU:O_GLU + BR] * _sig(ph_ref[:, O_GLU + BR:O_GLU + 2 * BR]), 0.0)
        hbuf[HALO:HALO + TM, :] = p_ref[:, O_GLU:O_GLU + BR] * _sig(p_ref[:, O_GLU + BR:O_GLU + 2 * BR])
        hbuf[HALO + TM:HALO + ext, :] = jnp.where(last, pn_ref[:, O_GLU:O_GLU + BR] * _sig(pn_ref[:, O_GLU + BR:O_GLU + 2 * BR]), 0.0)
        clg = clg_ref[...]
        clb = clb_ref[...]
        d_clg = jnp.zeros((1, BR), F32)
        d_clb = jnp.zeros((1, BR), F32)
        d_dwb = jnp.zeros((1, BR), F32)
        for r0, nr in _conv_sub_blocks(ext):
            in_tile = r0 < TM
            cc = jnp.zeros((nr, BR), F32) + dwb_ref[...]
            for k, hk in _causal_taps(hbuf, HALO, r0, nr, CONF_CONV):
                cc = cc + dww_ref[k:k + 1, :] * hk
            chat, rs = _ln_hat(cc)
            ln = chat * clg + clb
            if in_tile:
                dy = dyd_ref[r0:r0 + nr, :]
                dg = p_ref[r0:r0 + nr, O_DG:O_DG + BR]
            else:
                dy = jnp.where(last, dydn_ref[...], 0.0)
                dg = pn_ref[:, O_DG:O_DG + BR]
            dln = dy * _silu(dg) * _dsilu(ln)
            dc = _ln_bwd(dln * clg, chat, rs)
            dcc[r0:r0 + nr, :] = dc
            if in_tile:
                dp_ref[r0:r0 + nr, O_DG:O_DG + BR] = (dy * _silu(ln) * _dsilu(dg)).astype(BF16)
                d_clg = d_clg + jnp.sum(dln * chat, axis=0, keepdims=True)
                d_clb = d_clb + jnp.sum(dln, axis=0, keepdims=True)
                d_dwb = d_dwb + jnp.sum(dc, axis=0, keepdims=True)
        vec_ref[2:3, :] += d_dwb
        vec_ref[3:4, :] += d_clg
        vec_ref[4:5, :] += d_clb
        for r0, nr in _conv_sub_blocks(TM):
            dcur = dcc[r0:r0 + nr, :]
            dhh = jnp.zeros((nr, BR), F32)
            for k, hk in _causal_taps(hbuf, HALO, r0, nr, CONF_CONV):
                ddww_ref[8 * k:8 * k + 8, :] += _fold8(dcur * hk)
            for k, dk in _anticausal_taps(dcc, r0, nr, CONF_CONV):
                dhh = dhh + dww_ref[k:k + 1, :] * dk
            a = p_ref[r0:r0 + nr, O_GLU:O_GLU + BR]
            sg = _sig(p_ref[r0:r0 + nr, O_GLU + BR:O_GLU + 2 * BR])
            dp_ref[r0:r0 + nr, O_GLU:O_GLU + BR] = (dhh * sg).astype(BF16)
            dp_ref[r0:r0 + nr, O_GLU + BR:O_GLU + 2 * BR] = (dhh * a * sg * (1.0 - sg)).astype(BF16)

    full = lambda shape: pl.BlockSpec(shape, lambda i: tuple(0 for _ in shape))
    rpt = TM // HALO
    prev_map = lambda i: (jnp.maximum(i * rpt - 1, 0), 0)
    next_map = lambda i: (jnp.minimum((i + 1) * rpt, nt * rpt - 1), 0)
    ytile = pl.BlockSpec((TM, BR), lambda i: (i, 0))
    return pl.pallas_call(
        body, grid=(nt,),
        in_specs=[pl.BlockSpec((TM, N_MIX), lambda i: (i, 0)), pl.BlockSpec((HALO, N_MIX), prev_map),
                  pl.BlockSpec((HALO, N_MIX), next_map),
                  ytile, ytile, pl.BlockSpec((HALO, BR), next_map), ytile, pl.BlockSpec((HALO, BR), next_map),
                  ytile, ytile, ytile, ytile,
                  full((N_HEADS, CHUNK, CHUNK)), full((N_HEADS, CHUNK, CHUNK)), full((CHUNK, BR)), full((1, BR)), full((1, BR)),
                  full((8, BR)), full((32, BR)), full((1, BR)), full((1, BR)), full((1, BR))],
        out_specs=[pl.BlockSpec((TM, N_MIX), lambda i: (i, 0)), full((N_HEADS, CHUNK, CHUNK)), full((CHUNK, BR)),
                   full((16, BR)), full((64, BR)), full((256, BR))],
        out_shape=[jax.ShapeDtypeStruct((s, N_MIX), BF16), jax.ShapeDtypeStruct((N_HEADS, CHUNK, CHUNK), F32),
                   jax.ShapeDtypeStruct((CHUNK, BR), F32), jax.ShapeDtypeStruct((16, BR), F32),
                   jax.ShapeDtypeStruct((64, BR), F32), jax.ShapeDtypeStruct((256, BR), F32)],
        scratch_shapes=[pltpu.VMEM((HALO + TM, BR), F32), pltpu.VMEM((ext, BR), F32),
                        pltpu.VMEM((HALO + ext, BR), F32), pltpu.VMEM((ext, BR), F32)],
        compiler_params=_cp(("arbitrary",)), name=name)(
            pm, pm, pm, dya, dyc, dyc, dyd, dyd, *dqkv, dbg,
            wl["gw"], wl["gwt"], wl["gb"], wl["sgu_ln_g"], wl["sgu_ln_b"], wl["scw"], wl["dww"], wl["conf_dw_b"],
            wl["conf_ln_g"], wl["conf_ln_b"])


def _tri(lower):
    r = lax.broadcasted_iota(jnp.int32, (CUMB, CUMB), 0)
    c = lax.broadcasted_iota(jnp.int32, (CUMB, CUMB), 1)
    return jnp.where((r >= c) if lower else (r <= c), 1.0, 0.0).astype(F32)


def _dot_hi(a, b):
    return jnp.dot(a, b, preferred_element_type=F32, precision=lax.Precision.HIGHEST)


def _cum_fwd(pf, fb, name):
    s = pf.shape[0]

    def body(pf_ref, fb_ref, cum_ref, carry):
        i = pl.program_id(0)

        @pl.when(i == 0)
        def _():
            carry[...] = jnp.zeros_like(carry)

        z = pf_ref[...] + fb_ref[...]
        logf = jnp.minimum(z, 0.0) - jnp.log(1.0 + jnp.exp(-jnp.abs(z)))
        cum_ref[...] = _dot_hi(_tri(True), logf) + carry[...]
        carry[...] += jnp.sum(logf, axis=0, keepdims=True)

    return pl.pallas_call(
        body, grid=(s // CUMB,),
        in_specs=[pl.BlockSpec((CUMB, N_F), lambda i: (i, 0)), pl.BlockSpec((1, N_F), lambda i: (0, 0))],
        out_specs=pl.BlockSpec((CUMB, N_F), lambda i: (i, 0)),
        out_shape=jax.ShapeDtypeStruct((s, N_F), F32),
        scratch_shapes=[pltpu.VMEM((1, N_F), F32)],
        compiler_params=_cp(("arbitrary",)), name=name)(pf, fb)


def _cum_bwd(dcq, dck, pf, fb, name):
    s = pf.shape[0]
    nb = s // CUMB

    def body(dcq_ref, dck_ref, pf_ref, fb_ref, dpf_ref, dfb_ref, carry):
        i = pl.program_id(0)

        @pl.when(i == 0)
        def _():
            carry[...] = jnp.zeros_like(carry)
            dfb_ref[...] = jnp.zeros_like(dfb_ref)

        lane = lax.broadcasted_iota(jnp.int32, (1, N_F), 1)
        dc = dck_ref[...]
        for h in range(N_HEADS):
            dc = dc + jnp.where(lane == h, dcq_ref[h // 2, :, h % 2:h % 2 + 1], 0.0)
        dlogf = _dot_hi(_tri(False), dc) + carry[...]
        carry[...] += jnp.sum(dc, axis=0, keepdims=True)
        z = pf_ref[...] + fb_ref[...]
        dz = dlogf * (1.0 - _sig(z))
        dpf_ref[...] = dz.astype(BF16)
        dfb_ref[...] += _fold8(dz)

    rev = lambda i: (nb - 1 - i, 0)
    return pl.pallas_call(
        body, grid=(nb,),
        in_specs=[pl.BlockSpec((N_HEADS // 2, CUMB, 128), lambda i: (0, nb - 1 - i, 0)), pl.BlockSpec((CUMB, N_F), rev),
                  pl.BlockSpec((CUMB, N_F), rev), pl.BlockSpec((1, N_F), lambda i: (0, 0))],
        out_specs=[pl.BlockSpec((CUMB, N_F), rev), pl.BlockSpec((8, N_F), lambda i: (0, 0))],
        out_shape=[jax.ShapeDtypeStruct((s, N_F), BF16), jax.ShapeDtypeStruct((8, N_F), F32)],
        scratch_shapes=[pltpu.VMEM((1, N_F), F32)],
        compiler_params=_cp(("arbitrary",)), name=name)(dcq, dck, pf, fb)


def _causal_mask():
    r = lax.broadcasted_iota(jnp.int32, (BQ, BQ), 0)
    c = lax.broadcasted_iota(jnp.int32, (BQ, BQ), 1)
    return r >= c


def _attn_fwd(qkv, cq, ck, name, comm=None):
    s = qkv.shape[0]
    nb = s // BQ

    def body(q_ref, k_ref, v_ref, cq_ref, ck_ref, o_ref, lse_ref):
        for hh in range(2):
            lanes = slice(hh * HEAD_DIM, (hh + 1) * HEAD_DIM)
            for qi in range(nb):
                qs = qi * BQ
                qb = q_ref[qs:qs + BQ, lanes] * SCALE
                cqb = cq_ref[0, qs:qs + BQ, hh:hh + 1]

                def block(kj, carry, masked):
                    m, l, acc = carry
                    ks = pl.multiple_of(kj * BQ, BQ)
                    kb = k_ref[pl.ds(ks, BQ), lanes]
                    vb = v_ref[pl.ds(ks, BQ), lanes]
                    sc = _dot_nt(qb, kb) + (cqb - ck_ref[hh, kj])
                    if masked:
                        sc = jnp.where(_causal_mask(), sc, NEG)
                    m_new = jnp.maximum(m, jnp.max(sc, axis=-1, keepdims=True))
                    alpha = jnp.exp(m - m_new)
                    p = jnp.exp(sc - m_new)
                    l = alpha * l + jnp.sum(p, axis=-1, keepdims=True)
                    acc = alpha * acc + _dot(p.astype(BF16), vb)
                    return m_new, l, acc

                carry = (jnp.full((BQ, 1), NEG, F32), jnp.zeros((BQ, 1), F32), jnp.zeros((BQ, HEAD_DIM), F32))
                if qi > 0:
                    carry = lax.fori_loop(0, qi, lambda kj, cr: block(kj, cr, False), carry)
                m, l, acc = block(qi, carry, True)
                o_ref[qs:qs + BQ, lanes] = acc / l
                lse_ref[0, qs:qs + BQ, hh:hh + 1] = m + jnp.log(l)

    pair = lambda j: pl.BlockSpec((s, 2 * HEAD_DIM), lambda p: (0, 2 * j + p))
    cblk = pl.BlockSpec((1, s, 128), lambda p: (p, 0, 0))
    return _pcall(
        body, grid=(N_HEADS // 2,),
        in_specs=[pair(0), pair(1), pair(2), cblk, pl.BlockSpec((2, nb, 1, BQ), lambda p: (p, 0, 0, 0))],
        out_specs=[pl.BlockSpec((s, 2 * HEAD_DIM), lambda p: (0, p)), cblk],
        out_shape=[jax.ShapeDtypeStruct((s, BR), F32), jax.ShapeDtypeStruct((N_HEADS // 2, s, 128), F32)],
        operands=(qkv, qkv, qkv, cq, ck), name=name, comm=comm)


def _attn_bwd(qkv, cq, ck, o, lse, do, name, comm=None):
    s = qkv.shape[0]
    nb = s // BQ

    def body(q_ref, k_ref, v_ref, cq_ref, ck_ref, o_ref, lse_ref, do_ref, dq_ref, dk_ref, dv_ref, dcq_ref, dck_ref,
             delta, dq_acc):
        dq_acc[...] = jnp.zeros_like(dq_acc)
        dcq_ref[...] = jnp.zeros_like(dcq_ref)
        for hh in range(2):
            lanes = slice(hh * HEAD_DIM, (hh + 1) * HEAD_DIM)
            col = slice(hh, hh + 1)
            delta[...] = jnp.sum(do_ref[:, lanes] * o_ref[:, lanes], axis=-1, keepdims=True)
            for kj in range(nb):
                ks = kj * BQ
                kb = k_ref[ks:ks + BQ, lanes]
                vb = v_ref[ks:ks + BQ, lanes]
                ckb = ck_ref[hh, kj]

                def block(qi, carry, masked):
                    dk_acc, dv_acc, dck_acc = carry
                    qs = pl.multiple_of(qi * BQ, BQ)
                    qb = q_ref[pl.ds(qs, BQ), lanes] * SCALE
                    dob = do_ref[pl.ds(qs, BQ), lanes].astype(BF16)
                    sc = _dot_nt(qb, kb) + (cq_ref[0, pl.ds(qs, BQ), col] - ckb)
                    p = jnp.exp(sc - lse_ref[0, pl.ds(qs, BQ), col])
                    if masked:
                        p = jnp.where(_causal_mask(), p, 0.0)
                    dp = _dot_nt(dob, vb)
                    ds = p * (dp - delta[pl.ds(qs, BQ), :])
                    ds_b = ds.astype(BF16)
                    dv_acc = dv_acc + _dot_tn(p.astype(BF16), dob)
                    dk_acc = dk_acc + _dot_tn(ds_b, qb)
                    dq_acc[pl.ds(qs, BQ), lanes] += _dot(ds_b, kb) * SCALE
                    dcq_ref[0, pl.ds(qs, BQ), col] += jnp.sum(ds, axis=-1, keepdims=True)
                    dck_acc = dck_acc - jnp.sum(ds, axis=0, keepdims=True)
                    return dk_acc, dv_acc, dck_acc

                carry = (jnp.zeros((BQ, HEAD_DIM), F32), jnp.zeros((BQ, HEAD_DIM), F32), jnp.zeros((1, BQ), F32))
                carry = block(kj, carry, True)
                if kj < nb - 1:
                    carry = lax.fori_loop(kj + 1, nb, lambda qi, cr: block(qi, cr, False), carry)
                dk_ref[ks:ks + BQ, lanes] = carry[0].astype(BF16)
                dv_ref[ks:ks + BQ, lanes] = carry[1].astype(BF16)
                dck_ref[hh, kj] = carry[2]
        dq_ref[...] = dq_acc[...].astype(BF16)

    pair = lambda j: pl.BlockSpec((s, 2 * HEAD_DIM), lambda p: (0, 2 * j + p))
    half = pl.BlockSpec((s, 2 * HEAD_DIM), lambda p: (0, p))
    cblk = pl.BlockSpec((1, s, 128), lambda p: (p, 0, 0))
    kblk = pl.BlockSpec((2, nb, 1, BQ), lambda p: (p, 0, 0, 0))
    dshape = jax.ShapeDtypeStruct((s, BR), BF16)
    return _pcall(
        body, grid=(N_HEADS // 2,),
        in_specs=[pair(0), pair(1), pair(2), cblk, kblk, half, cblk, half],
        out_specs=[half, half, half, cblk, kblk],
        out_shape=[dshape, dshape, dshape, jax.ShapeDtypeStruct((N_HEADS // 2, s, 128), F32),
                   jax.ShapeDtypeStruct((N_HEADS, nb, 1, BQ), F32)],
        scratch_shapes=[pltpu.VMEM((s, 1), F32), pltpu.VMEM((s, 2 * HEAD_DIM), F32)],
        operands=(qkv, qkv, qkv, cq, ck, o, lse, do), name=name, comm=comm)


def _merge_fwd(x, ya, yc, yd, o, pm, pg, wb, wo, name, comm=None):
    s = x.shape[0]

    def body(x_ref, ya_ref, yc_ref, yd_ref, o_ref, bg_ref, pg_ref, wb_ref, wo_ref, xo_ref, yb_ref):
        yb = (o_ref[...] * _silu(bg_ref[...])).astype(BF16)
        yb_ref[...] = yb
        ys = (ya_ref[...], yb, yc_ref[...], yd_ref[...])
        merged = jnp.zeros((TMG, D_MODEL), F32)
        for n in range(N_BRANCH):
            merged = merged + _sig(pg_ref[:, n * D_MODEL:(n + 1) * D_MODEL]) * _dot(ys[n], wb_ref[n])
        xo_ref[...] = x_ref[...] + _dot(merged.astype(BF16), wo_ref[...])

    xt = pl.BlockSpec((TMG, D_MODEL), lambda i: (i, 0))
    yt = pl.BlockSpec((TMG, BR), lambda i: (i, 0))
    return _pcall(
        body, grid=(s // TMG,),
        in_specs=[xt, yt, yt, yt, yt, pl.BlockSpec((TMG, BR), lambda i: (i, O_BG // BR)),
                  pl.BlockSpec((TMG, N_MERGE), lambda i: (i, 0)),
                  pl.BlockSpec((N_BRANCH, BR, D_MODEL), lambda i: (0, 0, 0)), pl.BlockSpec((D_MODEL, D_MODEL), lambda i: (0, 0))],
        out_specs=[xt, yt],
        out_shape=[jax.ShapeDtypeStruct((s, D_MODEL), F32), jax.ShapeDtypeStruct((s, BR), BF16)],
        operands=(x, ya, yc, yd, o, pm, pg, wb, wo), name=name, comm=comm)


def _merge_bwd(dx, ya, yb, yc, yd, o, pm, pg, wb, wo, name, comm=None):
    s = dx.shape[0]
    nt = s // TMG

    def body(dx_ref, ya_ref, yb_ref, yc_ref, yd_ref, o_ref, bg_ref, pg_ref, wb_ref, wo_ref,
             dpg_ref, dya_ref, do_ref, dbg_ref, dyc_ref, dyd_ref, dwb_ref, dwo_ref, dwb_acc, dwo_acc):
        i = pl.program_id(0)

        @pl.when(i == 0)
        def _():
            dwb_acc[...] = jnp.zeros_like(dwb_acc)
            dwo_acc[...] = jnp.zeros_like(dwo_acc)

        dxb = dx_ref[...].astype(BF16)
        dmerged = _dot_nt(dxb, wo_ref[...])
        ys = (ya_ref[...], yb_ref[...], yc_ref[...], yd_ref[...])
        dys = (dya_ref, None, dyc_ref, dyd_ref)
        merged = jnp.zeros((TMG, D_MODEL), F32)
        for n in range(N_BRANCH):
            gate = _sig(pg_ref[:, n * D_MODEL:(n + 1) * D_MODEL])
            pr = _dot(ys[n], wb_ref[n])
            merged = merged + gate * pr
            dpg_ref[:, n * D_MODEL:(n + 1) * D_MODEL] = (dmerged * pr * gate * (1.0 - gate)).astype(BF16)
            dpr = (gate * dmerged).astype(BF16)
            dwb_acc[n] += _dot_tn(ys[n], dpr)
            dyn = _dot_nt(dpr, wb_ref[n])
            if n == 1:
                bg = bg_ref[...]
                do_ref[...] = dyn * _silu(bg)
                dbg_ref[...] = dyn * o_ref[...] * _dsilu(bg)
            else:
                dys[n][...] = dyn
        dwo_acc[...] += _dot_tn(merged.astype(BF16), dxb)

        @pl.when(i == nt - 1)
        def _():
            dwb_ref[...] = dwb_acc[...].astype(BF16)
            dwo_ref[...] = dwo_acc[...].astype(BF16)

    xt = pl.BlockSpec((TMG, D_MODEL), lambda i: (i, 0))
    yt = pl.BlockSpec((TMG, BR), lambda i: (i, 0))
    gt = pl.BlockSpec((TMG, N_MERGE), lambda i: (i, 0))
    wbs = pl.BlockSpec((N_BRANCH, BR, D_MODEL), lambda i: (0, 0, 0))
    wos = pl.BlockSpec((D_MODEL, D_MODEL), lambda i: (0, 0))
    yf = jax.ShapeDtypeStruct((s, BR), F32)
    return _pcall(
        body, grid=(nt,),
        in_specs=[xt, yt, yt, yt, yt, yt, pl.BlockSpec((TMG, BR), lambda i: (i, O_BG // BR)), gt, wbs, wos],
        out_specs=[gt, yt, yt, yt, yt, yt, wbs, wos],
        out_shape=[jax.ShapeDtypeStruct((s, N_MERGE), BF16), yf, yf, yf, yf, yf,
                   jax.ShapeDtypeStruct((N_BRANCH, BR, D_MODEL), BF16), jax.ShapeDtypeStruct((D_MODEL, D_MODEL), BF16)],
        scratch_shapes=[pltpu.VMEM((N_BRANCH, BR, D_MODEL), F32), pltpu.VMEM((D_MODEL, D_MODEL), F32)],
        operands=(dx, ya, yb, yc, yd, o, pm, pg, wb, wo), name=name, comm=comm)


def _layer_fwd(x, wl, tag, attach=None):
    attach = attach or {}

    def riding(stage):
        comm, sink = attach.get(stage, (None, None))
        return comm, (sink or (lambda res: None))

    s = x.shape[0]
    h = _rms_fwd(x, wl["norm_g"], "rms_fwd" + tag)
    comm, sink = riding("proj_mix")
    pm, res = _mm_nn(h, wl["w_mix"], 1792, "proj_mix" + tag, comm)
    sink(res)
    pg, _ = _mm_nn(h, wl["w_merge"], 2048, "proj_merge" + tag)
    pf, _ = _mm_nn(h, wl["w_f"], N_F, "proj_f" + tag)
    comm, sink = riding("mix_fwd")
    (ya, yc, yd, qkv), res = _mix_fwd(pm, wl, "mix_fwd" + tag, comm)
    sink(res)
    cum = _cum_fwd(pf, wl["f_bias"], "cum_fwd" + tag)
    cq = jnp.stack([jnp.pad(cum[:, 2 * p:2 * p + 2], ((0, 0), (0, 126))) for p in range(N_HEADS // 2)])
    ck = cum[:, :N_HEADS].T.reshape(N_HEADS, s // BQ, 1, BQ)
    comm, sink = riding("attn_fwd")
    (o, lse), res = _attn_fwd(qkv, cq, ck, "attn_fwd" + tag, comm)
    sink(res)
    comm, sink = riding("merge_fwd")
    (x_next, yb), res = _merge_fwd(x, ya, yc, yd, o, pm, pg, wl["wb"], wl["wo"], "merge_fwd" + tag, comm)
    sink(res)
    saved = dict(x=x, h=h, pm=pm, pg=pg, pf=pf, ya=ya, yb=yb, yc=yc, yd=yd, qkv=qkv, cq=cq, ck=ck, o=o, lse=lse)
    return x_next, saved


def _blocks_rows(g):
    return g.reshape(N_DEV, g.shape[0] // N_DEV, g.shape[1])


def _blocks_cols(g):
    return g.reshape(N_BRANCH * BR, N_DEV, D_MODEL // N_DEV).transpose(1, 0, 2)


def _layer_bwd(dx_next, sv, wl, tag, dist, riding, extra_small):
    s = dx_next.shape[0]
    (dpg, dya, do, dbg, dyc, dyd, dwb, dwo), rode = _merge_bwd(
        dx_next, sv["ya"], sv["yb"], sv["yc"], sv["yd"], sv["o"], sv["pm"], sv["pg"], wl["wb"], wl["wo"], "merge_bwd" + tag,
        riding)
    dw_merge = _mm_tn(sv["h"], dpg, 2048, "dw_merge" + tag)
    early = [(_blocks_rows(dw_merge), False), (_blocks_cols(dwb), False), (_blocks_rows(dwo), False)] if dist else None
    (dq, dk, dv, dcq, dck), early_out = _attn_bwd(sv["qkv"], sv["cq"], sv["ck"], sv["o"], sv["lse"], do, "attn_bwd" + tag, early)
    dck_cols = jnp.pad(dck.reshape(N_HEADS, s).T, ((0, 0), (0, N_F - N_HEADS)))
    dpf, dfb = _cum_bwd(dcq, dck_cols, sv["pf"], wl["f_bias"], "cum_bwd" + tag)
    dpm, dgw, dgb, vec, dscw, ddww = _mix_bwd(sv["pm"], dya, dyc, dyd, (dq, dk, dv), dbg, wl, "mix_bwd" + tag)
    dw_mix = _mm_tn(sv["h"], dpm, 1792, "dw_mix" + tag)
    dw_f = _mm_tn(sv["h"], dpf, N_F, "dw_f" + tag)
    causal = jnp.tril(jnp.ones((CHUNK, CHUNK), bool))
    small = dict(
        f_bias=dfb.sum(0)[:N_HEADS],
        sgu_w=jnp.where(causal[None], dgw, 0.0),
        sgu_b=dgb.reshape(CHUNK, N_HEADS, HEAD_DIM).sum(-1).T,
        sgu_ln_g=vec[0], sgu_ln_b=vec[1], conf_dw_b=vec[2], conf_ln_g=vec[3], conf_ln_b=vec[4],
        short_conv_w=dscw.reshape(8, 8, BR).sum(1)[:SHORT_CONV],
        conf_dw_w=ddww.reshape(32, 8, BR).sum(1)[:CONF_CONV],
    )
    slab, spans = _pack([small[nm] for nm in SMALL[1:]])
    late = [(_blocks_rows(dw_mix), False), (_blocks_rows(dw_f), False), (slab, True)] if dist else None
    dh, late_out = _dh(dpm, dpg, dpf, wl["w_mix"], wl["w_merge"], wl["w_f"], "dh" + tag, late)
    dx, dng = _rms_bwd(dh, sv["x"], wl["norm_g"], dx_next, "rms_bwd" + tag)
    small["norm_g"] = dng.sum(0)
    grads = dict(small, w_mix=dw_mix, w_merge=dw_merge, w_f=dw_f, wb=dwb, wo=dwo)
    last_slab, last_spans = _pack([small["norm_g"]] + list(extra_small))
    return dx, grads, early_out, (late_out, spans), ([(last_slab, True)], last_spans), rode


def _prep_layer_small(norm_g, f_bias, sgu_w, sgu_b, sgu_ln_g, sgu_ln_b, scw, dww, conf_dw_b, conf_ln_g, conf_ln_b):
    causal = jnp.tril(jnp.ones((CHUNK, CHUNK), bool))
    gw = jnp.where(causal[None], sgu_w, 0.0)
    row = lambda a: a.reshape(1, -1)
    return dict(
        norm_g=row(norm_g),
        f_bias=jnp.pad(row(f_bias), ((0, 0), (0, N_F - N_HEADS))),
        gw=gw.astype(BF16), gwt=gw.transpose(0, 2, 1).astype(BF16),
        gb=jnp.repeat(sgu_b.T, HEAD_DIM, axis=1),
        sgu_ln_g=row(sgu_ln_g), sgu_ln_b=row(sgu_ln_b),
        scw=jnp.pad(scw, ((0, 8 - SHORT_CONV), (0, 0))), dww=jnp.pad(dww, ((0, 32 - CONF_CONV), (0, 0))),
        conf_dw_b=row(conf_dw_b), conf_ln_g=row(conf_ln_g), conf_ln_b=row(conf_ln_b))


def _local_step(x, target, layers, final_g):
    saved = []
    for l in range(DEPTH):
        x, sv = _layer_fwd(x, layers[l], str(l))
        saved.append(sv)
    loss_p, dx, dfg = _loss_head(x, final_g.reshape(1, D_MODEL), target)
    grads = [None] * DEPTH
    for l in reversed(range(DEPTH)):
        dx, grads[l], _, _, _, _ = _layer_bwd(dx, saved[l], layers[l], str(l), False, None, [])
    return 0.5 / D_MODEL * jnp.sum(loss_p), dx, grads, dfg.sum(0)


def _sum8(a, name):
    _, r, c = a.shape
    tr = r
    while tr * c * a.dtype.itemsize * N_DEV > 4 * 1024 * 1024 and tr % 32 == 0:
        tr //= 2

    def body(a_ref, o_ref):
        acc = a_ref[0].astype(F32)
        for d in range(1, N_DEV):
            acc = acc + a_ref[d].astype(F32)
        o_ref[...] = acc

    return pl.pallas_call(
        body, grid=(r // tr,),
        in_specs=[pl.BlockSpec((N_DEV, tr, c), lambda i: (0, i, 0))],
        out_specs=pl.BlockSpec((tr, c), lambda i: (i, 0)),
        out_shape=jax.ShapeDtypeStruct((r, c), F32), compiler_params=_cp(("parallel",)), name=name)(a)


def _adamw(w, g, m, v, name):
    l, r, c = w.shape
    tr = r
    while tr * c * 4 > 1024 * 1024 and tr % 16 == 0:
        tr //= 2
    c1 = 1.0 - ADAM_B1 ** ADAM_STEP
    c2 = 1.0 - ADAM_B2 ** ADAM_STEP

    def body(w_ref, g_ref, m_ref, v_ref, d_ref, mo_ref, vo_ref):
        gv = g_ref[...]
        mn = ADAM_B1 * m_ref[...] + (1.0 - ADAM_B1) * gv
        vn = ADAM_B2 * v_ref[...] + (1.0 - ADAM_B2) * (gv * gv)
        mo_ref[...] = mn
        vo_ref[...] = vn
        d_ref[...] = -ADAM_LR * ((mn / c1) / (jnp.sqrt(vn / c2) + ADAM_EPS) + ADAM_WD * w_ref[...])

    blk = pl.BlockSpec((1, tr, c), lambda a, i: (a, i, 0))
    shp = jax.ShapeDtypeStruct((l, r, c), F32)
    return pl.pallas_call(
        body, grid=(l, r // tr), in_specs=[blk] * 4, out_specs=[blk] * 3, out_shape=[shp] * 3,
        compiler_params=_cp(("parallel", "parallel")), name=name)(w, g, m, v)


def _pack(parts):
    rows, spans, r = [], [], 0
    for p in parts:
        flat = p.reshape(-1)
        nr = -(-flat.shape[0] // 1024) * 8
        rows.append(jnp.pad(flat, (0, nr * 128 - flat.shape[0])).reshape(nr, 128))
        spans.append((r, nr, p.shape))
        r += nr
    return jnp.concatenate(rows, axis=0), spans


def _unpack(slab, spans):
    out = []
    for r, nr, shape in spans:
        size = math.prod(shape)
        out.append(slab[r:r + nr].reshape(-1)[:size].reshape(shape))
    return out


def _split_w_in(w):
    mix = jnp.concatenate([w[..., 0:1536], w[..., 1540:1796], w[..., 3332:3588], w[..., 1796:2820], w[..., 2820:3332]], axis=-1)
    return mix, w[..., 3588:7684], w[..., 1536:1540]


def _join_w_in(mix, merge, f):
    return jnp.concatenate([mix[..., 0:1536], f, mix[..., 1536:1792], mix[..., 2048:3072], mix[..., 3072:3584],
                            mix[..., 1792:2048], merge], axis=-1)


SMALL = ("norm_g", "f_bias", "sgu_w", "sgu_b", "sgu_ln_g", "sgu_ln_b", "short_conv_w", "conf_dw_w", "conf_dw_b",
         "conf_ln_g", "conf_ln_b")


def kernel(x, norm_g, w_in, f_bias, sgu_w, sgu_b, sgu_ln_g, sgu_ln_b, short_conv_w, conf_dw_w, conf_dw_b, conf_ln_g, conf_ln_b, w_branch, w_out, final_g, loss_target, m_norm_g, m_w_in, m_f_bias, m_sgu_w, m_sgu_b, m_sgu_ln_g, m_sgu_ln_b, m_short_conv_w, m_conf_dw_w, m_conf_dw_b, m_conf_ln_g, m_conf_ln_b, m_w_branch, m_w_out, m_final_g, v_norm_g, v_w_in, v_f_bias, v_sgu_w, v_sgu_b, v_sgu_ln_g, v_sgu_ln_b, v_short_conv_w, v_conf_dw_w, v_conf_dw_b, v_conf_ln_g, v_conf_ln_b, v_w_branch, v_w_out, v_final_g):
    me = 4 * lax.axis_index("x") + 2 * lax.axis_index("y") + lax.axis_index("c")
    rows = D_MODEL // N_DEV
    cshard = BR // N_DEV

    sh = []
    for l in range(DEPTH):
        mix, merge, f = _split_w_in(w_in[l])
        sh.append(dict(mix=mix.astype(BF16), merge=merge.astype(BF16),
                       f=jnp.pad(f, ((0, 0), (0, N_F - N_HEADS))).astype(BF16),
                       wb=w_branch[l].astype(BF16), wo=w_out[l].astype(BF16)))
    conv_slab, conv_spans = _pack([short_conv_w, conf_dw_w])
    g_mix, g_f, g_conv = _exchange([(sh[0]["mix"], True), (sh[0]["f"], True), (conv_slab, True)], "gather_first")
    conv_full = [_unpack(g_conv[d], conv_spans) for d in range(N_DEV)]
    scw_full = jnp.concatenate([cf[0] for cf in conv_full], axis=-1)
    dww_full = jnp.concatenate([cf[1] for cf in conv_full], axis=-1)
    layers = [_prep_layer_small(norm_g[l], f_bias[l], sgu_w[l], sgu_b[l], sgu_ln_g[l], sgu_ln_b[l], scw_full[l], dww_full[l],
                                conf_dw_b[l], conf_ln_g[l], conf_ln_b[l]) for l in range(DEPTH)]
    layers[0].update(w_mix=g_mix.reshape(D_MODEL, N_MIX), w_f=g_f.reshape(D_MODEL, N_F))

    def put_in(l):
        def sink(res):
            layers[l].update(w_mix=res[0].reshape(D_MODEL, N_MIX), w_f=res[1].reshape(D_MODEL, N_F),
                             w_merge=res[2].reshape(D_MODEL, N_MERGE))
        return sink

    def put_merge(l):
        def sink(res):
            layers[l].update(w_merge=res[0].reshape(D_MODEL, N_MERGE))
        return sink

    def put_out(l):
        def sink(res):
            layers[l].update(wb=res[0].transpose(1, 2, 0, 3).reshape(N_BRANCH, BR, D_MODEL), wo=res[1].reshape(D_MODEL, D_MODEL))
        return sink

    attach0 = {
        "proj_mix": ([(sh[0]["merge"], True)], put_merge(0)),
        "mix_fwd": ([(sh[0]["wb"], True), (sh[0]["wo"], True)], put_out(0)),
        "attn_fwd": ([(sh[1]["mix"], True), (sh[1]["f"], True), (sh[1]["merge"], True)], put_in(1)),
        "merge_fwd": ([(sh[1]["wb"], True), (sh[1]["wo"], True)], put_out(1)),
    }

    xs = x[0]
    xs, sv0 = _layer_fwd(xs, layers[0], "0", attach0)
    xs, sv1 = _layer_fwd(xs, layers[1], "1")
    loss_p, dx, dfg = _loss_head(xs, final_g.reshape(1, D_MODEL), loss_target[0])
    loss = lax.psum(0.5 / D_MODEL * jnp.sum(loss_p), ("x", "y", "c"))
    dx, g1, early1, (late1, spans1), (last1, lspans1), _ = _layer_bwd(dx, sv1, layers[1], "1", True, None, [dfg.sum(0)])
    dx, g0, early0, (late0, spans0), (last0, lspans0), last1_out = _layer_bwd(dx, sv0, layers[0], "0", True, last1, [])
    last0_out = _exchange(last0, "gather_last")

    red, small = [], []
    for l, (early, late, spans, last, lspans) in enumerate(((early0, late0, spans0, last0_out, lspans0),
                                                            (early1, late1, spans1, last1_out, lspans1))):
        t = str(l)
        red.append(dict(merge=_sum8(early[0], "sum_merge" + t), wb=_sum8(early[1], "sum_wb" + t), wo=_sum8(early[2], "sum_wo" + t),
                        mix=_sum8(late[0], "sum_mix" + t), f=_sum8(late[1], "sum_f" + t)))
        keys = ["norm_g"] + (["final_g"] if l == DEPTH - 1 else []) + list(SMALL[1:])
        small.append(dict(zip(keys, _unpack(_sum8(last[0], "sum_last" + t), lspans)
                              + _unpack(_sum8(late[2], "sum_small" + t), spans))))
    gs = {nm: jnp.stack([small[l][nm] for l in range(DEPTH)]) for nm in SMALL}
    gs["final_g"] = small[DEPTH - 1]["final_g"]
    gs["short_conv_w"] = lax.dynamic_slice_in_dim(gs["short_conv_w"], me * cshard, cshard, axis=2)
    gs["conf_dw_w"] = lax.dynamic_slice_in_dim(gs["conf_dw_w"], me * cshard, cshard, axis=2)
    g_w_in = jnp.stack([_join_w_in(red[l]["mix"], red[l]["merge"], red[l]["f"][:, :N_HEADS]) for l in range(DEPTH)])
    g_w_branch = jnp.stack([red[l]["wb"].reshape(N_BRANCH, BR, rows) for l in range(DEPTH)])
    g_w_out = jnp.stack([red[l]["wo"] for l in range(DEPTH)])

    d_w_in, nm_w_in, nv_w_in = _adamw(w_in, g_w_in, m_w_in, v_w_in, "adamw_w_in")
    flat = lambda a: a.reshape(DEPTH, N_BRANCH * BR, rows)
    d_w_branch, nm_w_branch, nv_w_branch = (a.reshape(w_branch.shape) for a in _adamw(
        flat(w_branch), flat(g_w_branch), flat(m_w_branch), flat(v_w_branch), "adamw_w_branch"))
    d_w_out, nm_w_out, nv_w_out = _adamw(w_out, g_w_out, m_w_out, v_w_out, "adamw_w_out")
    names = SMALL + ("final_g",)
    ws = dict(zip(names, (norm_g, f_bias, sgu_w, sgu_b, sgu_ln_g, sgu_ln_b, short_conv_w, conf_dw_w, conf_dw_b, conf_ln_g,
                          conf_ln_b, final_g)))
    ms = dict(zip(names, (m_norm_g, m_f_bias, m_sgu_w, m_sgu_b, m_sgu_ln_g, m_sgu_ln_b, m_short_conv_w, m_conf_dw_w,
                          m_conf_dw_b, m_conf_ln_g, m_conf_ln_b, m_final_g)))
    vs = dict(zip(names, (v_norm_g, v_f_bias, v_sgu_w, v_sgu_b, v_sgu_ln_g, v_sgu_ln_b, v_short_conv_w, v_conf_dw_w,
                          v_conf_dw_b, v_conf_ln_g, v_conf_ln_b, v_final_g)))
    w_slab, spans = _pack([ws[nm] for nm in names])
    g_slab, _ = _pack([gs[nm] for nm in names])
    m_slab, _ = _pack([ms[nm] for nm in names])
    v_slab, _ = _pack([vs[nm] for nm in names])
    d_s, nm_s, nv_s = (dict(zip(names, _unpack(a[0], spans))) for a in _adamw(w_slab[None], g_slab[None], m_slab[None],
                                                                              v_slab[None], "adamw_small"))

    def ordered(small, w_in_v, w_branch_v, w_out_v):
        return [small["norm_g"], w_in_v, small["f_bias"], small["sgu_w"], small["sgu_b"], small["sgu_ln_g"],
                small["sgu_ln_b"], small["short_conv_w"], small["conf_dw_w"], small["conf_dw_b"], small["conf_ln_g"],
                small["conf_ln_b"], w_branch_v, w_out_v, small["final_g"]]

    return (loss, dx[None], *ordered(gs, g_w_in, g_w_branch, g_w_out), *ordered(d_s, d_w_in, d_w_branch, d_w_out),
            *ordered(nm_s, nm_w_in, nm_w_branch, nm_w_out), *ordered(nv_s, nv_w_in, nv_w_branch, nv_w_out))
```

```python
import functools
import math

import jax
import jax.numpy as jnp
from jax import lax
from jax.experimental import pallas as pl
from jax.experimental.pallas import tpu as pltpu

F32 = jnp.float32
BF16 = jnp.bfloat16

D_MODEL = 1024
DEPTH = 2
N_BRANCH = 4
BR = 256
N_HEADS = 4
HEAD_DIM = 64
CHUNK = 128
SHORT_CONV = 3
CONF_CONV = 31
EPS = 1e-6
N_DEV = 8

ADAM_LR = 0.001
ADAM_B1 = 0.9
ADAM_B2 = 0.999
ADAM_EPS = 1e-08
ADAM_WD = 0.01
ADAM_STEP = 10

O_UV, O_AG, O_QKV, O_BG, O_DG, O_CIN, O_CG, O_GLU = 0, 512, 768, 1536, 1792, 2048, 2816, 3072
N_MIX = 3584
N_MERGE = N_BRANCH * D_MODEL
N_F = 128
IN_COLS = 7684
HALO = 32
TM = 512
TMG = 256
BQ = 512
SUB = 64
CUMB = 512
VMEM_LIMIT = 56 * 1024 * 1024
NEG = -1e30
SCALE = 1.0 / math.sqrt(HEAD_DIM)
GELU_K = math.sqrt(2.0 / math.pi)


def _cp(sem=None):
    return pltpu.CompilerParams(dimension_semantics=sem, vmem_limit_bytes=VMEM_LIMIT)


PEER_ORDER = (6, 4, 2, 7, 5, 3, 1)
RELAYED = (3, 5, 7)


def _xchg(cin, cout, send, recv, loc, modes, start):
    x, y, c = lax.axis_index("x"), lax.axis_index("y"), lax.axis_index("c")
    me = 4 * x + 2 * y + c

    def peer_of(kk):
        px, py, pc = lax.rem(x + (kk >> 2 & 1), 2), lax.rem(y + (kk >> 1 & 1), 2), lax.rem(c + (kk & 1), 2)
        return (px, py, pc), 4 * px + 2 * py + pc

    def remote(src, dst, a, kk, pid):
        return pltpu.make_async_remote_copy(src_ref=src, dst_ref=dst, send_sem=send.at[a, kk], recv_sem=recv.at[a, kk],
                                            device_id=pid, device_id_type=pl.DeviceIdType.MESH)

    def outgoing(a, kk):
        if modes[a] and kk in RELAYED:
            _, origin = peer_of(kk - 1)
            return remote(cout[a].at[origin], cout[a].at[origin], a, kk, peer_of(1)[0])
        pid, peer = peer_of(kk)
        return remote(cin[a] if modes[a] else cin[a].at[peer], cout[a].at[me], a, kk, pid)

    def arrival(a, kk):
        _, peer = peer_of(kk)
        return remote(cout[a].at[peer], cout[a].at[peer], a, kk, (x, y, c))

    for a, gather in enumerate(modes):
        cp = pltpu.make_async_copy(cin[a] if gather else cin[a].at[me], cout[a].at[me], loc.at[a])
        if start:
            cp.start()
        else:
            cp.wait()
    if start:
        for kk in PEER_ORDER:
            for a, gather in enumerate(modes):
                if not (gather and kk in RELAYED):
                    outgoing(a, kk).start()
        return
    for kk in RELAYED:
        for a, gather in enumerate(modes):
            if gather:
                arrival(a, kk - 1).wait_recv()
                outgoing(a, kk).start()
    for kk in PEER_ORDER:
        for a in range(len(modes)):
            outgoing(a, kk).wait_send()
    for kk in PEER_ORDER:
        for a, gather in enumerate(modes):
            if not (gather and kk + 1 in RELAYED):
                arrival(a, kk).wait_recv()


def _xchg_shapes(comm):
    return [jax.ShapeDtypeStruct((N_DEV,) + tuple(a.shape[(0 if gather else 1):]), a.dtype) for a, gather in comm]


def _xchg_sems(n):
    return [pltpu.SemaphoreType.DMA((n, N_DEV)), pltpu.SemaphoreType.DMA((n, N_DEV)), pltpu.SemaphoreType.DMA((n,))]


def _exchange(comm, name):
    n = len(comm)
    modes = [g for _, g in comm]

    def body(*refs):
        cin, cout, (send, recv, loc) = refs[:n], refs[n:2 * n], refs[2 * n:]
        _xchg(cin, cout, send, recv, loc, modes, True)
        _xchg(cin, cout, send, recv, loc, modes, False)

    anyspec = pl.BlockSpec(memory_space=pl.ANY)
    return pl.pallas_call(
        body, in_specs=[anyspec] * n, out_specs=[anyspec] * n, out_shape=_xchg_shapes(comm),
        scratch_shapes=_xchg_sems(n), name=name)(*[a for a, _ in comm])


def _pcall(body, *, grid, in_specs, out_specs, out_shape, operands, name, scratch_shapes=(), comm=None):
    if not comm:
        outs = pl.pallas_call(
            body, grid=grid, in_specs=in_specs, out_specs=out_specs, out_shape=out_shape, scratch_shapes=list(scratch_shapes),
            compiler_params=_cp(("arbitrary",) * len(grid)), name=name)(*operands)
        return list(outs), []
    n, nin, nout, nsc = len(comm), len(operands), len(out_shape), len(scratch_shapes)
    modes = [g for _, g in comm]

    def wrapped(*refs):
        ins, cin = refs[:nin], refs[nin:nin + n]
        outs, cout = refs[nin + n:nin + n + nout], refs[nin + n + nout:nin + 2 * n + nout]
        scratch = refs[nin + 2 * n + nout:]
        own, (send, recv, loc) = scratch[:nsc], scratch[nsc:]
        ids = [pl.program_id(d) for d in range(len(grid))]
        first = functools.reduce(jnp.logical_and, [i == 0 for i in ids])
        last = functools.reduce(jnp.logical_and, [i == g - 1 for i, g in zip(ids, grid)])

        @pl.when(first)
        def _():
            _xchg(cin, cout, send, recv, loc, modes, True)

        body(*ins, *outs, *own)

        @pl.when(last)
        def _():
            _xchg(cin, cout, send, recv, loc, modes, False)

    anyspec = pl.BlockSpec(memory_space=pl.ANY)
    res = pl.pallas_call(
        wrapped, grid=grid, in_specs=list(in_specs) + [anyspec] * n, out_specs=list(out_specs) + [anyspec] * n,
        out_shape=list(out_shape) + _xchg_shapes(comm), scratch_shapes=list(scratch_shapes) + _xchg_sems(n),
        compiler_params=_cp(("arbitrary",) * len(grid)), name=name)(*operands, *[a for a, _ in comm])
    return list(res[:nout]), list(res[nout:])


def _sig(x):
    return 1.0 / (1.0 + jnp.exp(-x))


def _silu(x):
    return x * _sig(x)


def _dsilu(x):
    s = _sig(x)
    return s * (1.0 + x * (1.0 - s))


def _gelu(x):
    return 0.5 * x * (1.0 + jnp.tanh(GELU_K * (x + 0.044715 * x * x * x)))


def _dgelu(x):
    t = jnp.tanh(GELU_K * (x + 0.044715 * x * x * x))
    return 0.5 * (1.0 + t) + 0.5 * x * (1.0 - t * t) * GELU_K * (1.0 + 3.0 * 0.044715 * x * x)


def _ln_hat(x):
    mu = jnp.mean(x, axis=-1, keepdims=True)
    xc = x - mu
    rs = lax.rsqrt(jnp.mean(xc * xc, axis=-1, keepdims=True) + EPS)
    return xc * rs, rs


def _ln_bwd(dhat, hat, rs):
    return rs * (dhat - jnp.mean(dhat, axis=-1, keepdims=True) - hat * jnp.mean(dhat * hat, axis=-1, keepdims=True))


def _dot(a, b):
    return jnp.dot(a, b, preferred_element_type=F32)


def _dot_nt(a, b):
    return lax.dot_general(a, b, (((1,), (1,)), ((), ())), preferred_element_type=F32)


def _dot_tn(a, b):
    return lax.dot_general(a, b, (((0,), (0,)), ((), ())), preferred_element_type=F32)


def _fold8(x):
    acc = x[0:8]
    for r in range(1, x.shape[0] // 8):
        acc = acc + x[8 * r:8 * r + 8]
    return acc


def _rms_fwd(x, g, name):
    s = x.shape[0]

    def body(x_ref, g_ref, h_ref):
        xv = x_ref[...]
        r = lax.rsqrt(jnp.mean(xv * xv, axis=-1, keepdims=True) + EPS)
        h_ref[...] = (xv * r * g_ref[...]).astype(BF16)

    return pl.pallas_call(
        body, grid=(s // TM,),
        in_specs=[pl.BlockSpec((TM, D_MODEL), lambda i: (i, 0)), pl.BlockSpec((1, D_MODEL), lambda i: (0, 0))],
        out_specs=pl.BlockSpec((TM, D_MODEL), lambda i: (i, 0)),
        out_shape=jax.ShapeDtypeStruct((s, D_MODEL), BF16), compiler_params=_cp(("parallel",)), name=name)(x, g)


def _rms_bwd(dh, x, g, dx_next, name):
    s = x.shape[0]

    def body(dh_ref, x_ref, g_ref, dxn_ref, dx_ref, dg_ref):
        i = pl.program_id(0)
        xv = x_ref[...]
        r = lax.rsqrt(jnp.mean(xv * xv, axis=-1, keepdims=True) + EPS)
        xn = xv * r
        dhv = dh_ref[...]
        dxn = dhv * g_ref[...]
        dx_ref[...] = dxn_ref[...] + r * (dxn - xn * jnp.mean(dxn * xn, axis=-1, keepdims=True))

        @pl.when(i == 0)
        def _():
            dg_ref[...] = jnp.zeros_like(dg_ref)

        dg_ref[...] += _fold8(dhv * xn)

    tile = pl.BlockSpec((TM, D_MODEL), lambda i: (i, 0))
    return pl.pallas_call(
        body, grid=(s // TM,),
        in_specs=[tile, tile, pl.BlockSpec((1, D_MODEL), lambda i: (0, 0)), tile],
        out_specs=[tile, pl.BlockSpec((8, D_MODEL), lambda i: (0, 0))],
        out_shape=[jax.ShapeDtypeStruct((s, D_MODEL), F32), jax.ShapeDtypeStruct((8, D_MODEL), F32)],
        compiler_params=_cp(("arbitrary",)), name=name)(dh, x, g, dx_next)


def _loss_head(x, g, target):
    s = x.shape[0]

    def body(x_ref, g_ref, t_ref, loss_ref, dx_ref, dg_ref):
        i = pl.program_id(0)
        xv = x_ref[...]
        r = lax.rsqrt(jnp.mean(xv * xv, axis=-1, keepdims=True) + EPS)
        xn = xv * r
        err = xn * g_ref[...] - t_ref[...]
        dy = err * (1.0 / D_MODEL)
        dxn = dy * g_ref[...]
        dx_ref[...] = r * (dxn - xn * jnp.mean(dxn * xn, axis=-1, keepdims=True))

        @pl.when(i == 0)
        def _():
            dg_ref[...] = jnp.zeros_like(dg_ref)
            loss_ref[...] = jnp.zeros_like(loss_ref)

        dg_ref[...] += _fold8(dy * xn)
        loss_ref[...] += _fold8(err * err)

    tile = pl.BlockSpec((TM, D_MODEL), lambda i: (i, 0))
    acc = pl.BlockSpec((8, D_MODEL), lambda i: (0, 0))
    return pl.pallas_call(
        body, grid=(s // TM,),
        in_specs=[tile, pl.BlockSpec((1, D_MODEL), lambda i: (0, 0)), tile],
        out_specs=[acc, tile, acc],
        out_shape=[jax.ShapeDtypeStruct((8, D_MODEL), F32), jax.ShapeDtypeStruct((s, D_MODEL), F32),
                   jax.ShapeDtypeStruct((8, D_MODEL), F32)],
        compiler_params=_cp(("arbitrary",)), name="loss_head")(x, g, target)


def _mm_nn(a, b, tn, name, comm=None):
    m, k = a.shape
    n = b.shape[1]
    tm = 512

    def body(a_ref, b_ref, o_ref):
        o_ref[...] = _dot(a_ref[...], b_ref[...])

    (out,), couts = _pcall(
        body, grid=(n // tn, m // tm),
        in_specs=[pl.BlockSpec((tm, k), lambda j, i: (i, 0)), pl.BlockSpec((k, tn), lambda j, i: (0, j))],
        out_specs=[pl.BlockSpec((tm, tn), lambda j, i: (i, j))],
        out_shape=[jax.ShapeDtypeStruct((m, n), F32)], operands=(a, b), name=name, comm=comm)
    return out, couts


def _dh(dpm, dpg, dpf, w_mix, w_merge, w_f, name, comm=None):
    s = dpm.shape[0]
    tm = 1024 if s % 1024 == 0 else 512
    tk = 512
    n1, n2 = N_MIX // tk, N_MERGE // tk

    def body(dpm_ref, dpg_ref, dpf_ref, wm_ref, wg_ref, wf_ref, o_ref):
        j = pl.program_id(1)

        @pl.when(j == 0)
        def _():
            o_ref[...] = _dot_nt(dpf_ref[...], wf_ref[...])

        @pl.when(j < n1)
        def _():
            o_ref[...] += _dot_nt(dpm_ref[...], wm_ref[...])

        @pl.when(j >= n1)
        def _():
            o_ref[...] += _dot_nt(dpg_ref[...], wg_ref[...])

    mix_j = lambda j: jnp.minimum(j, n1 - 1)
    merge_j = lambda j: jnp.maximum(j - n1, 0)
    (out,), couts = _pcall(
        body, grid=(s // tm, n1 + n2),
        in_specs=[pl.BlockSpec((tm, tk), lambda i, j: (i, mix_j(j))), pl.BlockSpec((tm, tk), lambda i, j: (i, merge_j(j))),
                  pl.BlockSpec((tm, N_F), lambda i, j: (i, 0)),
                  pl.BlockSpec((D_MODEL, tk), lambda i, j: (0, mix_j(j))), pl.BlockSpec((D_MODEL, tk), lambda i, j: (0, merge_j(j))),
                  pl.BlockSpec((D_MODEL, N_F), lambda i, j: (0, 0))],
        out_specs=[pl.BlockSpec((tm, D_MODEL), lambda i, j: (i, 0))],
        out_shape=[jax.ShapeDtypeStruct((s, D_MODEL), F32)], operands=(dpm, dpg, dpf, w_mix, w_merge, w_f), name=name, comm=comm)
    return out, couts


def _mm_tn(a, d, tn, name):
    m, k = a.shape
    n = d.shape[1]
    tm = 512
    nm = m // tm

    def body(a_ref, d_ref, o_ref, acc):
        i = pl.program_id(1)

        @pl.when(i == 0)
        def _():
            acc[...] = jnp.zeros_like(acc)

        acc[...] += _dot_tn(a_ref[...], d_ref[...])

        @pl.when(i == nm - 1)
        def _():
            o_ref[...] = acc[...].astype(BF16)

    return pl.pallas_call(
        body, grid=(n // tn, nm),
        in_specs=[pl.BlockSpec((tm, k), lambda j, i: (i, 0)), pl.BlockSpec((tm, tn), lambda j, i: (i, j))],
        out_specs=pl.BlockSpec((k, tn), lambda j, i: (0, j)),
        out_shape=jax.ShapeDtypeStruct((k, n), BF16), scratch_shapes=[pltpu.VMEM((k, tn), F32)],
        compiler_params=_cp(("parallel", "arbitrary")), name=name)(a, d)


def _lane_head():
    return lax.broadcasted_iota(jnp.int32, (1, BR), 1) // HEAD_DIM


def _gmlp_chunk_fwd(p_ref, r0, gw_ref, gb_ref, lg, lb):
    uv = p_ref[r0:r0 + CHUNK, O_UV:O_UV + 2 * BR]
    u = _gelu(uv[:, :BR])
    vhat, rs = _ln_hat(_gelu(uv[:, BR:]))
    vn = (vhat * lg + lb).astype(BF16)
    head = _lane_head()
    mixed = gb_ref[...]
    for h in range(N_HEADS):
        mixed = mixed + jnp.where(head == h, _dot(gw_ref[h], vn), 0.0)
    return uv, u, vhat, rs, vn, mixed


def _tap_groups(k_width):
    groups = []
    for b in range(8):
        taps = [(d // 8, k_width - 1 - d) for d in range(b, k_width, 8)]
        if taps:
            groups.append((b, taps))
    return groups


def _causal_taps(buf, off, r0, nr, k_width):
    lead = 8 * ((k_width - 1) // 8 + 1)
    win = buf[off + r0 - lead:off + r0 + nr, :]
    for b, taps in _tap_groups(k_width):
        shifted = win if b == 0 else pltpu.roll(win, b, 0)
        for a, k in taps:
            yield k, shifted[lead - 8 * a:lead - 8 * a + nr]


def _anticausal_taps(buf, r0, nr, k_width):
    lead = 8 * ((k_width - 1) // 8 + 1)
    win = buf[r0:r0 + nr + lead, :]
    for b, taps in _tap_groups(k_width):
        shifted = win if b == 0 else pltpu.roll(win, nr + lead - b, 0)
        for a, k in taps:
            yield k, shifted[8 * a:8 * a + nr]


def _conv_sub_blocks(rows):
    out = [(r, SUB) for r in range(0, rows - rows % SUB, SUB)]
    if rows % SUB:
        out.append((rows - rows % SUB, rows % SUB))
    return out


def _mix_fwd(pm, wl, name, comm=None):
    s = pm.shape[0]
    nt = s // TM

    def body(p_ref, ph_ref, gw_ref, gb_ref, lg_ref, lb_ref, scw_ref, dww_ref, dwb_ref, clg_ref, clb_ref,
             ya_ref, yc_ref, yd_ref, q_ref, k_ref, v_ref, zbuf, hbuf):
        i = pl.program_id(0)
        for h in range(N_HEADS):
            c0 = O_QKV + h * HEAD_DIM
            q_ref[h] = (p_ref[:, c0:c0 + HEAD_DIM] * SCALE).astype(BF16)
            k_ref[h] = p_ref[:, c0 + BR:c0 + BR + HEAD_DIM].astype(BF16)
            v_ref[h] = p_ref[:, c0 + 2 * BR:c0 + 2 * BR + HEAD_DIM].astype(BF16)
        lg = lg_ref[...]
        lb = lb_ref[...]
        for c in range(TM // CHUNK):
            r0 = c * CHUNK
            _, u, _, _, _, mixed = _gmlp_chunk_fwd(p_ref, r0, gw_ref, gb_ref, lg, lb)
            ag = p_ref[r0:r0 + CHUNK, O_AG:O_AG + BR]
            ya_ref[r0:r0 + CHUNK, :] = (u * mixed * _silu(ag)).astype(BF16)

        first = i > 0
        zbuf[0:HALO, :] = jnp.where(first, ph_ref[:, O_CIN + BR:O_CIN + 2 * BR] * ph_ref[:, O_CIN + 2 * BR:O_CIN + 3 * BR], 0.0)
        zbuf[HALO:HALO + TM, :] = p_ref[:, O_CIN + BR:O_CIN + 2 * BR] * p_ref[:, O_CIN + 2 * BR:O_CIN + 3 * BR]
        hbuf[0:HALO, :] = jnp.where(first, ph_ref[:, O_GLU:O_GLU + BR] * _sig(ph_ref[:, O_GLU + BR:O_GLU + 2 * BR]), 0.0)
        hbuf[HALO:HALO + TM, :] = p_ref[:, O_GLU:O_GLU + BR] * _sig(p_ref[:, O_GLU + BR:O_GLU + 2 * BR])

        clg = clg_ref[...]
        clb = clb_ref[...]
        for r0, nr in _conv_sub_blocks(TM):
            yc = jnp.zeros((nr, BR), F32)
            for k, zk in _causal_taps(zbuf, HALO, r0, nr, SHORT_CONV):
                yc = yc + scw_ref[k:k + 1, :] * zk
            bgate = p_ref[r0:r0 + nr, O_CIN:O_CIN + BR]
            cg = p_ref[r0:r0 + nr, O_CG:O_CG + BR]
            yc_ref[r0:r0 + nr, :] = (bgate * yc * _silu(cg)).astype(BF16)

            cc = jnp.zeros((nr, BR), F32) + dwb_ref[...]
            for k, hk in _causal_taps(hbuf, HALO, r0, nr, CONF_CONV):
                cc = cc + dww_ref[k:k + 1, :] * hk
            chat, _ = _ln_hat(cc)
            dg = p_ref[r0:r0 + nr, O_DG:O_DG + BR]
            yd_ref[r0:r0 + nr, :] = (_silu(chat * clg + clb) * _silu(dg)).astype(BF16)

    full = lambda shape: pl.BlockSpec(shape, lambda i: tuple(0 for _ in shape))
    ytile = pl.BlockSpec((TM, BR), lambda i: (i, 0))
    yshape = jax.ShapeDtypeStruct((s, BR), BF16)
    htile = pl.BlockSpec((N_HEADS, TM, HEAD_DIM), lambda i: (0, i, 0))
    hshape = jax.ShapeDtypeStruct((N_HEADS, s, HEAD_DIM), BF16)
    return _pcall(
        body, grid=(nt,),
        in_specs=[pl.BlockSpec((TM, N_MIX), lambda i: (i, 0)),
                  pl.BlockSpec((HALO, N_MIX), lambda i: (jnp.maximum(i * (TM // HALO) - 1, 0), 0)),
                  full((N_HEADS, CHUNK, CHUNK)), full((CHUNK, BR)), full((1, BR)), full((1, BR)),
                  full((8, BR)), full((32, BR)), full((1, BR)), full((1, BR)), full((1, BR))],
        out_specs=[ytile, ytile, ytile, htile, htile, htile],
        out_shape=[yshape, yshape, yshape, hshape, hshape, hshape],
        scratch_shapes=[pltpu.VMEM((HALO + TM, BR), F32), pltpu.VMEM((HALO + TM, BR), F32)],
        name=name, comm=comm, operands=(
            pm, pm, wl["gw"], wl["gb"], wl["sgu_ln_g"], wl["sgu_ln_b"], wl["scw"], wl["dww"], wl["conf_dw_b"],
            wl["conf_ln_g"], wl["conf_ln_b"]))


def _mix_bwd(pm, dya, dyc, dyd, dqkv, dbg, wl, name):
    s = pm.shape[0]
    nt = s // TM
    ext = TM + HALO

    def body(p_ref, ph_ref, pn_ref, dya_ref, dyc_ref, dycn_ref, dyd_ref, dydn_ref, dq_ref, dk_ref, dv_ref, dbg_ref,
             gw_ref, gwt_ref, gb_ref, lg_ref, lb_ref, scw_ref, dww_ref, dwb_ref, clg_ref, clb_ref,
             dp_ref, dgw_ref, dgb_ref, vec_ref, dscw_ref, ddww_ref, zbuf, dcb, hbuf, dcc):
        i = pl.program_id(0)

        @pl.when(i == 0)
        def _():
            dgw_ref[...] = jnp.zeros_like(dgw_ref)
            dgb_ref[...] = jnp.zeros_like(dgb_ref)
            vec_ref[...] = jnp.zeros_like(vec_ref)
            dscw_ref[...] = jnp.zeros_like(dscw_ref)
            ddww_ref[...] = jnp.zeros_like(ddww_ref)

        lg = lg_ref[...]
        lb = lb_ref[...]
        head = _lane_head()
        d_lg = jnp.zeros((1, BR), F32)
        d_lb = jnp.zeros((1, BR), F32)
        for c in range(TM // CHUNK):
            r0 = c * CHUNK
            uv, u, vhat, rs, vn, mixed = _gmlp_chunk_fwd(p_ref, r0, gw_ref, gb_ref, lg, lb)
            ag = p_ref[r0:r0 + CHUNK, O_AG:O_AG + BR]
            dy = dya_ref[r0:r0 + CHUNK, :]
            sa = _silu(ag)
            du = dy * mixed * sa
            dmx = dy * u * sa
            dp_ref[r0:r0 + CHUNK, O_AG:O_AG + BR] = (dy * u * mixed * _dsilu(ag)).astype(BF16)
            dgb_ref[...] += dmx
            dmx_b = dmx.astype(BF16)
            dvn = jnp.zeros((CHUNK, BR), F32)
            for h in range(N_HEADS):
                sel = head == h
                dgw_ref[h] += _dot_nt(jnp.where(sel, dmx, 0.0).astype(BF16), vn)
                dvn = dvn + jnp.where(sel, _dot(gwt_ref[h], dmx_b), 0.0)
            d_lg = d_lg + jnp.sum(dvn * vhat, axis=0, keepdims=True)
            d_lb = d_lb + jnp.sum(dvn, axis=0, keepdims=True)
            dv0 = _ln_bwd(dvn * lg, vhat, rs)
            dp_ref[r0:r0 + CHUNK, O_UV:O_UV + BR] = (du * _dgelu(uv[:, :BR])).astype(BF16)
            dp_ref[r0:r0 + CHUNK, O_UV + BR:O_UV + 2 * BR] = (dv0 * _dgelu(uv[:, BR:])).astype(BF16)
        vec_ref[0:1, :] += d_lg
        vec_ref[1:2, :] += d_lb

        for j, g_ref in enumerate((dq_ref, dk_ref, dv_ref)):
            dp_ref[:, O_QKV + j * BR:O_QKV + (j + 1) * BR] = jnp.concatenate(
                [g_ref[h] for h in range(N_HEADS)], axis=1).astype(BF16)
        dp_ref[:, O_BG:O_BG + BR] = dbg_ref[...].astype(BF16)

        first = i > 0
        last = i < nt - 1
        zbuf[0:HALO, :] = jnp.where(first, ph_ref[:, O_CIN + BR:O_CIN + 2 * BR] * ph_ref[:, O_CIN + 2 * BR:O_CIN + 3 * BR], 0.0)
        zbuf[HALO:HALO + TM, :] = p_ref[:, O_CIN + BR:O_CIN + 2 * BR] * p_ref[:, O_CIN + 2 * BR:O_CIN + 3 * BR]
        dcb[0:TM, :] = dyc_ref[...] * p_ref[:, O_CIN:O_CIN + BR] * _silu(p_ref[:, O_CG:O_CG + BR])
        dcb[TM:ext, :] = jnp.where(last, dycn_ref[...] * pn_ref[:, O_CIN:O_CIN + BR] * _silu(pn_ref[:, O_CG:O_CG + BR]), 0.0)
        for r0, nr in _conv_sub_blocks(TM):
            yc = jnp.zeros((nr, BR), F32)
            dz = jnp.zeros((nr, BR), F32)
            dcur = dcb[r0:r0 + nr, :]
            for k, zk in _causal_taps(zbuf, HALO, r0, nr, SHORT_CONV):
                yc = yc + scw_ref[k:k + 1, :] * zk
                dscw_ref[8 * k:8 * k + 8, :] += _fold8(dcur * zk)
            for k, dk in _anticausal_taps(dcb, r0, nr, SHORT_CONV):
                dz = dz + scw_ref[k:k + 1, :] * dk
            dy = dyc_ref[r0:r0 + nr, :]
            bgate = p_ref[r0:r0 + nr, O_CIN:O_CIN + BR]
            cg = p_ref[r0:r0 + nr, O_CG:O_CG + BR]
            dp_ref[r0:r0 + nr, O_CIN:O_CIN + BR] = (dy * yc * _silu(cg)).astype(BF16)
            dp_ref[r0:r0 + nr, O_CIN + BR:O_CIN + 2 * BR] = (dz * p_ref[r0:r0 + nr, O_CIN + 2 * BR:O_CIN + 3 * BR]).astype(BF16)
            dp_ref[r0:r0 + nr, O_CIN + 2 * BR:O_CIN + 3 * BR] = (dz * p_ref[r0:r0 + nr, O_CIN + BR:O_CIN + 2 * BR]).astype(BF16)
            dp_ref[r0:r0 + nr, O_CG:O_CG + BR] = (dy * bgate * yc * _dsilu(cg)).astype(BF16)

        hbuf[0:HALO, :] = jnp.where(first, ph_ref[:, O_GLU:O_GLU + BR] * _sig(ph_ref[:, O_GLU + BR:O_GLU + 2 * BR]), 0.0)
        hbuf[HALO:HALO + TM, :] = p_ref[:, O_GLU:O_GLU + BR] * _sig(p_ref[:, O_GLU + BR:O_GLU + 2 * BR])
        hbuf[HALO + TM:HALO + ext, :] = jnp.where(last, pn_ref[:, O_GLU:O_GLU + BR] * _sig(pn_ref[:, O_GLU + BR:O_GLU + 2 * BR]), 0.0)
        clg = clg_ref[...]
        clb = clb_ref[...]
        d_clg = jnp.zeros((1, BR), F32)
        d_clb = jnp.zeros((1, BR), F32)
        d_dwb = jnp.zeros((1, BR), F32)
        for r0, nr in _conv_sub_blocks(ext):
            in_tile = r0 < TM
            cc = jnp.zeros((nr, BR), F32) + dwb_ref[...]
            for k, hk in _causal_taps(hbuf, HALO, r0, nr, CONF_CONV):
                cc = cc + dww_ref[k:k + 1, :] * hk
            chat, rs = _ln_hat(cc)
            ln = chat * clg + clb
            if in_tile:
                dy = dyd_ref[r0:r0 + nr, :]
                dg = p_ref[r0:r0 + nr, O_DG:O_DG + BR]
            else:
                dy = jnp.where(last, dydn_ref[...], 0.0)
                dg = pn_ref[:, O_DG:O_DG + BR]
            dln = dy * _silu(dg) * _dsilu(ln)
            dc = _ln_bwd(dln * clg, chat, rs)
            dcc[r0:r0 + nr, :] = dc
            if in_tile:
                dp_ref[r0:r0 + nr, O_DG:O_DG + BR] = (dy * _silu(ln) * _dsilu(dg)).astype(BF16)
                d_clg = d_clg + jnp.sum(dln * chat, axis=0, keepdims=True)
                d_clb = d_clb + jnp.sum(dln, axis=0, keepdims=True)
                d_dwb = d_dwb + jnp.sum(dc, axis=0, keepdims=True)
        vec_ref[2:3, :] += d_dwb
        vec_ref[3:4, :] += d_clg
        vec_ref[4:5, :] += d_clb
        for r0, nr in _conv_sub_blocks(TM):
            dcur = dcc[r0:r0 + nr, :]
            dhh = jnp.zeros((nr, BR), F32)
            for k, hk in _causal_taps(hbuf, HALO, r0, nr, CONF_CONV):
                ddww_ref[8 * k:8 * k + 8, :] += _fold8(dcur * hk)
            for k, dk in _anticausal_taps(dcc, r0, nr, CONF_CONV):
                dhh = dhh + dww_ref[k:k + 1, :] * dk
            a = p_ref[r0:r0 + nr, O_GLU:O_GLU + BR]
            sg = _sig(p_ref[r0:r0 + nr, O_GLU + BR:O_GLU + 2 * BR])
            dp_ref[r0:r0 + nr, O_GLU:O_GLU + BR] = (dhh * sg).astype(BF16)
            dp_ref[r0:r0 + nr, O_GLU + BR:O_GLU + 2 * BR] = (dhh * a * sg * (1.0 - sg)).astype(BF16)

    full = lambda shape: pl.BlockSpec(shape, lambda i: tuple(0 for _ in shape))
    rpt = TM // HALO
    prev_map = lambda i: (jnp.maximum(i * rpt - 1, 0), 0)
    next_map = lambda i: (jnp.minimum((i + 1) * rpt, nt * rpt - 1), 0)
    ytile = pl.BlockSpec((TM, BR), lambda i: (i, 0))
    htile = pl.BlockSpec((N_HEADS, TM, HEAD_DIM), lambda i: (0, i, 0))
    return pl.pallas_call(
        body, grid=(nt,),
        in_specs=[pl.BlockSpec((TM, N_MIX), lambda i: (i, 0)), pl.BlockSpec((HALO, N_MIX), prev_map),
                  pl.BlockSpec((HALO, N_MIX), next_map),
                  ytile, ytile, pl.BlockSpec((HALO, BR), next_map), ytile, pl.BlockSpec((HALO, BR), next_map),
                  htile, htile, htile, ytile,
                  full((N_HEADS, CHUNK, CHUNK)), full((N_HEADS, CHUNK, CHUNK)), full((CHUNK, BR)), full((1, BR)), full((1, BR)),
                  full((8, BR)), full((32, BR)), full((1, BR)), full((1, BR)), full((1, BR))],
        out_specs=[pl.BlockSpec((TM, N_MIX), lambda i: (i, 0)), full((N_HEADS, CHUNK, CHUNK)), full((CHUNK, BR)),
                   full((16, BR)), full((64, BR)), full((256, BR))],
        out_shape=[jax.ShapeDtypeStruct((s, N_MIX), BF16), jax.ShapeDtypeStruct((N_HEADS, CHUNK, CHUNK), F32),
                   jax.ShapeDtypeStruct((CHUNK, BR), F32), jax.ShapeDtypeStruct((16, BR), F32),
                   jax.ShapeDtypeStruct((64, BR), F32), jax.ShapeDtypeStruct((256, BR), F32)],
        scratch_shapes=[pltpu.VMEM((HALO + TM, BR), F32), pltpu.VMEM((ext, BR), F32),
                        pltpu.VMEM((HALO + ext, BR), F32), pltpu.VMEM((ext, BR), F32)],
        compiler_params=_cp(("arbitrary",)), name=name)(
            pm, pm, pm, dya, dyc, dyc, dyd, dyd, *dqkv, dbg,
            wl["gw"], wl["gwt"], wl["gb"], wl["sgu_ln_g"], wl["sgu_ln_b"], wl["scw"], wl["dww"], wl["conf_dw_b"],
            wl["conf_ln_g"], wl["conf_ln_b"])


def _tri(lower):
    r = lax.broadcasted_iota(jnp.int32, (CUMB, CUMB), 0)
    c = lax.broadcasted_iota(jnp.int32, (CUMB, CUMB), 1)
    return jnp.where((r >= c) if lower else (r <= c), 1.0, 0.0).astype(F32)


def _dot_hi(a, b):
    return jnp.dot(a, b, preferred_element_type=F32, precision=lax.Precision.HIGHEST)


def _cum_fwd(pf, fb, name):
    s = pf.shape[0]

    def body(pf_ref, fb_ref, cum_ref, carry):
        i = pl.program_id(0)

        @pl.when(i == 0)
        def _():
            carry[...] = jnp.zeros_like(carry)

        z = pf_ref[...] + fb_ref[...]
        logf = jnp.minimum(z, 0.0) - jnp.log(1.0 + jnp.exp(-jnp.abs(z)))
        cum_ref[...] = _dot_hi(_tri(True), logf) + carry[...]
        carry[...] += jnp.sum(logf, axis=0, keepdims=True)

    return pl.pallas_call(
        body, grid=(s // CUMB,),
        in_specs=[pl.BlockSpec((CUMB, N_F), lambda i: (i, 0)), pl.BlockSpec((1, N_F), lambda i: (0, 0))],
        out_specs=pl.BlockSpec((CUMB, N_F), lambda i: (i, 0)),
        out_shape=jax.ShapeDtypeStruct((s, N_F), F32),
        scratch_shapes=[pltpu.VMEM((1, N_F), F32)],
        compiler_params=_cp(("arbitrary",)), name=name)(pf, fb)


def _cum_bwd(dcq, dck, pf, fb, name):
    s = pf.shape[0]
    nb = s // CUMB

    def body(dcq_ref, dck_ref, pf_ref, fb_ref, dpf_ref, dfb_ref, carry):
        i = pl.program_id(0)

        @pl.when(i == 0)
        def _():
            carry[...] = jnp.zeros_like(carry)
            dfb_ref[...] = jnp.zeros_like(dfb_ref)

        lane = lax.broadcasted_iota(jnp.int32, (1, N_F), 1)
        dc = dck_ref[...]
        for h in range(N_HEADS):
            dc = dc + jnp.where(lane == h, dcq_ref[h], 0.0)
        dlogf = _dot_hi(_tri(False), dc) + carry[...]
        carry[...] += jnp.sum(dc, axis=0, keepdims=True)
        z = pf_ref[...] + fb_ref[...]
        dz = dlogf * (1.0 - _sig(z))
        dpf_ref[...] = dz.astype(BF16)
        dfb_ref[...] += _fold8(dz)

    rev = lambda i: (nb - 1 - i, 0)
    return pl.pallas_call(
        body, grid=(nb,),
        in_specs=[pl.BlockSpec((N_HEADS, CUMB, 1), lambda i: (0, nb - 1 - i, 0)), pl.BlockSpec((CUMB, N_F), rev),
                  pl.BlockSpec((CUMB, N_F), rev), pl.BlockSpec((1, N_F), lambda i: (0, 0))],
        out_specs=[pl.BlockSpec((CUMB, N_F), rev), pl.BlockSpec((8, N_F), lambda i: (0, 0))],
        out_shape=[jax.ShapeDtypeStruct((s, N_F), BF16), jax.ShapeDtypeStruct((8, N_F), F32)],
        scratch_shapes=[pltpu.VMEM((1, N_F), F32)],
        compiler_params=_cp(("arbitrary",)), name=name)(dcq, dck, pf, fb)


def _causal_mask():
    r = lax.broadcasted_iota(jnp.int32, (BQ, BQ), 0)
    c = lax.broadcasted_iota(jnp.int32, (BQ, BQ), 1)
    return r >= c


def _attn_fwd(q, k, v, cq, ck, name, comm=None):
    s = q.shape[1]
    nb = s // BQ

    def body(q_ref, k_ref, v_ref, cq_ref, ck_ref, o_ref, lse_ref):
        for qi in range(nb):
            qs = qi * BQ
            qb = q_ref[0, qs:qs + BQ, :]
            cqb = cq_ref[0, qs:qs + BQ, :]

            def block(kj, carry, masked):
                m, l, acc = carry
                ks = pl.multiple_of(kj * BQ, BQ)
                kb = k_ref[0, pl.ds(ks, BQ), :]
                vb = v_ref[0, pl.ds(ks, BQ), :]
                sc = _dot_nt(qb, kb) + (cqb - ck_ref[0, kj])
                if masked:
                    sc = jnp.where(_causal_mask(), sc, NEG)
                m_new = jnp.maximum(m, jnp.max(sc, axis=-1, keepdims=True))
                alpha = jnp.exp(m - m_new)
                p = jnp.exp(sc - m_new)
                l = alpha * l + jnp.sum(p, axis=-1, keepdims=True)
                acc = alpha * acc + _dot(p.astype(BF16), vb)
                return m_new, l, acc

            carry = (jnp.full((BQ, 1), NEG, F32), jnp.zeros((BQ, 1), F32), jnp.zeros((BQ, HEAD_DIM), F32))
            if qi > 0:
                carry = lax.fori_loop(0, qi, lambda kj, cr: block(kj, cr, False), carry)
            m, l, acc = block(qi, carry, True)
            o_ref[0, qs:qs + BQ, :] = acc / l
            lse_ref[0, qs:qs + BQ, :] = m + jnp.log(l)

    hblk = pl.BlockSpec((1, s, HEAD_DIM), lambda h: (h, 0, 0))
    cblk = pl.BlockSpec((1, s, 1), lambda h: (h, 0, 0))
    return _pcall(
        body, grid=(N_HEADS,),
        in_specs=[hblk, hblk, hblk, cblk, pl.BlockSpec((1, nb, 1, BQ), lambda h: (h, 0, 0, 0))],
        out_specs=[hblk, cblk],
        out_shape=[jax.ShapeDtypeStruct((N_HEADS, s, HEAD_DIM), F32), jax.ShapeDtypeStruct((N_HEADS, s, 1), F32)],
        operands=(q, k, v, cq, ck), name=name, comm=comm)


def _attn_bwd(q, k, v, cq, ck, o, lse, do, name, comm=None):
    s = q.shape[1]
    nb = s // BQ

    def body(q_ref, k_ref, v_ref, cq_ref, ck_ref, o_ref, lse_ref, do_ref, dq_ref, dk_ref, dv_ref, dcq_ref, dck_ref, delta):
        delta[...] = jnp.sum(do_ref[0] * o_ref[0], axis=-1, keepdims=True)
        dq_ref[...] = jnp.zeros_like(dq_ref)
        dcq_ref[...] = jnp.zeros_like(dcq_ref)
        for kj in range(nb):
            ks = kj * BQ
            kb = k_ref[0, ks:ks + BQ, :]
            vb = v_ref[0, ks:ks + BQ, :]
            ckb = ck_ref[0, kj]

            def block(qi, carry, masked):
                dk_acc, dv_acc, dck_acc = carry
                qs = pl.multiple_of(qi * BQ, BQ)
                qb = q_ref[0, pl.ds(qs, BQ), :]
                dob = do_ref[0, pl.ds(qs, BQ), :].astype(BF16)
                sc = _dot_nt(qb, kb) + (cq_ref[0, pl.ds(qs, BQ), :] - ckb)
                p = jnp.exp(sc - lse_ref[0, pl.ds(qs, BQ), :])
                if masked:
                    p = jnp.where(_causal_mask(), p, 0.0)
                dp = _dot_nt(dob, vb)
                ds = p * (dp - delta[pl.ds(qs, BQ), :])
                ds_b = ds.astype(BF16)
                dv_acc = dv_acc + _dot_tn(p.astype(BF16), dob)
                dk_acc = dk_acc + _dot_tn(ds_b, qb)
                dq_ref[0, pl.ds(qs, BQ), :] += _dot(ds_b, kb) * SCALE
                dcq_ref[0, pl.ds(qs, BQ), :] += jnp.sum(ds, axis=-1, keepdims=True)
                dck_acc = dck_acc - jnp.sum(ds, axis=0, keepdims=True)
                return dk_acc, dv_acc, dck_acc

            carry = (jnp.zeros((BQ, HEAD_DIM), F32), jnp.zeros((BQ, HEAD_DIM), F32), jnp.zeros((1, BQ), F32))
            carry = block(kj, carry, True)
            if kj < nb - 1:
                carry = lax.fori_loop(kj + 1, nb, lambda qi, cr: block(qi, cr, False), carry)
            dk_ref[0, ks:ks + BQ, :] = carry[0]
            dv_ref[0, ks:ks + BQ, :] = carry[1]
            dck_ref[0, kj] = carry[2]

    hblk = pl.BlockSpec((1, s, HEAD_DIM), lambda h: (h, 0, 0))
    cblk = pl.BlockSpec((1, s, 1), lambda h: (h, 0, 0))
    kblk = pl.BlockSpec((1, nb, 1, BQ), lambda h: (h, 0, 0, 0))
    hshape = jax.ShapeDtypeStruct((N_HEADS, s, HEAD_DIM), F32)
    return _pcall(
        body, grid=(N_HEADS,),
        in_specs=[hblk, hblk, hblk, cblk, kblk, hblk, cblk, hblk],
        out_specs=[hblk, hblk, hblk, cblk, kblk],
        out_shape=[hshape, hshape, hshape, jax.ShapeDtypeStruct((N_HEADS, s, 1), F32),
                   jax.ShapeDtypeStruct((N_HEADS, nb, 1, BQ), F32)],
        scratch_shapes=[pltpu.VMEM((s, 1), F32)],
        operands=(q, k, v, cq, ck, o, lse, do), name=name, comm=comm)


def _merge_fwd(x, ya, yc, yd, o, pm, pg, wb, wo, name, comm=None):
    s = x.shape[0]

    def body(x_ref, ya_ref, yc_ref, yd_ref, o_ref, bg_ref, pg_ref, wb_ref, wo_ref, xo_ref, yb_ref):
        o = jnp.concatenate([o_ref[h] for h in range(N_HEADS)], axis=1)
        yb = (o * _silu(bg_ref[...])).astype(BF16)
        yb_ref[...] = yb
        ys = (ya_ref[...], yb, yc_ref[...], yd_ref[...])
        merged = jnp.zeros((TMG, D_MODEL), F32)
        for n in range(N_BRANCH):
            merged = merged + _sig(pg_ref[:, n * D_MODEL:(n + 1) * D_MODEL]) * _dot(ys[n], wb_ref[n])
        xo_ref[...] = x_ref[...] + _dot(merged.astype(BF16), wo_ref[...])

    xt = pl.BlockSpec((TMG, D_MODEL), lambda i: (i, 0))
    yt = pl.BlockSpec((TMG, BR), lambda i: (i, 0))
    return _pcall(
        body, grid=(s // TMG,),
        in_specs=[xt, yt, yt, yt, pl.BlockSpec((N_HEADS, TMG, HEAD_DIM), lambda i: (0, i, 0)),
                  pl.BlockSpec((TMG, BR), lambda i: (i, O_BG // BR)), pl.BlockSpec((TMG, N_MERGE), lambda i: (i, 0)),
                  pl.BlockSpec((N_BRANCH, BR, D_MODEL), lambda i: (0, 0, 0)), pl.BlockSpec((D_MODEL, D_MODEL), lambda i: (0, 0))],
        out_specs=[xt, yt],
        out_shape=[jax.ShapeDtypeStruct((s, D_MODEL), F32), jax.ShapeDtypeStruct((s, BR), BF16)],
        operands=(x, ya, yc, yd, o, pm, pg, wb, wo), name=name, comm=comm)


def _merge_bwd(dx, ya, yb, yc, yd, o, pm, pg, wb, wo, name, comm=None):
    s = dx.shape[0]
    nt = s // TMG

    def body(dx_ref, ya_ref, yb_ref, yc_ref, yd_ref, o_ref, bg_ref, pg_ref, wb_ref, wo_ref,
             dpg_ref, dya_ref, do_ref, dbg_ref, dyc_ref, dyd_ref, dwb_ref, dwo_ref, dwb_acc, dwo_acc):
        i = pl.program_id(0)

        @pl.when(i == 0)
        def _():
            dwb_acc[...] = jnp.zeros_like(dwb_acc)
            dwo_acc[...] = jnp.zeros_like(dwo_acc)

        dxb = dx_ref[...].astype(BF16)
        dmerged = _dot_nt(dxb, wo_ref[...])
        ys = (ya_ref[...], yb_ref[...], yc_ref[...], yd_ref[...])
        dys = (dya_ref, None, dyc_ref, dyd_ref)
        merged = jnp.zeros((TMG, D_MODEL), F32)
        for n in range(N_BRANCH):
            gate = _sig(pg_ref[:, n * D_MODEL:(n + 1) * D_MODEL])
            pr = _dot(ys[n], wb_ref[n])
            merged = merged + gate * pr
            dpg_ref[:, n * D_MODEL:(n + 1) * D_MODEL] = (dmerged * pr * gate * (1.0 - gate)).astype(BF16)
            dpr = (gate * dmerged).astype(BF16)
            dwb_acc[n] += _dot_tn(ys[n], dpr)
            dyn = _dot_nt(dpr, wb_ref[n])
            if n == 1:
                bg = bg_ref[...]
                do = dyn * _silu(bg)
                for h in range(N_HEADS):
                    do_ref[h] = do[:, h * HEAD_DIM:(h + 1) * HEAD_DIM]
                dbg_ref[...] = dyn * jnp.concatenate([o_ref[h] for h in range(N_HEADS)], axis=1) * _dsilu(bg)
            else:
                dys[n][...] = dyn
        dwo_acc[...] += _dot_tn(merged.astype(BF16), dxb)

        @pl.when(i == nt - 1)
        def _():
            dwb_ref[...] = dwb_acc[...].astype(BF16)
            dwo_ref[...] = dwo_acc[...].astype(BF16)

    xt = pl.BlockSpec((TMG, D_MODEL), lambda i: (i, 0))
    yt = pl.BlockSpec((TMG, BR), lambda i: (i, 0))
    gt = pl.BlockSpec((TMG, N_MERGE), lambda i: (i, 0))
    wbs = pl.BlockSpec((N_BRANCH, BR, D_MODEL), lambda i: (0, 0, 0))
    wos = pl.BlockSpec((D_MODEL, D_MODEL), lambda i: (0, 0))
    yf = jax.ShapeDtypeStruct((s, BR), F32)
    ht = pl.BlockSpec((N_HEADS, TMG, HEAD_DIM), lambda i: (0, i, 0))
    return _pcall(
        body, grid=(nt,),
        in_specs=[xt, yt, yt, yt, yt, ht, pl.BlockSpec((TMG, BR), lambda i: (i, O_BG // BR)), gt, wbs, wos],
        out_specs=[gt, yt, ht, yt, yt, yt, wbs, wos],
        out_shape=[jax.ShapeDtypeStruct((s, N_MERGE), BF16), yf, jax.ShapeDtypeStruct((N_HEADS, s, HEAD_DIM), F32), yf, yf, yf,
                   jax.ShapeDtypeStruct((N_BRANCH, BR, D_MODEL), BF16), jax.ShapeDtypeStruct((D_MODEL, D_MODEL), BF16)],
        scratch_shapes=[pltpu.VMEM((N_BRANCH, BR, D_MODEL), F32), pltpu.VMEM((D_MODEL, D_MODEL), F32)],
        operands=(dx, ya, yb, yc, yd, o, pm, pg, wb, wo), name=name, comm=comm)


def _layer_fwd(x, wl, tag, attach=None):
    attach = attach or {}

    def riding(stage):
        comm, sink = attach.get(stage, (None, None))
        return comm, (sink or (lambda res: None))

    s = x.shape[0]
    h = _rms_fwd(x, wl["norm_g"], "rms_fwd" + tag)
    comm, sink = riding("proj_mix")
    pm, res = _mm_nn(h, wl["w_mix"], 1792, "proj_mix" + tag, comm)
    sink(res)
    pg, _ = _mm_nn(h, wl["w_merge"], 2048, "proj_merge" + tag)
    pf, _ = _mm_nn(h, wl["w_f"], N_F, "proj_f" + tag)
    comm, sink = riding("mix_fwd")
    (ya, yc, yd, q, k, v), res = _mix_fwd(pm, wl, "mix_fwd" + tag, comm)
    sink(res)
    cum = _cum_fwd(pf, wl["f_bias"], "cum_fwd" + tag)
    cum_t = cum[:, :N_HEADS].T
    cq = cum_t.reshape(N_HEADS, s, 1)
    ck = cum_t.reshape(N_HEADS, s // BQ, 1, BQ)
    comm, sink = riding("attn_fwd")
    (o, lse), res = _attn_fwd(q, k, v, cq, ck, "attn_fwd" + tag, comm)
    sink(res)
    comm, sink = riding("merge_fwd")
    (x_next, yb), res = _merge_fwd(x, ya, yc, yd, o, pm, pg, wl["wb"], wl["wo"], "merge_fwd" + tag, comm)
    sink(res)
    saved = dict(x=x, h=h, pm=pm, pg=pg, pf=pf, ya=ya, yb=yb, yc=yc, yd=yd, q=q, k=k, v=v, cq=cq, ck=ck, o=o, lse=lse)
    return x_next, saved


def _blocks_rows(g):
    return g.reshape(N_DEV, g.shape[0] // N_DEV, g.shape[1])


def _blocks_cols(g):
    return g.reshape(N_BRANCH * BR, N_DEV, D_MODEL // N_DEV).transpose(1, 0, 2)


def _layer_bwd(dx_next, sv, wl, tag, dist, riding, extra_small):
    s = dx_next.shape[0]
    (dpg, dya, do, dbg, dyc, dyd, dwb, dwo), rode = _merge_bwd(
        dx_next, sv["ya"], sv["yb"], sv["yc"], sv["yd"], sv["o"], sv["pm"], sv["pg"], wl["wb"], wl["wo"], "merge_bwd" + tag,
        riding)
    dw_merge = _mm_tn(sv["h"], dpg, 2048, "dw_merge" + tag)
    early = [(_blocks_rows(dw_merge), False), (_blocks_cols(dwb), False), (_blocks_rows(dwo), False)] if dist else None
    (dq, dk, dv, dcq, dck), early_out = _attn_bwd(sv["q"], sv["k"], sv["v"], sv["cq"], sv["ck"], sv["o"], sv["lse"], do,
                                                  "attn_bwd" + tag, early)
    dck_cols = jnp.pad(dck.reshape(N_HEADS, s).T, ((0, 0), (0, N_F - N_HEADS)))
    dpf, dfb = _cum_bwd(dcq, dck_cols, sv["pf"], wl["f_bias"], "cum_bwd" + tag)
    dpm, dgw, dgb, vec, dscw, ddww = _mix_bwd(sv["pm"], dya, dyc, dyd, (dq, dk, dv), dbg, wl, "mix_bwd" + tag)
    dw_mix = _mm_tn(sv["h"], dpm, 1792, "dw_mix" + tag)
    dw_f = _mm_tn(sv["h"], dpf, N_F, "dw_f" + tag)
    causal = jnp.tril(jnp.ones((CHUNK, CHUNK), bool))
    small = dict(
        f_bias=dfb.sum(0)[:N_HEADS],
        sgu_w=jnp.where(causal[None], dgw, 0.0),
        sgu_b=dgb.reshape(CHUNK, N_HEADS, HEAD_DIM).sum(-1).T,
        sgu_ln_g=vec[0], sgu_ln_b=vec[1], conf_dw_b=vec[2], conf_ln_g=vec[3], conf_ln_b=vec[4],
        short_conv_w=dscw.reshape(8, 8, BR).sum(1)[:SHORT_CONV],
        conf_dw_w=ddww.reshape(32, 8, BR).sum(1)[:CONF_CONV],
    )
    slab, spans = _pack([small[nm] for nm in SMALL[1:]])
    late = [(_blocks_rows(dw_mix), False), (_blocks_rows(dw_f), False), (slab, True)] if dist else None
    dh, late_out = _dh(dpm, dpg, dpf, wl["w_mix"], wl["w_merge"], wl["w_f"], "dh" + tag, late)
    dx, dng = _rms_bwd(dh, sv["x"], wl["norm_g"], dx_next, "rms_bwd" + tag)
    small["norm_g"] = dng.sum(0)
    grads = dict(small, w_mix=dw_mix, w_merge=dw_merge, w_f=dw_f, wb=dwb, wo=dwo)
    last_slab, last_spans = _pack([small["norm_g"]] + list(extra_small))
    return dx, grads, early_out, (late_out, spans), ([(last_slab, True)], last_spans), rode


def _prep_layer_small(norm_g, f_bias, sgu_w, sgu_b, sgu_ln_g, sgu_ln_b, scw, dww, conf_dw_b, conf_ln_g, conf_ln_b):
    causal = jnp.tril(jnp.ones((CHUNK, CHUNK), bool))
    gw = jnp.where(causal[None], sgu_w, 0.0)
    row = lambda a: a.reshape(1, -1)
    return dict(
        norm_g=row(norm_g),
        f_bias=jnp.pad(row(f_bias), ((0, 0), (0, N_F - N_HEADS))),
        gw=gw.astype(BF16), gwt=gw.transpose(0, 2, 1).astype(BF16),
        gb=jnp.repeat(sgu_b.T, HEAD_DIM, axis=1),
        sgu_ln_g=row(sgu_ln_g), sgu_ln_b=row(sgu_ln_b),
        scw=jnp.pad(scw, ((0, 8 - SHORT_CONV), (0, 0))), dww=jnp.pad(dww, ((0, 32 - CONF_CONV), (0, 0))),
        conf_dw_b=row(conf_dw_b), conf_ln_g=row(conf_ln_g), conf_ln_b=row(conf_ln_b))


def _local_step(x, target, layers, final_g):
    saved = []
    for l in range(DEPTH):
        x, sv = _layer_fwd(x, layers[l], str(l))
        saved.append(sv)
    loss_p, dx, dfg = _loss_head(x, final_g.reshape(1, D_MODEL), target)
    grads = [None] * DEPTH
    for l in reversed(range(DEPTH)):
        dx, grads[l], _, _, _, _ = _layer_bwd(dx, saved[l], layers[l], str(l), False, None, [])
    return 0.5 / D_MODEL * jnp.sum(loss_p), dx, grads, dfg.sum(0)


def _sum8(a, name):
    _, r, c = a.shape
    tr = r
    while tr * c * a.dtype.itemsize * N_DEV > 4 * 1024 * 1024 and tr % 32 == 0:
        tr //= 2

    def body(a_ref, o_ref):
        acc = a_ref[0].astype(F32)
        for d in range(1, N_DEV):
            acc = acc + a_ref[d].astype(F32)
        o_ref[...] = acc

    return pl.pallas_call(
        body, grid=(r // tr,),
        in_specs=[pl.BlockSpec((N_DEV, tr, c), lambda i: (0, i, 0))],
        out_specs=pl.BlockSpec((tr, c), lambda i: (i, 0)),
        out_shape=jax.ShapeDtypeStruct((r, c), F32), compiler_params=_cp(("parallel",)), name=name)(a)


def _adamw(w, g, m, v, name):
    l, r, c = w.shape
    tr = r
    while tr * c * 4 > 1024 * 1024 and tr % 16 == 0:
        tr //= 2
    c1 = 1.0 - ADAM_B1 ** ADAM_STEP
    c2 = 1.0 - ADAM_B2 ** ADAM_STEP

    def body(w_ref, g_ref, m_ref, v_ref, d_ref, mo_ref, vo_ref):
        gv = g_ref[...]
        mn = ADAM_B1 * m_ref[...] + (1.0 - ADAM_B1) * gv
        vn = ADAM_B2 * v_ref[...] + (1.0 - ADAM_B2) * (gv * gv)
        mo_ref[...] = mn
        vo_ref[...] = vn
        d_ref[...] = -ADAM_LR * ((mn / c1) / (jnp.sqrt(vn / c2) + ADAM_EPS) + ADAM_WD * w_ref[...])

    blk = pl.BlockSpec((1, tr, c), lambda a, i: (a, i, 0))
    shp = jax.ShapeDtypeStruct((l, r, c), F32)
    return pl.pallas_call(
        body, grid=(l, r // tr), in_specs=[blk] * 4, out_specs=[blk] * 3, out_shape=[shp] * 3,
        compiler_params=_cp(("parallel", "parallel")), name=name)(w, g, m, v)


def _pack(parts):
    rows, spans, r = [], [], 0
    for p in parts:
        flat = p.reshape(-1)
        nr = -(-flat.shape[0] // 1024) * 8
        rows.append(jnp.pad(flat, (0, nr * 128 - flat.shape[0])).reshape(nr, 128))
        spans.append((r, nr, p.shape))
        r += nr
    return jnp.concatenate(rows, axis=0), spans


def _unpack(slab, spans):
    out = []
    for r, nr, shape in spans:
        size = math.prod(shape)
        out.append(slab[r:r + nr].reshape(-1)[:size].reshape(shape))
    return out


def _split_w_in(w):
    mix = jnp.concatenate([w[..., 0:1536], w[..., 1540:1796], w[..., 3332:3588], w[..., 1796:2820], w[..., 2820:3332]], axis=-1)
    return mix, w[..., 3588:7684], w[..., 1536:1540]


def _join_w_in(mix, merge, f):
    return jnp.concatenate([mix[..., 0:1536], f, mix[..., 1536:1792], mix[..., 2048:3072], mix[..., 3072:3584],
                            mix[..., 1792:2048], merge], axis=-1)


SMALL = ("norm_g", "f_bias", "sgu_w", "sgu_b", "sgu_ln_g", "sgu_ln_b", "short_conv_w", "conf_dw_w", "conf_dw_b",
         "conf_ln_g", "conf_ln_b")


def kernel(x, norm_g, w_in, f_bias, sgu_w, sgu_b, sgu_ln_g, sgu_ln_b, short_conv_w, conf_dw_w, conf_dw_b, conf_ln_g, conf_ln_b, w_branch, w_out, final_g, loss_target, m_norm_g, m_w_in, m_f_bias, m_sgu_w, m_sgu_b, m_sgu_ln_g, m_sgu_ln_b, m_short_conv_w, m_conf_dw_w, m_conf_dw_b, m_conf_ln_g, m_conf_ln_b, m_w_branch, m_w_out, m_final_g, v_norm_g, v_w_in, v_f_bias, v_sgu_w, v_sgu_b, v_sgu_ln_g, v_sgu_ln_b, v_short_conv_w, v_conf_dw_w, v_conf_dw_b, v_conf_ln_g, v_conf_ln_b, v_w_branch, v_w_out, v_final_g):
    me = 4 * lax.axis_index("x") + 2 * lax.axis_index("y") + lax.axis_index("c")
    rows = D_MODEL // N_DEV
    cshard = BR // N_DEV

    sh = []
    for l in range(DEPTH):
        mix, merge, f = _split_w_in(w_in[l])
        sh.append(dict(mix=mix.astype(BF16), merge=merge.astype(BF16),
                       f=jnp.pad(f, ((0, 0), (0, N_F - N_HEADS))).astype(BF16),
                       wb=w_branch[l].astype(BF16), wo=w_out[l].astype(BF16)))
    conv_slab, conv_spans = _pack([short_conv_w, conf_dw_w])
    g_mix, g_f, g_conv = _exchange([(sh[0]["mix"], True), (sh[0]["f"], True), (conv_slab, True)], "gather_first")
    conv_full = [_unpack(g_conv[d], conv_spans) for d in range(N_DEV)]
    scw_full = jnp.concatenate([cf[0] for cf in conv_full], axis=-1)
    dww_full = jnp.concatenate([cf[1] for cf in conv_full], axis=-1)
    layers = [_prep_layer_small(norm_g[l], f_bias[l], sgu_w[l], sgu_b[l], sgu_ln_g[l], sgu_ln_b[l], scw_full[l], dww_full[l],
                                conf_dw_b[l], conf_ln_g[l], conf_ln_b[l]) for l in range(DEPTH)]
    layers[0].update(w_mix=g_mix.reshape(D_MODEL, N_MIX), w_f=g_f.reshape(D_MODEL, N_F))

    def put_in(l):
        def sink(res):
            layers[l].update(w_mix=res[0].reshape(D_MODEL, N_MIX), w_f=res[1].reshape(D_MODEL, N_F),
                             w_merge=res[2].reshape(D_MODEL, N_MERGE))
        return sink

    def put_merge(l):
        def sink(res):
            layers[l].update(w_merge=res[0].reshape(D_MODEL, N_MERGE))
        return sink

    def put_out(l):
        def sink(res):
            layers[l].update(wb=res[0].transpose(1, 2, 0, 3).reshape(N_BRANCH, BR, D_MODEL), wo=res[1].reshape(D_MODEL, D_MODEL))
        return sink

    attach0 = {
        "proj_mix": ([(sh[0]["merge"], True)], put_merge(0)),
        "mix_fwd": ([(sh[0]["wb"], True), (sh[0]["wo"], True)], put_out(0)),
        "attn_fwd": ([(sh[1]["mix"], True), (sh[1]["f"], True), (sh[1]["merge"], True)], put_in(1)),
        "merge_fwd": ([(sh[1]["wb"], True), (sh[1]["wo"], True)], put_out(1)),
    }

    xs = x[0]
    xs, sv0 = _layer_fwd(xs, layers[0], "0", attach0)
    xs, sv1 = _layer_fwd(xs, layers[1], "1")
    loss_p, dx, dfg = _loss_head(xs, final_g.reshape(1, D_MODEL), loss_target[0])
    loss = lax.psum(0.5 / D_MODEL * jnp.sum(loss_p), ("x", "y", "c"))
    dx, g1, early1, (late1, spans1), (last1, lspans1), _ = _layer_bwd(dx, sv1, layers[1], "1", True, None, [dfg.sum(0)])
    dx, g0, early0, (late0, spans0), (last0, lspans0), last1_out = _layer_bwd(dx, sv0, layers[0], "0", True, last1, [])
    last0_out = _exchange(last0, "gather_last")

    red, small = [], []
    for l, (early, late, spans, last, lspans) in enumerate(((early0, late0, spans0, last0_out, lspans0),
                                                            (early1, late1, spans1, last1_out, lspans1))):
        t = str(l)
        red.append(dict(merge=_sum8(early[0], "sum_merge" + t), wb=_sum8(early[1], "sum_wb" + t), wo=_sum8(early[2], "sum_wo" + t),
                        mix=_sum8(late[0], "sum_mix" + t), f=_sum8(late[1], "sum_f" + t)))
        keys = ["norm_g"] + (["final_g"] if l == DEPTH - 1 else []) + list(SMALL[1:])
        small.append(dict(zip(keys, _unpack(_sum8(last[0], "sum_last" + t), lspans)
                              + _unpack(_sum8(late[2], "sum_small" + t), spans))))
    gs = {nm: jnp.stack([small[l][nm] for l in range(DEPTH)]) for nm in SMALL}
    gs["final_g"] = small[DEPTH - 1]["final_g"]
    gs["short_conv_w"] = lax.dynamic_slice_in_dim(gs["short_conv_w"], me * cshard, cshard, axis=2)
    gs["conf_dw_w"] = lax.dynamic_slice_in_dim(gs["conf_dw_w"], me * cshard, cshard, axis=2)
    g_w_in = jnp.stack([_join_w_in(red[l]["mix"], red[l]["merge"], red[l]["f"][:, :N_HEADS]) for l in range(DEPTH)])
    g_w_branch = jnp.stack([red[l]["wb"].reshape(N_BRANCH, BR, rows) for l in range(DEPTH)])
    g_w_out = jnp.stack([red[l]["wo"] for l in range(DEPTH)])

    d_w_in, nm_w_in, nv_w_in = _adamw(w_in, g_w_in, m_w_in, v_w_in, "adamw_w_in")
    flat = lambda a: a.reshape(DEPTH, N_BRANCH * BR, rows)
    d_w_branch, nm_w_branch, nv_w_branch = (a.reshape(w_branch.shape) for a in _adamw(
        flat(w_branch), flat(g_w_branch), flat(m_w_branch), flat(v_w_branch), "adamw_w_branch"))
    d_w_out, nm_w_out, nv_w_out = _adamw(w_out, g_w_out, m_w_out, v_w_out, "adamw_w_out")
    names = SMALL + ("final_g",)
    ws = dict(zip(names, (norm_g, f_bias, sgu_w, sgu_b, sgu_ln_g, sgu_ln_b, short_conv_w, conf_dw_w, conf_dw_b, conf_ln_g,
                          conf_ln_b, final_g)))
    ms = dict(zip(names, (m_norm_g, m_f_bias, m_sgu_w, m_sgu_b, m_sgu_ln_g, m_sgu_ln_b, m_short_conv_w, m_conf_dw_w,
                          m_conf_dw_b, m_conf_ln_g, m_conf_ln_b, m_final_g)))
    vs = dict(zip(names, (v_norm_g, v_f_bias, v_sgu_w, v_sgu_b, v_sgu_ln_g, v_sgu_ln_b, v_short_conv_w, v_conf_dw_w,
                          v_conf_dw_b, v_conf_ln_g, v_conf_ln_b, v_final_g)))
    w_slab, spans = _pack([ws[nm] for nm in names])
    g_slab, _ = _pack([gs[nm] for nm in names])
    m_slab, _ = _pack([ms[nm] for nm in names])
    v_slab, _ = _pack([vs[nm] for nm in names])
    d_s, nm_s, nv_s = (dict(zip(names, _unpack(a[0], spans))) for a in _adamw(w_slab[None], g_slab[None], m_slab[None],
                                                                              v_slab[None], "adamw_small"))

    def ordered(small, w_in_v, w_branch_v, w_out_v):
        return [small["norm_g"], w_in_v, small["f_bias"], small["sgu_w"], small["sgu_b"], small["sgu_ln_g"],
                small["sgu_ln_b"], small["short_conv_w"], small["conf_dw_w"], small["conf_dw_b"], small["conf_ln_g"],
                small["conf_ln_b"], w_branch_v, w_out_v, small["final_g"]]

    return (loss, dx[None], *ordered(gs, g_w_in, g_w_branch, g_w_out), *ordered(d_s, d_w_in, d_w_branch, d_w_out),
            *ordered(nm_s, nm_w_in, nm_w_branch, nm_w_out), *ordered(nv_s, nv_w_in, nv_w_branch, nv_w_out))
```

```python
import functools
import math

import jax
import jax.numpy as jnp
from jax import lax
from jax.experimental import pallas as pl
from jax.experimental.pallas import tpu as pltpu

F32 = jnp.float32
BF16 = jnp.bfloat16

D_MODEL = 1024
DEPTH = 2
N_BRANCH = 4
BR = 256
N_HEADS = 4
HEAD_DIM = 64
CHUNK = 128
SHORT_CONV = 3
CONF_CONV = 31
EPS = 1e-6
N_DEV = 8

ADAM_LR = 0.001
ADAM_B1 = 0.9
ADAM_B2 = 0.999
ADAM_EPS = 1e-08
ADAM_WD = 0.01
ADAM_STEP = 10

O_UV, O_AG, O_QKV, O_BG, O_DG, O_CIN, O_CG, O_GLU = 0, 512, 768, 1536, 1792, 2048, 2816, 3072
N_MIX = 3584
N_MERGE = N_BRANCH * D_MODEL
N_F = 128
IN_COLS = 7684
HALO = 32
TM = 512
TMG = 256
BQ = 512
SUB = 64
CUMB = 512
VMEM_LIMIT = 56 * 1024 * 1024
NEG = -1e30
SCALE = 1.0 / math.sqrt(HEAD_DIM)
GELU_K = math.sqrt(2.0 / math.pi)


def _cp(sem=None):
    return pltpu.CompilerParams(dimension_semantics=sem, vmem_limit_bytes=VMEM_LIMIT)


PEER_ORDER = (6, 4, 2, 7, 5, 3, 1)
RELAYED = (3, 5, 7)


def _xchg(cin, cout, send, recv, loc, modes, phase):
    x, y, c = lax.axis_index("x"), lax.axis_index("y"), lax.axis_index("c")
    me = 4 * x + 2 * y + c

    def peer_of(kk):
        px, py, pc = lax.rem(x + (kk >> 2 & 1), 2), lax.rem(y + (kk >> 1 & 1), 2), lax.rem(c + (kk & 1), 2)
        return (px, py, pc), 4 * px + 2 * py + pc

    def remote(src, dst, a, kk, pid):
        return pltpu.make_async_remote_copy(src_ref=src, dst_ref=dst, send_sem=send.at[a, kk], recv_sem=recv.at[a, kk],
                                            device_id=pid, device_id_type=pl.DeviceIdType.MESH)

    def outgoing(a, kk):
        if modes[a] and kk in RELAYED:
            _, origin = peer_of(kk - 1)
            return remote(cout[a].at[origin], cout[a].at[origin], a, kk, peer_of(1)[0])
        pid, peer = peer_of(kk)
        return remote(cin[a] if modes[a] else cin[a].at[peer], cout[a].at[me], a, kk, pid)

    def arrival(a, kk):
        _, peer = peer_of(kk)
        return remote(cout[a].at[peer], cout[a].at[peer], a, kk, (x, y, c))

    if phase == "relay":
        for kk in RELAYED:
            for a, gather in enumerate(modes):
                if gather:
                    arrival(a, kk - 1).wait_recv()
                    outgoing(a, kk).start()
        return
    for a, gather in enumerate(modes):
        cp = pltpu.make_async_copy(cin[a] if gather else cin[a].at[me], cout[a].at[me], loc.at[a])
        if phase == "start":
            cp.start()
        else:
            cp.wait()
    if phase == "start":
        for kk in PEER_ORDER:
            for a, gather in enumerate(modes):
                if not (gather and kk in RELAYED):
                    outgoing(a, kk).start()
        return
    for kk in PEER_ORDER:
        for a in range(len(modes)):
            outgoing(a, kk).wait_send()
    for kk in PEER_ORDER:
        for a, gather in enumerate(modes):
            if not (gather and kk + 1 in RELAYED):
                arrival(a, kk).wait_recv()


def _xchg_shapes(comm):
    return [jax.ShapeDtypeStruct((N_DEV,) + tuple(a.shape[(0 if gather else 1):]), a.dtype) for a, gather in comm]


def _xchg_sems(n):
    return [pltpu.SemaphoreType.DMA((n, N_DEV)), pltpu.SemaphoreType.DMA((n, N_DEV)), pltpu.SemaphoreType.DMA((n,))]


def _exchange(comm, name):
    n = len(comm)
    modes = [g for _, g in comm]

    def body(*refs):
        cin, cout, (send, recv, loc) = refs[:n], refs[n:2 * n], refs[2 * n:]
        for phase in ("start", "relay", "finish"):
            _xchg(cin, cout, send, recv, loc, modes, phase)

    anyspec = pl.BlockSpec(memory_space=pl.ANY)
    return pl.pallas_call(
        body, in_specs=[anyspec] * n, out_specs=[anyspec] * n, out_shape=_xchg_shapes(comm),
        scratch_shapes=_xchg_sems(n), name=name)(*[a for a, _ in comm])


def _pcall(body, *, grid, in_specs, out_specs, out_shape, operands, name, scratch_shapes=(), comm=None):
    if not comm:
        outs = pl.pallas_call(
            body, grid=grid, in_specs=in_specs, out_specs=out_specs, out_shape=out_shape, scratch_shapes=list(scratch_shapes),
            compiler_params=_cp(("arbitrary",) * len(grid)), name=name)(*operands)
        return list(outs), []
    n, nin, nout, nsc = len(comm), len(operands), len(out_shape), len(scratch_shapes)
    modes = [g for _, g in comm]

    def wrapped(*refs):
        ins, cin = refs[:nin], refs[nin:nin + n]
        outs, cout = refs[nin + n:nin + n + nout], refs[nin + n + nout:nin + 2 * n + nout]
        scratch = refs[nin + 2 * n + nout:]
        own, (send, recv, loc) = scratch[:nsc], scratch[nsc:]
        ids = [pl.program_id(d) for d in range(len(grid))]
        first = functools.reduce(jnp.logical_and, [i == 0 for i in ids])
        last = functools.reduce(jnp.logical_and, [i == g - 1 for i, g in zip(ids, grid)])

        @pl.when(first)
        def _():
            _xchg(cin, cout, send, recv, loc, modes, "start")

        @pl.when(last)
        def _():
            _xchg(cin, cout, send, recv, loc, modes, "relay")

        body(*ins, *outs, *own)

        @pl.when(last)
        def _():
            _xchg(cin, cout, send, recv, loc, modes, "finish")

    anyspec = pl.BlockSpec(memory_space=pl.ANY)
    res = pl.pallas_call(
        wrapped, grid=grid, in_specs=list(in_specs) + [anyspec] * n, out_specs=list(out_specs) + [anyspec] * n,
        out_shape=list(out_shape) + _xchg_shapes(comm), scratch_shapes=list(scratch_shapes) + _xchg_sems(n),
        compiler_params=_cp(("arbitrary",) * len(grid)), name=name)(*operands, *[a for a, _ in comm])
    return list(res[:nout]), list(res[nout:])


def _sig(x):
    return 1.0 / (1.0 + jnp.exp(-x))


def _silu(x):
    return x * _sig(x)


def _dsilu(x):
    s = _sig(x)
    return s * (1.0 + x * (1.0 - s))


def _gelu(x):
    return 0.5 * x * (1.0 + jnp.tanh(GELU_K * (x + 0.044715 * x * x * x)))


def _dgelu(x):
    t = jnp.tanh(GELU_K * (x + 0.044715 * x * x * x))
    return 0.5 * (1.0 + t) + 0.5 * x * (1.0 - t * t) * GELU_K * (1.0 + 3.0 * 0.044715 * x * x)


def _ln_hat(x):
    mu = jnp.mean(x, axis=-1, keepdims=True)
    xc = x - mu
    rs = lax.rsqrt(jnp.mean(xc * xc, axis=-1, keepdims=True) + EPS)
    return xc * rs, rs


def _ln_bwd(dhat, hat, rs):
    return rs * (dhat - jnp.mean(dhat, axis=-1, keepdims=True) - hat * jnp.mean(dhat * hat, axis=-1, keepdims=True))


def _dot(a, b):
    return jnp.dot(a, b, preferred_element_type=F32)


def _dot_nt(a, b):
    return lax.dot_general(a, b, (((1,), (1,)), ((), ())), preferred_element_type=F32)


def _dot_tn(a, b):
    return lax.dot_general(a, b, (((0,), (0,)), ((), ())), preferred_element_type=F32)


def _fold8(x):
    acc = x[0:8]
    for r in range(1, x.shape[0] // 8):
        acc = acc + x[8 * r:8 * r + 8]
    return acc


def _rms_fwd(x, g, name):
    s = x.shape[0]

    def body(x_ref, g_ref, h_ref):
        xv = x_ref[...]
        r = lax.rsqrt(jnp.mean(xv * xv, axis=-1, keepdims=True) + EPS)
        h_ref[...] = (xv * r * g_ref[...]).astype(BF16)

    return pl.pallas_call(
        body, grid=(s // TM,),
        in_specs=[pl.BlockSpec((TM, D_MODEL), lambda i: (i, 0)), pl.BlockSpec((1, D_MODEL), lambda i: (0, 0))],
        out_specs=pl.BlockSpec((TM, D_MODEL), lambda i: (i, 0)),
        out_shape=jax.ShapeDtypeStruct((s, D_MODEL), BF16), compiler_params=_cp(("parallel",)), name=name)(x, g)


def _rms_bwd(dh, x, g, dx_next, name):
    s = x.shape[0]

    def body(dh_ref, x_ref, g_ref, dxn_ref, dx_ref, dg_ref):
        i = pl.program_id(0)
        xv = x_ref[...]
        r = lax.rsqrt(jnp.mean(xv * xv, axis=-1, keepdims=True) + EPS)
        xn = xv * r
        dhv = dh_ref[...]
        dxn = dhv * g_ref[...]
        dx_ref[...] = dxn_ref[...] + r * (dxn - xn * jnp.mean(dxn * xn, axis=-1, keepdims=True))

        @pl.when(i == 0)
        def _():
            dg_ref[...] = jnp.zeros_like(dg_ref)

        dg_ref[...] += _fold8(dhv * xn)

    tile = pl.BlockSpec((TM, D_MODEL), lambda i: (i, 0))
    return pl.pallas_call(
        body, grid=(s // TM,),
        in_specs=[tile, tile, pl.BlockSpec((1, D_MODEL), lambda i: (0, 0)), tile],
        out_specs=[tile, pl.BlockSpec((8, D_MODEL), lambda i: (0, 0))],
        out_shape=[jax.ShapeDtypeStruct((s, D_MODEL), F32), jax.ShapeDtypeStruct((8, D_MODEL), F32)],
        compiler_params=_cp(("arbitrary",)), name=name)(dh, x, g, dx_next)


def _loss_head(x, g, target):
    s = x.shape[0]

    def body(x_ref, g_ref, t_ref, loss_ref, dx_ref, dg_ref):
        i = pl.program_id(0)
        xv = x_ref[...]
        r = lax.rsqrt(jnp.mean(xv * xv, axis=-1, keepdims=True) + EPS)
        xn = xv * r
        err = xn * g_ref[...] - t_ref[...]
        dy = err * (1.0 / D_MODEL)
        dxn = dy * g_ref[...]
        dx_ref[...] = r * (dxn - xn * jnp.mean(dxn * xn, axis=-1, keepdims=True))

        @pl.when(i == 0)
        def _():
            dg_ref[...] = jnp.zeros_like(dg_ref)
            loss_ref[...] = jnp.zeros_like(loss_ref)

        dg_ref[...] += _fold8(dy * xn)
        loss_ref[...] += _fold8(err * err)

    tile = pl.BlockSpec((TM, D_MODEL), lambda i: (i, 0))
    acc = pl.BlockSpec((8, D_MODEL), lambda i: (0, 0))
    return pl.pallas_call(
        body, grid=(s // TM,),
        in_specs=[tile, pl.BlockSpec((1, D_MODEL), lambda i: (0, 0)), tile],
        out_specs=[acc, tile, acc],
        out_shape=[jax.ShapeDtypeStruct((8, D_MODEL), F32), jax.ShapeDtypeStruct((s, D_MODEL), F32),
                   jax.ShapeDtypeStruct((8, D_MODEL), F32)],
        compiler_params=_cp(("arbitrary",)), name="loss_head")(x, g, target)


def _mm_nn(a, b, tn, name, comm=None):
    m, k = a.shape
    n = b.shape[1]
    tm = 512

    def body(a_ref, b_ref, o_ref):
        o_ref[...] = _dot(a_ref[...], b_ref[...])

    (out,), couts = _pcall(
        body, grid=(n // tn, m // tm),
        in_specs=[pl.BlockSpec((tm, k), lambda j, i: (i, 0)), pl.BlockSpec((k, tn), lambda j, i: (0, j))],
        out_specs=[pl.BlockSpec((tm, tn), lambda j, i: (i, j))],
        out_shape=[jax.ShapeDtypeStruct((m, n), F32)], operands=(a, b), name=name, comm=comm)
    return out, couts


def _dh(dpm, dpg, dpf, w_mix, w_merge, w_f, name, comm=None):
    s = dpm.shape[0]
    tm = 1024 if s % 1024 == 0 else 512
    tk1, tk2 = 896, 1024
    n1, n2 = N_MIX // tk1, N_MERGE // tk2

    def body(dpm_ref, dpg_ref, dpf_ref, wm_ref, wg_ref, wf_ref, o_ref):
        j = pl.program_id(1)

        @pl.when(j == 0)
        def _():
            o_ref[...] = _dot_nt(dpf_ref[...], wf_ref[...])

        @pl.when(j < n1)
        def _():
            o_ref[...] += _dot_nt(dpm_ref[...], wm_ref[...])

        @pl.when(j >= n1)
        def _():
            o_ref[...] += _dot_nt(dpg_ref[...], wg_ref[...])

    mix_j = lambda j: jnp.minimum(j, n1 - 1)
    merge_j = lambda j: jnp.maximum(j - n1, 0)
    (out,), couts = _pcall(
        body, grid=(s // tm, n1 + n2),
        in_specs=[pl.BlockSpec((tm, tk1), lambda i, j: (i, mix_j(j))), pl.BlockSpec((tm, tk2), lambda i, j: (i, merge_j(j))),
                  pl.BlockSpec((tm, N_F), lambda i, j: (i, 0)),
                  pl.BlockSpec((D_MODEL, tk1), lambda i, j: (0, mix_j(j))), pl.BlockSpec((D_MODEL, tk2), lambda i, j: (0, merge_j(j))),
                  pl.BlockSpec((D_MODEL, N_F), lambda i, j: (0, 0))],
        out_specs=[pl.BlockSpec((tm, D_MODEL), lambda i, j: (i, 0))],
        out_shape=[jax.ShapeDtypeStruct((s, D_MODEL), F32)], operands=(dpm, dpg, dpf, w_mix, w_merge, w_f), name=name, comm=comm)
    return out, couts


def _mm_tn(a, d, tn, name):
    m, k = a.shape
    n = d.shape[1]
    tm = 512
    nm = m // tm

    def body(a_ref, d_ref, o_ref, acc):
        i = pl.program_id(1)

        @pl.when(i == 0)
        def _():
            acc[...] = jnp.zeros_like(acc)

        acc[...] += _dot_tn(a_ref[...], d_ref[...])

        @pl.when(i == nm - 1)
        def _():
            o_ref[...] = acc[...].astype(BF16)

    return pl.pallas_call(
        body, grid=(n // tn, nm),
        in_specs=[pl.BlockSpec((tm, k), lambda j, i: (i, 0)), pl.BlockSpec((tm, tn), lambda j, i: (i, j))],
        out_specs=pl.BlockSpec((k, tn), lambda j, i: (0, j)),
        out_shape=jax.ShapeDtypeStruct((k, n), BF16), scratch_shapes=[pltpu.VMEM((k, tn), F32)],
        compiler_params=_cp(("parallel", "arbitrary")), name=name)(a, d)


def _lane_head():
    return lax.broadcasted_iota(jnp.int32, (1, BR), 1) // HEAD_DIM


def _gmlp_chunk_fwd(p_ref, r0, gw_ref, gb_ref, lg, lb):
    uv = p_ref[r0:r0 + CHUNK, O_UV:O_UV + 2 * BR]
    u = _gelu(uv[:, :BR])
    vhat, rs = _ln_hat(_gelu(uv[:, BR:]))
    vn = (vhat * lg + lb).astype(BF16)
    head = _lane_head()
    mixed = gb_ref[...]
    for h in range(N_HEADS):
        mixed = mixed + jnp.where(head == h, _dot(gw_ref[h], vn), 0.0)
    return uv, u, vhat, rs, vn, mixed


def _tap_groups(k_width):
    groups = []
    for b in range(8):
        taps = [(d // 8, k_width - 1 - d) for d in range(b, k_width, 8)]
        if taps:
            groups.append((b, taps))
    return groups


def _causal_taps(buf, off, r0, nr, k_width):
    lead = 8 * ((k_width - 1) // 8 + 1)
    win = buf[off + r0 - lead:off + r0 + nr, :]
    for b, taps in _tap_groups(k_width):
        shifted = win if b == 0 else pltpu.roll(win, b, 0)
        for a, k in taps:
            yield k, shifted[lead - 8 * a:lead - 8 * a + nr]


def _anticausal_taps(buf, r0, nr, k_width):
    lead = 8 * ((k_width - 1) // 8 + 1)
    win = buf[r0:r0 + nr + lead, :]
    for b, taps in _tap_groups(k_width):
        shifted = win if b == 0 else pltpu.roll(win, nr + lead - b, 0)
        for a, k in taps:
            yield k, shifted[8 * a:8 * a + nr]


def _conv_sub_blocks(rows):
    out = [(r, SUB) for r in range(0, rows - rows % SUB, SUB)]
    if rows % SUB:
        out.append((rows - rows % SUB, rows % SUB))
    return out


def _mix_fwd(pm, wl, name, comm=None):
    s = pm.shape[0]
    nt = s // TM

    def body(p_ref, ph_ref, gw_ref, gb_ref, lg_ref, lb_ref, scw_ref, dww_ref, dwb_ref, clg_ref, clb_ref,
             ya_ref, yc_ref, yd_ref, q_ref, k_ref, v_ref, cc_ref, zbuf, hbuf):
        i = pl.program_id(0)
        for h in range(N_HEADS):
            c0 = O_QKV + h * HEAD_DIM
            q_ref[h] = (p_ref[:, c0:c0 + HEAD_DIM] * SCALE).astype(BF16)
            k_ref[h] = p_ref[:, c0 + BR:c0 + BR + HEAD_DIM].astype(BF16)
            v_ref[h] = p_ref[:, c0 + 2 * BR:c0 + 2 * BR + HEAD_DIM].astype(BF16)
        lg = lg_ref[...]
        lb = lb_ref[...]
        for c in range(TM // CHUNK):
            r0 = c * CHUNK
            _, u, _, _, _, mixed = _gmlp_chunk_fwd(p_ref, r0, gw_ref, gb_ref, lg, lb)
            ag = p_ref[r0:r0 + CHUNK, O_AG:O_AG + BR]
            ya_ref[r0:r0 + CHUNK, :] = (u * mixed * _silu(ag)).astype(BF16)

        first = i > 0
        zbuf[0:HALO, :] = jnp.where(first, ph_ref[:, O_CIN + BR:O_CIN + 2 * BR] * ph_ref[:, O_CIN + 2 * BR:O_CIN + 3 * BR], 0.0)
        zbuf[HALO:HALO + TM, :] = p_ref[:, O_CIN + BR:O_CIN + 2 * BR] * p_ref[:, O_CIN + 2 * BR:O_CIN + 3 * BR]
        hbuf[0:HALO, :] = jnp.where(first, ph_ref[:, O_GLU:O_GLU + BR] * _sig(ph_ref[:, O_GLU + BR:O_GLU + 2 * BR]), 0.0)
        hbuf[HALO:HALO + TM, :] = p_ref[:, O_GLU:O_GLU + BR] * _sig(p_ref[:, O_GLU + BR:O_GLU + 2 * BR])

        clg = clg_ref[...]
        clb = clb_ref[...]
        for r0, nr in _conv_sub_blocks(TM):
            yc = jnp.zeros((nr, BR), F32)
            for k, zk in _causal_taps(zbuf, HALO, r0, nr, SHORT_CONV):
                yc = yc + scw_ref[k:k + 1, :] * zk
            bgate = p_ref[r0:r0 + nr, O_CIN:O_CIN + BR]
            cg = p_ref[r0:r0 + nr, O_CG:O_CG + BR]
            yc_ref[r0:r0 + nr, :] = (bgate * yc * _silu(cg)).astype(BF16)

            cc = jnp.zeros((nr, BR), F32) + dwb_ref[...]
            for k, hk in _causal_taps(hbuf, HALO, r0, nr, CONF_CONV):
                cc = cc + dww_ref[k:k + 1, :] * hk
            cc_ref[r0:r0 + nr, :] = cc
            chat, _ = _ln_hat(cc)
            dg = p_ref[r0:r0 + nr, O_DG:O_DG + BR]
            yd_ref[r0:r0 + nr, :] = (_silu(chat * clg + clb) * _silu(dg)).astype(BF16)

    full = lambda shape: pl.BlockSpec(shape, lambda i: tuple(0 for _ in shape))
    ytile = pl.BlockSpec((TM, BR), lambda i: (i, 0))
    yshape = jax.ShapeDtypeStruct((s, BR), BF16)
    htile = pl.BlockSpec((N_HEADS, TM, HEAD_DIM), lambda i: (0, i, 0))
    hshape = jax.ShapeDtypeStruct((N_HEADS, s, HEAD_DIM), BF16)
    return _pcall(
        body, grid=(nt,),
        in_specs=[pl.BlockSpec((TM, N_MIX), lambda i: (i, 0)),
                  pl.BlockSpec((HALO, N_MIX), lambda i: (jnp.maximum(i * (TM // HALO) - 1, 0), 0)),
                  full((N_HEADS, CHUNK, CHUNK)), full((CHUNK, BR)), full((1, BR)), full((1, BR)),
                  full((8, BR)), full((32, BR)), full((1, BR)), full((1, BR)), full((1, BR))],
        out_specs=[ytile, ytile, ytile, htile, htile, htile, ytile],
        out_shape=[yshape, yshape, yshape, hshape, hshape, hshape, jax.ShapeDtypeStruct((s, BR), F32)],
        scratch_shapes=[pltpu.VMEM((HALO + TM, BR), F32), pltpu.VMEM((HALO + TM, BR), F32)],
        name=name, comm=comm, operands=(
            pm, pm, wl["gw"], wl["gb"], wl["sgu_ln_g"], wl["sgu_ln_b"], wl["scw"], wl["dww"], wl["conf_dw_b"],
            wl["conf_ln_g"], wl["conf_ln_b"]))


def _mix_bwd(pm, cc, dya, dyc, dyd, dqkv, dbg, wl, name):
    s = pm.shape[0]
    nt = s // TM
    ext = TM + HALO

    def body(p_ref, ph_ref, pn_ref, cc_ref, ccn_ref, dya_ref, dyc_ref, dycn_ref, dyd_ref, dydn_ref, dq_ref, dk_ref, dv_ref, dbg_ref,
             gw_ref, gwt_ref, gb_ref, lg_ref, lb_ref, scw_ref, dww_ref, dwb_ref, clg_ref, clb_ref,
             dp_ref, dgw_ref, dgb_ref, vec_ref, dscw_ref, ddww_ref, zbuf, dcb, hbuf, dcc):
        i = pl.program_id(0)

        @pl.when(i == 0)
        def _():
            dgw_ref[...] = jnp.zeros_like(dgw_ref)
            dgb_ref[...] = jnp.zeros_like(dgb_ref)
            vec_ref[...] = jnp.zeros_like(vec_ref)
            dscw_ref[...] = jnp.zeros_like(dscw_ref)
            ddww_ref[...] = jnp.zeros_like(ddww_ref)

        lg = lg_ref[...]
        lb = lb_ref[...]
        head = _lane_head()
        d_lg = jnp.zeros((1, BR), F32)
        d_lb = jnp.zeros((1, BR), F32)
        for c in range(TM // CHUNK):
            r0 = c * CHUNK
            uv, u, vhat, rs, vn, mixed = _gmlp_chunk_fwd(p_ref, r0, gw_ref, gb_ref, lg, lb)
            ag = p_ref[r0:r0 + CHUNK, O_AG:O_AG + BR]
            dy = dya_ref[r0:r0 + CHUNK, :]
            sa = _silu(ag)
            du = dy * mixed * sa
            dmx = dy * u * sa
            dp_ref[r0:r0 + CHUNK, O_AG:O_AG + BR] = (dy * u * mixed * _dsilu(ag)).astype(BF16)
            dgb_ref[...] += dmx
            dmx_b = dmx.astype(BF16)
            dvn = jnp.zeros((CHUNK, BR), F32)
            for h in range(N_HEADS):
                sel = head == h
                dgw_ref[h] += _dot_nt(jnp.where(sel, dmx, 0.0).astype(BF16), vn)
                dvn = dvn + jnp.where(sel, _dot(gwt_ref[h], dmx_b), 0.0)
            d_lg = d_lg + jnp.sum(dvn * vhat, axis=0, keepdims=True)
            d_lb = d_lb + jnp.sum(dvn, axis=0, keepdims=True)
            dv0 = _ln_bwd(dvn * lg, vhat, rs)
            dp_ref[r0:r0 + CHUNK, O_UV:O_UV + BR] = (du * _dgelu(uv[:, :BR])).astype(BF16)
            dp_ref[r0:r0 + CHUNK, O_UV + BR:O_UV + 2 * BR] = (dv0 * _dgelu(uv[:, BR:])).astype(BF16)
        vec_ref[0:1, :] += d_lg
        vec_ref[1:2, :] += d_lb

        for j, g_ref in enumerate((dq_ref, dk_ref, dv_ref)):
            dp_ref[:, O_QKV + j * BR:O_QKV + (j + 1) * BR] = jnp.concatenate(
                [g_ref[h] for h in range(N_HEADS)], axis=1).astype(BF16)
        dp_ref[:, O_BG:O_BG + BR] = dbg_ref[...].astype(BF16)

        first = i > 0
        last = i < nt - 1
        zbuf[0:HALO, :] = jnp.where(first, ph_ref[:, O_CIN + BR:O_CIN + 2 * BR] * ph_ref[:, O_CIN + 2 * BR:O_CIN + 3 * BR], 0.0)
        zbuf[HALO:HALO + TM, :] = p_ref[:, O_CIN + BR:O_CIN + 2 * BR] * p_ref[:, O_CIN + 2 * BR:O_CIN + 3 * BR]
        dcb[0:TM, :] = dyc_ref[...] * p_ref[:, O_CIN:O_CIN + BR] * _silu(p_ref[:, O_CG:O_CG + BR])
        dcb[TM:ext, :] = jnp.where(last, dycn_ref[...] * pn_ref[:, O_CIN:O_CIN + BR] * _silu(pn_ref[:, O_CG:O_CG + BR]), 0.0)
        for r0, nr in _conv_sub_blocks(TM):
            yc = jnp.zeros((nr, BR), F32)
            dz = jnp.zeros((nr, BR), F32)
            dcur = dcb[r0:r0 + nr, :]
            for k, zk in _causal_taps(zbuf, HALO, r0, nr, SHORT_CONV):
                yc = yc + scw_ref[k:k + 1, :] * zk
                dscw_ref[8 * k:8 * k + 8, :] += _fold8(dcur * zk)
            for k, dk in _anticausal_taps(dcb, r0, nr, SHORT_CONV):
                dz = dz + scw_ref[k:k + 1, :] * dk
            dy = dyc_ref[r0:r0 + nr, :]
            bgate = p_ref[r0:r0 + nr, O_CIN:O_CIN + BR]
            cg = p_ref[r0:r0 + nr, O_CG:O_CG + BR]
            dp_ref[r0:r0 + nr, O_CIN:O_CIN + BR] = (dy * yc * _silu(cg)).astype(BF16)
            dp_ref[r0:r0 + nr, O_CIN + BR:O_CIN + 2 * BR] = (dz * p_ref[r0:r0 + nr, O_CIN + 2 * BR:O_CIN + 3 * BR]).astype(BF16)
            dp_ref[r0:r0 + nr, O_CIN + 2 * BR:O_CIN + 3 * BR] = (dz * p_ref[r0:r0 + nr, O_CIN + BR:O_CIN + 2 * BR]).astype(BF16)
            dp_ref[r0:r0 + nr, O_CG:O_CG + BR] = (dy * bgate * yc * _dsilu(cg)).astype(BF16)

        hbuf[0:HALO, :] = jnp.where(first, ph_ref[:, O_GLU:O_GLU + BR] * _sig(ph_ref[:, O_GLU + BR:O_GLU + 2 * BR]), 0.0)
        hbuf[HALO:HALO + TM, :] = p_ref[:, O_GLU:O_GLU + BR] * _sig(p_ref[:, O_GLU + BR:O_GLU + 2 * BR])
        clg = clg_ref[...]
        clb = clb_ref[...]
        d_clg = jnp.zeros((1, BR), F32)
        d_clb = jnp.zeros((1, BR), F32)
        d_dwb = jnp.zeros((1, BR), F32)
        for r0, nr in _conv_sub_blocks(ext):
            in_tile = r0 < TM
            chat, rs = _ln_hat(cc_ref[r0:r0 + nr, :] if in_tile else ccn_ref[...])
            ln = chat * clg + clb
            if in_tile:
                dy = dyd_ref[r0:r0 + nr, :]
                dg = p_ref[r0:r0 + nr, O_DG:O_DG + BR]
            else:
                dy = jnp.where(last, dydn_ref[...], 0.0)
                dg = pn_ref[:, O_DG:O_DG + BR]
            dln = dy * _silu(dg) * _dsilu(ln)
            dc = _ln_bwd(dln * clg, chat, rs)
            dcc[r0:r0 + nr, :] = dc
            if in_tile:
                dp_ref[r0:r0 + nr, O_DG:O_DG + BR] = (dy * _silu(ln) * _dsilu(dg)).astype(BF16)
                d_clg = d_clg + jnp.sum(dln * chat, axis=0, keepdims=True)
                d_clb = d_clb + jnp.sum(dln, axis=0, keepdims=True)
                d_dwb = d_dwb + jnp.sum(dc, axis=0, keepdims=True)
        vec_ref[2:3, :] += d_dwb
        vec_ref[3:4, :] += d_clg
        vec_ref[4:5, :] += d_clb
        for r0, nr in _conv_sub_blocks(TM):
            dcur = dcc[r0:r0 + nr, :]
            dhh = jnp.zeros((nr, BR), F32)
            for k, hk in _causal_taps(hbuf, HALO, r0, nr, CONF_CONV):
                ddww_ref[8 * k:8 * k + 8, :] += _fold8(dcur * hk)
            for k, dk in _anticausal_taps(dcc, r0, nr, CONF_CONV):
                dhh = dhh + dww_ref[k:k + 1, :] * dk
            a = p_ref[r0:r0 + nr, O_GLU:O_GLU + BR]
            sg = _sig(p_ref[r0:r0 + nr, O_GLU + BR:O_GLU + 2 * BR])
            dp_ref[r0:r0 + nr, O_GLU:O_GLU + BR] = (dhh * sg).astype(BF16)
            dp_ref[r0:r0 + nr, O_GLU + BR:O_GLU + 2 * BR] = (dhh * a * sg * (1.0 - sg)).astype(BF16)

    full = lambda shape: pl.BlockSpec(shape, lambda i: tuple(0 for _ in shape))
    rpt = TM // HALO
    prev_map = lambda i: (jnp.maximum(i * rpt - 1, 0), 0)
    next_map = lambda i: (jnp.minimum((i + 1) * rpt, nt * rpt - 1), 0)
    ytile = pl.BlockSpec((TM, BR), lambda i: (i, 0))
    htile = pl.BlockSpec((N_HEADS, TM, HEAD_DIM), lambda i: (0, i, 0))
    return pl.pallas_call(
        body, grid=(nt,),
        in_specs=[pl.BlockSpec((TM, N_MIX), lambda i: (i, 0)), pl.BlockSpec((HALO, N_MIX), prev_map),
                  pl.BlockSpec((HALO, N_MIX), next_map),
                  ytile, pl.BlockSpec((HALO, BR), next_map),
                  ytile, ytile, pl.BlockSpec((HALO, BR), next_map), ytile, pl.BlockSpec((HALO, BR), next_map),
                  htile, htile, htile, ytile,
                  full((N_HEADS, CHUNK, CHUNK)), full((N_HEADS, CHUNK, CHUNK)), full((CHUNK, BR)), full((1, BR)), full((1, BR)),
                  full((8, BR)), full((32, BR)), full((1, BR)), full((1, BR)), full((1, BR))],
        out_specs=[pl.BlockSpec((TM, N_MIX), lambda i: (i, 0)), full((N_HEADS, CHUNK, CHUNK)), full((CHUNK, BR)),
                   full((16, BR)), full((64, BR)), full((256, BR))],
        out_shape=[jax.ShapeDtypeStruct((s, N_MIX), BF16), jax.ShapeDtypeStruct((N_HEADS, CHUNK, CHUNK), F32),
                   jax.ShapeDtypeStruct((CHUNK, BR), F32), jax.ShapeDtypeStruct((16, BR), F32),
                   jax.ShapeDtypeStruct((64, BR), F32), jax.ShapeDtypeStruct((256, BR), F32)],
        scratch_shapes=[pltpu.VMEM((HALO + TM, BR), F32), pltpu.VMEM((ext, BR), F32),
                        pltpu.VMEM((HALO + TM, BR), F32), pltpu.VMEM((ext, BR), F32)],
        compiler_params=_cp(("arbitrary",)), name=name)(
            pm, pm, pm, cc, cc, dya, dyc, dyc, dyd, dyd, *dqkv, dbg,
            wl["gw"], wl["gwt"], wl["gb"], wl["sgu_ln_g"], wl["sgu_ln_b"], wl["scw"], wl["dww"], wl["conf_dw_b"],
            wl["conf_ln_g"], wl["conf_ln_b"])


def _tri(lower):
    r = lax.broadcasted_iota(jnp.int32, (CUMB, CUMB), 0)
    c = lax.broadcasted_iota(jnp.int32, (CUMB, CUMB), 1)
    return jnp.where((r >= c) if lower else (r <= c), 1.0, 0.0).astype(F32)


def _dot_hi(a, b):
    return jnp.dot(a, b, preferred_element_type=F32, precision=lax.Precision.HIGHEST)


def _cum_fwd(pf, fb, name):
    s = pf.shape[0]

    def body(pf_ref, fb_ref, cum_ref, carry):
        i = pl.program_id(0)

        @pl.when(i == 0)
        def _():
            carry[...] = jnp.zeros_like(carry)

        z = pf_ref[...] + fb_ref[...]
        logf = jnp.minimum(z, 0.0) - jnp.log(1.0 + jnp.exp(-jnp.abs(z)))
        cum_ref[...] = _dot_hi(_tri(True), logf) + carry[...]
        carry[...] += jnp.sum(logf, axis=0, keepdims=True)

    return pl.pallas_call(
        body, grid=(s // CUMB,),
        in_specs=[pl.BlockSpec((CUMB, N_F), lambda i: (i, 0)), pl.BlockSpec((1, N_F), lambda i: (0, 0))],
        out_specs=pl.BlockSpec((CUMB, N_F), lambda i: (i, 0)),
        out_shape=jax.ShapeDtypeStruct((s, N_F), F32),
        scratch_shapes=[pltpu.VMEM((1, N_F), F32)],
        compiler_params=_cp(("arbitrary",)), name=name)(pf, fb)


def _cum_bwd(dcq, dck, pf, fb, name):
    s = pf.shape[0]
    nb = s // CUMB

    def body(dcq_ref, dck_ref, pf_ref, fb_ref, dpf_ref, dfb_ref, carry):
        i = pl.program_id(0)

        @pl.when(i == 0)
        def _():
            carry[...] = jnp.zeros_like(carry)
            dfb_ref[...] = jnp.zeros_like(dfb_ref)

        lane = lax.broadcasted_iota(jnp.int32, (1, N_F), 1)
        dc = dck_ref[...]
        for h in range(N_HEADS):
            dc = dc + jnp.where(lane == h, dcq_ref[h], 0.0)
        dlogf = _dot_hi(_tri(False), dc) + carry[...]
        carry[...] += jnp.sum(dc, axis=0, keepdims=True)
        z = pf_ref[...] + fb_ref[...]
        dz = dlogf * (1.0 - _sig(z))
        dpf_ref[...] = dz.astype(BF16)
        dfb_ref[...] += _fold8(dz)

    rev = lambda i: (nb - 1 - i, 0)
    return pl.pallas_call(
        body, grid=(nb,),
        in_specs=[pl.BlockSpec((N_HEADS, CUMB, 1), lambda i: (0, nb - 1 - i, 0)), pl.BlockSpec((CUMB, N_F), rev),
                  pl.BlockSpec((CUMB, N_F), rev), pl.BlockSpec((1, N_F), lambda i: (0, 0))],
        out_specs=[pl.BlockSpec((CUMB, N_F), rev), pl.BlockSpec((8, N_F), lambda i: (0, 0))],
        out_shape=[jax.ShapeDtypeStruct((s, N_F), BF16), jax.ShapeDtypeStruct((8, N_F), F32)],
        scratch_shapes=[pltpu.VMEM((1, N_F), F32)],
        compiler_params=_cp(("arbitrary",)), name=name)(dcq, dck, pf, fb)


def _causal_mask():
    r = lax.broadcasted_iota(jnp.int32, (BQ, BQ), 0)
    c = lax.broadcasted_iota(jnp.int32, (BQ, BQ), 1)
    return r >= c


def _attn_fwd(q, k, v, cq, ck, name, comm=None):
    s = q.shape[1]
    nb = s // BQ

    def body(q_ref, k_ref, v_ref, cq_ref, ck_ref, o_ref, lse_ref):
        for qi in range(nb):
            qs = qi * BQ
            qb = q_ref[0, qs:qs + BQ, :]
            cqb = cq_ref[0, qs:qs + BQ, :]

            def block(kj, carry, masked):
                m, l, acc = carry
                ks = pl.multiple_of(kj * BQ, BQ)
                kb = k_ref[0, pl.ds(ks, BQ), :]
                vb = v_ref[0, pl.ds(ks, BQ), :]
                sc = _dot_nt(qb, kb) + (cqb - ck_ref[0, kj])
                if masked:
                    sc = jnp.where(_causal_mask(), sc, NEG)
                m_new = jnp.maximum(m, jnp.max(sc, axis=-1, keepdims=True))
                alpha = jnp.exp(m - m_new)
                p = jnp.exp(sc - m_new)
                l = alpha * l + jnp.sum(p, axis=-1, keepdims=True)
                acc = alpha * acc + _dot(p.astype(BF16), vb)
                return m_new, l, acc

            carry = (jnp.full((BQ, 1), NEG, F32), jnp.zeros((BQ, 1), F32), jnp.zeros((BQ, HEAD_DIM), F32))
            if qi > 0:
                carry = lax.fori_loop(0, qi, lambda kj, cr: block(kj, cr, False), carry)
            m, l, acc = block(qi, carry, True)
            o_ref[0, qs:qs + BQ, :] = acc / l
            lse_ref[0, qs:qs + BQ, :] = m + jnp.log(l)

    hblk = pl.BlockSpec((1, s, HEAD_DIM), lambda h: (h, 0, 0))
    cblk = pl.BlockSpec((1, s, 1), lambda h: (h, 0, 0))
    return _pcall(
        body, grid=(N_HEADS,),
        in_specs=[hblk, hblk, hblk, cblk, pl.BlockSpec((1, nb, 1, BQ), lambda h: (h, 0, 0, 0))],
        out_specs=[hblk, cblk],
        out_shape=[jax.ShapeDtypeStruct((N_HEADS, s, HEAD_DIM), F32), jax.ShapeDtypeStruct((N_HEADS, s, 1), F32)],
        operands=(q, k, v, cq, ck), name=name, comm=comm)


def _attn_bwd(q, k, v, cq, ck, o, lse, do, name, comm=None):
    s = q.shape[1]
    nb = s // BQ

    def body(q_ref, k_ref, v_ref, cq_ref, ck_ref, o_ref, lse_ref, do_ref, dq_ref, dk_ref, dv_ref, dcq_ref, dck_ref, delta):
        delta[...] = jnp.sum(do_ref[0] * o_ref[0], axis=-1, keepdims=True)
        dq_ref[...] = jnp.zeros_like(dq_ref)
        dcq_ref[...] = jnp.zeros_like(dcq_ref)
        for kj in range(nb):
            ks = kj * BQ
            kb = k_ref[0, ks:ks + BQ, :]
            vb = v_ref[0, ks:ks + BQ, :]
            ckb = ck_ref[0, kj]

            def block(qi, carry, masked):
                dk_acc, dv_acc, dck_acc = carry
                qs = pl.multiple_of(qi * BQ, BQ)
                qb = q_ref[0, pl.ds(qs, BQ), :]
                dob = do_ref[0, pl.ds(qs, BQ), :].astype(BF16)
                sc = _dot_nt(qb, kb) + (cq_ref[0, pl.ds(qs, BQ), :] - ckb)
                p = jnp.exp(sc - lse_ref[0, pl.ds(qs, BQ), :])
                if masked:
                    p = jnp.where(_causal_mask(), p, 0.0)
                dp = _dot_nt(dob, vb)
                ds = p * (dp - delta[pl.ds(qs, BQ), :])
                ds_b = ds.astype(BF16)
                dv_acc = dv_acc + _dot_tn(p.astype(BF16), dob)
                dk_acc = dk_acc + _dot_tn(ds_b, qb)
                dq_ref[0, pl.ds(qs, BQ), :] += _dot(ds_b, kb) * SCALE
                dcq_ref[0, pl.ds(qs, BQ), :] += jnp.sum(ds, axis=-1, keepdims=True)
                dck_acc = dck_acc - jnp.sum(ds, axis=0, keepdims=True)
                return dk_acc, dv_acc, dck_acc

            carry = (jnp.zeros((BQ, HEAD_DIM), F32), jnp.zeros((BQ, HEAD_DIM), F32), jnp.zeros((1, BQ), F32))
            carry = block(kj, carry, True)
            if kj < nb - 1:
                carry = lax.fori_loop(kj + 1, nb, lambda qi, cr: block(qi, cr, False), carry)
            dk_ref[0, ks:ks + BQ, :] = carry[0]
            dv_ref[0, ks:ks + BQ, :] = carry[1]
            dck_ref[0, kj] = carry[2]

    hblk = pl.BlockSpec((1, s, HEAD_DIM), lambda h: (h, 0, 0))
    cblk = pl.BlockSpec((1, s, 1), lambda h: (h, 0, 0))
    kblk = pl.BlockSpec((1, nb, 1, BQ), lambda h: (h, 0, 0, 0))
    hshape = jax.ShapeDtypeStruct((N_HEADS, s, HEAD_DIM), F32)
    return _pcall(
        body, grid=(N_HEADS,),
        in_specs=[hblk, hblk, hblk, cblk, kblk, hblk, cblk, hblk],
        out_specs=[hblk, hblk, hblk, cblk, kblk],
        out_shape=[hshape, hshape, hshape, jax.ShapeDtypeStruct((N_HEADS, s, 1), F32),
                   jax.ShapeDtypeStruct((N_HEADS, nb, 1, BQ), F32)],
        scratch_shapes=[pltpu.VMEM((s, 1), F32)],
        operands=(q, k, v, cq, ck, o, lse, do), name=name, comm=comm)


def _merge_fwd(x, ya, yc, yd, o, pm, pg, wb, wo, name, comm=None):
    s = x.shape[0]

    def body(x_ref, ya_ref, yc_ref, yd_ref, o_ref, bg_ref, pg_ref, wb_ref, wo_ref, xo_ref, yb_ref):
        o = jnp.concatenate([o_ref[h] for h in range(N_HEADS)], axis=1)
        yb = (o * _silu(bg_ref[...])).astype(BF16)
        yb_ref[...] = yb
        ys = (ya_ref[...], yb, yc_ref[...], yd_ref[...])
        merged = jnp.zeros((TMG, D_MODEL), F32)
        for n in range(N_BRANCH):
            merged = merged + _sig(pg_ref[:, n * D_MODEL:(n + 1) * D_MODEL]) * _dot(ys[n], wb_ref[n])
        xo_ref[...] = x_ref[...] + _dot(merged.astype(BF16), wo_ref[...])

    xt = pl.BlockSpec((TMG, D_MODEL), lambda i: (i, 0))
    yt = pl.BlockSpec((TMG, BR), lambda i: (i, 0))
    return _pcall(
        body, grid=(s // TMG,),
        in_specs=[xt, yt, yt, yt, pl.BlockSpec((N_HEADS, TMG, HEAD_DIM), lambda i: (0, i, 0)),
                  pl.BlockSpec((TMG, BR), lambda i: (i, O_BG // BR)), pl.BlockSpec((TMG, N_MERGE), lambda i: (i, 0)),
                  pl.BlockSpec((N_BRANCH, BR, D_MODEL), lambda i: (0, 0, 0)), pl.BlockSpec((D_MODEL, D_MODEL), lambda i: (0, 0))],
        out_specs=[xt, yt],
        out_shape=[jax.ShapeDtypeStruct((s, D_MODEL), F32), jax.ShapeDtypeStruct((s, BR), BF16)],
        operands=(x, ya, yc, yd, o, pm, pg, wb, wo), name=name, comm=comm)


def _merge_bwd(dx, ya, yb, yc, yd, o, pm, pg, wb, wo, name, comm=None):
    s = dx.shape[0]
    nt = s // TMG

    def body(dx_ref, ya_ref, yb_ref, yc_ref, yd_ref, o_ref, bg_ref, pg_ref, wb_ref, wo_ref,
             dpg_ref, dya_ref, do_ref, dbg_ref, dyc_ref, dyd_ref, dwb_ref, dwo_ref, dwb_acc, dwo_acc):
        i = pl.program_id(0)

        @pl.when(i == 0)
        def _():
            dwb_acc[...] = jnp.zeros_like(dwb_acc)
            dwo_acc[...] = jnp.zeros_like(dwo_acc)

        dxb = dx_ref[...].astype(BF16)
        dmerged = _dot_nt(dxb, wo_ref[...])
        ys = (ya_ref[...], yb_ref[...], yc_ref[...], yd_ref[...])
        dys = (dya_ref, None, dyc_ref, dyd_ref)
        merged = jnp.zeros((TMG, D_MODEL), F32)
        for n in range(N_BRANCH):
            gate = _sig(pg_ref[:, n * D_MODEL:(n + 1) * D_MODEL])
            pr = _dot(ys[n], wb_ref[n])
            merged = merged + gate * pr
            dpg_ref[:, n * D_MODEL:(n + 1) * D_MODEL] = (dmerged * pr * gate * (1.0 - gate)).astype(BF16)
            dpr = (gate * dmerged).astype(BF16)
            dwb_acc[n] += _dot_tn(ys[n], dpr)
            dyn = _dot_nt(dpr, wb_ref[n])
            if n == 1:
                bg = bg_ref[...]
                do = dyn * _silu(bg)
                for h in range(N_HEADS):
                    do_ref[h] = do[:, h * HEAD_DIM:(h + 1) * HEAD_DIM]
                dbg_ref[...] = dyn * jnp.concatenate([o_ref[h] for h in range(N_HEADS)], axis=1) * _dsilu(bg)
            else:
                dys[n][...] = dyn
        dwo_acc[...] += _dot_tn(merged.astype(BF16), dxb)

        @pl.when(i == nt - 1)
        def _():
            dwb_ref[...] = dwb_acc[...].astype(BF16)
            dwo_ref[...] = dwo_acc[...].astype(BF16)

    xt = pl.BlockSpec((TMG, D_MODEL), lambda i: (i, 0))
    yt = pl.BlockSpec((TMG, BR), lambda i: (i, 0))
    gt = pl.BlockSpec((TMG, N_MERGE), lambda i: (i, 0))
    wbs = pl.BlockSpec((N_BRANCH, BR, D_MODEL), lambda i: (0, 0, 0))
    wos = pl.BlockSpec((D_MODEL, D_MODEL), lambda i: (0, 0))
    yf = jax.ShapeDtypeStruct((s, BR), F32)
    ht = pl.BlockSpec((N_HEADS, TMG, HEAD_DIM), lambda i: (0, i, 0))
    return _pcall(
        body, grid=(nt,),
        in_specs=[xt, yt, yt, yt, yt, ht, pl.BlockSpec((TMG, BR), lambda i: (i, O_BG // BR)), gt, wbs, wos],
        out_specs=[gt, yt, ht, yt, yt, yt, wbs, wos],
        out_shape=[jax.ShapeDtypeStruct((s, N_MERGE), BF16), yf, jax.ShapeDtypeStruct((N_HEADS, s, HEAD_DIM), F32), yf, yf, yf,
                   jax.ShapeDtypeStruct((N_BRANCH, BR, D_MODEL), BF16), jax.ShapeDtypeStruct((D_MODEL, D_MODEL), BF16)],
        scratch_shapes=[pltpu.VMEM((N_BRANCH, BR, D_MODEL), F32), pltpu.VMEM((D_MODEL, D_MODEL), F32)],
        operands=(dx, ya, yb, yc, yd, o, pm, pg, wb, wo), name=name, comm=comm)


def _layer_fwd(x, wl, tag, attach=None):
    attach = attach or {}

    def riding(stage):
        comm, sink = attach.get(stage, (None, None))
        return comm, (sink or (lambda res: None))

    s = x.shape[0]
    h = _rms_fwd(x, wl["norm_g"], "rms_fwd" + tag)
    comm, sink = riding("proj_mix")
    pm, res = _mm_nn(h, wl["w_mix"], 1792, "proj_mix" + tag, comm)
    sink(res)
    pg, _ = _mm_nn(h, wl["w_merge"], 2048, "proj_merge" + tag)
    pf, _ = _mm_nn(h, wl["w_f"], N_F, "proj_f" + tag)
    comm, sink = riding("mix_fwd")
    (ya, yc, yd, q, k, v, cc), res = _mix_fwd(pm, wl, "mix_fwd" + tag, comm)
    sink(res)
    cum = _cum_fwd(pf, wl["f_bias"], "cum_fwd" + tag)
    cum_t = cum[:, :N_HEADS].T
    cq = cum_t.reshape(N_HEADS, s, 1)
    ck = cum_t.reshape(N_HEADS, s // BQ, 1, BQ)
    comm, sink = riding("attn_fwd")
    (o, lse), res = _attn_fwd(q, k, v, cq, ck, "attn_fwd" + tag, comm)
    sink(res)
    comm, sink = riding("merge_fwd")
    (x_next, yb), res = _merge_fwd(x, ya, yc, yd, o, pm, pg, wl["wb"], wl["wo"], "merge_fwd" + tag, comm)
    sink(res)
    saved = dict(x=x, h=h, pm=pm, pg=pg, pf=pf, cc=cc, ya=ya, yb=yb, yc=yc, yd=yd, q=q, k=k, v=v, cq=cq, ck=ck, o=o, lse=lse)
    return x_next, saved


def _blocks_rows(g):
    return g.reshape(N_DEV, g.shape[0] // N_DEV, g.shape[1])


def _blocks_cols(g):
    return g.reshape(N_BRANCH * BR, N_DEV, D_MODEL // N_DEV).transpose(1, 0, 2)


def _layer_bwd(dx_next, sv, wl, tag, dist, riding, extra_small):
    s = dx_next.shape[0]
    (dpg, dya, do, dbg, dyc, dyd, dwb, dwo), rode = _merge_bwd(
        dx_next, sv["ya"], sv["yb"], sv["yc"], sv["yd"], sv["o"], sv["pm"], sv["pg"], wl["wb"], wl["wo"], "merge_bwd" + tag,
        riding)
    dw_merge = _mm_tn(sv["h"], dpg, 2048, "dw_merge" + tag)
    early = [(_blocks_rows(dw_merge), False), (_blocks_cols(dwb), False), (_blocks_rows(dwo), False)] if dist else None
    (dq, dk, dv, dcq, dck), early_out = _attn_bwd(sv["q"], sv["k"], sv["v"], sv["cq"], sv["ck"], sv["o"], sv["lse"], do,
                                                  "attn_bwd" + tag, early)
    dck_cols = jnp.pad(dck.reshape(N_HEADS, s).T, ((0, 0), (0, N_F - N_HEADS)))
    dpf, dfb = _cum_bwd(dcq, dck_cols, sv["pf"], wl["f_bias"], "cum_bwd" + tag)
    dpm, dgw, dgb, vec, dscw, ddww = _mix_bwd(sv["pm"], sv["cc"], dya, dyc, dyd, (dq, dk, dv), dbg, wl, "mix_bwd" + tag)
    dw_mix = _mm_tn(sv["h"], dpm, 1792, "dw_mix" + tag)
    dw_f = _mm_tn(sv["h"], dpf, N_F, "dw_f" + tag)
    causal = jnp.tril(jnp.ones((CHUNK, CHUNK), bool))
    small = dict(
        f_bias=dfb.sum(0)[:N_HEADS],
        sgu_w=jnp.where(causal[None], dgw, 0.0),
        sgu_b=dgb.reshape(CHUNK, N_HEADS, HEAD_DIM).sum(-1).T,
        sgu_ln_g=vec[0], sgu_ln_b=vec[1], conf_dw_b=vec[2], conf_ln_g=vec[3], conf_ln_b=vec[4],
        short_conv_w=dscw.reshape(8, 8, BR).sum(1)[:SHORT_CONV],
        conf_dw_w=ddww.reshape(32, 8, BR).sum(1)[:CONF_CONV],
    )
    slab, spans = _pack([small[nm] for nm in SMALL[1:]])
    late = [(_blocks_rows(dw_mix), False), (_blocks_rows(dw_f), False), (slab, True)] if dist else None
    dh, late_out = _dh(dpm, dpg, dpf, wl["w_mix"], wl["w_merge"], wl["w_f"], "dh" + tag, late)
    dx, dng = _rms_bwd(dh, sv["x"], wl["norm_g"], dx_next, "rms_bwd" + tag)
    small["norm_g"] = dng.sum(0)
    grads = dict(small, w_mix=dw_mix, w_merge=dw_merge, w_f=dw_f, wb=dwb, wo=dwo)
    last_slab, last_spans = _pack([small["norm_g"]] + list(extra_small))
    return dx, grads, early_out, (late_out, spans), ([(last_slab, True)], last_spans), rode


def _prep_layer_small(norm_g, f_bias, sgu_w, sgu_b, sgu_ln_g, sgu_ln_b, scw, dww, conf_dw_b, conf_ln_g, conf_ln_b):
    causal = jnp.tril(jnp.ones((CHUNK, CHUNK), bool))
    gw = jnp.where(causal[None], sgu_w, 0.0)
    row = lambda a: a.reshape(1, -1)
    return dict(
        norm_g=row(norm_g),
        f_bias=jnp.pad(row(f_bias), ((0, 0), (0, N_F - N_HEADS))),
        gw=gw.astype(BF16), gwt=gw.transpose(0, 2, 1).astype(BF16),
        gb=jnp.repeat(sgu_b.T, HEAD_DIM, axis=1),
        sgu_ln_g=row(sgu_ln_g), sgu_ln_b=row(sgu_ln_b),
        scw=jnp.pad(scw, ((0, 8 - SHORT_CONV), (0, 0))), dww=jnp.pad(dww, ((0, 32 - CONF_CONV), (0, 0))),
        conf_dw_b=row(conf_dw_b), conf_ln_g=row(conf_ln_g), conf_ln_b=row(conf_ln_b))


def _local_step(x, target, layers, final_g):
    saved = []
    for l in range(DEPTH):
        x, sv = _layer_fwd(x, layers[l], str(l))
        saved.append(sv)
    loss_p, dx, dfg = _loss_head(x, final_g.reshape(1, D_MODEL), target)
    grads = [None] * DEPTH
    for l in reversed(range(DEPTH)):
        dx, grads[l], _, _, _, _ = _layer_bwd(dx, saved[l], layers[l], str(l), False, None, [])
    return 0.5 / D_MODEL * jnp.sum(loss_p), dx, grads, dfg.sum(0)


def _sum8(a, name):
    _, r, c = a.shape
    tr = r
    while tr * c * a.dtype.itemsize * N_DEV > 4 * 1024 * 1024 and tr % 32 == 0:
        tr //= 2

    def body(a_ref, o_ref):
        acc = a_ref[0].astype(F32)
        for d in range(1, N_DEV):
            acc = acc + a_ref[d].astype(F32)
        o_ref[...] = acc

    return pl.pallas_call(
        body, grid=(r // tr,),
        in_specs=[pl.BlockSpec((N_DEV, tr, c), lambda i: (0, i, 0))],
        out_specs=pl.BlockSpec((tr, c), lambda i: (i, 0)),
        out_shape=jax.ShapeDtypeStruct((r, c), F32), compiler_params=_cp(("parallel",)), name=name)(a)


def _adamw(w, g, m, v, name):
    l, r, c = w.shape
    tr = r
    while tr * c * 4 > 1024 * 1024 and tr % 16 == 0:
        tr //= 2
    c1 = 1.0 - ADAM_B1 ** ADAM_STEP
    c2 = 1.0 - ADAM_B2 ** ADAM_STEP

    def body(w_ref, g_ref, m_ref, v_ref, d_ref, mo_ref, vo_ref):
        gv = g_ref[...]
        mn = ADAM_B1 * m_ref[...] + (1.0 - ADAM_B1) * gv
        vn = ADAM_B2 * v_ref[...] + (1.0 - ADAM_B2) * (gv * gv)
        mo_ref[...] = mn
        vo_ref[...] = vn
        d_ref[...] = -ADAM_LR * ((mn / c1) / (jnp.sqrt(vn / c2) + ADAM_EPS) + ADAM_WD * w_ref[...])

    blk = pl.BlockSpec((1, tr, c), lambda a, i: (a, i, 0))
    shp = jax.ShapeDtypeStruct((l, r, c), F32)
    return pl.pallas_call(
        body, grid=(l, r // tr), in_specs=[blk] * 4, out_specs=[blk] * 3, out_shape=[shp] * 3,
        compiler_params=_cp(("parallel", "parallel")), name=name)(w, g, m, v)


def _pack(parts):
    rows, spans, r = [], [], 0
    for p in parts:
        flat = p.reshape(-1)
        nr = -(-flat.shape[0] // 1024) * 8
        rows.append(jnp.pad(flat, (0, nr * 128 - flat.shape[0])).reshape(nr, 128))
        spans.append((r, nr, p.shape))
        r += nr
    return jnp.concatenate(rows, axis=0), spans


def _unpack(slab, spans):
    out = []
    for r, nr, shape in spans:
        size = math.prod(shape)
        out.append(slab[r:r + nr].reshape(-1)[:size].reshape(shape))
    return out


def _split_w_in(w):
    mix = jnp.concatenate([w[..., 0:1536], w[..., 1540:1796], w[..., 3332:3588], w[..., 1796:2820], w[..., 2820:3332]], axis=-1)
    return mix, w[..., 3588:7684], w[..., 1536:1540]


def _join_w_in(mix, merge, f):
    return jnp.concatenate([mix[..., 0:1536], f, mix[..., 1536:1792], mix[..., 2048:3072], mix[..., 3072:3584],
                            mix[..., 1792:2048], merge], axis=-1)


SMALL = ("norm_g", "f_bias", "sgu_w", "sgu_b", "sgu_ln_g", "sgu_ln_b", "short_conv_w", "conf_dw_w", "conf_dw_b",
         "conf_ln_g", "conf_ln_b")


def kernel(x, norm_g, w_in, f_bias, sgu_w, sgu_b, sgu_ln_g, sgu_ln_b, short_conv_w, conf_dw_w, conf_dw_b, conf_ln_g, conf_ln_b, w_branch, w_out, final_g, loss_target, m_norm_g, m_w_in, m_f_bias, m_sgu_w, m_sgu_b, m_sgu_ln_g, m_sgu_ln_b, m_short_conv_w, m_conf_dw_w, m_conf_dw_b, m_conf_ln_g, m_conf_ln_b, m_w_branch, m_w_out, m_final_g, v_norm_g, v_w_in, v_f_bias, v_sgu_w, v_sgu_b, v_sgu_ln_g, v_sgu_ln_b, v_short_conv_w, v_conf_dw_w, v_conf_dw_b, v_conf_ln_g, v_conf_ln_b, v_w_branch, v_w_out, v_final_g):
    me = 4 * lax.axis_index("x") + 2 * lax.axis_index("y") + lax.axis_index("c")
    rows = D_MODEL // N_DEV
    cshard = BR // N_DEV

    sh = []
    for l in range(DEPTH):
        mix, merge, f = _split_w_in(w_in[l])
        sh.append(dict(mix=mix.astype(BF16), merge=merge.astype(BF16),
                       f=jnp.pad(f, ((0, 0), (0, N_F - N_HEADS))).astype(BF16),
                       wb=w_branch[l].astype(BF16), wo=w_out[l].astype(BF16)))
    conv_slab, conv_spans = _pack([short_conv_w, conf_dw_w])
    g_mix, g_f, g_conv = _exchange([(sh[0]["mix"], True), (sh[0]["f"], True), (conv_slab, True)], "gather_first")
    conv_full = [_unpack(g_conv[d], conv_spans) for d in range(N_DEV)]
    scw_full = jnp.concatenate([cf[0] for cf in conv_full], axis=-1)
    dww_full = jnp.concatenate([cf[1] for cf in conv_full], axis=-1)
    layers = [_prep_layer_small(norm_g[l], f_bias[l], sgu_w[l], sgu_b[l], sgu_ln_g[l], sgu_ln_b[l], scw_full[l], dww_full[l],
                                conf_dw_b[l], conf_ln_g[l], conf_ln_b[l]) for l in range(DEPTH)]
    layers[0].update(w_mix=g_mix.reshape(D_MODEL, N_MIX), w_f=g_f.reshape(D_MODEL, N_F))

    def put_in(l):
        def sink(res):
            layers[l].update(w_mix=res[0].reshape(D_MODEL, N_MIX), w_f=res[1].reshape(D_MODEL, N_F),
                             w_merge=res[2].reshape(D_MODEL, N_MERGE))
        return sink

    def put_merge(l):
        def sink(res):
            layers[l].update(w_merge=res[0].reshape(D_MODEL, N_MERGE))
        return sink

    def put_out(l):
        def sink(res):
            layers[l].update(wb=res[0].transpose(1, 2, 0, 3).reshape(N_BRANCH, BR, D_MODEL), wo=res[1].reshape(D_MODEL, D_MODEL))
        return sink

    attach0 = {
        "proj_mix": ([(sh[0]["merge"], True)], put_merge(0)),
        "mix_fwd": ([(sh[0]["wb"], True), (sh[0]["wo"], True)], put_out(0)),
        "attn_fwd": ([(sh[1]["mix"], True), (sh[1]["f"], True), (sh[1]["merge"], True)], put_in(1)),
        "merge_fwd": ([(sh[1]["wb"], True), (sh[1]["wo"], True)], put_out(1)),
    }

    xs = x[0]
    xs, sv0 = _layer_fwd(xs, layers[0], "0", attach0)
    xs, sv1 = _layer_fwd(xs, layers[1], "1")
    loss_p, dx, dfg = _loss_head(xs, final_g.reshape(1, D_MODEL), loss_target[0])
    loss_local = (0.5 / D_MODEL * jnp.sum(loss_p)).reshape(1)
    dx, g1, early1, (late1, spans1), (last1, lspans1), _ = _layer_bwd(dx, sv1, layers[1], "1", True, None, [dfg.sum(0)])
    dx, g0, early0, (late0, spans0), (last0, lspans0), last1_out = _layer_bwd(dx, sv0, layers[0], "0", True, last1, [loss_local])
    last0_out = _exchange(last0, "gather_last")

    red, small = [], []
    for l, (early, late, spans, last, lspans) in enumerate(((early0, late0, spans0, last0_out, lspans0),
                                                            (early1, late1, spans1, last1_out, lspans1))):
        t = str(l)
        red.append(dict(merge=_sum8(early[0], "sum_merge" + t), wb=_sum8(early[1], "sum_wb" + t), wo=_sum8(early[2], "sum_wo" + t),
                        mix=_sum8(late[0], "sum_mix" + t), f=_sum8(late[1], "sum_f" + t)))
        keys = ["norm_g", "final_g" if l == DEPTH - 1 else "loss"] + list(SMALL[1:])
        small.append(dict(zip(keys, _unpack(_sum8(last[0], "sum_last" + t), lspans)
                              + _unpack(_sum8(late[2], "sum_small" + t), spans))))
    gs = {nm: jnp.stack([small[l][nm] for l in range(DEPTH)]) for nm in SMALL}
    gs["final_g"] = small[DEPTH - 1]["final_g"]
    loss = small[0]["loss"][0]
    gs["short_conv_w"] = lax.dynamic_slice_in_dim(gs["short_conv_w"], me * cshard, cshard, axis=2)
    gs["conf_dw_w"] = lax.dynamic_slice_in_dim(gs["conf_dw_w"], me * cshard, cshard, axis=2)
    g_w_in = jnp.stack([_join_w_in(red[l]["mix"], red[l]["merge"], red[l]["f"][:, :N_HEADS]) for l in range(DEPTH)])
    g_w_branch = jnp.stack([red[l]["wb"].reshape(N_BRANCH, BR, rows) for l in range(DEPTH)])
    g_w_out = jnp.stack([red[l]["wo"] for l in range(DEPTH)])

    d_w_in, nm_w_in, nv_w_in = _adamw(w_in, g_w_in, m_w_in, v_w_in, "adamw_w_in")
    flat = lambda a: a.reshape(DEPTH, N_BRANCH * BR, rows)
    d_w_branch, nm_w_branch, nv_w_branch = (a.reshape(w_branch.shape) for a in _adamw(
        flat(w_branch), flat(g_w_branch), flat(m_w_branch), flat(v_w_branch), "adamw_w_branch"))
    d_w_out, nm_w_out, nv_w_out = _adamw(w_out, g_w_out, m_w_out, v_w_out, "adamw_w_out")
    names = SMALL + ("final_g",)
    ws = dict(zip(names, (norm_g, f_bias, sgu_w, sgu_b, sgu_ln_g, sgu_ln_b, short_conv_w, conf_dw_w, conf_dw_b, conf_ln_g,
                          conf_ln_b, final_g)))
    ms = dict(zip(names, (m_norm_g, m_f_bias, m_sgu_w, m_sgu_b, m_sgu_ln_g, m_sgu_ln_b, m_short_conv_w, m_conf_dw_w,
                          m_conf_dw_b, m_conf_ln_g, m_conf_ln_b, m_final_g)))
    vs = dict(zip(names, (v_norm_g, v_f_bias, v_sgu_w, v_sgu_b, v_sgu_ln_g, v_sgu_ln_b, v_short_conv_w, v_conf_dw_w,
                          v_conf_dw_b, v_conf_ln_g, v_conf_ln_b, v_final_g)))
    w_slab, spans = _pack([ws[nm] for nm in names])
    g_slab, _ = _pack([gs[nm] for nm in names])
    m_slab, _ = _pack([ms[nm] for nm in names])
    v_slab, _ = _pack([vs[nm] for nm in names])
    d_s, nm_s, nv_s = (dict(zip(names, _unpack(a[0], spans))) for a in _adamw(w_slab[None], g_slab[None], m_slab[None],
                                                                              v_slab[None], "adamw_small"))

    def ordered(small, w_in_v, w_branch_v, w_out_v):
        return [small["norm_g"], w_in_v, small["f_bias"], small["sgu_w"], small["sgu_b"], small["sgu_ln_g"],
                small["sgu_ln_b"], small["short_conv_w"], small["conf_dw_w"], small["conf_dw_b"], small["conf_ln_g"],
                small["conf_ln_b"], w_branch_v, w_out_v, small["final_g"]]

    return (loss, dx[None], *ordered(gs, g_w_in, g_w_branch, g_w_out), *ordered(d_s, d_w_in, d_w_branch, d_w_out),
            *ordered(nm_s, nm_w_in, nm_w_branch, nm_w_out), *ordered(nv_s, nv_w_in, nv_w_branch, nv_w_out))
```

```python
import functools
import math

import jax
import jax.numpy as jnp
from jax import lax
from jax.experimental import pallas as pl
from jax.experimental.pallas import tpu as pltpu

F32 = jnp.float32
BF16 = jnp.bfloat16

D_MODEL = 1024
DEPTH = 2
N_BRANCH = 4
BR = 256
N_HEADS = 4
HEAD_DIM = 64
CHUNK = 128
SHORT_CONV = 3
CONF_CONV = 31
EPS = 1e-6
N_DEV = 8

ADAM_LR = 0.001
ADAM_B1 = 0.9
ADAM_B2 = 0.999
ADAM_EPS = 1e-08
ADAM_WD = 0.01
ADAM_STEP = 10

O_UV, O_AG, O_QKV, O_BG, O_DG, O_CIN, O_CG, O_GLU = 0, 512, 768, 1536, 1792, 2048, 2816, 3072
N_MIX = 3584
N_MERGE = N_BRANCH * D_MODEL
N_F = 128
IN_COLS = 7684
HALO = 32
TM = 512
TMG = 256
BQ = 512
SUB = 64
CUMB = 512
VMEM_LIMIT = 56 * 1024 * 1024
NEG = -1e30
SCALE = 1.0 / math.sqrt(HEAD_DIM)
GELU_K = math.sqrt(2.0 / math.pi)


def _cp(sem=None):
    return pltpu.CompilerParams(dimension_semantics=sem, vmem_limit_bytes=VMEM_LIMIT)


PEER_ORDER = (6, 4, 2, 7, 5, 3, 1)
RELAYED = (3, 5, 7)


def _xchg(cin, cout, send, recv, loc, modes, phase):
    x, y, c = lax.axis_index("x"), lax.axis_index("y"), lax.axis_index("c")
    me = 4 * x + 2 * y + c

    def peer_of(kk):
        px, py, pc = lax.rem(x + (kk >> 2 & 1), 2), lax.rem(y + (kk >> 1 & 1), 2), lax.rem(c + (kk & 1), 2)
        return (px, py, pc), 4 * px + 2 * py + pc

    def remote(src, dst, a, kk, pid):
        return pltpu.make_async_remote_copy(src_ref=src, dst_ref=dst, send_sem=send.at[a, kk], recv_sem=recv.at[a, kk],
                                            device_id=pid, device_id_type=pl.DeviceIdType.MESH)

    def outgoing(a, kk):
        if modes[a] and kk in RELAYED:
            _, origin = peer_of(kk - 1)
            return remote(cout[a].at[origin], cout[a].at[origin], a, kk, peer_of(1)[0])
        pid, peer = peer_of(kk)
        return remote(cin[a] if modes[a] else cin[a].at[peer], cout[a].at[me], a, kk, pid)

    def arrival(a, kk):
        _, peer = peer_of(kk)
        return remote(cout[a].at[peer], cout[a].at[peer], a, kk, (x, y, c))

    if phase == "relay":
        for kk in RELAYED:
            for a, gather in enumerate(modes):
                if gather:
                    arrival(a, kk - 1).wait_recv()
                    outgoing(a, kk).start()
        return
    for a, gather in enumerate(modes):
        cp = pltpu.make_async_copy(cin[a] if gather else cin[a].at[me], cout[a].at[me], loc.at[a])
        if phase == "start":
            cp.start()
        else:
            cp.wait()
    if phase == "start":
        for kk in PEER_ORDER:
            for a, gather in enumerate(modes):
                if not (gather and kk in RELAYED):
                    outgoing(a, kk).start()
        return
    for kk in PEER_ORDER:
        for a in range(len(modes)):
            outgoing(a, kk).wait_send()
    for kk in PEER_ORDER:
        for a, gather in enumerate(modes):
            if not (gather and kk + 1 in RELAYED):
                arrival(a, kk).wait_recv()


def _xchg_shapes(comm):
    return [jax.ShapeDtypeStruct((N_DEV,) + tuple(a.shape[(0 if gather else 1):]), a.dtype) for a, gather in comm]


def _xchg_sems(n):
    return [pltpu.SemaphoreType.DMA((n, N_DEV)), pltpu.SemaphoreType.DMA((n, N_DEV)), pltpu.SemaphoreType.DMA((n,))]


def _exchange(comm, name):
    n = len(comm)
    modes = [g for _, g in comm]

    def body(*refs):
        cin, cout, (send, recv, loc) = refs[:n], refs[n:2 * n], refs[2 * n:]
        for phase in ("start", "relay", "finish"):
            _xchg(cin, cout, send, recv, loc, modes, phase)

    anyspec = pl.BlockSpec(memory_space=pl.ANY)
    return pl.pallas_call(
        body, in_specs=[anyspec] * n, out_specs=[anyspec] * n, out_shape=_xchg_shapes(comm),
        scratch_shapes=_xchg_sems(n), name=name)(*[a for a, _ in comm])


def _pcall(body, *, grid, in_specs, out_specs, out_shape, operands, name, scratch_shapes=(), comm=None):
    if not comm:
        outs = pl.pallas_call(
            body, grid=grid, in_specs=in_specs, out_specs=out_specs, out_shape=out_shape, scratch_shapes=list(scratch_shapes),
            compiler_params=_cp(("arbitrary",) * len(grid)), name=name)(*operands)
        return list(outs), []
    n, nin, nout, nsc = len(comm), len(operands), len(out_shape), len(scratch_shapes)
    modes = [g for _, g in comm]

    def wrapped(*refs):
        ins, cin = refs[:nin], refs[nin:nin + n]
        outs, cout = refs[nin + n:nin + n + nout], refs[nin + n + nout:nin + 2 * n + nout]
        scratch = refs[nin + 2 * n + nout:]
        own, (send, recv, loc) = scratch[:nsc], scratch[nsc:]
        ids = [pl.program_id(d) for d in range(len(grid))]
        first = functools.reduce(jnp.logical_and, [i == 0 for i in ids])
        last = functools.reduce(jnp.logical_and, [i == g - 1 for i, g in zip(ids, grid)])

        @pl.when(first)
        def _():
            _xchg(cin, cout, send, recv, loc, modes, "start")

        @pl.when(last)
        def _():
            _xchg(cin, cout, send, recv, loc, modes, "relay")

        body(*ins, *outs, *own)

        @pl.when(last)
        def _():
            _xchg(cin, cout, send, recv, loc, modes, "finish")

    anyspec = pl.BlockSpec(memory_space=pl.ANY)
    res = pl.pallas_call(
        wrapped, grid=grid, in_specs=list(in_specs) + [anyspec] * n, out_specs=list(out_specs) + [anyspec] * n,
        out_shape=list(out_shape) + _xchg_shapes(comm), scratch_shapes=list(scratch_shapes) + _xchg_sems(n),
        compiler_params=_cp(("arbitrary",) * len(grid)), name=name)(*operands, *[a for a, _ in comm])
    return list(res[:nout]), list(res[nout:])


def _sig(x):
    return 0.5 * jnp.tanh(0.5 * x) + 0.5


def _silu(x):
    return x * _sig(x)


def _dsilu(x):
    s = _sig(x)
    return s * (1.0 + x * (1.0 - s))


def _gelu(x):
    return 0.5 * x * (1.0 + jnp.tanh(GELU_K * (x + 0.044715 * x * x * x)))


def _dgelu(x):
    t = jnp.tanh(GELU_K * (x + 0.044715 * x * x * x))
    return 0.5 * (1.0 + t) + 0.5 * x * (1.0 - t * t) * GELU_K * (1.0 + 3.0 * 0.044715 * x * x)


def _ln_hat(x):
    mu = jnp.mean(x, axis=-1, keepdims=True)
    xc = x - mu
    rs = lax.rsqrt(jnp.mean(xc * xc, axis=-1, keepdims=True) + EPS)
    return xc * rs, rs


def _ln_bwd(dhat, hat, rs):
    return rs * (dhat - jnp.mean(dhat, axis=-1, keepdims=True) - hat * jnp.mean(dhat * hat, axis=-1, keepdims=True))


def _dot(a, b):
    return jnp.dot(a, b, preferred_element_type=F32)


def _dot_nt(a, b):
    return lax.dot_general(a, b, (((1,), (1,)), ((), ())), preferred_element_type=F32)


def _dot_tn(a, b):
    return lax.dot_general(a, b, (((0,), (0,)), ((), ())), preferred_element_type=F32)


def _fold8(x):
    acc = x[0:8]
    for r in range(1, x.shape[0] // 8):
        acc = acc + x[8 * r:8 * r + 8]
    return acc


def _rms_fwd(x, g, name, comm=None):
    s = x.shape[0]

    def body(x_ref, g_ref, h_ref):
        xv = x_ref[...]
        r = lax.rsqrt(jnp.mean(xv * xv, axis=-1, keepdims=True) + EPS)
        h_ref[...] = (xv * r * g_ref[...]).astype(BF16)

    (h,), couts = _pcall(
        body, grid=(s // TM,),
        in_specs=[pl.BlockSpec((TM, D_MODEL), lambda i: (i, 0)), pl.BlockSpec((1, D_MODEL), lambda i: (0, 0))],
        out_specs=[pl.BlockSpec((TM, D_MODEL), lambda i: (i, 0))],
        out_shape=[jax.ShapeDtypeStruct((s, D_MODEL), BF16)], operands=(x, g), name=name, comm=comm)
    return h, couts


def _rms_bwd(dh, x, g, dx_next, name):
    s = x.shape[0]

    def body(dh_ref, x_ref, g_ref, dxn_ref, dx_ref, dg_ref):
        i = pl.program_id(0)
        xv = x_ref[...]
        r = lax.rsqrt(jnp.mean(xv * xv, axis=-1, keepdims=True) + EPS)
        xn = xv * r
        dhv = dh_ref[...]
        dxn = dhv * g_ref[...]
        dx_ref[...] = dxn_ref[...] + r * (dxn - xn * jnp.mean(dxn * xn, axis=-1, keepdims=True))

        @pl.when(i == 0)
        def _():
            dg_ref[...] = jnp.zeros_like(dg_ref)

        dg_ref[...] += _fold8(dhv * xn)

    tile = pl.BlockSpec((TM, D_MODEL), lambda i: (i, 0))
    return pl.pallas_call(
        body, grid=(s // TM,),
        in_specs=[tile, tile, pl.BlockSpec((1, D_MODEL), lambda i: (0, 0)), tile],
        out_specs=[tile, pl.BlockSpec((8, D_MODEL), lambda i: (0, 0))],
        out_shape=[jax.ShapeDtypeStruct((s, D_MODEL), F32), jax.ShapeDtypeStruct((8, D_MODEL), F32)],
        compiler_params=_cp(("arbitrary",)), name=name)(dh, x, g, dx_next)


def _loss_head(x, g, target):
    s = x.shape[0]

    def body(x_ref, g_ref, t_ref, loss_ref, dx_ref, dg_ref):
        i = pl.program_id(0)
        xv = x_ref[...]
        r = lax.rsqrt(jnp.mean(xv * xv, axis=-1, keepdims=True) + EPS)
        xn = xv * r
        err = xn * g_ref[...] - t_ref[...]
        dy = err * (1.0 / D_MODEL)
        dxn = dy * g_ref[...]
        dx_ref[...] = r * (dxn - xn * jnp.mean(dxn * xn, axis=-1, keepdims=True))

        @pl.when(i == 0)
        def _():
            dg_ref[...] = jnp.zeros_like(dg_ref)
            loss_ref[...] = jnp.zeros_like(loss_ref)

        dg_ref[...] += _fold8(dy * xn)
        loss_ref[...] += _fold8(err * err)

    tile = pl.BlockSpec((TM, D_MODEL), lambda i: (i, 0))
    acc = pl.BlockSpec((8, D_MODEL), lambda i: (0, 0))
    return pl.pallas_call(
        body, grid=(s // TM,),
        in_specs=[tile, pl.BlockSpec((1, D_MODEL), lambda i: (0, 0)), tile],
        out_specs=[acc, tile, acc],
        out_shape=[jax.ShapeDtypeStruct((8, D_MODEL), F32), jax.ShapeDtypeStruct((s, D_MODEL), F32),
                   jax.ShapeDtypeStruct((8, D_MODEL), F32)],
        compiler_params=_cp(("arbitrary",)), name="loss_head")(x, g, target)


def _mm_nn(a, b, tn, name, comm=None):
    m, k = a.shape
    n = b.shape[1]
    tm = 512

    def body(a_ref, b_ref, o_ref):
        o_ref[...] = _dot(a_ref[...], b_ref[...])

    (out,), couts = _pcall(
        body, grid=(n // tn, m // tm),
        in_specs=[pl.BlockSpec((tm, k), lambda j, i: (i, 0)), pl.BlockSpec((k, tn), lambda j, i: (0, j))],
        out_specs=[pl.BlockSpec((tm, tn), lambda j, i: (i, j))],
        out_shape=[jax.ShapeDtypeStruct((m, n), F32)], operands=(a, b), name=name, comm=comm)
    return out, couts


def _dh(dpm, dpg, dpf, w_mix, w_merge, w_f, name, comm=None):
    s = dpm.shape[0]
    tm = 1024 if s % 1024 == 0 else 512
    tk1, tk2 = 896, 1024
    n1, n2 = N_MIX // tk1, N_MERGE // tk2

    def body(dpm_ref, dpg_ref, dpf_ref, wm_ref, wg_ref, wf_ref, o_ref):
        j = pl.program_id(1)

        @pl.when(j == 0)
        def _():
            o_ref[...] = _dot_nt(dpf_ref[...], wf_ref[...])

        @pl.when(j < n1)
        def _():
            o_ref[...] += _dot_nt(dpm_ref[...], wm_ref[...])

        @pl.when(j >= n1)
        def _():
            o_ref[...] += _dot_nt(dpg_ref[...], wg_ref[...])

    mix_j = lambda j: jnp.minimum(j, n1 - 1)
    merge_j = lambda j: jnp.maximum(j - n1, 0)
    (out,), couts = _pcall(
        body, grid=(s // tm, n1 + n2),
        in_specs=[pl.BlockSpec((tm, tk1), lambda i, j: (i, mix_j(j))), pl.BlockSpec((tm, tk2), lambda i, j: (i, merge_j(j))),
                  pl.BlockSpec((tm, N_F), lambda i, j: (i, 0)),
                  pl.BlockSpec((D_MODEL, tk1), lambda i, j: (0, mix_j(j))), pl.BlockSpec((D_MODEL, tk2), lambda i, j: (0, merge_j(j))),
                  pl.BlockSpec((D_MODEL, N_F), lambda i, j: (0, 0))],
        out_specs=[pl.BlockSpec((tm, D_MODEL), lambda i, j: (i, 0))],
        out_shape=[jax.ShapeDtypeStruct((s, D_MODEL), F32)], operands=(dpm, dpg, dpf, w_mix, w_merge, w_f), name=name, comm=comm)
    return out, couts


def _mm_tn(a, d, tn, name):
    m, k = a.shape
    n = d.shape[1]
    tm = 512
    nm = m // tm

    def body(a_ref, d_ref, o_ref, acc):
        i = pl.program_id(1)

        @pl.when(i == 0)
        def _():
            acc[...] = jnp.zeros_like(acc)

        acc[...] += _dot_tn(a_ref[...], d_ref[...])

        @pl.when(i == nm - 1)
        def _():
            o_ref[...] = acc[...].astype(BF16)

    return pl.pallas_call(
        body, grid=(n // tn, nm),
        in_specs=[pl.BlockSpec((tm, k), lambda j, i: (i, 0)), pl.BlockSpec((tm, tn), lambda j, i: (i, j))],
        out_specs=pl.BlockSpec((k, tn), lambda j, i: (0, j)),
        out_shape=jax.ShapeDtypeStruct((k, n), BF16), scratch_shapes=[pltpu.VMEM((k, tn), F32)],
        compiler_params=_cp(("parallel", "arbitrary")), name=name)(a, d)


def _lane_head():
    return lax.broadcasted_iota(jnp.int32, (1, BR), 1) // HEAD_DIM


def _gmlp_chunk_fwd(p_ref, r0, gw_ref, gb_ref, lg, lb):
    uv = p_ref[r0:r0 + CHUNK, O_UV:O_UV + 2 * BR]
    u = _gelu(uv[:, :BR])
    vhat, rs = _ln_hat(_gelu(uv[:, BR:]))
    vn = (vhat * lg + lb).astype(BF16)
    head = _lane_head()
    mixed = gb_ref[...]
    for h in range(N_HEADS):
        mixed = mixed + jnp.where(head == h, _dot(gw_ref[h], vn), 0.0)
    return uv, u, vhat, rs, vn, mixed


def _tap_groups(k_width):
    groups = []
    for b in range(8):
        taps = [(d // 8, k_width - 1 - d) for d in range(b, k_width, 8)]
        if taps:
            groups.append((b, taps))
    return groups


def _causal_taps(buf, off, r0, nr, k_width):
    lead = 8 * ((k_width - 1) // 8 + 1)
    win = buf[off + r0 - lead:off + r0 + nr, :]
    for b, taps in _tap_groups(k_width):
        shifted = win if b == 0 else pltpu.roll(win, b, 0)
        for a, k in taps:
            yield k, shifted[lead - 8 * a:lead - 8 * a + nr]


def _anticausal_taps(buf, r0, nr, k_width):
    lead = 8 * ((k_width - 1) // 8 + 1)
    win = buf[r0:r0 + nr + lead, :]
    for b, taps in _tap_groups(k_width):
        shifted = win if b == 0 else pltpu.roll(win, nr + lead - b, 0)
        for a, k in taps:
            yield k, shifted[8 * a:8 * a + nr]


def _conv_sub_blocks(rows):
    out = [(r, SUB) for r in range(0, rows - rows % SUB, SUB)]
    if rows % SUB:
        out.append((rows - rows % SUB, rows % SUB))
    return out


def _mix_fwd(pm, wl, name, comm=None):
    s = pm.shape[0]
    nt = s // TM

    def body(p_ref, ph_ref, gw_ref, gb_ref, lg_ref, lb_ref, scw_ref, dww_ref, dwb_ref, clg_ref, clb_ref,
             ya_ref, yc_ref, yd_ref, q_ref, k_ref, v_ref, cc_ref, zbuf, hbuf):
        i = pl.program_id(0)
        for h in range(N_HEADS):
            c0 = O_QKV + h * HEAD_DIM
            q_ref[h] = (p_ref[:, c0:c0 + HEAD_DIM] * SCALE).astype(BF16)
            k_ref[h] = p_ref[:, c0 + BR:c0 + BR + HEAD_DIM].astype(BF16)
            v_ref[h] = p_ref[:, c0 + 2 * BR:c0 + 2 * BR + HEAD_DIM].astype(BF16)
        lg = lg_ref[...]
        lb = lb_ref[...]
        for c in range(TM // CHUNK):
            r0 = c * CHUNK
            _, u, _, _, _, mixed = _gmlp_chunk_fwd(p_ref, r0, gw_ref, gb_ref, lg, lb)
            ag = p_ref[r0:r0 + CHUNK, O_AG:O_AG + BR]
            ya_ref[r0:r0 + CHUNK, :] = (u * mixed * _silu(ag)).astype(BF16)

        first = i > 0
        zbuf[0:HALO, :] = jnp.where(first, ph_ref[:, O_CIN + BR:O_CIN + 2 * BR] * ph_ref[:, O_CIN + 2 * BR:O_CIN + 3 * BR], 0.0)
        zbuf[HALO:HALO + TM, :] = p_ref[:, O_CIN + BR:O_CIN + 2 * BR] * p_ref[:, O_CIN + 2 * BR:O_CIN + 3 * BR]
        hbuf[0:HALO, :] = jnp.where(first, ph_ref[:, O_GLU:O_GLU + BR] * _sig(ph_ref[:, O_GLU + BR:O_GLU + 2 * BR]), 0.0)
        hbuf[HALO:HALO + TM, :] = p_ref[:, O_GLU:O_GLU + BR] * _sig(p_ref[:, O_GLU + BR:O_GLU + 2 * BR])

        clg = clg_ref[...]
        clb = clb_ref[...]
        for r0, nr in _conv_sub_blocks(TM):
            yc = jnp.zeros((nr, BR), F32)
            for k, zk in _causal_taps(zbuf, HALO, r0, nr, SHORT_CONV):
                yc = yc + scw_ref[k:k + 1, :] * zk
            bgate = p_ref[r0:r0 + nr, O_CIN:O_CIN + BR]
            cg = p_ref[r0:r0 + nr, O_CG:O_CG + BR]
            yc_ref[r0:r0 + nr, :] = (bgate * yc * _silu(cg)).astype(BF16)

            cc = jnp.zeros((nr, BR), F32) + dwb_ref[...]
            for k, hk in _causal_taps(hbuf, HALO, r0, nr, CONF_CONV):
                cc = cc + dww_ref[k:k + 1, :] * hk
            cc_ref[r0:r0 + nr, :] = cc
            chat, _ = _ln_hat(cc)
            dg = p_ref[r0:r0 + nr, O_DG:O_DG + BR]
            yd_ref[r0:r0 + nr, :] = (_silu(chat * clg + clb) * _silu(dg)).astype(BF16)

    full = lambda shape: pl.BlockSpec(shape, lambda i: tuple(0 for _ in shape))
    ytile = pl.BlockSpec((TM, BR), lambda i: (i, 0))
    yshape = jax.ShapeDtypeStruct((s, BR), BF16)
    htile = pl.BlockSpec((N_HEADS, TM, HEAD_DIM), lambda i: (0, i, 0))
    hshape = jax.ShapeDtypeStruct((N_HEADS, s, HEAD_DIM), BF16)
    return _pcall(
        body, grid=(nt,),
        in_specs=[pl.BlockSpec((TM, N_MIX), lambda i: (i, 0)),
                  pl.BlockSpec((HALO, N_MIX), lambda i: (jnp.maximum(i * (TM // HALO) - 1, 0), 0)),
                  full((N_HEADS, CHUNK, CHUNK)), full((CHUNK, BR)), full((1, BR)), full((1, BR)),
                  full((8, BR)), full((32, BR)), full((1, BR)), full((1, BR)), full((1, BR))],
        out_specs=[ytile, ytile, ytile, htile, htile, htile, ytile],
        out_shape=[yshape, yshape, yshape, hshape, hshape, hshape, jax.ShapeDtypeStruct((s, BR), F32)],
        scratch_shapes=[pltpu.VMEM((HALO + TM, BR), F32), pltpu.VMEM((HALO + TM, BR), F32)],
        name=name, comm=comm, operands=(
            pm, pm, wl["gw"], wl["gb"], wl["sgu_ln_g"], wl["sgu_ln_b"], wl["scw"], wl["dww"], wl["conf_dw_b"],
            wl["conf_ln_g"], wl["conf_ln_b"]))


def _mix_bwd(pm, cc, dya, dyc, dyd, dqkv, dbg, wl, name):
    s = pm.shape[0]
    nt = s // TM
    ext = TM + HALO

    def body(p_ref, ph_ref, pn_ref, cc_ref, ccn_ref, dya_ref, dyc_ref, dycn_ref, dyd_ref, dydn_ref, dq_ref, dk_ref, dv_ref, dbg_ref,
             gw_ref, gwt_ref, gb_ref, lg_ref, lb_ref, scw_ref, dww_ref, dwb_ref, clg_ref, clb_ref,
             dp_ref, dgw_ref, dgb_ref, vec_ref, dscw_ref, ddww_ref, zbuf, dcb, hbuf, dcc):
        i = pl.program_id(0)

        @pl.when(i == 0)
        def _():
            dgw_ref[...] = jnp.zeros_like(dgw_ref)
            dgb_ref[...] = jnp.zeros_like(dgb_ref)
            vec_ref[...] = jnp.zeros_like(vec_ref)
            dscw_ref[...] = jnp.zeros_like(dscw_ref)
            ddww_ref[...] = jnp.zeros_like(ddww_ref)

        lg = lg_ref[...]
        lb = lb_ref[...]
        head = _lane_head()
        d_lg = jnp.zeros((1, BR), F32)
        d_lb = jnp.zeros((1, BR), F32)
        for c in range(TM // CHUNK):
            r0 = c * CHUNK
            uv, u, vhat, rs, vn, mixed = _gmlp_chunk_fwd(p_ref, r0, gw_ref, gb_ref, lg, lb)
            ag = p_ref[r0:r0 + CHUNK, O_AG:O_AG + BR]
            dy = dya_ref[r0:r0 + CHUNK, :]
            sa = _silu(ag)
            du = dy * mixed * sa
            dmx = dy * u * sa
            dp_ref[r0:r0 + CHUNK, O_AG:O_AG + BR] = (dy * u * mixed * _dsilu(ag)).astype(BF16)
            dgb_ref[...] += dmx
            dmx_b = dmx.astype(BF16)
            dvn = jnp.zeros((CHUNK, BR), F32)
            for h in range(N_HEADS):
                sel = head == h
                dgw_ref[h] += _dot_nt(jnp.where(sel, dmx, 0.0).astype(BF16), vn)
                dvn = dvn + jnp.where(sel, _dot(gwt_ref[h], dmx_b), 0.0)
            d_lg = d_lg + jnp.sum(dvn * vhat, axis=0, keepdims=True)
            d_lb = d_lb + jnp.sum(dvn, axis=0, keepdims=True)
            dv0 = _ln_bwd(dvn * lg, vhat, rs)
            dp_ref[r0:r0 + CHUNK, O_UV:O_UV + BR] = (du * _dgelu(uv[:, :BR])).astype(BF16)
            dp_ref[r0:r0 + CHUNK, O_UV + BR:O_UV + 2 * BR] = (dv0 * _dgelu(uv[:, BR:])).astype(BF16)
        vec_ref[0:1, :] += d_lg
        vec_ref[1:2, :] += d_lb

        for j, g_ref in enumerate((dq_ref, dk_ref, dv_ref)):
            dp_ref[:, O_QKV + j * BR:O_QKV + (j + 1) * BR] = jnp.concatenate(
                [g_ref[h] for h in range(N_HEADS)], axis=1).astype(BF16)
        dp_ref[:, O_BG:O_BG + BR] = dbg_ref[...].astype(BF16)

        first = i > 0
        last = i < nt - 1
        zbuf[0:HALO, :] = jnp.where(first, ph_ref[:, O_CIN + BR:O_CIN + 2 * BR] * ph_ref[:, O_CIN + 2 * BR:O_CIN + 3 * BR], 0.0)
        zbuf[HALO:HALO + TM, :] = p_ref[:, O_CIN + BR:O_CIN + 2 * BR] * p_ref[:, O_CIN + 2 * BR:O_CIN + 3 * BR]
        dcb[0:TM, :] = dyc_ref[...] * p_ref[:, O_CIN:O_CIN + BR] * _silu(p_ref[:, O_CG:O_CG + BR])
        dcb[TM:ext, :] = jnp.where(last, dycn_ref[...] * pn_ref[:, O_CIN:O_CIN + BR] * _silu(pn_ref[:, O_CG:O_CG + BR]), 0.0)
        for r0, nr in _conv_sub_blocks(TM):
            yc = jnp.zeros((nr, BR), F32)
            dz = jnp.zeros((nr, BR), F32)
            dcur = dcb[r0:r0 + nr, :]
            for k, zk in _causal_taps(zbuf, HALO, r0, nr, SHORT_CONV):
                yc = yc + scw_ref[k:k + 1, :] * zk
                dscw_ref[8 * k:8 * k + 8, :] += _fold8(dcur * zk)
            for k, dk in _anticausal_taps(dcb, r0, nr, SHORT_CONV):
                dz = dz + scw_ref[k:k + 1, :] * dk
            dy = dyc_ref[r0:r0 + nr, :]
            bgate = p_ref[r0:r0 + nr, O_CIN:O_CIN + BR]
            cg = p_ref[r0:r0 + nr, O_CG:O_CG + BR]
            dp_ref[r0:r0 + nr, O_CIN:O_CIN + BR] = (dy * yc * _silu(cg)).astype(BF16)
            dp_ref[r0:r0 + nr, O_CIN + BR:O_CIN + 2 * BR] = (dz * p_ref[r0:r0 + nr, O_CIN + 2 * BR:O_CIN + 3 * BR]).astype(BF16)
            dp_ref[r0:r0 + nr, O_CIN + 2 * BR:O_CIN + 3 * BR] = (dz * p_ref[r0:r0 + nr, O_CIN + BR:O_CIN + 2 * BR]).astype(BF16)
            dp_ref[r0:r0 + nr, O_CG:O_CG + BR] = (dy * bgate * yc * _dsilu(cg)).astype(BF16)

        hbuf[0:HALO, :] = jnp.where(first, ph_ref[:, O_GLU:O_GLU + BR] * _sig(ph_ref[:, O_GLU + BR:O_GLU + 2 * BR]), 0.0)
        hbuf[HALO:HALO + TM, :] = p_ref[:, O_GLU:O_GLU + BR] * _sig(p_ref[:, O_GLU + BR:O_GLU + 2 * BR])
        clg = clg_ref[...]
        clb = clb_ref[...]
        d_clg = jnp.zeros((1, BR), F32)
        d_clb = jnp.zeros((1, BR), F32)
        d_dwb = jnp.zeros((1, BR), F32)
        for r0, nr in _conv_sub_blocks(ext):
            in_tile = r0 < TM
            chat, rs = _ln_hat(cc_ref[r0:r0 + nr, :] if in_tile else ccn_ref[...])
            ln = chat * clg + clb
            if in_tile:
                dy = dyd_ref[r0:r0 + nr, :]
                dg = p_ref[r0:r0 + nr, O_DG:O_DG + BR]
            else:
                dy = jnp.where(last, dydn_ref[...], 0.0)
                dg = pn_ref[:, O_DG:O_DG + BR]
            dln = dy * _silu(dg) * _dsilu(ln)
            dc = _ln_bwd(dln * clg, chat, rs)
            dcc[r0:r0 + nr, :] = dc
            if in_tile:
                dp_ref[r0:r0 + nr, O_DG:O_DG + BR] = (dy * _silu(ln) * _dsilu(dg)).astype(BF16)
                d_clg = d_clg + jnp.sum(dln * chat, axis=0, keepdims=True)
                d_clb = d_clb + jnp.sum(dln, axis=0, keepdims=True)
                d_dwb = d_dwb + jnp.sum(dc, axis=0, keepdims=True)
        vec_ref[2:3, :] += d_dwb
        vec_ref[3:4, :] += d_clg
        vec_ref[4:5, :] += d_clb
        for r0, nr in _conv_sub_blocks(TM):
            dcur = dcc[r0:r0 + nr, :]
            dhh = jnp.zeros((nr, BR), F32)
            for k, hk in _causal_taps(hbuf, HALO, r0, nr, CONF_CONV):
                ddww_ref[8 * k:8 * k + 8, :] += _fold8(dcur * hk)
            for k, dk in _anticausal_taps(dcc, r0, nr, CONF_CONV):
                dhh = dhh + dww_ref[k:k + 1, :] * dk
            a = p_ref[r0:r0 + nr, O_GLU:O_GLU + BR]
            sg = _sig(p_ref[r0:r0 + nr, O_GLU + BR:O_GLU + 2 * BR])
            dp_ref[r0:r0 + nr, O_GLU:O_GLU + BR] = (dhh * sg).astype(BF16)
            dp_ref[r0:r0 + nr, O_GLU + BR:O_GLU + 2 * BR] = (dhh * a * sg * (1.0 - sg)).astype(BF16)

    full = lambda shape: pl.BlockSpec(shape, lambda i: tuple(0 for _ in shape))
    rpt = TM // HALO
    prev_map = lambda i: (jnp.maximum(i * rpt - 1, 0), 0)
    next_map = lambda i: (jnp.minimum((i + 1) * rpt, nt * rpt - 1), 0)
    ytile = pl.BlockSpec((TM, BR), lambda i: (i, 0))
    htile = pl.BlockSpec((N_HEADS, TM, HEAD_DIM), lambda i: (0, i, 0))
    return pl.pallas_call(
        body, grid=(nt,),
        in_specs=[pl.BlockSpec((TM, N_MIX), lambda i: (i, 0)), pl.BlockSpec((HALO, N_MIX), prev_map),
                  pl.BlockSpec((HALO, N_MIX), next_map),
                  ytile, pl.BlockSpec((HALO, BR), next_map),
                  ytile, ytile, pl.BlockSpec((HALO, BR), next_map), ytile, pl.BlockSpec((HALO, BR), next_map),
                  htile, htile, htile, ytile,
                  full((N_HEADS, CHUNK, CHUNK)), full((N_HEADS, CHUNK, CHUNK)), full((CHUNK, BR)), full((1, BR)), full((1, BR)),
                  full((8, BR)), full((32, BR)), full((1, BR)), full((1, BR)), full((1, BR))],
        out_specs=[pl.BlockSpec((TM, N_MIX), lambda i: (i, 0)), full((N_HEADS, CHUNK, CHUNK)), full((CHUNK, BR)),
                   full((16, BR)), full((64, BR)), full((256, BR))],
        out_shape=[jax.ShapeDtypeStruct((s, N_MIX), BF16), jax.ShapeDtypeStruct((N_HEADS, CHUNK, CHUNK), F32),
                   jax.ShapeDtypeStruct((CHUNK, BR), F32), jax.ShapeDtypeStruct((16, BR), F32),
                   jax.ShapeDtypeStruct((64, BR), F32), jax.ShapeDtypeStruct((256, BR), F32)],
        scratch_shapes=[pltpu.VMEM((HALO + TM, BR), F32), pltpu.VMEM((ext, BR), F32),
                        pltpu.VMEM((HALO + TM, BR), F32), pltpu.VMEM((ext, BR), F32)],
        compiler_params=_cp(("arbitrary",)), name=name)(
            pm, pm, pm, cc, cc, dya, dyc, dyc, dyd, dyd, *dqkv, dbg,
            wl["gw"], wl["gwt"], wl["gb"], wl["sgu_ln_g"], wl["sgu_ln_b"], wl["scw"], wl["dww"], wl["conf_dw_b"],
            wl["conf_ln_g"], wl["conf_ln_b"])


def _tri(lower):
    r = lax.broadcasted_iota(jnp.int32, (CUMB, CUMB), 0)
    c = lax.broadcasted_iota(jnp.int32, (CUMB, CUMB), 1)
    return jnp.where((r >= c) if lower else (r <= c), 1.0, 0.0).astype(F32)


def _dot_hi(a, b):
    return jnp.dot(a, b, preferred_element_type=F32, precision=lax.Precision.HIGHEST)


def _cum_fwd(pf, fb, name):
    s = pf.shape[0]

    def body(pf_ref, fb_ref, cum_ref, carry):
        i = pl.program_id(0)

        @pl.when(i == 0)
        def _():
            carry[...] = jnp.zeros_like(carry)

        z = pf_ref[...] + fb_ref[...]
        logf = jnp.minimum(z, 0.0) - jnp.log(1.0 + jnp.exp(-jnp.abs(z)))
        cum_ref[...] = _dot_hi(_tri(True), logf) + carry[...]
        carry[...] += jnp.sum(logf, axis=0, keepdims=True)

    return pl.pallas_call(
        body, grid=(s // CUMB,),
        in_specs=[pl.BlockSpec((CUMB, N_F), lambda i: (i, 0)), pl.BlockSpec((1, N_F), lambda i: (0, 0))],
        out_specs=pl.BlockSpec((CUMB, N_F), lambda i: (i, 0)),
        out_shape=jax.ShapeDtypeStruct((s, N_F), F32),
        scratch_shapes=[pltpu.VMEM((1, N_F), F32)],
        compiler_params=_cp(("arbitrary",)), name=name)(pf, fb)


def _cum_bwd(dcq, dck, pf, fb, name):
    s = pf.shape[0]
    nb = s // CUMB

    def body(dcq_ref, dck_ref, pf_ref, fb_ref, dpf_ref, dfb_ref, carry):
        i = pl.program_id(0)

        @pl.when(i == 0)
        def _():
            carry[...] = jnp.zeros_like(carry)
            dfb_ref[...] = jnp.zeros_like(dfb_ref)

        lane = lax.broadcasted_iota(jnp.int32, (1, N_F), 1)
        dc = dck_ref[...]
        for h in range(N_HEADS):
            dc = dc + jnp.where(lane == h, dcq_ref[h], 0.0)
        dlogf = _dot_hi(_tri(False), dc) + carry[...]
        carry[...] += jnp.sum(dc, axis=0, keepdims=True)
        z = pf_ref[...] + fb_ref[...]
        dz = dlogf * (1.0 - _sig(z))
        dpf_ref[...] = dz.astype(BF16)
        dfb_ref[...] += _fold8(dz)

    rev = lambda i: (nb - 1 - i, 0)
    return pl.pallas_call(
        body, grid=(nb,),
        in_specs=[pl.BlockSpec((N_HEADS, CUMB, 1), lambda i: (0, nb - 1 - i, 0)), pl.BlockSpec((CUMB, N_F), rev),
                  pl.BlockSpec((CUMB, N_F), rev), pl.BlockSpec((1, N_F), lambda i: (0, 0))],
        out_specs=[pl.BlockSpec((CUMB, N_F), rev), pl.BlockSpec((8, N_F), lambda i: (0, 0))],
        out_shape=[jax.ShapeDtypeStruct((s, N_F), BF16), jax.ShapeDtypeStruct((8, N_F), F32)],
        scratch_shapes=[pltpu.VMEM((1, N_F), F32)],
        compiler_params=_cp(("arbitrary",)), name=name)(dcq, dck, pf, fb)


def _causal_mask():
    r = lax.broadcasted_iota(jnp.int32, (BQ, BQ), 0)
    c = lax.broadcasted_iota(jnp.int32, (BQ, BQ), 1)
    return r >= c


def _attn_fwd(q, k, v, cq, ck, name, comm=None):
    s = q.shape[1]
    nb = s // BQ

    def body(q_ref, k_ref, v_ref, cq_ref, ck_ref, o_ref, lse_ref):
        for qi in range(nb):
            qs = qi * BQ
            qb = q_ref[0, qs:qs + BQ, :]
            cqb = cq_ref[0, qs:qs + BQ, :]

            def block(kj, carry, masked):
                m, l, acc = carry
                ks = pl.multiple_of(kj * BQ, BQ)
                kb = k_ref[0, pl.ds(ks, BQ), :]
                vb = v_ref[0, pl.ds(ks, BQ), :]
                sc = _dot_nt(qb, kb) + (cqb - ck_ref[0, kj])
                if masked:
                    sc = jnp.where(_causal_mask(), sc, NEG)
                m_new = jnp.maximum(m, jnp.max(sc, axis=-1, keepdims=True))
                alpha = jnp.exp(m - m_new)
                p = jnp.exp(sc - m_new)
                l = alpha * l + jnp.sum(p, axis=-1, keepdims=True)
                acc = alpha * acc + _dot(p.astype(BF16), vb)
                return m_new, l, acc

            carry = (jnp.full((BQ, 1), NEG, F32), jnp.zeros((BQ, 1), F32), jnp.zeros((BQ, HEAD_DIM), F32))
            if qi > 0:
                carry = lax.fori_loop(0, qi, lambda kj, cr: block(kj, cr, False), carry)
            m, l, acc = block(qi, carry, True)
            o_ref[0, qs:qs + BQ, :] = acc / l
            lse_ref[0, qs:qs + BQ, :] = m + jnp.log(l)

    hblk = pl.BlockSpec((1, s, HEAD_DIM), lambda h: (h, 0, 0))
    cblk = pl.BlockSpec((1, s, 1), lambda h: (h, 0, 0))
    return _pcall(
        body, grid=(N_HEADS,),
        in_specs=[hblk, hblk, hblk, cblk, pl.BlockSpec((1, nb, 1, BQ), lambda h: (h, 0, 0, 0))],
        out_specs=[hblk, cblk],
        out_shape=[jax.ShapeDtypeStruct((N_HEADS, s, HEAD_DIM), F32), jax.ShapeDtypeStruct((N_HEADS, s, 1), F32)],
        operands=(q, k, v, cq, ck), name=name, comm=comm)


def _attn_bwd(q, k, v, cq, ck, o, lse, do, name, comm=None):
    s = q.shape[1]
    nb = s // BQ

    def body(q_ref, k_ref, v_ref, cq_ref, ck_ref, o_ref, lse_ref, do_ref, dq_ref, dk_ref, dv_ref, dcq_ref, dck_ref, delta):
        delta[...] = jnp.sum(do_ref[0] * o_ref[0], axis=-1, keepdims=True)
        dq_ref[...] = jnp.zeros_like(dq_ref)
        dcq_ref[...] = jnp.zeros_like(dcq_ref)
        for kj in range(nb):
            ks = kj * BQ
            kb = k_ref[0, ks:ks + BQ, :]
            vb = v_ref[0, ks:ks + BQ, :]
            ckb = ck_ref[0, kj]

            def block(qi, carry, masked):
                dk_acc, dv_acc, dck_acc = carry
                qs = pl.multiple_of(qi * BQ, BQ)
                qb = q_ref[0, pl.ds(qs, BQ), :]
                dob = do_ref[0, pl.ds(qs, BQ), :].astype(BF16)
                sc = _dot_nt(qb, kb) + (cq_ref[0, pl.ds(qs, BQ), :] - ckb)
                p = jnp.exp(sc - lse_ref[0, pl.ds(qs, BQ), :])
                if masked:
                    p = jnp.where(_causal_mask(), p, 0.0)
                dp = _dot_nt(dob, vb)
                ds = p * (dp - delta[pl.ds(qs, BQ), :])
                ds_b = ds.astype(BF16)
                dv_acc = dv_acc + _dot_tn(p.astype(BF16), dob)
                dk_acc = dk_acc + _dot_tn(ds_b, qb)
                dq_ref[0, pl.ds(qs, BQ), :] += _dot(ds_b, kb) * SCALE
                dcq_ref[0, pl.ds(qs, BQ), :] += jnp.sum(ds, axis=-1, keepdims=True)
                dck_acc = dck_acc - jnp.sum(ds, axis=0, keepdims=True)
                return dk_acc, dv_acc, dck_acc

            carry = (jnp.zeros((BQ, HEAD_DIM), F32), jnp.zeros((BQ, HEAD_DIM), F32), jnp.zeros((1, BQ), F32))
            carry = block(kj, carry, True)
            if kj < nb - 1:
                carry = lax.fori_loop(kj + 1, nb, lambda qi, cr: block(qi, cr, False), carry)
            dk_ref[0, ks:ks + BQ, :] = carry[0]
            dv_ref[0, ks:ks + BQ, :] = carry[1]
            dck_ref[0, kj] = carry[2]

    hblk = pl.BlockSpec((1, s, HEAD_DIM), lambda h: (h, 0, 0))
    cblk = pl.BlockSpec((1, s, 1), lambda h: (h, 0, 0))
    kblk = pl.BlockSpec((1, nb, 1, BQ), lambda h: (h, 0, 0, 0))
    hshape = jax.ShapeDtypeStruct((N_HEADS, s, HEAD_DIM), F32)
    return _pcall(
        body, grid=(N_HEADS,),
        in_specs=[hblk, hblk, hblk, cblk, kblk, hblk, cblk, hblk],
        out_specs=[hblk, hblk, hblk, cblk, kblk],
        out_shape=[hshape, hshape, hshape, jax.ShapeDtypeStruct((N_HEADS, s, 1), F32),
                   jax.ShapeDtypeStruct((N_HEADS, nb, 1, BQ), F32)],
        scratch_shapes=[pltpu.VMEM((s, 1), F32)],
        operands=(q, k, v, cq, ck, o, lse, do), name=name, comm=comm)


def _merge_fwd(x, ya, yc, yd, o, pm, pg, wb, wo, name, comm=None):
    s = x.shape[0]

    def body(x_ref, ya_ref, yc_ref, yd_ref, o_ref, bg_ref, pg_ref, wb_ref, wo_ref, xo_ref, yb_ref):
        o = jnp.concatenate([o_ref[h] for h in range(N_HEADS)], axis=1)
        yb = (o * _silu(bg_ref[...])).astype(BF16)
        yb_ref[...] = yb
        ys = (ya_ref[...], yb, yc_ref[...], yd_ref[...])
        merged = jnp.zeros((TMG, D_MODEL), F32)
        for n in range(N_BRANCH):
            merged = merged + _sig(pg_ref[:, n * D_MODEL:(n + 1) * D_MODEL]) * _dot(ys[n], wb_ref[n])
        xo_ref[...] = x_ref[...] + _dot(merged.astype(BF16), wo_ref[...])

    xt = pl.BlockSpec((TMG, D_MODEL), lambda i: (i, 0))
    yt = pl.BlockSpec((TMG, BR), lambda i: (i, 0))
    return _pcall(
        body, grid=(s // TMG,),
        in_specs=[xt, yt, yt, yt, pl.BlockSpec((N_HEADS, TMG, HEAD_DIM), lambda i: (0, i, 0)),
                  pl.BlockSpec((TMG, BR), lambda i: (i, O_BG // BR)), pl.BlockSpec((TMG, N_MERGE), lambda i: (i, 0)),
                  pl.BlockSpec((N_BRANCH, BR, D_MODEL), lambda i: (0, 0, 0)), pl.BlockSpec((D_MODEL, D_MODEL), lambda i: (0, 0))],
        out_specs=[xt, yt],
        out_shape=[jax.ShapeDtypeStruct((s, D_MODEL), F32), jax.ShapeDtypeStruct((s, BR), BF16)],
        operands=(x, ya, yc, yd, o, pm, pg, wb, wo), name=name, comm=comm)


def _merge_bwd(dx, ya, yb, yc, yd, o, pm, pg, wb, wo, name, comm=None):
    s = dx.shape[0]
    nt = s // TMG

    def body(dx_ref, ya_ref, yb_ref, yc_ref, yd_ref, o_ref, bg_ref, pg_ref, wb_ref, wo_ref,
             dpg_ref, dya_ref, do_ref, dbg_ref, dyc_ref, dyd_ref, dwb_ref, dwo_ref, dwb_acc, dwo_acc):
        i = pl.program_id(0)

        @pl.when(i == 0)
        def _():
            dwb_acc[...] = jnp.zeros_like(dwb_acc)
            dwo_acc[...] = jnp.zeros_like(dwo_acc)

        dxb = dx_ref[...].astype(BF16)
        dmerged = _dot_nt(dxb, wo_ref[...])
        ys = (ya_ref[...], yb_ref[...], yc_ref[...], yd_ref[...])
        dys = (dya_ref, None, dyc_ref, dyd_ref)
        merged = jnp.zeros((TMG, D_MODEL), F32)
        for n in range(N_BRANCH):
            gate = _sig(pg_ref[:, n * D_MODEL:(n + 1) * D_MODEL])
            pr = _dot(ys[n], wb_ref[n])
            merged = merged + gate * pr
            dpg_ref[:, n * D_MODEL:(n + 1) * D_MODEL] = (dmerged * pr * gate * (1.0 - gate)).astype(BF16)
            dpr = (gate * dmerged).astype(BF16)
            dwb_acc[n] += _dot_tn(ys[n], dpr)
            dyn = _dot_nt(dpr, wb_ref[n])
            if n == 1:
                bg = bg_ref[...]
                do = dyn * _silu(bg)
                for h in range(N_HEADS):
                    do_ref[h] = do[:, h * HEAD_DIM:(h + 1) * HEAD_DIM]
                dbg_ref[...] = dyn * jnp.concatenate([o_ref[h] for h in range(N_HEADS)], axis=1) * _dsilu(bg)
            else:
                dys[n][...] = dyn
        dwo_acc[...] += _dot_tn(merged.astype(BF16), dxb)

        @pl.when(i == nt - 1)
        def _():
            dwb_ref[...] = dwb_acc[...].astype(BF16)
            dwo_ref[...] = dwo_acc[...].astype(BF16)

    xt = pl.BlockSpec((TMG, D_MODEL), lambda i: (i, 0))
    yt = pl.BlockSpec((TMG, BR), lambda i: (i, 0))
    gt = pl.BlockSpec((TMG, N_MERGE), lambda i: (i, 0))
    wbs = pl.BlockSpec((N_BRANCH, BR, D_MODEL), lambda i: (0, 0, 0))
    wos = pl.BlockSpec((D_MODEL, D_MODEL), lambda i: (0, 0))
    yf = jax.ShapeDtypeStruct((s, BR), F32)
    ht = pl.BlockSpec((N_HEADS, TMG, HEAD_DIM), lambda i: (0, i, 0))
    return _pcall(
        body, grid=(nt,),
        in_specs=[xt, yt, yt, yt, yt, ht, pl.BlockSpec((TMG, BR), lambda i: (i, O_BG // BR)), gt, wbs, wos],
        out_specs=[gt, yt, ht, yt, yt, yt, wbs, wos],
        out_shape=[jax.ShapeDtypeStruct((s, N_MERGE), BF16), yf, jax.ShapeDtypeStruct((N_HEADS, s, HEAD_DIM), F32), yf, yf, yf,
                   jax.ShapeDtypeStruct((N_BRANCH, BR, D_MODEL), BF16), jax.ShapeDtypeStruct((D_MODEL, D_MODEL), BF16)],
        scratch_shapes=[pltpu.VMEM((N_BRANCH, BR, D_MODEL), F32), pltpu.VMEM((D_MODEL, D_MODEL), F32)],
        operands=(dx, ya, yb, yc, yd, o, pm, pg, wb, wo), name=name, comm=comm)


def _layer_fwd(x, wl, tag, attach=None):
    attach = attach or {}

    def riding(stage):
        comm, sink = attach.get(stage, (None, None))
        return comm, (sink or (lambda res: None))

    s = x.shape[0]
    comm, sink = riding("rms_fwd")
    h, res = _rms_fwd(x, wl["norm_g"], "rms_fwd" + tag, comm)
    sink(res)
    comm, sink = riding("proj_mix")
    pm, res = _mm_nn(h, wl["w_mix"], 1792, "proj_mix" + tag, comm)
    sink(res)
    comm, sink = riding("mix_fwd")
    (ya, yc, yd, q, k, v, cc), res = _mix_fwd(pm, wl, "mix_fwd" + tag, comm)
    sink(res)
    comm, sink = riding("proj_merge")
    pg, res = _mm_nn(h, wl["w_merge"], 2048, "proj_merge" + tag, comm)
    sink(res)
    pf, _ = _mm_nn(h, wl["w_f"], N_F, "proj_f" + tag)
    cum = _cum_fwd(pf, wl["f_bias"], "cum_fwd" + tag)
    cum_t = cum[:, :N_HEADS].T
    cq = cum_t.reshape(N_HEADS, s, 1)
    ck = cum_t.reshape(N_HEADS, s // BQ, 1, BQ)
    comm, sink = riding("attn_fwd")
    (o, lse), res = _attn_fwd(q, k, v, cq, ck, "attn_fwd" + tag, comm)
    sink(res)
    comm, sink = riding("merge_fwd")
    (x_next, yb), res = _merge_fwd(x, ya, yc, yd, o, pm, pg, wl["wb"], wl["wo"], "merge_fwd" + tag, comm)
    sink(res)
    saved = dict(x=x, h=h, pm=pm, pg=pg, pf=pf, cc=cc, ya=ya, yb=yb, yc=yc, yd=yd, q=q, k=k, v=v, cq=cq, ck=ck, o=o, lse=lse)
    return x_next, saved


def _blocks_rows(g):
    return g.reshape(N_DEV, g.shape[0] // N_DEV, g.shape[1])


def _blocks_cols(g):
    return g.reshape(N_BRANCH * BR, N_DEV, D_MODEL // N_DEV).transpose(1, 0, 2)


def _layer_bwd(dx_next, sv, wl, tag, dist, riding, extra_small):
    s = dx_next.shape[0]
    (dpg, dya, do, dbg, dyc, dyd, dwb, dwo), rode = _merge_bwd(
        dx_next, sv["ya"], sv["yb"], sv["yc"], sv["yd"], sv["o"], sv["pm"], sv["pg"], wl["wb"], wl["wo"], "merge_bwd" + tag,
        riding)
    dw_merge = _mm_tn(sv["h"], dpg, 2048, "dw_merge" + tag)
    early = [(_blocks_rows(dw_merge), False), (_blocks_cols(dwb), False), (_blocks_rows(dwo), False)] if dist else None
    (dq, dk, dv, dcq, dck), early_out = _attn_bwd(sv["q"], sv["k"], sv["v"], sv["cq"], sv["ck"], sv["o"], sv["lse"], do,
                                                  "attn_bwd" + tag, early)
    dck_cols = jnp.pad(dck.reshape(N_HEADS, s).T, ((0, 0), (0, N_F - N_HEADS)))
    dpf, dfb = _cum_bwd(dcq, dck_cols, sv["pf"], wl["f_bias"], "cum_bwd" + tag)
    dpm, dgw, dgb, vec, dscw, ddww = _mix_bwd(sv["pm"], sv["cc"], dya, dyc, dyd, (dq, dk, dv), dbg, wl, "mix_bwd" + tag)
    dw_mix = _mm_tn(sv["h"], dpm, 1792, "dw_mix" + tag)
    dw_f = _mm_tn(sv["h"], dpf, N_F, "dw_f" + tag)
    causal = jnp.tril(jnp.ones((CHUNK, CHUNK), bool))
    small = dict(
        f_bias=dfb.sum(0)[:N_HEADS],
        sgu_w=jnp.where(causal[None], dgw, 0.0),
        sgu_b=dgb.reshape(CHUNK, N_HEADS, HEAD_DIM).sum(-1).T,
        sgu_ln_g=vec[0], sgu_ln_b=vec[1], conf_dw_b=vec[2], conf_ln_g=vec[3], conf_ln_b=vec[4],
        short_conv_w=dscw.reshape(8, 8, BR).sum(1)[:SHORT_CONV],
        conf_dw_w=ddww.reshape(32, 8, BR).sum(1)[:CONF_CONV],
    )
    slab, spans = _pack([small[nm] for nm in SMALL[1:]])
    late = [(_blocks_rows(dw_mix), False), (_blocks_rows(dw_f), False), (slab, True)] if dist else None
    dh, late_out = _dh(dpm, dpg, dpf, wl["w_mix"], wl["w_merge"], wl["w_f"], "dh" + tag, late)
    dx, dng = _rms_bwd(dh, sv["x"], wl["norm_g"], dx_next, "rms_bwd" + tag)
    small["norm_g"] = dng.sum(0)
    grads = dict(small, w_mix=dw_mix, w_merge=dw_merge, w_f=dw_f, wb=dwb, wo=dwo)
    last_slab, last_spans = _pack([small["norm_g"]] + list(extra_small))
    return dx, grads, early_out, (late_out, spans), ([(last_slab, True)], last_spans), rode


def _prep_layer_small(norm_g, f_bias, sgu_w, sgu_b, sgu_ln_g, sgu_ln_b, conf_dw_b, conf_ln_g, conf_ln_b):
    causal = jnp.tril(jnp.ones((CHUNK, CHUNK), bool))
    gw = jnp.where(causal[None], sgu_w, 0.0)
    row = lambda a: a.reshape(1, -1)
    return dict(
        norm_g=row(norm_g),
        f_bias=jnp.pad(row(f_bias), ((0, 0), (0, N_F - N_HEADS))),
        gw=gw.astype(BF16), gwt=gw.transpose(0, 2, 1).astype(BF16),
        gb=jnp.repeat(sgu_b.T, HEAD_DIM, axis=1),
        sgu_ln_g=row(sgu_ln_g), sgu_ln_b=row(sgu_ln_b),
        conf_dw_b=row(conf_dw_b), conf_ln_g=row(conf_ln_g), conf_ln_b=row(conf_ln_b))


def _local_step(x, target, layers, final_g):
    saved = []
    for l in range(DEPTH):
        x, sv = _layer_fwd(x, layers[l], str(l))
        saved.append(sv)
    loss_p, dx, dfg = _loss_head(x, final_g.reshape(1, D_MODEL), target)
    grads = [None] * DEPTH
    for l in reversed(range(DEPTH)):
        dx, grads[l], _, _, _, _ = _layer_bwd(dx, saved[l], layers[l], str(l), False, None, [])
    return 0.5 / D_MODEL * jnp.sum(loss_p), dx, grads, dfg.sum(0)


def _sum8(a, name):
    _, r, c = a.shape
    tr = r
    while tr * c * a.dtype.itemsize * N_DEV > 4 * 1024 * 1024 and tr % 32 == 0:
        tr //= 2

    def body(a_ref, o_ref):
        acc = a_ref[0].astype(F32)
        for d in range(1, N_DEV):
            acc = acc + a_ref[d].astype(F32)
        o_ref[...] = acc

    return pl.pallas_call(
        body, grid=(r // tr,),
        in_specs=[pl.BlockSpec((N_DEV, tr, c), lambda i: (0, i, 0))],
        out_specs=pl.BlockSpec((tr, c), lambda i: (i, 0)),
        out_shape=jax.ShapeDtypeStruct((r, c), F32), compiler_params=_cp(("parallel",)), name=name)(a)


def _adamw(w, g, m, v, name):
    l, r, c = w.shape
    tr = r
    while tr * c * 4 > 1024 * 1024 and tr % 16 == 0:
        tr //= 2
    c1 = 1.0 - ADAM_B1 ** ADAM_STEP
    c2 = 1.0 - ADAM_B2 ** ADAM_STEP

    def body(w_ref, g_ref, m_ref, v_ref, d_ref, mo_ref, vo_ref):
        gv = g_ref[...]
        mn = ADAM_B1 * m_ref[...] + (1.0 - ADAM_B1) * gv
        vn = ADAM_B2 * v_ref[...] + (1.0 - ADAM_B2) * (gv * gv)
        mo_ref[...] = mn
        vo_ref[...] = vn
        d_ref[...] = -ADAM_LR * ((mn / c1) / (jnp.sqrt(vn / c2) + ADAM_EPS) + ADAM_WD * w_ref[...])

    blk = pl.BlockSpec((1, tr, c), lambda a, i: (a, i, 0))
    shp = jax.ShapeDtypeStruct((l, r, c), F32)
    return pl.pallas_call(
        body, grid=(l, r // tr), in_specs=[blk] * 4, out_specs=[blk] * 3, out_shape=[shp] * 3,
        compiler_params=_cp(("parallel", "parallel")), name=name)(w, g, m, v)


def _pack(parts):
    rows, spans, r = [], [], 0
    for p in parts:
        flat = p.reshape(-1)
        nr = -(-flat.shape[0] // 1024) * 8
        rows.append(jnp.pad(flat, (0, nr * 128 - flat.shape[0])).reshape(nr, 128))
        spans.append((r, nr, p.shape))
        r += nr
    return jnp.concatenate(rows, axis=0), spans


def _unpack(slab, spans):
    out = []
    for r, nr, shape in spans:
        size = math.prod(shape)
        out.append(slab[r:r + nr].reshape(-1)[:size].reshape(shape))
    return out


def _split_w_in(w):
    mix = jnp.concatenate([w[..., 0:1536], w[..., 1540:1796], w[..., 3332:3588], w[..., 1796:2820], w[..., 2820:3332]], axis=-1)
    return mix, w[..., 3588:7684], w[..., 1536:1540]


def _join_w_in(mix, merge, f):
    return jnp.concatenate([mix[..., 0:1536], f, mix[..., 1536:1792], mix[..., 2048:3072], mix[..., 3072:3584],
                            mix[..., 1792:2048], merge], axis=-1)


SMALL = ("norm_g", "f_bias", "sgu_w", "sgu_b", "sgu_ln_g", "sgu_ln_b", "short_conv_w", "conf_dw_w", "conf_dw_b",
         "conf_ln_g", "conf_ln_b")


def kernel(x, norm_g, w_in, f_bias, sgu_w, sgu_b, sgu_ln_g, sgu_ln_b, short_conv_w, conf_dw_w, conf_dw_b, conf_ln_g, conf_ln_b, w_branch, w_out, final_g, loss_target, m_norm_g, m_w_in, m_f_bias, m_sgu_w, m_sgu_b, m_sgu_ln_g, m_sgu_ln_b, m_short_conv_w, m_conf_dw_w, m_conf_dw_b, m_conf_ln_g, m_conf_ln_b, m_w_branch, m_w_out, m_final_g, v_norm_g, v_w_in, v_f_bias, v_sgu_w, v_sgu_b, v_sgu_ln_g, v_sgu_ln_b, v_short_conv_w, v_conf_dw_w, v_conf_dw_b, v_conf_ln_g, v_conf_ln_b, v_w_branch, v_w_out, v_final_g):
    me = 4 * lax.axis_index("x") + 2 * lax.axis_index("y") + lax.axis_index("c")
    rows = D_MODEL // N_DEV
    cshard = BR // N_DEV

    sh = []
    for l in range(DEPTH):
        mix, merge, f = _split_w_in(w_in[l])
        sh.append(dict(mix=mix.astype(BF16), merge=merge.astype(BF16),
                       f=jnp.pad(f, ((0, 0), (0, N_F - N_HEADS))).astype(BF16),
                       wb=w_branch[l].astype(BF16), wo=w_out[l].astype(BF16)))
    conv_slab, conv_spans = _pack([short_conv_w, conf_dw_w])
    layers = [_prep_layer_small(norm_g[l], f_bias[l], sgu_w[l], sgu_b[l], sgu_ln_g[l], sgu_ln_b[l],
                                conf_dw_b[l], conf_ln_g[l], conf_ln_b[l]) for l in range(DEPTH)]
    half = N_MERGE // 2
    merge_halves = []

    def put_first(res):
        conv_full = [_unpack(res[2][d], conv_spans) for d in range(N_DEV)]
        scw_full = jnp.concatenate([cf[0] for cf in conv_full], axis=-1)
        dww_full = jnp.concatenate([cf[1] for cf in conv_full], axis=-1)
        for l in range(DEPTH):
            layers[l].update(scw=jnp.pad(scw_full[l], ((0, 8 - SHORT_CONV), (0, 0))),
                             dww=jnp.pad(dww_full[l], ((0, 32 - CONF_CONV), (0, 0))))
        layers[0].update(w_mix=res[0].reshape(D_MODEL, N_MIX), w_f=res[1].reshape(D_MODEL, N_F))

    def put_in(l):
        def sink(res):
            layers[l].update(w_mix=res[0].reshape(D_MODEL, N_MIX), w_f=res[1].reshape(D_MODEL, N_F),
                             w_merge=res[2].reshape(D_MODEL, N_MERGE))
        return sink

    def put_merge_half(res):
        merge_halves.append(res[0])
        if len(merge_halves) == 2:
            layers[0].update(w_merge=jnp.concatenate(merge_halves, axis=-1).reshape(D_MODEL, N_MERGE))

    def put_out(l):
        def sink(res):
            layers[l].update(wb=res[0].transpose(1, 2, 0, 3).reshape(N_BRANCH, BR, D_MODEL), wo=res[1].reshape(D_MODEL, D_MODEL))
        return sink

    attach0 = {
        "rms_fwd": ([(sh[0]["mix"], True), (sh[0]["f"], True), (conv_slab, True)], put_first),
        "proj_mix": ([(sh[0]["merge"][:, :half], True)], put_merge_half),
        "mix_fwd": ([(sh[0]["merge"][:, half:], True)], put_merge_half),
        "proj_merge": ([(sh[0]["wb"], True), (sh[0]["wo"], True)], put_out(0)),
        "attn_fwd": ([(sh[1]["mix"], True), (sh[1]["f"], True), (sh[1]["merge"], True)], put_in(1)),
    }
    attach1 = {"proj_mix": ([(sh[1]["wb"], True), (sh[1]["wo"], True)], put_out(1))}

    xs = x[0]
    xs, sv0 = _layer_fwd(xs, layers[0], "0", attach0)
    xs, sv1 = _layer_fwd(xs, layers[1], "1", attach1)
    loss_p, dx, dfg = _loss_head(xs, final_g.reshape(1, D_MODEL), loss_target[0])
    loss_local = (0.5 / D_MODEL * jnp.sum(loss_p)).reshape(1)
    dx, g1, early1, (late1, spans1), (last1, lspans1), _ = _layer_bwd(dx, sv1, layers[1], "1", True, None, [dfg.sum(0)])
    dx, g0, early0, (late0, spans0), (last0, lspans0), last1_out = _layer_bwd(dx, sv0, layers[0], "0", True, last1, [loss_local])
    last0_out = _exchange(last0, "gather_last")

    red, small = [], []
    for l, (early, late, spans, last, lspans) in enumerate(((early0, late0, spans0, last0_out, lspans0),
                                                            (early1, late1, spans1, last1_out, lspans1))):
        t = str(l)
        red.append(dict(merge=_sum8(early[0], "sum_merge" + t), wb=_sum8(early[1], "sum_wb" + t), wo=_sum8(early[2], "sum_wo" + t),
                        mix=_sum8(late[0], "sum_mix" + t), f=_sum8(late[1], "sum_f" + t)))
        keys = ["norm_g", "final_g" if l == DEPTH - 1 else "loss"] + list(SMALL[1:])
        small.append(dict(zip(keys, _unpack(_sum8(last[0], "sum_last" + t), lspans)
                              + _unpack(_sum8(late[2], "sum_small" + t), spans))))
    gs = {nm: jnp.stack([small[l][nm] for l in range(DEPTH)]) for nm in SMALL}
    gs["final_g"] = small[DEPTH - 1]["final_g"]
    loss = small[0]["loss"][0]
    gs["short_conv_w"] = lax.dynamic_slice_in_dim(gs["short_conv_w"], me * cshard, cshard, axis=2)
    gs["conf_dw_w"] = lax.dynamic_slice_in_dim(gs["conf_dw_w"], me * cshard, cshard, axis=2)
    g_w_in = jnp.stack([_join_w_in(red[l]["mix"], red[l]["merge"], red[l]["f"][:, :N_HEADS]) for l in range(DEPTH)])
    g_w_branch = jnp.stack([red[l]["wb"].reshape(N_BRANCH, BR, rows) for l in range(DEPTH)])
    g_w_out = jnp.stack([red[l]["wo"] for l in range(DEPTH)])

    d_w_in, nm_w_in, nv_w_in = _adamw(w_in, g_w_in, m_w_in, v_w_in, "adamw_w_in")
    flat = lambda a: a.reshape(DEPTH, N_BRANCH * BR, rows)
    d_w_branch, nm_w_branch, nv_w_branch = (a.reshape(w_branch.shape) for a in _adamw(
        flat(w_branch), flat(g_w_branch), flat(m_w_branch), flat(v_w_branch), "adamw_w_branch"))
    d_w_out, nm_w_out, nv_w_out = _adamw(w_out, g_w_out, m_w_out, v_w_out, "adamw_w_out")
    names = SMALL + ("final_g",)
    ws = dict(zip(names, (norm_g, f_bias, sgu_w, sgu_b, sgu_ln_g, sgu_ln_b, short_conv_w, conf_dw_w, conf_dw_b, conf_ln_g,
                          conf_ln_b, final_g)))
    ms = dict(zip(names, (m_norm_g, m_f_bias, m_sgu_w, m_sgu_b, m_sgu_ln_g, m_sgu_ln_b, m_short_conv_w, m_conf_dw_w,
                          m_conf_dw_b, m_conf_ln_g, m_conf_ln_b, m_final_g)))
    vs = dict(zip(names, (v_norm_g, v_f_bias, v_sgu_w, v_sgu_b, v_sgu_ln_g, v_sgu_ln_b, v_short_conv_w, v_conf_dw_w,
                          v_conf_dw_b, v_conf_ln_g, v_conf_ln_b, v_final_g)))
    w_slab, spans = _pack([ws[nm] for nm in names])
    g_slab, _ = _pack([gs[nm] for nm in names])
    m_slab, _ = _pack([ms[nm] for nm in names])
    v_slab, _ = _pack([vs[nm] for nm in names])
    d_s, nm_s, nv_s = (dict(zip(names, _unpack(a[0], spans))) for a in _adamw(w_slab[None], g_slab[None], m_slab[None],
                                                                              v_slab[None], "adamw_small"))

    def ordered(small, w_in_v, w_branch_v, w_out_v):
        return [small["norm_g"], w_in_v, small["f_bias"], small["sgu_w"], small["sgu_b"], small["sgu_ln_g"],
                small["sgu_ln_b"], small["short_conv_w"], small["conf_dw_w"], small["conf_dw_b"], small["conf_ln_g"],
                small["conf_ln_b"], w_branch_v, w_out_v, small["final_g"]]

    return (loss, dx[None], *ordered(gs, g_w_in, g_w_branch, g_w_out), *ordered(d_s, d_w_in, d_w_branch, d_w_out),
            *ordered(nm_s, nm_w_in, nm_w_branch, nm_w_out), *ordered(nv_s, nv_w_in, nv_w_branch, nv_w_out))
```

```python
import functools
import math

import jax
import jax.numpy as jnp
from jax import lax
from jax.experimental import pallas as pl
from jax.experimental.pallas import tpu as pltpu

F32 = jnp.float32
BF16 = jnp.bfloat16

D_MODEL = 1024
DEPTH = 2
N_BRANCH = 4
BR = 256
N_HEADS = 4
HEAD_DIM = 64
CHUNK = 128
SHORT_CONV = 3
CONF_CONV = 31
EPS = 1e-6
N_DEV = 8

ADAM_LR = 0.001
ADAM_B1 = 0.9
ADAM_B2 = 0.999
ADAM_EPS = 1e-08
ADAM_WD = 0.01
ADAM_STEP = 10

O_UV, O_AG, O_QKV, O_BG, O_DG, O_CIN, O_CG, O_GLU = 0, 512, 768, 1536, 1792, 2048, 2816, 3072
N_MIX = 3584
N_MERGE = N_BRANCH * D_MODEL
N_F = 128
IN_COLS = 7684
HALO = 32
TM = 512
TMG = 256
FQ = 1024
BQ = 512
BK = 512
SUB = 64
CUMB = 512
VMEM_LIMIT = 56 * 1024 * 1024
NEG = -1e30
SCALE = 1.0 / math.sqrt(HEAD_DIM)
GELU_K = math.sqrt(2.0 / math.pi)


def _cp(sem=None):
    return pltpu.CompilerParams(dimension_semantics=sem, vmem_limit_bytes=VMEM_LIMIT)


PEER_ORDER = (6, 4, 2, 7, 5, 3, 1)
RELAYED = (3, 5, 7)


def _xchg(cin, cout, send, recv, loc, modes, phase):
    x, y, c = lax.axis_index("x"), lax.axis_index("y"), lax.axis_index("c")
    me = 4 * x + 2 * y + c

    def peer_of(kk):
        px, py, pc = lax.rem(x + (kk >> 2 & 1), 2), lax.rem(y + (kk >> 1 & 1), 2), lax.rem(c + (kk & 1), 2)
        return (px, py, pc), 4 * px + 2 * py + pc

    def remote(src, dst, a, kk, pid):
        return pltpu.make_async_remote_copy(src_ref=src, dst_ref=dst, send_sem=send.at[a, kk], recv_sem=recv.at[a, kk],
                                            device_id=pid, device_id_type=pl.DeviceIdType.MESH)

    def outgoing(a, kk):
        if modes[a] and kk in RELAYED:
            _, origin = peer_of(kk - 1)
            return remote(cout[a].at[origin], cout[a].at[origin], a, kk, peer_of(1)[0])
        pid, peer = peer_of(kk)
        return remote(cin[a] if modes[a] else cin[a].at[peer], cout[a].at[me], a, kk, pid)

    def arrival(a, kk):
        _, peer = peer_of(kk)
        return remote(cout[a].at[peer], cout[a].at[peer], a, kk, (x, y, c))

    if phase == "relay":
        for kk in RELAYED:
            for a, gather in enumerate(modes):
                if gather:
                    arrival(a, kk - 1).wait_recv()
                    outgoing(a, kk).start()
        return
    for a, gather in enumerate(modes):
        cp = pltpu.make_async_copy(cin[a] if gather else cin[a].at[me], cout[a].at[me], loc.at[a])
        if phase == "start":
            cp.start()
        else:
            cp.wait()
    if phase == "start":
        for kk in PEER_ORDER:
            for a, gather in enumerate(modes):
                if not (gather and kk in RELAYED):
                    outgoing(a, kk).start()
        return
    for kk in PEER_ORDER:
        for a in range(len(modes)):
            outgoing(a, kk).wait_send()
    for kk in PEER_ORDER:
        for a, gather in enumerate(modes):
            if not (gather and kk + 1 in RELAYED):
                arrival(a, kk).wait_recv()


def _xchg_shapes(comm):
    return [jax.ShapeDtypeStruct((N_DEV,) + tuple(a.shape[(0 if gather else 1):]), a.dtype) for a, gather in comm]


def _xchg_sems(n):
    return [pltpu.SemaphoreType.DMA((n, N_DEV)), pltpu.SemaphoreType.DMA((n, N_DEV)), pltpu.SemaphoreType.DMA((n,))]


def _exchange(comm, name):
    n = len(comm)
    modes = [g for _, g in comm]

    def body(*refs):
        cin, cout, (send, recv, loc) = refs[:n], refs[n:2 * n], refs[2 * n:]
        for phase in ("start", "relay", "finish"):
            _xchg(cin, cout, send, recv, loc, modes, phase)

    anyspec = pl.BlockSpec(memory_space=pl.ANY)
    return pl.pallas_call(
        body, in_specs=[anyspec] * n, out_specs=[anyspec] * n, out_shape=_xchg_shapes(comm),
        scratch_shapes=_xchg_sems(n), name=name)(*[a for a, _ in comm])


def _pcall(body, *, grid, in_specs, out_specs, out_shape, operands, name, scratch_shapes=(), comm=None):
    if not comm:
        outs = pl.pallas_call(
            body, grid=grid, in_specs=in_specs, out_specs=out_specs, out_shape=out_shape, scratch_shapes=list(scratch_shapes),
            compiler_params=_cp(("arbitrary",) * len(grid)), name=name)(*operands)
        return list(outs), []
    n, nin, nout, nsc = len(comm), len(operands), len(out_shape), len(scratch_shapes)
    modes = [g for _, g in comm]

    def wrapped(*refs):
        ins, cin = refs[:nin], refs[nin:nin + n]
        outs, cout = refs[nin + n:nin + n + nout], refs[nin + n + nout:nin + 2 * n + nout]
        scratch = refs[nin + 2 * n + nout:]
        own, (send, recv, loc) = scratch[:nsc], scratch[nsc:]
        ids = [pl.program_id(d) for d in range(len(grid))]
        first = functools.reduce(jnp.logical_and, [i == 0 for i in ids])
        last = functools.reduce(jnp.logical_and, [i == g - 1 for i, g in zip(ids, grid)])

        @pl.when(first)
        def _():
            _xchg(cin, cout, send, recv, loc, modes, "start")

        @pl.when(last)
        def _():
            _xchg(cin, cout, send, recv, loc, modes, "relay")

        body(*ins, *outs, *own)

        @pl.when(last)
        def _():
            _xchg(cin, cout, send, recv, loc, modes, "finish")

    anyspec = pl.BlockSpec(memory_space=pl.ANY)
    res = pl.pallas_call(
        wrapped, grid=grid, in_specs=list(in_specs) + [anyspec] * n, out_specs=list(out_specs) + [anyspec] * n,
        out_shape=list(out_shape) + _xchg_shapes(comm), scratch_shapes=list(scratch_shapes) + _xchg_sems(n),
        compiler_params=_cp(("arbitrary",) * len(grid)), name=name)(*operands, *[a for a, _ in comm])
    return list(res[:nout]), list(res[nout:])


def _sig(x):
    return 0.5 * jnp.tanh(0.5 * x) + 0.5


def _silu(x):
    return x * _sig(x)


def _dsilu(x):
    s = _sig(x)
    return s * (1.0 + x * (1.0 - s))


def _gelu(x):
    return 0.5 * x * (1.0 + jnp.tanh(GELU_K * (x + 0.044715 * x * x * x)))


def _dgelu(x):
    t = jnp.tanh(GELU_K * (x + 0.044715 * x * x * x))
    return 0.5 * (1.0 + t) + 0.5 * x * (1.0 - t * t) * GELU_K * (1.0 + 3.0 * 0.044715 * x * x)


def _ln_hat(x):
    mu = jnp.mean(x, axis=-1, keepdims=True)
    xc = x - mu
    rs = lax.rsqrt(jnp.mean(xc * xc, axis=-1, keepdims=True) + EPS)
    return xc * rs, rs


def _ln_bwd(dhat, hat, rs):
    return rs * (dhat - jnp.mean(dhat, axis=-1, keepdims=True) - hat * jnp.mean(dhat * hat, axis=-1, keepdims=True))


def _dot(a, b):
    return jnp.dot(a, b, preferred_element_type=F32)


def _dot_nt(a, b):
    return lax.dot_general(a, b, (((1,), (1,)), ((), ())), preferred_element_type=F32)


def _dot_tn(a, b):
    return lax.dot_general(a, b, (((0,), (0,)), ((), ())), preferred_element_type=F32)


def _fold8(x):
    acc = x[0:8]
    for r in range(1, x.shape[0] // 8):
        acc = acc + x[8 * r:8 * r + 8]
    return acc


def _rms_fwd(x, g, name, comm=None):
    s = x.shape[0]

    def body(x_ref, g_ref, h_ref):
        xv = x_ref[...]
        r = lax.rsqrt(jnp.mean(xv * xv, axis=-1, keepdims=True) + EPS)
        h_ref[...] = (xv * r * g_ref[...]).astype(BF16)

    (h,), couts = _pcall(
        body, grid=(s // TM,),
        in_specs=[pl.BlockSpec((TM, D_MODEL), lambda i: (i, 0)), pl.BlockSpec((1, D_MODEL), lambda i: (0, 0))],
        out_specs=[pl.BlockSpec((TM, D_MODEL), lambda i: (i, 0))],
        out_shape=[jax.ShapeDtypeStruct((s, D_MODEL), BF16)], operands=(x, g), name=name, comm=comm)
    return h, couts


def _rms_bwd(dh, x, g, dx_next, name):
    s = x.shape[0]

    def body(dh_ref, x_ref, g_ref, dxn_ref, dx_ref, dg_ref):
        i = pl.program_id(0)
        xv = x_ref[...]
        r = lax.rsqrt(jnp.mean(xv * xv, axis=-1, keepdims=True) + EPS)
        xn = xv * r
        dhv = dh_ref[...]
        dxn = dhv * g_ref[...]
        dx_ref[...] = dxn_ref[...] + r * (dxn - xn * jnp.mean(dxn * xn, axis=-1, keepdims=True))

        @pl.when(i == 0)
        def _():
            dg_ref[...] = jnp.zeros_like(dg_ref)

        dg_ref[...] += _fold8(dhv * xn)

    tile = pl.BlockSpec((TM, D_MODEL), lambda i: (i, 0))
    return pl.pallas_call(
        body, grid=(s // TM,),
        in_specs=[tile, tile, pl.BlockSpec((1, D_MODEL), lambda i: (0, 0)), tile],
        out_specs=[tile, pl.BlockSpec((8, D_MODEL), lambda i: (0, 0))],
        out_shape=[jax.ShapeDtypeStruct((s, D_MODEL), F32), jax.ShapeDtypeStruct((8, D_MODEL), F32)],
        compiler_params=_cp(("arbitrary",)), name=name)(dh, x, g, dx_next)


def _loss_head(x, g, target):
    s = x.shape[0]

    def body(x_ref, g_ref, t_ref, loss_ref, dx_ref, dg_ref):
        i = pl.program_id(0)
        xv = x_ref[...]
        r = lax.rsqrt(jnp.mean(xv * xv, axis=-1, keepdims=True) + EPS)
        xn = xv * r
        err = xn * g_ref[...] - t_ref[...]
        dy = err * (1.0 / D_MODEL)
        dxn = dy * g_ref[...]
        dx_ref[...] = r * (dxn - xn * jnp.mean(dxn * xn, axis=-1, keepdims=True))

        @pl.when(i == 0)
        def _():
            dg_ref[...] = jnp.zeros_like(dg_ref)
            loss_ref[...] = jnp.zeros_like(loss_ref)

        dg_ref[...] += _fold8(dy * xn)
        loss_ref[...] += _fold8(err * err)

    tile = pl.BlockSpec((TM, D_MODEL), lambda i: (i, 0))
    acc = pl.BlockSpec((8, D_MODEL), lambda i: (0, 0))
    return pl.pallas_call(
        body, grid=(s // TM,),
        in_specs=[tile, pl.BlockSpec((1, D_MODEL), lambda i: (0, 0)), tile],
        out_specs=[acc, tile, acc],
        out_shape=[jax.ShapeDtypeStruct((8, D_MODEL), F32), jax.ShapeDtypeStruct((s, D_MODEL), F32),
                   jax.ShapeDtypeStruct((8, D_MODEL), F32)],
        compiler_params=_cp(("arbitrary",)), name="loss_head")(x, g, target)


def _mm_nn(a, b, tn, name, comm=None):
    m, k = a.shape
    n = b.shape[1]
    tm = 512

    def body(a_ref, b_ref, o_ref):
        o_ref[...] = _dot(a_ref[...], b_ref[...])

    (out,), couts = _pcall(
        body, grid=(n // tn, m // tm),
        in_specs=[pl.BlockSpec((tm, k), lambda j, i: (i, 0)), pl.BlockSpec((k, tn), lambda j, i: (0, j))],
        out_specs=[pl.BlockSpec((tm, tn), lambda j, i: (i, j))],
        out_shape=[jax.ShapeDtypeStruct((m, n), F32)], operands=(a, b), name=name, comm=comm)
    return out, couts


def _dh(dpm, dpg, dpf, w_mix, w_merge, w_f, name, comm=None):
    s = dpm.shape[0]
    tm = 1024 if s % 1024 == 0 else 512
    tk1, tk2 = 896, 1024
    n1, n2 = N_MIX // tk1, N_MERGE // tk2

    def body(dpm_ref, dpg_ref, dpf_ref, wm_ref, wg_ref, wf_ref, o_ref):
        j = pl.program_id(1)

        @pl.when(j == 0)
        def _():
            o_ref[...] = _dot_nt(dpf_ref[...], wf_ref[...])

        @pl.when(j < n1)
        def _():
            o_ref[...] += _dot_nt(dpm_ref[...], wm_ref[...])

        @pl.when(j >= n1)
        def _():
            o_ref[...] += _dot_nt(dpg_ref[...], wg_ref[...])

    mix_j = lambda j: jnp.minimum(j, n1 - 1)
    merge_j = lambda j: jnp.maximum(j - n1, 0)
    (out,), couts = _pcall(
        body, grid=(s // tm, n1 + n2),
        in_specs=[pl.BlockSpec((tm, tk1), lambda i, j: (i, mix_j(j))), pl.BlockSpec((tm, tk2), lambda i, j: (i, merge_j(j))),
                  pl.BlockSpec((tm, N_F), lambda i, j: (i, 0)),
                  pl.BlockSpec((D_MODEL, tk1), lambda i, j: (0, mix_j(j))), pl.BlockSpec((D_MODEL, tk2), lambda i, j: (0, merge_j(j))),
                  pl.BlockSpec((D_MODEL, N_F), lambda i, j: (0, 0))],
        out_specs=[pl.BlockSpec((tm, D_MODEL), lambda i, j: (i, 0))],
        out_shape=[jax.ShapeDtypeStruct((s, D_MODEL), F32)], operands=(dpm, dpg, dpf, w_mix, w_merge, w_f), name=name, comm=comm)
    return out, couts


def _mm_tn(a, d, tn, name):
    m, k = a.shape
    n = d.shape[1]
    tm = 512
    nm = m // tm

    def body(a_ref, d_ref, o_ref, acc):
        i = pl.program_id(1)

        @pl.when(i == 0)
        def _():
            acc[...] = jnp.zeros_like(acc)

        acc[...] += _dot_tn(a_ref[...], d_ref[...])

        @pl.when(i == nm - 1)
        def _():
            o_ref[...] = acc[...].astype(BF16)

    return pl.pallas_call(
        body, grid=(n // tn, nm),
        in_specs=[pl.BlockSpec((tm, k), lambda j, i: (i, 0)), pl.BlockSpec((tm, tn), lambda j, i: (i, j))],
        out_specs=pl.BlockSpec((k, tn), lambda j, i: (0, j)),
        out_shape=jax.ShapeDtypeStruct((k, n), BF16), scratch_shapes=[pltpu.VMEM((k, tn), F32)],
        compiler_params=_cp(("parallel", "arbitrary")), name=name)(a, d)


def _lane_head():
    return lax.broadcasted_iota(jnp.int32, (1, BR), 1) // HEAD_DIM


def _gmlp_chunk_fwd(p_ref, r0, gw_ref, gb_ref, lg, lb):
    uv = p_ref[r0:r0 + CHUNK, O_UV:O_UV + 2 * BR]
    u = _gelu(uv[:, :BR])
    vhat, rs = _ln_hat(_gelu(uv[:, BR:]))
    vn = (vhat * lg + lb).astype(BF16)
    head = _lane_head()
    mixed = gb_ref[...]
    for h in range(N_HEADS):
        mixed = mixed + jnp.where(head == h, _dot(gw_ref[h], vn), 0.0)
    return uv, u, vhat, rs, vn, mixed


def _tap_groups(k_width):
    groups = []
    for b in range(8):
        taps = [(d // 8, k_width - 1 - d) for d in range(b, k_width, 8)]
        if taps:
            groups.append((b, taps))
    return groups


def _causal_taps(buf, off, r0, nr, k_width):
    lead = 8 * ((k_width - 1) // 8 + 1)
    win = buf[off + r0 - lead:off + r0 + nr, :]
    for b, taps in _tap_groups(k_width):
        shifted = win if b == 0 else pltpu.roll(win, b, 0)
        for a, k in taps:
            yield k, shifted[lead - 8 * a:lead - 8 * a + nr]


def _anticausal_taps(buf, r0, nr, k_width):
    lead = 8 * ((k_width - 1) // 8 + 1)
    win = buf[r0:r0 + nr + lead, :]
    for b, taps in _tap_groups(k_width):
        shifted = win if b == 0 else pltpu.roll(win, nr + lead - b, 0)
        for a, k in taps:
            yield k, shifted[8 * a:8 * a + nr]


def _conv_sub_blocks(rows):
    out = [(r, SUB) for r in range(0, rows - rows % SUB, SUB)]
    if rows % SUB:
        out.append((rows - rows % SUB, rows % SUB))
    return out


def _mix_fwd(pm, wl, name, comm=None):
    s = pm.shape[0]
    nt = s // TM

    def body(p_ref, ph_ref, gw_ref, gb_ref, lg_ref, lb_ref, scw_ref, dww_ref, dwb_ref, clg_ref, clb_ref,
             ya_ref, yc_ref, yd_ref, q_ref, k_ref, v_ref, cc_ref, zbuf, hbuf):
        i = pl.program_id(0)
        for h in range(N_HEADS):
            c0 = O_QKV + h * HEAD_DIM
            q_ref[h] = (p_ref[:, c0:c0 + HEAD_DIM] * SCALE).astype(BF16)
            k_ref[h] = p_ref[:, c0 + BR:c0 + BR + HEAD_DIM].astype(BF16)
            v_ref[h] = p_ref[:, c0 + 2 * BR:c0 + 2 * BR + HEAD_DIM].astype(BF16)
        lg = lg_ref[...]
        lb = lb_ref[...]
        for c in range(TM // CHUNK):
            r0 = c * CHUNK
            _, u, _, _, _, mixed = _gmlp_chunk_fwd(p_ref, r0, gw_ref, gb_ref, lg, lb)
            ag = p_ref[r0:r0 + CHUNK, O_AG:O_AG + BR]
            ya_ref[r0:r0 + CHUNK, :] = (u * mixed * _silu(ag)).astype(BF16)

        first = i > 0
        zbuf[0:HALO, :] = jnp.where(first, ph_ref[:, O_CIN + BR:O_CIN + 2 * BR] * ph_ref[:, O_CIN + 2 * BR:O_CIN + 3 * BR], 0.0)
        zbuf[HALO:HALO + TM, :] = p_ref[:, O_CIN + BR:O_CIN + 2 * BR] * p_ref[:, O_CIN + 2 * BR:O_CIN + 3 * BR]
        hbuf[0:HALO, :] = jnp.where(first, ph_ref[:, O_GLU:O_GLU + BR] * _sig(ph_ref[:, O_GLU + BR:O_GLU + 2 * BR]), 0.0)
        hbuf[HALO:HALO + TM, :] = p_ref[:, O_GLU:O_GLU + BR] * _sig(p_ref[:, O_GLU + BR:O_GLU + 2 * BR])

        clg = clg_ref[...]
        clb = clb_ref[...]
        for r0, nr in _conv_sub_blocks(TM):
            yc = jnp.zeros((nr, BR), F32)
            for k, zk in _causal_taps(zbuf, HALO, r0, nr, SHORT_CONV):
                yc = yc + scw_ref[k:k + 1, :] * zk
            bgate = p_ref[r0:r0 + nr, O_CIN:O_CIN + BR]
            cg = p_ref[r0:r0 + nr, O_CG:O_CG + BR]
            yc_ref[r0:r0 + nr, :] = (bgate * yc * _silu(cg)).astype(BF16)

            cc = jnp.zeros((nr, BR), F32) + dwb_ref[...]
            for k, hk in _causal_taps(hbuf, HALO, r0, nr, CONF_CONV):
                cc = cc + dww_ref[k:k + 1, :] * hk
            cc_ref[r0:r0 + nr, :] = cc
            chat, _ = _ln_hat(cc)
            dg = p_ref[r0:r0 + nr, O_DG:O_DG + BR]
            yd_ref[r0:r0 + nr, :] = (_silu(chat * clg + clb) * _silu(dg)).astype(BF16)

    full = lambda shape: pl.BlockSpec(shape, lambda i: tuple(0 for _ in shape))
    ytile = pl.BlockSpec((TM, BR), lambda i: (i, 0))
    yshape = jax.ShapeDtypeStruct((s, BR), BF16)
    htile = pl.BlockSpec((N_HEADS, TM, HEAD_DIM), lambda i: (0, i, 0))
    hshape = jax.ShapeDtypeStruct((N_HEADS, s, HEAD_DIM), BF16)
    return _pcall(
        body, grid=(nt,),
        in_specs=[pl.BlockSpec((TM, N_MIX), lambda i: (i, 0)),
                  pl.BlockSpec((HALO, N_MIX), lambda i: (jnp.maximum(i * (TM // HALO) - 1, 0), 0)),
                  full((N_HEADS, CHUNK, CHUNK)), full((CHUNK, BR)), full((1, BR)), full((1, BR)),
                  full((8, BR)), full((32, BR)), full((1, BR)), full((1, BR)), full((1, BR))],
        out_specs=[ytile, ytile, ytile, htile, htile, htile, ytile],
        out_shape=[yshape, yshape, yshape, hshape, hshape, hshape, jax.ShapeDtypeStruct((s, BR), F32)],
        scratch_shapes=[pltpu.VMEM((HALO + TM, BR), F32), pltpu.VMEM((HALO + TM, BR), F32)],
        name=name, comm=comm, operands=(
            pm, pm, wl["gw"], wl["gb"], wl["sgu_ln_g"], wl["sgu_ln_b"], wl["scw"], wl["dww"], wl["conf_dw_b"],
            wl["conf_ln_g"], wl["conf_ln_b"]))


def _mix_bwd(pm, cc, dya, dyc, dyd, dqkv, dbg, wl, name):
    s = pm.shape[0]
    nt = s // TM
    ext = TM + HALO

    def body(p_ref, ph_ref, pn_ref, cc_ref, ccn_ref, dya_ref, dyc_ref, dycn_ref, dyd_ref, dydn_ref, dq_ref, dk_ref, dv_ref, dbg_ref,
             gw_ref, gwt_ref, gb_ref, lg_ref, lb_ref, scw_ref, dww_ref, dwb_ref, clg_ref, clb_ref,
             dp_ref, dgw_ref, dgb_ref, vec_ref, dscw_ref, ddww_ref, zbuf, dcb, hbuf, dcc):
        i = pl.program_id(0)

        @pl.when(i == 0)
        def _():
            dgw_ref[...] = jnp.zeros_like(dgw_ref)
            dgb_ref[...] = jnp.zeros_like(dgb_ref)
            vec_ref[...] = jnp.zeros_like(vec_ref)
            dscw_ref[...] = jnp.zeros_like(dscw_ref)
            ddww_ref[...] = jnp.zeros_like(ddww_ref)

        lg = lg_ref[...]
        lb = lb_ref[...]
        head = _lane_head()
        d_lg = jnp.zeros((1, BR), F32)
        d_lb = jnp.zeros((1, BR), F32)
        for c in range(TM // CHUNK):
            r0 = c * CHUNK
            uv, u, vhat, rs, vn, mixed = _gmlp_chunk_fwd(p_ref, r0, gw_ref, gb_ref, lg, lb)
            ag = p_ref[r0:r0 + CHUNK, O_AG:O_AG + BR]
            dy = dya_ref[r0:r0 + CHUNK, :]
            sa = _silu(ag)
            du = dy * mixed * sa
            dmx = dy * u * sa
            dp_ref[r0:r0 + CHUNK, O_AG:O_AG + BR] = (dy * u * mixed * _dsilu(ag)).astype(BF16)
            dgb_ref[...] += dmx
            dmx_b = dmx.astype(BF16)
            dvn = jnp.zeros((CHUNK, BR), F32)
            for h in range(N_HEADS):
                sel = head == h
                dgw_ref[h] += _dot_nt(jnp.where(sel, dmx, 0.0).astype(BF16), vn)
                dvn = dvn + jnp.where(sel, _dot(gwt_ref[h], dmx_b), 0.0)
            d_lg = d_lg + jnp.sum(dvn * vhat, axis=0, keepdims=True)
            d_lb = d_lb + jnp.sum(dvn, axis=0, keepdims=True)
            dv0 = _ln_bwd(dvn * lg, vhat, rs)
            dp_ref[r0:r0 + CHUNK, O_UV:O_UV + BR] = (du * _dgelu(uv[:, :BR])).astype(BF16)
            dp_ref[r0:r0 + CHUNK, O_UV + BR:O_UV + 2 * BR] = (dv0 * _dgelu(uv[:, BR:])).astype(BF16)
        vec_ref[0:1, :] += d_lg
        vec_ref[1:2, :] += d_lb

        for j, g_ref in enumerate((dq_ref, dk_ref, dv_ref)):
            dp_ref[:, O_QKV + j * BR:O_QKV + (j + 1) * BR] = jnp.concatenate(
                [g_ref[h] for h in range(N_HEADS)], axis=1).astype(BF16)
        dp_ref[:, O_BG:O_BG + BR] = dbg_ref[...].astype(BF16)

        first = i > 0
        last = i < nt - 1
        zbuf[0:HALO, :] = jnp.where(first, ph_ref[:, O_CIN + BR:O_CIN + 2 * BR] * ph_ref[:, O_CIN + 2 * BR:O_CIN + 3 * BR], 0.0)
        zbuf[HALO:HALO + TM, :] = p_ref[:, O_CIN + BR:O_CIN + 2 * BR] * p_ref[:, O_CIN + 2 * BR:O_CIN + 3 * BR]
        dcb[0:TM, :] = dyc_ref[...] * p_ref[:, O_CIN:O_CIN + BR] * _silu(p_ref[:, O_CG:O_CG + BR])
        dcb[TM:ext, :] = jnp.where(last, dycn_ref[...] * pn_ref[:, O_CIN:O_CIN + BR] * _silu(pn_ref[:, O_CG:O_CG + BR]), 0.0)
        for r0, nr in _conv_sub_blocks(TM):
            yc = jnp.zeros((nr, BR), F32)
            dz = jnp.zeros((nr, BR), F32)
            dcur = dcb[r0:r0 + nr, :]
            for k, zk in _causal_taps(zbuf, HALO, r0, nr, SHORT_CONV):
                yc = yc + scw_ref[k:k + 1, :] * zk
                dscw_ref[8 * k:8 * k + 8, :] += _fold8(dcur * zk)
            for k, dk in _anticausal_taps(dcb, r0, nr, SHORT_CONV):
                dz = dz + scw_ref[k:k + 1, :] * dk
            dy = dyc_ref[r0:r0 + nr, :]
            bgate = p_ref[r0:r0 + nr, O_CIN:O_CIN + BR]
            cg = p_ref[r0:r0 + nr, O_CG:O_CG + BR]
            dp_ref[r0:r0 + nr, O_CIN:O_CIN + BR] = (dy * yc * _silu(cg)).astype(BF16)
            dp_ref[r0:r0 + nr, O_CIN + BR:O_CIN + 2 * BR] = (dz * p_ref[r0:r0 + nr, O_CIN + 2 * BR:O_CIN + 3 * BR]).astype(BF16)
            dp_ref[r0:r0 + nr, O_CIN + 2 * BR:O_CIN + 3 * BR] = (dz * p_ref[r0:r0 + nr, O_CIN + BR:O_CIN + 2 * BR]).astype(BF16)
            dp_ref[r0:r0 + nr, O_CG:O_CG + BR] = (dy * bgate * yc * _dsilu(cg)).astype(BF16)

        hbuf[0:HALO, :] = jnp.where(first, ph_ref[:, O_GLU:O_GLU + BR] * _sig(ph_ref[:, O_GLU + BR:O_GLU + 2 * BR]), 0.0)
        hbuf[HALO:HALO + TM, :] = p_ref[:, O_GLU:O_GLU + BR] * _sig(p_ref[:, O_GLU + BR:O_GLU + 2 * BR])
        clg = clg_ref[...]
        clb = clb_ref[...]
        d_clg = jnp.zeros((1, BR), F32)
        d_clb = jnp.zeros((1, BR), F32)
        d_dwb = jnp.zeros((1, BR), F32)
        for r0, nr in _conv_sub_blocks(ext):
            in_tile = r0 < TM
            chat, rs = _ln_hat(cc_ref[r0:r0 + nr, :] if in_tile else ccn_ref[...])
            ln = chat * clg + clb
            if in_tile:
                dy = dyd_ref[r0:r0 + nr, :]
                dg = p_ref[r0:r0 + nr, O_DG:O_DG + BR]
            else:
                dy = jnp.where(last, dydn_ref[...], 0.0)
                dg = pn_ref[:, O_DG:O_DG + BR]
            dln = dy * _silu(dg) * _dsilu(ln)
            dc = _ln_bwd(dln * clg, chat, rs)
            dcc[r0:r0 + nr, :] = dc
            if in_tile:
                dp_ref[r0:r0 + nr, O_DG:O_DG + BR] = (dy * _silu(ln) * _dsilu(dg)).astype(BF16)
                d_clg = d_clg + jnp.sum(dln * chat, axis=0, keepdims=True)
                d_clb = d_clb + jnp.sum(dln, axis=0, keepdims=True)
                d_dwb = d_dwb + jnp.sum(dc, axis=0, keepdims=True)
        vec_ref[2:3, :] += d_dwb
        vec_ref[3:4, :] += d_clg
        vec_ref[4:5, :] += d_clb
        for r0, nr in _conv_sub_blocks(TM):
            dcur = dcc[r0:r0 + nr, :]
            dhh = jnp.zeros((nr, BR), F32)
            for k, hk in _causal_taps(hbuf, HALO, r0, nr, CONF_CONV):
                ddww_ref[8 * k:8 * k + 8, :] += _fold8(dcur * hk)
            for k, dk in _anticausal_taps(dcc, r0, nr, CONF_CONV):
                dhh = dhh + dww_ref[k:k + 1, :] * dk
            a = p_ref[r0:r0 + nr, O_GLU:O_GLU + BR]
            sg = _sig(p_ref[r0:r0 + nr, O_GLU + BR:O_GLU + 2 * BR])
            dp_ref[r0:r0 + nr, O_GLU:O_GLU + BR] = (dhh * sg).astype(BF16)
            dp_ref[r0:r0 + nr, O_GLU + BR:O_GLU + 2 * BR] = (dhh * a * sg * (1.0 - sg)).astype(BF16)

    full = lambda shape: pl.BlockSpec(shape, lambda i: tuple(0 for _ in shape))
    rpt = TM // HALO
    prev_map = lambda i: (jnp.maximum(i * rpt - 1, 0), 0)
    next_map = lambda i: (jnp.minimum((i + 1) * rpt, nt * rpt - 1), 0)
    ytile = pl.BlockSpec((TM, BR), lambda i: (i, 0))
    htile = pl.BlockSpec((N_HEADS, TM, HEAD_DIM), lambda i: (0, i, 0))
    return pl.pallas_call(
        body, grid=(nt,),
        in_specs=[pl.BlockSpec((TM, N_MIX), lambda i: (i, 0)), pl.BlockSpec((HALO, N_MIX), prev_map),
                  pl.BlockSpec((HALO, N_MIX), next_map),
                  ytile, pl.BlockSpec((HALO, BR), next_map),
                  ytile, ytile, pl.BlockSpec((HALO, BR), next_map), ytile, pl.BlockSpec((HALO, BR), next_map),
                  htile, htile, htile, ytile,
                  full((N_HEADS, CHUNK, CHUNK)), full((N_HEADS, CHUNK, CHUNK)), full((CHUNK, BR)), full((1, BR)), full((1, BR)),
                  full((8, BR)), full((32, BR)), full((1, BR)), full((1, BR)), full((1, BR))],
        out_specs=[pl.BlockSpec((TM, N_MIX), lambda i: (i, 0)), full((N_HEADS, CHUNK, CHUNK)), full((CHUNK, BR)),
                   full((16, BR)), full((64, BR)), full((256, BR))],
        out_shape=[jax.ShapeDtypeStruct((s, N_MIX), BF16), jax.ShapeDtypeStruct((N_HEADS, CHUNK, CHUNK), F32),
                   jax.ShapeDtypeStruct((CHUNK, BR), F32), jax.ShapeDtypeStruct((16, BR), F32),
                   jax.ShapeDtypeStruct((64, BR), F32), jax.ShapeDtypeStruct((256, BR), F32)],
        scratch_shapes=[pltpu.VMEM((HALO + TM, BR), F32), pltpu.VMEM((ext, BR), F32),
                        pltpu.VMEM((HALO + TM, BR), F32), pltpu.VMEM((ext, BR), F32)],
        compiler_params=_cp(("arbitrary",)), name=name)(
            pm, pm, pm, cc, cc, dya, dyc, dyc, dyd, dyd, *dqkv, dbg,
            wl["gw"], wl["gwt"], wl["gb"], wl["sgu_ln_g"], wl["sgu_ln_b"], wl["scw"], wl["dww"], wl["conf_dw_b"],
            wl["conf_ln_g"], wl["conf_ln_b"])


def _tri(lower):
    r = lax.broadcasted_iota(jnp.int32, (CUMB, CUMB), 0)
    c = lax.broadcasted_iota(jnp.int32, (CUMB, CUMB), 1)
    return jnp.where((r >= c) if lower else (r <= c), 1.0, 0.0).astype(F32)


def _dot_hi(a, b):
    return jnp.dot(a, b, preferred_element_type=F32, precision=lax.Precision.HIGHEST)


def _cum_fwd(pf, fb, name):
    s = pf.shape[0]

    def body(pf_ref, fb_ref, cum_ref, carry):
        i = pl.program_id(0)

        @pl.when(i == 0)
        def _():
            carry[...] = jnp.zeros_like(carry)

        z = pf_ref[...] + fb_ref[...]
        logf = jnp.minimum(z, 0.0) - jnp.log(1.0 + jnp.exp(-jnp.abs(z)))
        cum_ref[...] = _dot_hi(_tri(True), logf) + carry[...]
        carry[...] += jnp.sum(logf, axis=0, keepdims=True)

    return pl.pallas_call(
        body, grid=(s // CUMB,),
        in_specs=[pl.BlockSpec((CUMB, N_F), lambda i: (i, 0)), pl.BlockSpec((1, N_F), lambda i: (0, 0))],
        out_specs=pl.BlockSpec((CUMB, N_F), lambda i: (i, 0)),
        out_shape=jax.ShapeDtypeStruct((s, N_F), F32),
        scratch_shapes=[pltpu.VMEM((1, N_F), F32)],
        compiler_params=_cp(("arbitrary",)), name=name)(pf, fb)


def _cum_bwd(dcq, dck, pf, fb, name):
    s = pf.shape[0]
    nb = s // CUMB

    def body(dcq_ref, dck_ref, pf_ref, fb_ref, dpf_ref, dfb_ref, carry):
        i = pl.program_id(0)

        @pl.when(i == 0)
        def _():
            carry[...] = jnp.zeros_like(carry)
            dfb_ref[...] = jnp.zeros_like(dfb_ref)

        lane = lax.broadcasted_iota(jnp.int32, (1, N_F), 1)
        dc = dck_ref[...]
        for h in range(N_HEADS):
            dc = dc + jnp.where(lane == h, dcq_ref[h], 0.0)
        dlogf = _dot_hi(_tri(False), dc) + carry[...]
        carry[...] += jnp.sum(dc, axis=0, keepdims=True)
        z = pf_ref[...] + fb_ref[...]
        dz = dlogf * (1.0 - _sig(z))
        dpf_ref[...] = dz.astype(BF16)
        dfb_ref[...] += _fold8(dz)

    rev = lambda i: (nb - 1 - i, 0)
    return pl.pallas_call(
        body, grid=(nb,),
        in_specs=[pl.BlockSpec((N_HEADS, CUMB, 1), lambda i: (0, nb - 1 - i, 0)), pl.BlockSpec((CUMB, N_F), rev),
                  pl.BlockSpec((CUMB, N_F), rev), pl.BlockSpec((1, N_F), lambda i: (0, 0))],
        out_specs=[pl.BlockSpec((CUMB, N_F), rev), pl.BlockSpec((8, N_F), lambda i: (0, 0))],
        out_shape=[jax.ShapeDtypeStruct((s, N_F), BF16), jax.ShapeDtypeStruct((8, N_F), F32)],
        scratch_shapes=[pltpu.VMEM((1, N_F), F32)],
        compiler_params=_cp(("arbitrary",)), name=name)(dcq, dck, pf, fb)


def _causal_mask(nr, nc, r0, c0):
    r = lax.broadcasted_iota(jnp.int32, (nr, nc), 0) + r0
    c = lax.broadcasted_iota(jnp.int32, (nr, nc), 1) + c0
    return r >= c


def _attn_fwd(q, k, v, cq, ck, name, comm=None):
    s = q.shape[1]
    nb = s // FQ

    def body(q_ref, k_ref, v_ref, cq_ref, ck_ref, o_ref, lse_ref):
        for qi in range(nb):
            qs = qi * FQ
            qb = q_ref[0, qs:qs + FQ, :]
            cqb = cq_ref[0, qs:qs + FQ, :]

            def block(kj, carry, masked):
                m, l, acc = carry
                ks = pl.multiple_of(kj * FQ, FQ)
                kb = k_ref[0, pl.ds(ks, FQ), :]
                vb = v_ref[0, pl.ds(ks, FQ), :]
                sc = _dot_nt(qb, kb) + (cqb - ck_ref[0, kj])
                if masked:
                    sc = jnp.where(_causal_mask(FQ, FQ, 0, 0), sc, NEG)
                m_new = jnp.maximum(m, jnp.max(sc, axis=-1, keepdims=True))
                alpha = jnp.exp(m - m_new)
                p = jnp.exp(sc - m_new)
                l = alpha * l + jnp.sum(p, axis=-1, keepdims=True)
                acc = alpha * acc + _dot(p.astype(BF16), vb)
                return m_new, l, acc

            carry = (jnp.full((FQ, 1), NEG, F32), jnp.zeros((FQ, 1), F32), jnp.zeros((FQ, HEAD_DIM), F32))
            if qi > 0:
                carry = lax.fori_loop(0, qi, lambda kj, cr: block(kj, cr, False), carry)
            m, l, acc = block(qi, carry, True)
            o_ref[0, qs:qs + FQ, :] = acc / l
            lse_ref[0, qs:qs + FQ, :] = m + jnp.log(l)

    hblk = pl.BlockSpec((1, s, HEAD_DIM), lambda h: (h, 0, 0))
    cblk = pl.BlockSpec((1, s, 1), lambda h: (h, 0, 0))
    return _pcall(
        body, grid=(N_HEADS,),
        in_specs=[hblk, hblk, hblk, cblk, pl.BlockSpec((1, nb, 1, FQ), lambda h: (h, 0, 0, 0))],
        out_specs=[hblk, cblk],
        out_shape=[jax.ShapeDtypeStruct((N_HEADS, s, HEAD_DIM), F32), jax.ShapeDtypeStruct((N_HEADS, s, 1), F32)],
        operands=(q, k, v, cq, ck), name=name, comm=comm)


def _attn_bwd(q, k, v, cq, ck, o, lse, do, name, comm=None):
    s = q.shape[1]
    nq, nk = s // BQ, s // BK

    def body(q_ref, k_ref, v_ref, cq_ref, ck_ref, o_ref, lse_ref, do_ref, dq_ref, dk_ref, dv_ref, dcq_ref, dck_ref, delta):
        delta[...] = jnp.sum(do_ref[0] * o_ref[0], axis=-1, keepdims=True)
        dq_ref[...] = jnp.zeros_like(dq_ref)
        dcq_ref[...] = jnp.zeros_like(dcq_ref)
        for kj in range(nk):
            ks = kj * BK
            kb = k_ref[0, ks:ks + BK, :]
            vb = v_ref[0, ks:ks + BK, :]
            ckb = ck_ref[0, kj]
            q0 = ks // BQ

            def block(qi, carry, masked):
                dk_acc, dv_acc, dck_acc = carry
                qs = qi * BQ if masked else pl.multiple_of(qi * BQ, BQ)
                qb = q_ref[0, pl.ds(qs, BQ), :]
                dob = do_ref[0, pl.ds(qs, BQ), :].astype(BF16)
                sc = _dot_nt(qb, kb) + (cq_ref[0, pl.ds(qs, BQ), :] - ckb)
                p = jnp.exp(sc - lse_ref[0, pl.ds(qs, BQ), :])
                if masked:
                    p = jnp.where(_causal_mask(BQ, BK, qs, ks), p, 0.0)
                dp = _dot_nt(dob, vb)
                ds = p * (dp - delta[pl.ds(qs, BQ), :])
                ds_b = ds.astype(BF16)
                dv_acc = dv_acc + _dot_tn(p.astype(BF16), dob)
                dk_acc = dk_acc + _dot_tn(ds_b, qb)
                dq_ref[0, pl.ds(qs, BQ), :] += _dot(ds_b, kb) * SCALE
                dcq_ref[0, pl.ds(qs, BQ), :] += jnp.sum(ds, axis=-1, keepdims=True)
                dck_acc = dck_acc - jnp.sum(ds, axis=0, keepdims=True)
                return dk_acc, dv_acc, dck_acc

            carry = (jnp.zeros((BK, HEAD_DIM), F32), jnp.zeros((BK, HEAD_DIM), F32), jnp.zeros((1, BK), F32))
            carry = block(q0, carry, True)
            if q0 < nq - 1:
                carry = lax.fori_loop(q0 + 1, nq, lambda qi, cr: block(qi, cr, False), carry)
            dk_ref[0, ks:ks + BK, :] = carry[0]
            dv_ref[0, ks:ks + BK, :] = carry[1]
            dck_ref[0, kj] = carry[2]

    hblk = pl.BlockSpec((1, s, HEAD_DIM), lambda h: (h, 0, 0))
    cblk = pl.BlockSpec((1, s, 1), lambda h: (h, 0, 0))
    kblk = pl.BlockSpec((1, nk, 1, BK), lambda h: (h, 0, 0, 0))
    hshape = jax.ShapeDtypeStruct((N_HEADS, s, HEAD_DIM), F32)
    return _pcall(
        body, grid=(N_HEADS,),
        in_specs=[hblk, hblk, hblk, cblk, kblk, hblk, cblk, hblk],
        out_specs=[hblk, hblk, hblk, cblk, kblk],
        out_shape=[hshape, hshape, hshape, jax.ShapeDtypeStruct((N_HEADS, s, 1), F32),
                   jax.ShapeDtypeStruct((N_HEADS, nk, 1, BK), F32)],
        scratch_shapes=[pltpu.VMEM((s, 1), F32)],
        operands=(q, k, v, cq, ck, o, lse, do), name=name, comm=comm)


def _merge_fwd(x, ya, yc, yd, o, pm, pg, wb, wo, name, comm=None):
    s = x.shape[0]

    def body(x_ref, ya_ref, yc_ref, yd_ref, o_ref, bg_ref, pg_ref, wb_ref, wo_ref, xo_ref, yb_ref):
        o = jnp.concatenate([o_ref[h] for h in range(N_HEADS)], axis=1)
        yb = (o * _silu(bg_ref[...])).astype(BF16)
        yb_ref[...] = yb
        ys = (ya_ref[...], yb, yc_ref[...], yd_ref[...])
        merged = jnp.zeros((TMG, D_MODEL), F32)
        for n in range(N_BRANCH):
            merged = merged + _sig(pg_ref[:, n * D_MODEL:(n + 1) * D_MODEL]) * _dot(ys[n], wb_ref[n])
        xo_ref[...] = x_ref[...] + _dot(merged.astype(BF16), wo_ref[...])

    xt = pl.BlockSpec((TMG, D_MODEL), lambda i: (i, 0))
    yt = pl.BlockSpec((TMG, BR), lambda i: (i, 0))
    return _pcall(
        body, grid=(s // TMG,),
        in_specs=[xt, yt, yt, yt, pl.BlockSpec((N_HEADS, TMG, HEAD_DIM), lambda i: (0, i, 0)),
                  pl.BlockSpec((TMG, BR), lambda i: (i, O_BG // BR)), pl.BlockSpec((TMG, N_MERGE), lambda i: (i, 0)),
                  pl.BlockSpec((N_BRANCH, BR, D_MODEL), lambda i: (0, 0, 0)), pl.BlockSpec((D_MODEL, D_MODEL), lambda i: (0, 0))],
        out_specs=[xt, yt],
        out_shape=[jax.ShapeDtypeStruct((s, D_MODEL), F32), jax.ShapeDtypeStruct((s, BR), BF16)],
        operands=(x, ya, yc, yd, o, pm, pg, wb, wo), name=name, comm=comm)


def _merge_bwd(dx, ya, yb, yc, yd, o, pm, pg, wb, wo, name, comm=None):
    s = dx.shape[0]
    nt = s // TMG

    def body(dx_ref, ya_ref, yb_ref, yc_ref, yd_ref, o_ref, bg_ref, pg_ref, wb_ref, wo_ref,
             dpg_ref, dya_ref, do_ref, dbg_ref, dyc_ref, dyd_ref, dwb_ref, dwo_ref, dwb_acc, dwo_acc):
        i = pl.program_id(0)

        @pl.when(i == 0)
        def _():
            dwb_acc[...] = jnp.zeros_like(dwb_acc)
            dwo_acc[...] = jnp.zeros_like(dwo_acc)

        dxb = dx_ref[...].astype(BF16)
        dmerged = _dot_nt(dxb, wo_ref[...])
        ys = (ya_ref[...], yb_ref[...], yc_ref[...], yd_ref[...])
        dys = (dya_ref, None, dyc_ref, dyd_ref)
        merged = jnp.zeros((TMG, D_MODEL), F32)
        for n in range(N_BRANCH):
            gate = _sig(pg_ref[:, n * D_MODEL:(n + 1) * D_MODEL])
            pr = _dot(ys[n], wb_ref[n])
            merged = merged + gate * pr
            dpg_ref[:, n * D_MODEL:(n + 1) * D_MODEL] = (dmerged * pr * gate * (1.0 - gate)).astype(BF16)
            dpr = (gate * dmerged).astype(BF16)
            dwb_acc[n] += _dot_tn(ys[n], dpr)
            dyn = _dot_nt(dpr, wb_ref[n])
            if n == 1:
                bg = bg_ref[...]
                do = dyn * _silu(bg)
                for h in range(N_HEADS):
                    do_ref[h] = do[:, h * HEAD_DIM:(h + 1) * HEAD_DIM]
                dbg_ref[...] = dyn * jnp.concatenate([o_ref[h] for h in range(N_HEADS)], axis=1) * _dsilu(bg)
            else:
                dys[n][...] = dyn
        dwo_acc[...] += _dot_tn(merged.astype(BF16), dxb)

        @pl.when(i == nt - 1)
        def _():
            dwb_ref[...] = dwb_acc[...].astype(BF16)
            dwo_ref[...] = dwo_acc[...].astype(BF16)

    xt = pl.BlockSpec((TMG, D_MODEL), lambda i: (i, 0))
    yt = pl.BlockSpec((TMG, BR), lambda i: (i, 0))
    gt = pl.BlockSpec((TMG, N_MERGE), lambda i: (i, 0))
    wbs = pl.BlockSpec((N_BRANCH, BR, D_MODEL), lambda i: (0, 0, 0))
    wos = pl.BlockSpec((D_MODEL, D_MODEL), lambda i: (0, 0))
    yf = jax.ShapeDtypeStruct((s, BR), F32)
    ht = pl.BlockSpec((N_HEADS, TMG, HEAD_DIM), lambda i: (0, i, 0))
    return _pcall(
        body, grid=(nt,),
        in_specs=[xt, yt, yt, yt, yt, ht, pl.BlockSpec((TMG, BR), lambda i: (i, O_BG // BR)), gt, wbs, wos],
        out_specs=[gt, yt, ht, yt, yt, yt, wbs, wos],
        out_shape=[jax.ShapeDtypeStruct((s, N_MERGE), BF16), yf, jax.ShapeDtypeStruct((N_HEADS, s, HEAD_DIM), F32), yf, yf, yf,
                   jax.ShapeDtypeStruct((N_BRANCH, BR, D_MODEL), BF16), jax.ShapeDtypeStruct((D_MODEL, D_MODEL), BF16)],
        scratch_shapes=[pltpu.VMEM((N_BRANCH, BR, D_MODEL), F32), pltpu.VMEM((D_MODEL, D_MODEL), F32)],
        operands=(dx, ya, yb, yc, yd, o, pm, pg, wb, wo), name=name, comm=comm)


def _layer_fwd(x, wl, tag, attach=None):
    attach = attach or {}

    def riding(stage):
        comm, sink = attach.get(stage, (None, None))
        return comm, (sink or (lambda res: None))

    s = x.shape[0]
    comm, sink = riding("rms_fwd")
    h, res = _rms_fwd(x, wl["norm_g"], "rms_fwd" + tag, comm)
    sink(res)
    comm, sink = riding("proj_mix")
    pm, res = _mm_nn(h, wl["w_mix"], 1792, "proj_mix" + tag, comm)
    sink(res)
    comm, sink = riding("mix_fwd")
    (ya, yc, yd, q, k, v, cc), res = _mix_fwd(pm, wl, "mix_fwd" + tag, comm)
    sink(res)
    comm, sink = riding("proj_merge")
    pg, res = _mm_nn(h, wl["w_merge"], 2048, "proj_merge" + tag, comm)
    sink(res)
    pf, _ = _mm_nn(h, wl["w_f"], N_F, "proj_f" + tag)
    cum = _cum_fwd(pf, wl["f_bias"], "cum_fwd" + tag)
    cum_t = cum[:, :N_HEADS].T
    cq = cum_t.reshape(N_HEADS, s, 1)
    ck = cum_t.reshape(N_HEADS, s // BK, 1, BK)
    comm, sink = riding("attn_fwd")
    (o, lse), res = _attn_fwd(q, k, v, cq, cum_t.reshape(N_HEADS, s // FQ, 1, FQ), "attn_fwd" + tag, comm)
    sink(res)
    comm, sink = riding("merge_fwd")
    (x_next, yb), res = _merge_fwd(x, ya, yc, yd, o, pm, pg, wl["wb"], wl["wo"], "merge_fwd" + tag, comm)
    sink(res)
    saved = dict(x=x, h=h, pm=pm, pg=pg, pf=pf, cc=cc, ya=ya, yb=yb, yc=yc, yd=yd, q=q, k=k, v=v, cq=cq, ck=ck, o=o, lse=lse)
    return x_next, saved


def _blocks_rows(g):
    return g.reshape(N_DEV, g.shape[0] // N_DEV, g.shape[1])


def _blocks_cols(g):
    return g.reshape(N_BRANCH * BR, N_DEV, D_MODEL // N_DEV).transpose(1, 0, 2)


def _layer_bwd(dx_next, sv, wl, tag, dist, riding, extra_small):
    s = dx_next.shape[0]
    (dpg, dya, do, dbg, dyc, dyd, dwb, dwo), rode = _merge_bwd(
        dx_next, sv["ya"], sv["yb"], sv["yc"], sv["yd"], sv["o"], sv["pm"], sv["pg"], wl["wb"], wl["wo"], "merge_bwd" + tag,
        riding)
    dw_merge = _mm_tn(sv["h"], dpg, 2048, "dw_merge" + tag)
    early = [(_blocks_rows(dw_merge), False), (_blocks_cols(dwb), False), (_blocks_rows(dwo), False)] if dist else None
    (dq, dk, dv, dcq, dck), early_out = _attn_bwd(sv["q"], sv["k"], sv["v"], sv["cq"], sv["ck"], sv["o"], sv["lse"], do,
                                                  "attn_bwd" + tag, early)
    dck_cols = jnp.pad(dck.reshape(N_HEADS, s).T, ((0, 0), (0, N_F - N_HEADS)))
    dpf, dfb = _cum_bwd(dcq, dck_cols, sv["pf"], wl["f_bias"], "cum_bwd" + tag)
    dpm, dgw, dgb, vec, dscw, ddww = _mix_bwd(sv["pm"], sv["cc"], dya, dyc, dyd, (dq, dk, dv), dbg, wl, "mix_bwd" + tag)
    dw_mix = _mm_tn(sv["h"], dpm, 1792, "dw_mix" + tag)
    dw_f = _mm_tn(sv["h"], dpf, N_F, "dw_f" + tag)
    causal = jnp.tril(jnp.ones((CHUNK, CHUNK), bool))
    small = dict(
        f_bias=dfb.sum(0)[:N_HEADS],
        sgu_w=jnp.where(causal[None], dgw, 0.0),
        sgu_b=dgb.reshape(CHUNK, N_HEADS, HEAD_DIM).sum(-1).T,
        sgu_ln_g=vec[0], sgu_ln_b=vec[1], conf_dw_b=vec[2], conf_ln_g=vec[3], conf_ln_b=vec[4],
        short_conv_w=dscw.reshape(8, 8, BR).sum(1)[:SHORT_CONV],
        conf_dw_w=ddww.reshape(32, 8, BR).sum(1)[:CONF_CONV],
    )
    slab, spans = _pack([small[nm] for nm in SMALL[1:]])
    late = [(_blocks_rows(dw_mix), False), (_blocks_rows(dw_f), False), (slab, True)] if dist else None
    dh, late_out = _dh(dpm, dpg, dpf, wl["w_mix"], wl["w_merge"], wl["w_f"], "dh" + tag, late)
    dx, dng = _rms_bwd(dh, sv["x"], wl["norm_g"], dx_next, "rms_bwd" + tag)
    small["norm_g"] = dng.sum(0)
    grads = dict(small, w_mix=dw_mix, w_merge=dw_merge, w_f=dw_f, wb=dwb, wo=dwo)
    last_slab, last_spans = _pack([small["norm_g"]] + list(extra_small))
    return dx, grads, early_out, (late_out, spans), ([(last_slab, True)], last_spans), rode


def _prep_layer_small(norm_g, f_bias, sgu_w, sgu_b, sgu_ln_g, sgu_ln_b, conf_dw_b, conf_ln_g, conf_ln_b):
    causal = jnp.tril(jnp.ones((CHUNK, CHUNK), bool))
    gw = jnp.where(causal[None], sgu_w, 0.0)
    row = lambda a: a.reshape(1, -1)
    return dict(
        norm_g=row(norm_g),
        f_bias=jnp.pad(row(f_bias), ((0, 0), (0, N_F - N_HEADS))),
        gw=gw.astype(BF16), gwt=gw.transpose(0, 2, 1).astype(BF16),
        gb=jnp.repeat(sgu_b.T, HEAD_DIM, axis=1),
        sgu_ln_g=row(sgu_ln_g), sgu_ln_b=row(sgu_ln_b),
        conf_dw_b=row(conf_dw_b), conf_ln_g=row(conf_ln_g), conf_ln_b=row(conf_ln_b))


def _local_step(x, target, layers, final_g):
    saved = []
    for l in range(DEPTH):
        x, sv = _layer_fwd(x, layers[l], str(l))
        saved.append(sv)
    loss_p, dx, dfg = _loss_head(x, final_g.reshape(1, D_MODEL), target)
    grads = [None] * DEPTH
    for l in reversed(range(DEPTH)):
        dx, grads[l], _, _, _, _ = _layer_bwd(dx, saved[l], layers[l], str(l), False, None, [])
    return 0.5 / D_MODEL * jnp.sum(loss_p), dx, grads, dfg.sum(0)


def _sum8(a, name):
    _, r, c = a.shape
    tr = r
    while tr * c * a.dtype.itemsize * N_DEV > 4 * 1024 * 1024 and tr % 32 == 0:
        tr //= 2

    def body(a_ref, o_ref):
        acc = a_ref[0].astype(F32)
        for d in range(1, N_DEV):
            acc = acc + a_ref[d].astype(F32)
        o_ref[...] = acc

    return pl.pallas_call(
        body, grid=(r // tr,),
        in_specs=[pl.BlockSpec((N_DEV, tr, c), lambda i: (0, i, 0))],
        out_specs=pl.BlockSpec((tr, c), lambda i: (i, 0)),
        out_shape=jax.ShapeDtypeStruct((r, c), F32), compiler_params=_cp(("parallel",)), name=name)(a)


def _adamw(w, g, m, v, name):
    l, r, c = w.shape
    tr = r
    while tr * c * 4 > 1024 * 1024 and tr % 16 == 0:
        tr //= 2
    c1 = 1.0 - ADAM_B1 ** ADAM_STEP
    c2 = 1.0 - ADAM_B2 ** ADAM_STEP

    def body(w_ref, g_ref, m_ref, v_ref, d_ref, mo_ref, vo_ref):
        gv = g_ref[...]
        mn = ADAM_B1 * m_ref[...] + (1.0 - ADAM_B1) * gv
        vn = ADAM_B2 * v_ref[...] + (1.0 - ADAM_B2) * (gv * gv)
        mo_ref[...] = mn
        vo_ref[...] = vn
        d_ref[...] = -ADAM_LR * ((mn / c1) / (jnp.sqrt(vn / c2) + ADAM_EPS) + ADAM_WD * w_ref[...])

    blk = pl.BlockSpec((1, tr, c), lambda a, i: (a, i, 0))
    shp = jax.ShapeDtypeStruct((l, r, c), F32)
    return pl.pallas_call(
        body, grid=(l, r // tr), in_specs=[blk] * 4, out_specs=[blk] * 3, out_shape=[shp] * 3,
        compiler_params=_cp(("parallel", "parallel")), name=name)(w, g, m, v)


def _pack(parts):
    rows, spans, r = [], [], 0
    for p in parts:
        flat = p.reshape(-1)
        nr = -(-flat.shape[0] // 1024) * 8
        rows.append(jnp.pad(flat, (0, nr * 128 - flat.shape[0])).reshape(nr, 128))
        spans.append((r, nr, p.shape))
        r += nr
    return jnp.concatenate(rows, axis=0), spans


def _unpack(slab, spans):
    out = []
    for r, nr, shape in spans:
        size = math.prod(shape)
        out.append(slab[r:r + nr].reshape(-1)[:size].reshape(shape))
    return out


def _split_w_in(w):
    mix = jnp.concatenate([w[..., 0:1536], w[..., 1540:1796], w[..., 3332:3588], w[..., 1796:2820], w[..., 2820:3332]], axis=-1)
    return mix, w[..., 3588:7684], w[..., 1536:1540]


def _join_w_in(mix, merge, f):
    return jnp.concatenate([mix[..., 0:1536], f, mix[..., 1536:1792], mix[..., 2048:3072], mix[..., 3072:3584],
                            mix[..., 1792:2048], merge], axis=-1)


SMALL = ("norm_g", "f_bias", "sgu_w", "sgu_b", "sgu_ln_g", "sgu_ln_b", "short_conv_w", "conf_dw_w", "conf_dw_b",
         "conf_ln_g", "conf_ln_b")


def kernel(x, norm_g, w_in, f_bias, sgu_w, sgu_b, sgu_ln_g, sgu_ln_b, short_conv_w, conf_dw_w, conf_dw_b, conf_ln_g, conf_ln_b, w_branch, w_out, final_g, loss_target, m_norm_g, m_w_in, m_f_bias, m_sgu_w, m_sgu_b, m_sgu_ln_g, m_sgu_ln_b, m_short_conv_w, m_conf_dw_w, m_conf_dw_b, m_conf_ln_g, m_conf_ln_b, m_w_branch, m_w_out, m_final_g, v_norm_g, v_w_in, v_f_bias, v_sgu_w, v_sgu_b, v_sgu_ln_g, v_sgu_ln_b, v_short_conv_w, v_conf_dw_w, v_conf_dw_b, v_conf_ln_g, v_conf_ln_b, v_w_branch, v_w_out, v_final_g):
    me = 4 * lax.axis_index("x") + 2 * lax.axis_index("y") + lax.axis_index("c")
    rows = D_MODEL // N_DEV
    cshard = BR // N_DEV

    sh = []
    for l in range(DEPTH):
        mix, merge, f = _split_w_in(w_in[l])
        sh.append(dict(mix=mix.astype(BF16), merge=merge.astype(BF16),
                       f=jnp.pad(f, ((0, 0), (0, N_F - N_HEADS))).astype(BF16),
                       wb=w_branch[l].astype(BF16), wo=w_out[l].astype(BF16)))
    conv_slab, conv_spans = _pack([short_conv_w, conf_dw_w])
    layers = [_prep_layer_small(norm_g[l], f_bias[l], sgu_w[l], sgu_b[l], sgu_ln_g[l], sgu_ln_b[l],
                                conf_dw_b[l], conf_ln_g[l], conf_ln_b[l]) for l in range(DEPTH)]
    half = N_MERGE // 2
    merge_halves = []

    def put_first(res):
        conv_full = [_unpack(res[2][d], conv_spans) for d in range(N_DEV)]
        scw_full = jnp.concatenate([cf[0] for cf in conv_full], axis=-1)
        dww_full = jnp.concatenate([cf[1] for cf in conv_full], axis=-1)
        for l in range(DEPTH):
            layers[l].update(scw=jnp.pad(scw_full[l], ((0, 8 - SHORT_CONV), (0, 0))),
                             dww=jnp.pad(dww_full[l], ((0, 32 - CONF_CONV), (0, 0))))
        layers[0].update(w_mix=res[0].reshape(D_MODEL, N_MIX), w_f=res[1].reshape(D_MODEL, N_F))

    def put_in(l):
        def sink(res):
            layers[l].update(w_mix=res[0].reshape(D_MODEL, N_MIX), w_f=res[1].reshape(D_MODEL, N_F),
                             w_merge=res[2].reshape(D_MODEL, N_MERGE))
        return sink

    def put_merge_half(res):
        merge_halves.append(res[0])
        if len(merge_halves) == 2:
            layers[0].update(w_merge=jnp.concatenate(merge_halves, axis=-1).reshape(D_MODEL, N_MERGE))

    def put_out(l):
        def sink(res):
            layers[l].update(wb=res[0].transpose(1, 2, 0, 3).reshape(N_BRANCH, BR, D_MODEL), wo=res[1].reshape(D_MODEL, D_MODEL))
        return sink

    attach0 = {
        "rms_fwd": ([(sh[0]["mix"], True), (sh[0]["f"], True), (conv_slab, True)], put_first),
        "proj_mix": ([(sh[0]["merge"][:, :half], True)], put_merge_half),
        "mix_fwd": ([(sh[0]["merge"][:, half:], True)], put_merge_half),
        "proj_merge": ([(sh[0]["wb"], True), (sh[0]["wo"], True)], put_out(0)),
        "attn_fwd": ([(sh[1]["mix"], True), (sh[1]["f"], True), (sh[1]["merge"], True)], put_in(1)),
    }
    attach1 = {"proj_mix": ([(sh[1]["wb"], True), (sh[1]["wo"], True)], put_out(1))}

    xs = x[0]
    xs, sv0 = _layer_fwd(xs, layers[0], "0", attach0)
    xs, sv1 = _layer_fwd(xs, layers[1], "1", attach1)
    loss_p, dx, dfg = _loss_head(xs, final_g.reshape(1, D_MODEL), loss_target[0])
    loss_local = (0.5 / D_MODEL * jnp.sum(loss_p)).reshape(1)
    dx, g1, early1, (late1, spans1), (last1, lspans1), _ = _layer_bwd(dx, sv1, layers[1], "1", True, None, [dfg.sum(0)])
    dx, g0, early0, (late0, spans0), (last0, lspans0), last1_out = _layer_bwd(dx, sv0, layers[0], "0", True, last1, [loss_local])
    last0_out = _exchange(last0, "gather_last")

    red, small = [], []
    for l, (early, late, spans, last, lspans) in enumerate(((early0, late0, spans0, last0_out, lspans0),
                                                            (early1, late1, spans1, last1_out, lspans1))):
        t = str(l)
        red.append(dict(merge=_sum8(early[0], "sum_merge" + t), wb=_sum8(early[1], "sum_wb" + t), wo=_sum8(early[2], "sum_wo" + t),
                        mix=_sum8(late[0], "sum_mix" + t), f=_sum8(late[1], "sum_f" + t)))
        keys = ["norm_g", "final_g" if l == DEPTH - 1 else "loss"] + list(SMALL[1:])
        small.append(dict(zip(keys, _unpack(_sum8(last[0], "sum_last" + t), lspans)
                              + _unpack(_sum8(late[2], "sum_small" + t), spans))))
    gs = {nm: jnp.stack([small[l][nm] for l in range(DEPTH)]) for nm in SMALL}
    gs["final_g"] = small[DEPTH - 1]["final_g"]
    loss = small[0]["loss"][0]
    gs["short_conv_w"] = lax.dynamic_slice_in_dim(gs["short_conv_w"], me * cshard, cshard, axis=2)
    gs["conf_dw_w"] = lax.dynamic_slice_in_dim(gs["conf_dw_w"], me * cshard, cshard, axis=2)
    g_w_in = jnp.stack([_join_w_in(red[l]["mix"], red[l]["merge"], red[l]["f"][:, :N_HEADS]) for l in range(DEPTH)])
    g_w_branch = jnp.stack([red[l]["wb"].reshape(N_BRANCH, BR, rows) for l in range(DEPTH)])
    g_w_out = jnp.stack([red[l]["wo"] for l in range(DEPTH)])

    d_w_in, nm_w_in, nv_w_in = _adamw(w_in, g_w_in, m_w_in, v_w_in, "adamw_w_in")
    flat = lambda a: a.reshape(DEPTH, N_BRANCH * BR, rows)
    d_w_branch, nm_w_branch, nv_w_branch = (a.reshape(w_branch.shape) for a in _adamw(
        flat(w_branch), flat(g_w_branch), flat(m_w_branch), flat(v_w_branch), "adamw_w_branch"))
    d_w_out, nm_w_out, nv_w_out = _adamw(w_out, g_w_out, m_w_out, v_w_out, "adamw_w_out")
    names = SMALL + ("final_g",)
    ws = dict(zip(names, (norm_g, f_bias, sgu_w, sgu_b, sgu_ln_g, sgu_ln_b, short_conv_w, conf_dw_w, conf_dw_b, conf_ln_g,
                          conf_ln_b, final_g)))
    ms = dict(zip(names, (m_norm_g, m_f_bias, m_sgu_w, m_sgu_b, m_sgu_ln_g, m_sgu_ln_b, m_short_conv_w, m_conf_dw_w,
                          m_conf_dw_b, m_conf_ln_g, m_conf_ln_b, m_final_g)))
    vs = dict(zip(names, (v_norm_g, v_f_bias, v_sgu_w, v_sgu_b, v_sgu_ln_g, v_sgu_ln_b, v_short_conv_w, v_conf_dw_w,
                          v_conf_dw_b, v_conf_ln_g, v_conf_ln_b, v_final_g)))
    w_slab, spans = _pack([ws[nm] for nm in names])
    g_slab, _ = _pack([gs[nm] for nm in names])
    m_slab, _ = _pack([ms[nm] for nm in names])
    v_slab, _ = _pack([vs[nm] for nm in names])
    d_s, nm_s, nv_s = (dict(zip(names, _unpack(a[0], spans))) for a in _adamw(w_slab[None], g_slab[None], m_slab[None],
                                                                              v_slab[None], "adamw_small"))

    def ordered(small, w_in_v, w_branch_v, w_out_v):
        return [small["norm_g"], w_in_v, small["f_bias"], small["sgu_w"], small["sgu_b"], small["sgu_ln_g"],
                small["sgu_ln_b"], small["short_conv_w"], small["conf_dw_w"], small["conf_dw_b"], small["conf_ln_g"],
                small["conf_ln_b"], w_branch_v, w_out_v, small["final_g"]]

    return (loss, dx[None], *ordered(gs, g_w_in, g_w_branch, g_w_out), *ordered(d_s, d_w_in, d_w_branch, d_w_out),
            *ordered(nm_s, nm_w_in, nm_w_branch, nm_w_out), *ordered(nv_s, nv_w_in, nv_w_branch, nv_w_out))
```

```python
import functools
import math

import jax
import jax.numpy as jnp
from jax import lax
from jax.experimental import pallas as pl
from jax.experimental.pallas import tpu as pltpu

F32 = jnp.float32
BF16 = jnp.bfloat16

D_MODEL = 1024
DEPTH = 2
N_BRANCH = 4
BR = 256
N_HEADS = 4
HEAD_DIM = 64
CHUNK = 128
SHORT_CONV = 3
CONF_CONV = 31
EPS = 1e-6
N_DEV = 8

ADAM_LR = 0.001
ADAM_B1 = 0.9
ADAM_B2 = 0.999
ADAM_EPS = 1e-08
ADAM_WD = 0.01
ADAM_STEP = 10

O_UV, O_AG, O_QKV, O_BG, O_DG, O_CIN, O_CG, O_GLU = 0, 512, 768, 1536, 1792, 2048, 2816, 3072
N_MIX = 3584
N_MERGE = N_BRANCH * D_MODEL
N_F = 128
IN_COLS = 7684
HALO = 32
TM = 512
TMG = 256
FQ = 1024
BQ = 1024
BK = 512
SUB = 64
CUMB = 512
VMEM_LIMIT = 56 * 1024 * 1024
NEG = -1e30
SCALE = 1.0 / math.sqrt(HEAD_DIM)
GELU_K = math.sqrt(2.0 / math.pi)


def _cp(sem=None):
    return pltpu.CompilerParams(dimension_semantics=sem, vmem_limit_bytes=VMEM_LIMIT)


PEER_ORDER = (6, 4, 2, 7, 5, 3, 1)
RELAYED = (3, 5, 7)


def _xchg(cin, cout, send, recv, loc, modes, phase):
    x, y, c = lax.axis_index("x"), lax.axis_index("y"), lax.axis_index("c")
    me = 4 * x + 2 * y + c

    def peer_of(kk):
        px, py, pc = lax.rem(x + (kk >> 2 & 1), 2), lax.rem(y + (kk >> 1 & 1), 2), lax.rem(c + (kk & 1), 2)
        return (px, py, pc), 4 * px + 2 * py + pc

    def remote(src, dst, a, kk, pid):
        return pltpu.make_async_remote_copy(src_ref=src, dst_ref=dst, send_sem=send.at[a, kk], recv_sem=recv.at[a, kk],
                                            device_id=pid, device_id_type=pl.DeviceIdType.MESH)

    def outgoing(a, kk):
        if modes[a] and kk in RELAYED:
            _, origin = peer_of(kk - 1)
            return remote(cout[a].at[origin], cout[a].at[origin], a, kk, peer_of(1)[0])
        pid, peer = peer_of(kk)
        return remote(cin[a] if modes[a] else cin[a].at[peer], cout[a].at[me], a, kk, pid)

    def arrival(a, kk):
        _, peer = peer_of(kk)
        return remote(cout[a].at[peer], cout[a].at[peer], a, kk, (x, y, c))

    if phase == "relay":
        for kk in RELAYED:
            for a, gather in enumerate(modes):
                if gather:
                    arrival(a, kk - 1).wait_recv()
                    outgoing(a, kk).start()
        return
    for a, gather in enumerate(modes):
        cp = pltpu.make_async_copy(cin[a] if gather else cin[a].at[me], cout[a].at[me], loc.at[a])
        if phase == "start":
            cp.start()
        else:
            cp.wait()
    if phase == "start":
        for kk in PEER_ORDER:
            for a, gather in enumerate(modes):
                if not (gather and kk in RELAYED):
                    outgoing(a, kk).start()
        return
    for kk in PEER_ORDER:
        for a in range(len(modes)):
            outgoing(a, kk).wait_send()
    for kk in PEER_ORDER:
        for a, gather in enumerate(modes):
            if not (gather and kk + 1 in RELAYED):
                arrival(a, kk).wait_recv()


def _xchg_shapes(comm):
    return [jax.ShapeDtypeStruct((N_DEV,) + tuple(a.shape[(0 if gather else 1):]), a.dtype) for a, gather in comm]


def _xchg_sems(n):
    return [pltpu.SemaphoreType.DMA((n, N_DEV)), pltpu.SemaphoreType.DMA((n, N_DEV)), pltpu.SemaphoreType.DMA((n,))]


def _exchange(comm, name):
    n = len(comm)
    modes = [g for _, g in comm]

    def body(*refs):
        cin, cout, (send, recv, loc) = refs[:n], refs[n:2 * n], refs[2 * n:]
        for phase in ("start", "relay", "finish"):
            _xchg(cin, cout, send, recv, loc, modes, phase)

    anyspec = pl.BlockSpec(memory_space=pl.ANY)
    return pl.pallas_call(
        body, in_specs=[anyspec] * n, out_specs=[anyspec] * n, out_shape=_xchg_shapes(comm),
        scratch_shapes=_xchg_sems(n), name=name)(*[a for a, _ in comm])


def _pcall(body, *, grid, in_specs, out_specs, out_shape, operands, name, scratch_shapes=(), comm=None):
    if not comm:
        outs = pl.pallas_call(
            body, grid=grid, in_specs=in_specs, out_specs=out_specs, out_shape=out_shape, scratch_shapes=list(scratch_shapes),
            compiler_params=_cp(("arbitrary",) * len(grid)), name=name)(*operands)
        return list(outs), []
    n, nin, nout, nsc = len(comm), len(operands), len(out_shape), len(scratch_shapes)
    modes = [g for _, g in comm]

    def wrapped(*refs):
        ins, cin = refs[:nin], refs[nin:nin + n]
        outs, cout = refs[nin + n:nin + n + nout], refs[nin + n + nout:nin + 2 * n + nout]
        scratch = refs[nin + 2 * n + nout:]
        own, (send, recv, loc) = scratch[:nsc], scratch[nsc:]
        ids = [pl.program_id(d) for d in range(len(grid))]
        first = functools.reduce(jnp.logical_and, [i == 0 for i in ids])
        last = functools.reduce(jnp.logical_and, [i == g - 1 for i, g in zip(ids, grid)])

        @pl.when(first)
        def _():
            _xchg(cin, cout, send, recv, loc, modes, "start")

        @pl.when(last)
        def _():
            _xchg(cin, cout, send, recv, loc, modes, "relay")

        body(*ins, *outs, *own)

        @pl.when(last)
        def _():
            _xchg(cin, cout, send, recv, loc, modes, "finish")

    anyspec = pl.BlockSpec(memory_space=pl.ANY)
    res = pl.pallas_call(
        wrapped, grid=grid, in_specs=list(in_specs) + [anyspec] * n, out_specs=list(out_specs) + [anyspec] * n,
        out_shape=list(out_shape) + _xchg_shapes(comm), scratch_shapes=list(scratch_shapes) + _xchg_sems(n),
        compiler_params=_cp(("arbitrary",) * len(grid)), name=name)(*operands, *[a for a, _ in comm])
    return list(res[:nout]), list(res[nout:])


def _sig(x):
    return 0.5 * jnp.tanh(0.5 * x) + 0.5


def _silu(x):
    return x * _sig(x)


def _dsilu(x):
    s = _sig(x)
    return s * (1.0 + x * (1.0 - s))


def _gelu(x):
    return 0.5 * x * (1.0 + jnp.tanh(GELU_K * (x + 0.044715 * x * x * x)))


def _dgelu(x):
    t = jnp.tanh(GELU_K * (x + 0.044715 * x * x * x))
    return 0.5 * (1.0 + t) + 0.5 * x * (1.0 - t * t) * GELU_K * (1.0 + 3.0 * 0.044715 * x * x)


def _ln_hat(x):
    mu = jnp.mean(x, axis=-1, keepdims=True)
    xc = x - mu
    rs = lax.rsqrt(jnp.mean(xc * xc, axis=-1, keepdims=True) + EPS)
    return xc * rs, rs


def _ln_bwd(dhat, hat, rs):
    return rs * (dhat - jnp.mean(dhat, axis=-1, keepdims=True) - hat * jnp.mean(dhat * hat, axis=-1, keepdims=True))


def _dot(a, b):
    return jnp.dot(a, b, preferred_element_type=F32)


def _dot_nt(a, b):
    return lax.dot_general(a, b, (((1,), (1,)), ((), ())), preferred_element_type=F32)


def _dot_tn(a, b):
    return lax.dot_general(a, b, (((0,), (0,)), ((), ())), preferred_element_type=F32)


def _fold8(x):
    acc = x[0:8]
    for r in range(1, x.shape[0] // 8):
        acc = acc + x[8 * r:8 * r + 8]
    return acc


def _rms_fwd(x, g, name, comm=None):
    s = x.shape[0]

    def body(x_ref, g_ref, h_ref):
        xv = x_ref[...]
        r = lax.rsqrt(jnp.mean(xv * xv, axis=-1, keepdims=True) + EPS)
        h_ref[...] = (xv * r * g_ref[...]).astype(BF16)

    (h,), couts = _pcall(
        body, grid=(s // TM,),
        in_specs=[pl.BlockSpec((TM, D_MODEL), lambda i: (i, 0)), pl.BlockSpec((1, D_MODEL), lambda i: (0, 0))],
        out_specs=[pl.BlockSpec((TM, D_MODEL), lambda i: (i, 0))],
        out_shape=[jax.ShapeDtypeStruct((s, D_MODEL), BF16)], operands=(x, g), name=name, comm=comm)
    return h, couts


def _rms_bwd(dh, x, g, dx_next, name):
    s = x.shape[0]

    def body(dh_ref, x_ref, g_ref, dxn_ref, dx_ref, dg_ref):
        i = pl.program_id(0)
        xv = x_ref[...]
        r = lax.rsqrt(jnp.mean(xv * xv, axis=-1, keepdims=True) + EPS)
        xn = xv * r
        dhv = dh_ref[...]
        dxn = dhv * g_ref[...]
        dx_ref[...] = dxn_ref[...] + r * (dxn - xn * jnp.mean(dxn * xn, axis=-1, keepdims=True))

        @pl.when(i == 0)
        def _():
            dg_ref[...] = jnp.zeros_like(dg_ref)

        dg_ref[...] += _fold8(dhv * xn)

    tile = pl.BlockSpec((TM, D_MODEL), lambda i: (i, 0))
    return pl.pallas_call(
        body, grid=(s // TM,),
        in_specs=[tile, tile, pl.BlockSpec((1, D_MODEL), lambda i: (0, 0)), tile],
        out_specs=[tile, pl.BlockSpec((8, D_MODEL), lambda i: (0, 0))],
        out_shape=[jax.ShapeDtypeStruct((s, D_MODEL), F32), jax.ShapeDtypeStruct((8, D_MODEL), F32)],
        compiler_params=_cp(("arbitrary",)), name=name)(dh, x, g, dx_next)


def _loss_head(x, g, target):
    s = x.shape[0]

    def body(x_ref, g_ref, t_ref, loss_ref, dx_ref, dg_ref):
        i = pl.program_id(0)
        xv = x_ref[...]
        r = lax.rsqrt(jnp.mean(xv * xv, axis=-1, keepdims=True) + EPS)
        xn = xv * r
        err = xn * g_ref[...] - t_ref[...]
        dy = err * (1.0 / D_MODEL)
        dxn = dy * g_ref[...]
        dx_ref[...] = r * (dxn - xn * jnp.mean(dxn * xn, axis=-1, keepdims=True))

        @pl.when(i == 0)
        def _():
            dg_ref[...] = jnp.zeros_like(dg_ref)
            loss_ref[...] = jnp.zeros_like(loss_ref)

        dg_ref[...] += _fold8(dy * xn)
        loss_ref[...] += _fold8(err * err)

    tile = pl.BlockSpec((TM, D_MODEL), lambda i: (i, 0))
    acc = pl.BlockSpec((8, D_MODEL), lambda i: (0, 0))
    return pl.pallas_call(
        body, grid=(s // TM,),
        in_specs=[tile, pl.BlockSpec((1, D_MODEL), lambda i: (0, 0)), tile],
        out_specs=[acc, tile, acc],
        out_shape=[jax.ShapeDtypeStruct((8, D_MODEL), F32), jax.ShapeDtypeStruct((s, D_MODEL), F32),
                   jax.ShapeDtypeStruct((8, D_MODEL), F32)],
        compiler_params=_cp(("arbitrary",)), name="loss_head")(x, g, target)


def _mm_nn(a, b, tn, name, comm=None):
    m, k = a.shape
    n = b.shape[1]
    tm = 512

    def body(a_ref, b_ref, o_ref):
        o_ref[...] = _dot(a_ref[...], b_ref[...])

    (out,), couts = _pcall(
        body, grid=(n // tn, m // tm),
        in_specs=[pl.BlockSpec((tm, k), lambda j, i: (i, 0)), pl.BlockSpec((k, tn), lambda j, i: (0, j))],
        out_specs=[pl.BlockSpec((tm, tn), lambda j, i: (i, j))],
        out_shape=[jax.ShapeDtypeStruct((m, n), F32)], operands=(a, b), name=name, comm=comm)
    return out, couts


def _dh(dpm, dpg, dpf, w_mix, w_merge, w_f, name, comm=None):
    s = dpm.shape[0]
    tm = 1024 if s % 1024 == 0 else 512
    tk1, tk2 = 896, 1024
    n1, n2 = N_MIX // tk1, N_MERGE // tk2

    def body(dpm_ref, dpg_ref, dpf_ref, wm_ref, wg_ref, wf_ref, o_ref):
        j = pl.program_id(1)

        @pl.when(j == 0)
        def _():
            o_ref[...] = _dot_nt(dpf_ref[...], wf_ref[...])

        @pl.when(j < n1)
        def _():
            o_ref[...] += _dot_nt(dpm_ref[...], wm_ref[...])

        @pl.when(j >= n1)
        def _():
            o_ref[...] += _dot_nt(dpg_ref[...], wg_ref[...])

    mix_j = lambda j: jnp.minimum(j, n1 - 1)
    merge_j = lambda j: jnp.maximum(j - n1, 0)
    (out,), couts = _pcall(
        body, grid=(s // tm, n1 + n2),
        in_specs=[pl.BlockSpec((tm, tk1), lambda i, j: (i, mix_j(j))), pl.BlockSpec((tm, tk2), lambda i, j: (i, merge_j(j))),
                  pl.BlockSpec((tm, N_F), lambda i, j: (i, 0)),
                  pl.BlockSpec((D_MODEL, tk1), lambda i, j: (0, mix_j(j))), pl.BlockSpec((D_MODEL, tk2), lambda i, j: (0, merge_j(j))),
                  pl.BlockSpec((D_MODEL, N_F), lambda i, j: (0, 0))],
        out_specs=[pl.BlockSpec((tm, D_MODEL), lambda i, j: (i, 0))],
        out_shape=[jax.ShapeDtypeStruct((s, D_MODEL), F32)], operands=(dpm, dpg, dpf, w_mix, w_merge, w_f), name=name, comm=comm)
    return out, couts


def _mm_tn(a, d, tn, name):
    m, k = a.shape
    n = d.shape[1]
    tm = 512
    nm = m // tm

    def body(a_ref, d_ref, o_ref, acc):
        i = pl.program_id(1)

        @pl.when(i == 0)
        def _():
            acc[...] = jnp.zeros_like(acc)

        acc[...] += _dot_tn(a_ref[...], d_ref[...])

        @pl.when(i == nm - 1)
        def _():
            o_ref[...] = acc[...].astype(BF16)

    return pl.pallas_call(
        body, grid=(n // tn, nm),
        in_specs=[pl.BlockSpec((tm, k), lambda j, i: (i, 0)), pl.BlockSpec((tm, tn), lambda j, i: (i, j))],
        out_specs=pl.BlockSpec((k, tn), lambda j, i: (0, j)),
        out_shape=jax.ShapeDtypeStruct((k, n), BF16), scratch_shapes=[pltpu.VMEM((k, tn), F32)],
        compiler_params=_cp(("parallel", "arbitrary")), name=name)(a, d)


def _lane_head():
    return lax.broadcasted_iota(jnp.int32, (1, BR), 1) // HEAD_DIM


def _gmlp_chunk_fwd(p_ref, r0, gw_ref, gb_ref, lg, lb):
    uv = p_ref[r0:r0 + CHUNK, O_UV:O_UV + 2 * BR]
    u = _gelu(uv[:, :BR])
    vhat, rs = _ln_hat(_gelu(uv[:, BR:]))
    vn = (vhat * lg + lb).astype(BF16)
    head = _lane_head()
    mixed = gb_ref[...]
    for h in range(N_HEADS):
        mixed = mixed + jnp.where(head == h, _dot(gw_ref[h], vn), 0.0)
    return uv, u, vhat, rs, vn, mixed


def _tap_groups(k_width):
    groups = []
    for b in range(8):
        taps = [(d // 8, k_width - 1 - d) for d in range(b, k_width, 8)]
        if taps:
            groups.append((b, taps))
    return groups


def _causal_taps(buf, off, r0, nr, k_width):
    lead = 8 * ((k_width - 1) // 8 + 1)
    win = buf[off + r0 - lead:off + r0 + nr, :]
    for b, taps in _tap_groups(k_width):
        shifted = win if b == 0 else pltpu.roll(win, b, 0)
        for a, k in taps:
            yield k, shifted[lead - 8 * a:lead - 8 * a + nr]


def _anticausal_taps(buf, r0, nr, k_width):
    lead = 8 * ((k_width - 1) // 8 + 1)
    win = buf[r0:r0 + nr + lead, :]
    for b, taps in _tap_groups(k_width):
        shifted = win if b == 0 else pltpu.roll(win, nr + lead - b, 0)
        for a, k in taps:
            yield k, shifted[8 * a:8 * a + nr]


def _conv_sub_blocks(rows):
    out = [(r, SUB) for r in range(0, rows - rows % SUB, SUB)]
    if rows % SUB:
        out.append((rows - rows % SUB, rows % SUB))
    return out


def _mix_fwd(pm, wl, name, comm=None):
    s = pm.shape[0]
    nt = s // TM

    def body(p_ref, ph_ref, gw_ref, gb_ref, lg_ref, lb_ref, scw_ref, dww_ref, dwb_ref, clg_ref, clb_ref,
             ya_ref, yc_ref, yd_ref, q_ref, k_ref, v_ref, cc_ref, zbuf, hbuf):
        i = pl.program_id(0)
        for h in range(N_HEADS):
            c0 = O_QKV + h * HEAD_DIM
            q_ref[h] = (p_ref[:, c0:c0 + HEAD_DIM] * SCALE).astype(BF16)
            k_ref[h] = p_ref[:, c0 + BR:c0 + BR + HEAD_DIM].astype(BF16)
            v_ref[h] = p_ref[:, c0 + 2 * BR:c0 + 2 * BR + HEAD_DIM].astype(BF16)
        lg = lg_ref[...]
        lb = lb_ref[...]
        for c in range(TM // CHUNK):
            r0 = c * CHUNK
            _, u, _, _, _, mixed = _gmlp_chunk_fwd(p_ref, r0, gw_ref, gb_ref, lg, lb)
            ag = p_ref[r0:r0 + CHUNK, O_AG:O_AG + BR]
            ya_ref[r0:r0 + CHUNK, :] = (u * mixed * _silu(ag)).astype(BF16)

        first = i > 0
        zbuf[0:HALO, :] = jnp.where(first, ph_ref[:, O_CIN + BR:O_CIN + 2 * BR] * ph_ref[:, O_CIN + 2 * BR:O_CIN + 3 * BR], 0.0)
        zbuf[HALO:HALO + TM, :] = p_ref[:, O_CIN + BR:O_CIN + 2 * BR] * p_ref[:, O_CIN + 2 * BR:O_CIN + 3 * BR]
        hbuf[0:HALO, :] = jnp.where(first, ph_ref[:, O_GLU:O_GLU + BR] * _sig(ph_ref[:, O_GLU + BR:O_GLU + 2 * BR]), 0.0)
        hbuf[HALO:HALO + TM, :] = p_ref[:, O_GLU:O_GLU + BR] * _sig(p_ref[:, O_GLU + BR:O_GLU + 2 * BR])

        clg = clg_ref[...]
        clb = clb_ref[...]
        for r0, nr in _conv_sub_blocks(TM):
            yc = jnp.zeros((nr, BR), F32)
            for k, zk in _causal_taps(zbuf, HALO, r0, nr, SHORT_CONV):
                yc = yc + scw_ref[k:k + 1, :] * zk
            bgate = p_ref[r0:r0 + nr, O_CIN:O_CIN + BR]
            cg = p_ref[r0:r0 + nr, O_CG:O_CG + BR]
            yc_ref[r0:r0 + nr, :] = (bgate * yc * _silu(cg)).astype(BF16)

            cc = jnp.zeros((nr, BR), F32) + dwb_ref[...]
            for k, hk in _causal_taps(hbuf, HALO, r0, nr, CONF_CONV):
                cc = cc + dww_ref[k:k + 1, :] * hk
            cc_ref[r0:r0 + nr, :] = cc
            chat, _ = _ln_hat(cc)
            dg = p_ref[r0:r0 + nr, O_DG:O_DG + BR]
            yd_ref[r0:r0 + nr, :] = (_silu(chat * clg + clb) * _silu(dg)).astype(BF16)

    full = lambda shape: pl.BlockSpec(shape, lambda i: tuple(0 for _ in shape))
    ytile = pl.BlockSpec((TM, BR), lambda i: (i, 0))
    yshape = jax.ShapeDtypeStruct((s, BR), BF16)
    htile = pl.BlockSpec((N_HEADS, TM, HEAD_DIM), lambda i: (0, i, 0))
    hshape = jax.ShapeDtypeStruct((N_HEADS, s, HEAD_DIM), BF16)
    return _pcall(
        body, grid=(nt,),
        in_specs=[pl.BlockSpec((TM, N_MIX), lambda i: (i, 0)),
                  pl.BlockSpec((HALO, N_MIX), lambda i: (jnp.maximum(i * (TM // HALO) - 1, 0), 0)),
                  full((N_HEADS, CHUNK, CHUNK)), full((CHUNK, BR)), full((1, BR)), full((1, BR)),
                  full((8, BR)), full((32, BR)), full((1, BR)), full((1, BR)), full((1, BR))],
        out_specs=[ytile, ytile, ytile, htile, htile, htile, ytile],
        out_shape=[yshape, yshape, yshape, hshape, hshape, hshape, jax.ShapeDtypeStruct((s, BR), F32)],
        scratch_shapes=[pltpu.VMEM((HALO + TM, BR), F32), pltpu.VMEM((HALO + TM, BR), F32)],
        name=name, comm=comm, operands=(
            pm, pm, wl["gw"], wl["gb"], wl["sgu_ln_g"], wl["sgu_ln_b"], wl["scw"], wl["dww"], wl["conf_dw_b"],
            wl["conf_ln_g"], wl["conf_ln_b"]))


def _mix_bwd(pm, cc, dya, dyc, dyd, dqkv, dbg, wl, name):
    s = pm.shape[0]
    nt = s // TM
    ext = TM + HALO

    def body(p_ref, ph_ref, pn_ref, cc_ref, ccn_ref, dya_ref, dyc_ref, dycn_ref, dyd_ref, dydn_ref, dq_ref, dk_ref, dv_ref, dbg_ref,
             gw_ref, gwt_ref, gb_ref, lg_ref, lb_ref, scw_ref, dww_ref, dwb_ref, clg_ref, clb_ref,
             dp_ref, dgw_ref, dgb_ref, vec_ref, dscw_ref, ddww_ref, zbuf, dcb, hbuf, dcc):
        i = pl.program_id(0)

        @pl.when(i == 0)
        def _():
            dgw_ref[...] = jnp.zeros_like(dgw_ref)
            dgb_ref[...] = jnp.zeros_like(dgb_ref)
            vec_ref[...] = jnp.zeros_like(vec_ref)
            dscw_ref[...] = jnp.zeros_like(dscw_ref)
            ddww_ref[...] = jnp.zeros_like(ddww_ref)

        lg = lg_ref[...]
        lb = lb_ref[...]
        head = _lane_head()
        d_lg = jnp.zeros((1, BR), F32)
        d_lb = jnp.zeros((1, BR), F32)
        for c in range(TM // CHUNK):
            r0 = c * CHUNK
            uv, u, vhat, rs, vn, mixed = _gmlp_chunk_fwd(p_ref, r0, gw_ref, gb_ref, lg, lb)
            ag = p_ref[r0:r0 + CHUNK, O_AG:O_AG + BR]
            dy = dya_ref[r0:r0 + CHUNK, :]
            sa = _silu(ag)
            du = dy * mixed * sa
            dmx = dy * u * sa
            dp_ref[r0:r0 + CHUNK, O_AG:O_AG + BR] = (dy * u * mixed * _dsilu(ag)).astype(BF16)
            dgb_ref[...] += dmx
            dmx_b = dmx.astype(BF16)
            dvn = jnp.zeros((CHUNK, BR), F32)
            for h in range(N_HEADS):
                sel = head == h
                dgw_ref[h] += _dot_nt(jnp.where(sel, dmx, 0.0).astype(BF16), vn)
                dvn = dvn + jnp.where(sel, _dot(gwt_ref[h], dmx_b), 0.0)
            d_lg = d_lg + jnp.sum(dvn * vhat, axis=0, keepdims=True)
            d_lb = d_lb + jnp.sum(dvn, axis=0, keepdims=True)
            dv0 = _ln_bwd(dvn * lg, vhat, rs)
            dp_ref[r0:r0 + CHUNK, O_UV:O_UV + BR] = (du * _dgelu(uv[:, :BR])).astype(BF16)
            dp_ref[r0:r0 + CHUNK, O_UV + BR:O_UV + 2 * BR] = (dv0 * _dgelu(uv[:, BR:])).astype(BF16)
        vec_ref[0:1, :] += d_lg
        vec_ref[1:2, :] += d_lb

        for j, g_ref in enumerate((dq_ref, dk_ref, dv_ref)):
            dp_ref[:, O_QKV + j * BR:O_QKV + (j + 1) * BR] = jnp.concatenate(
                [g_ref[h] for h in range(N_HEADS)], axis=1).astype(BF16)
        dp_ref[:, O_BG:O_BG + BR] = dbg_ref[...].astype(BF16)

        first = i > 0
        last = i < nt - 1
        zbuf[0:HALO, :] = jnp.where(first, ph_ref[:, O_CIN + BR:O_CIN + 2 * BR] * ph_ref[:, O_CIN + 2 * BR:O_CIN + 3 * BR], 0.0)
        zbuf[HALO:HALO + TM, :] = p_ref[:, O_CIN + BR:O_CIN + 2 * BR] * p_ref[:, O_CIN + 2 * BR:O_CIN + 3 * BR]
        dcb[0:TM, :] = dyc_ref[...] * p_ref[:, O_CIN:O_CIN + BR] * _silu(p_ref[:, O_CG:O_CG + BR])
        dcb[TM:ext, :] = jnp.where(last, dycn_ref[...] * pn_ref[:, O_CIN:O_CIN + BR] * _silu(pn_ref[:, O_CG:O_CG + BR]), 0.0)
        for r0, nr in _conv_sub_blocks(TM):
            yc = jnp.zeros((nr, BR), F32)
            dz = jnp.zeros((nr, BR), F32)
            dcur = dcb[r0:r0 + nr, :]
            for k, zk in _causal_taps(zbuf, HALO, r0, nr, SHORT_CONV):
                yc = yc + scw_ref[k:k + 1, :] * zk
                dscw_ref[8 * k:8 * k + 8, :] += _fold8(dcur * zk)
            for k, dk in _anticausal_taps(dcb, r0, nr, SHORT_CONV):
                dz = dz + scw_ref[k:k + 1, :] * dk
            dy = dyc_ref[r0:r0 + nr, :]
            bgate = p_ref[r0:r0 + nr, O_CIN:O_CIN + BR]
            cg = p_ref[r0:r0 + nr, O_CG:O_CG + BR]
            dp_ref[r0:r0 + nr, O_CIN:O_CIN + BR] = (dy * yc * _silu(cg)).astype(BF16)
            dp_ref[r0:r0 + nr, O_CIN + BR:O_CIN + 2 * BR] = (dz * p_ref[r0:r0 + nr, O_CIN + 2 * BR:O_CIN + 3 * BR]).astype(BF16)
            dp_ref[r0:r0 + nr, O_CIN + 2 * BR:O_CIN + 3 * BR] = (dz * p_ref[r0:r0 + nr, O_CIN + BR:O_CIN + 2 * BR]).astype(BF16)
            dp_ref[r0:r0 + nr, O_CG:O_CG + BR] = (dy * bgate * yc * _dsilu(cg)).astype(BF16)

        hbuf[0:HALO, :] = jnp.where(first, ph_ref[:, O_GLU:O_GLU + BR] * _sig(ph_ref[:, O_GLU + BR:O_GLU + 2 * BR]), 0.0)
        hbuf[HALO:HALO + TM, :] = p_ref[:, O_GLU:O_GLU + BR] * _sig(p_ref[:, O_GLU + BR:O_GLU + 2 * BR])
        clg = clg_ref[...]
        clb = clb_ref[...]
        d_clg = jnp.zeros((1, BR), F32)
        d_clb = jnp.zeros((1, BR), F32)
        d_dwb = jnp.zeros((1, BR), F32)
        for r0, nr in _conv_sub_blocks(ext):
            in_tile = r0 < TM
            chat, rs = _ln_hat(cc_ref[r0:r0 + nr, :] if in_tile else ccn_ref[...])
            ln = chat * clg + clb
            if in_tile:
                dy = dyd_ref[r0:r0 + nr, :]
                dg = p_ref[r0:r0 + nr, O_DG:O_DG + BR]
            else:
                dy = jnp.where(last, dydn_ref[...], 0.0)
                dg = pn_ref[:, O_DG:O_DG + BR]
            dln = dy * _silu(dg) * _dsilu(ln)
            dc = _ln_bwd(dln * clg, chat, rs)
            dcc[r0:r0 + nr, :] = dc
            if in_tile:
                dp_ref[r0:r0 + nr, O_DG:O_DG + BR] = (dy * _silu(ln) * _dsilu(dg)).astype(BF16)
                d_clg = d_clg + jnp.sum(dln * chat, axis=0, keepdims=True)
                d_clb = d_clb + jnp.sum(dln, axis=0, keepdims=True)
                d_dwb = d_dwb + jnp.sum(dc, axis=0, keepdims=True)
        vec_ref[2:3, :] += d_dwb
        vec_ref[3:4, :] += d_clg
        vec_ref[4:5, :] += d_clb
        for r0, nr in _conv_sub_blocks(TM):
            dcur = dcc[r0:r0 + nr, :]
            dhh = jnp.zeros((nr, BR), F32)
            for k, hk in _causal_taps(hbuf, HALO, r0, nr, CONF_CONV):
                ddww_ref[8 * k:8 * k + 8, :] += _fold8(dcur * hk)
            for k, dk in _anticausal_taps(dcc, r0, nr, CONF_CONV):
                dhh = dhh + dww_ref[k:k + 1, :] * dk
            a = p_ref[r0:r0 + nr, O_GLU:O_GLU + BR]
            sg = _sig(p_ref[r0:r0 + nr, O_GLU + BR:O_GLU + 2 * BR])
            dp_ref[r0:r0 + nr, O_GLU:O_GLU + BR] = (dhh * sg).astype(BF16)
            dp_ref[r0:r0 + nr, O_GLU + BR:O_GLU + 2 * BR] = (dhh * a * sg * (1.0 - sg)).astype(BF16)

    full = lambda shape: pl.BlockSpec(shape, lambda i: tuple(0 for _ in shape))
    rpt = TM // HALO
    prev_map = lambda i: (jnp.maximum(i * rpt - 1, 0), 0)
    next_map = lambda i: (jnp.minimum((i + 1) * rpt, nt * rpt - 1), 0)
    ytile = pl.BlockSpec((TM, BR), lambda i: (i, 0))
    htile = pl.BlockSpec((N_HEADS, TM, HEAD_DIM), lambda i: (0, i, 0))
    return pl.pallas_call(
        body, grid=(nt,),
        in_specs=[pl.BlockSpec((TM, N_MIX), lambda i: (i, 0)), pl.BlockSpec((HALO, N_MIX), prev_map),
                  pl.BlockSpec((HALO, N_MIX), next_map),
                  ytile, pl.BlockSpec((HALO, BR), next_map),
                  ytile, ytile, pl.BlockSpec((HALO, BR), next_map), ytile, pl.BlockSpec((HALO, BR), next_map),
                  htile, htile, htile, ytile,
                  full((N_HEADS, CHUNK, CHUNK)), full((N_HEADS, CHUNK, CHUNK)), full((CHUNK, BR)), full((1, BR)), full((1, BR)),
                  full((8, BR)), full((32, BR)), full((1, BR)), full((1, BR)), full((1, BR))],
        out_specs=[pl.BlockSpec((TM, N_MIX), lambda i: (i, 0)), full((N_HEADS, CHUNK, CHUNK)), full((CHUNK, BR)),
                   full((16, BR)), full((64, BR)), full((256, BR))],
        out_shape=[jax.ShapeDtypeStruct((s, N_MIX), BF16), jax.ShapeDtypeStruct((N_HEADS, CHUNK, CHUNK), F32),
                   jax.ShapeDtypeStruct((CHUNK, BR), F32), jax.ShapeDtypeStruct((16, BR), F32),
                   jax.ShapeDtypeStruct((64, BR), F32), jax.ShapeDtypeStruct((256, BR), F32)],
        scratch_shapes=[pltpu.VMEM((HALO + TM, BR), F32), pltpu.VMEM((ext, BR), F32),
                        pltpu.VMEM((HALO + TM, BR), F32), pltpu.VMEM((ext, BR), F32)],
        compiler_params=_cp(("arbitrary",)), name=name)(
            pm, pm, pm, cc, cc, dya, dyc, dyc, dyd, dyd, *dqkv, dbg,
            wl["gw"], wl["gwt"], wl["gb"], wl["sgu_ln_g"], wl["sgu_ln_b"], wl["scw"], wl["dww"], wl["conf_dw_b"],
            wl["conf_ln_g"], wl["conf_ln_b"])


def _tri(lower):
    r = lax.broadcasted_iota(jnp.int32, (CUMB, CUMB), 0)
    c = lax.broadcasted_iota(jnp.int32, (CUMB, CUMB), 1)
    return jnp.where((r >= c) if lower else (r <= c), 1.0, 0.0).astype(F32)


def _dot_hi(a, b):
    return jnp.dot(a, b, preferred_element_type=F32, precision=lax.Precision.HIGHEST)


def _cum_fwd(pf, fb, name):
    s = pf.shape[0]

    def body(pf_ref, fb_ref, cum_ref, carry):
        i = pl.program_id(0)

        @pl.when(i == 0)
        def _():
            carry[...] = jnp.zeros_like(carry)

        z = pf_ref[...] + fb_ref[...]
        logf = jnp.minimum(z, 0.0) - jnp.log(1.0 + jnp.exp(-jnp.abs(z)))
        cum_ref[...] = _dot_hi(_tri(True), logf) + carry[...]
        carry[...] += jnp.sum(logf, axis=0, keepdims=True)

    return pl.pallas_call(
        body, grid=(s // CUMB,),
        in_specs=[pl.BlockSpec((CUMB, N_F), lambda i: (i, 0)), pl.BlockSpec((1, N_F), lambda i: (0, 0))],
        out_specs=pl.BlockSpec((CUMB, N_F), lambda i: (i, 0)),
        out_shape=jax.ShapeDtypeStruct((s, N_F), F32),
        scratch_shapes=[pltpu.VMEM((1, N_F), F32)],
        compiler_params=_cp(("arbitrary",)), name=name)(pf, fb)


def _cum_bwd(dcq, dck, pf, fb, name):
    s = pf.shape[0]
    nb = s // CUMB

    def body(dcq_ref, dck_ref, pf_ref, fb_ref, dpf_ref, dfb_ref, carry):
        i = pl.program_id(0)

        @pl.when(i == 0)
        def _():
            carry[...] = jnp.zeros_like(carry)
            dfb_ref[...] = jnp.zeros_like(dfb_ref)

        lane = lax.broadcasted_iota(jnp.int32, (1, N_F), 1)
        dc = dck_ref[...]
        for h in range(N_HEADS):
            dc = dc + jnp.where(lane == h, dcq_ref[h], 0.0)
        dlogf = _dot_hi(_tri(False), dc) + carry[...]
        carry[...] += jnp.sum(dc, axis=0, keepdims=True)
        z = pf_ref[...] + fb_ref[...]
        dz = dlogf * (1.0 - _sig(z))
        dpf_ref[...] = dz.astype(BF16)
        dfb_ref[...] += _fold8(dz)

    rev = lambda i: (nb - 1 - i, 0)
    return pl.pallas_call(
        body, grid=(nb,),
        in_specs=[pl.BlockSpec((N_HEADS, CUMB, 1), lambda i: (0, nb - 1 - i, 0)), pl.BlockSpec((CUMB, N_F), rev),
                  pl.BlockSpec((CUMB, N_F), rev), pl.BlockSpec((1, N_F), lambda i: (0, 0))],
        out_specs=[pl.BlockSpec((CUMB, N_F), rev), pl.BlockSpec((8, N_F), lambda i: (0, 0))],
        out_shape=[jax.ShapeDtypeStruct((s, N_F), BF16), jax.ShapeDtypeStruct((8, N_F), F32)],
        scratch_shapes=[pltpu.VMEM((1, N_F), F32)],
        compiler_params=_cp(("arbitrary",)), name=name)(dcq, dck, pf, fb)


def _causal_mask(nr, nc, r0, c0):
    r = lax.broadcasted_iota(jnp.int32, (nr, nc), 0) + r0
    c = lax.broadcasted_iota(jnp.int32, (nr, nc), 1) + c0
    return r >= c


def _attn_fwd(q, k, v, cq, ck, name, comm=None):
    s = q.shape[1]
    nb = s // FQ

    def body(q_ref, k_ref, v_ref, cq_ref, ck_ref, o_ref, lse_ref):
        for qi in range(nb):
            qs = qi * FQ
            qb = q_ref[0, qs:qs + FQ, :]
            cqb = cq_ref[0, qs:qs + FQ, :]

            def block(kj, carry, masked):
                m, l, acc = carry
                ks = pl.multiple_of(kj * FQ, FQ)
                kb = k_ref[0, pl.ds(ks, FQ), :]
                vb = v_ref[0, pl.ds(ks, FQ), :]
                sc = _dot_nt(qb, kb) + (cqb - ck_ref[0, kj])
                if masked:
                    sc = jnp.where(_causal_mask(FQ, FQ, 0, 0), sc, NEG)
                m_new = jnp.maximum(m, jnp.max(sc, axis=-1, keepdims=True))
                alpha = jnp.exp(m - m_new)
                p = jnp.exp(sc - m_new)
                l = alpha * l + jnp.sum(p, axis=-1, keepdims=True)
                acc = alpha * acc + _dot(p.astype(BF16), vb)
                return m_new, l, acc

            carry = (jnp.full((FQ, 1), NEG, F32), jnp.zeros((FQ, 1), F32), jnp.zeros((FQ, HEAD_DIM), F32))
            if qi > 0:
                carry = lax.fori_loop(0, qi, lambda kj, cr: block(kj, cr, False), carry)
            m, l, acc = block(qi, carry, True)
            o_ref[0, qs:qs + FQ, :] = acc / l
            lse_ref[0, qs:qs + FQ, :] = m + jnp.log(l)

    hblk = pl.BlockSpec((1, s, HEAD_DIM), lambda h: (h, 0, 0))
    cblk = pl.BlockSpec((1, s, 1), lambda h: (h, 0, 0))
    return _pcall(
        body, grid=(N_HEADS,),
        in_specs=[hblk, hblk, hblk, cblk, pl.BlockSpec((1, nb, 1, FQ), lambda h: (h, 0, 0, 0))],
        out_specs=[hblk, cblk],
        out_shape=[jax.ShapeDtypeStruct((N_HEADS, s, HEAD_DIM), F32), jax.ShapeDtypeStruct((N_HEADS, s, 1), F32)],
        operands=(q, k, v, cq, ck), name=name, comm=comm)


def _attn_bwd(q, k, v, cq, ck, o, lse, do, name, comm=None):
    s = q.shape[1]
    nq, nk = s // BQ, s // BK

    def body(q_ref, k_ref, v_ref, cq_ref, ck_ref, o_ref, lse_ref, do_ref, dq_ref, dk_ref, dv_ref, dcq_ref, dck_ref, delta):
        delta[...] = jnp.sum(do_ref[0] * o_ref[0], axis=-1, keepdims=True)
        dq_ref[...] = jnp.zeros_like(dq_ref)
        dcq_ref[...] = jnp.zeros_like(dcq_ref)
        for kj in range(nk):
            ks = kj * BK
            kb = k_ref[0, ks:ks + BK, :]
            vb = v_ref[0, ks:ks + BK, :]
            ckb = ck_ref[0, kj]

            def block(qs, nr, carry, masked):
                dk_acc, dv_acc, dck_acc = carry
                rows = pl.ds(qs, nr)
                qb = q_ref[0, rows, :]
                dob = do_ref[0, rows, :].astype(BF16)
                sc = _dot_nt(qb, kb) + (cq_ref[0, rows, :] - ckb)
                p = jnp.exp(sc - lse_ref[0, rows, :])
                if masked:
                    p = jnp.where(_causal_mask(nr, BK, 0, 0), p, 0.0)
                dp = _dot_nt(dob, vb)
                ds = p * (dp - delta[rows, :])
                ds_b = ds.astype(BF16)
                dv_acc = dv_acc + _dot_tn(p.astype(BF16), dob)
                dk_acc = dk_acc + _dot_tn(ds_b, qb)
                dq_ref[0, rows, :] += _dot(ds_b, kb) * SCALE
                dcq_ref[0, rows, :] += jnp.sum(ds, axis=-1, keepdims=True)
                dck_acc = dck_acc - jnp.sum(ds, axis=0, keepdims=True)
                return dk_acc, dv_acc, dck_acc

            carry = (jnp.zeros((BK, HEAD_DIM), F32), jnp.zeros((BK, HEAD_DIM), F32), jnp.zeros((1, BK), F32))
            carry = block(ks, BK, carry, True)
            below = ks + BK
            if below % BQ and below < s:
                carry = block(below, BK, carry, False)
                below += BK
            if below < s:
                carry = lax.fori_loop(below // BQ, nq, lambda qi, cr: block(pl.multiple_of(qi * BQ, BQ), BQ, cr, False), carry)
            dk_ref[0, ks:ks + BK, :] = carry[0]
            dv_ref[0, ks:ks + BK, :] = carry[1]
            dck_ref[0, kj] = carry[2]

    hblk = pl.BlockSpec((1, s, HEAD_DIM), lambda h: (h, 0, 0))
    cblk = pl.BlockSpec((1, s, 1), lambda h: (h, 0, 0))
    kblk = pl.BlockSpec((1, nk, 1, BK), lambda h: (h, 0, 0, 0))
    hshape = jax.ShapeDtypeStruct((N_HEADS, s, HEAD_DIM), F32)
    return _pcall(
        body, grid=(N_HEADS,),
        in_specs=[hblk, hblk, hblk, cblk, kblk, hblk, cblk, hblk],
        out_specs=[hblk, hblk, hblk, cblk, kblk],
        out_shape=[hshape, hshape, hshape, jax.ShapeDtypeStruct((N_HEADS, s, 1), F32),
                   jax.ShapeDtypeStruct((N_HEADS, nk, 1, BK), F32)],
        scratch_shapes=[pltpu.VMEM((s, 1), F32)],
        operands=(q, k, v, cq, ck, o, lse, do), name=name, comm=comm)


def _merge_fwd(x, ya, yc, yd, o, pm, pg, wb, wo, name, comm=None):
    s = x.shape[0]

    def body(x_ref, ya_ref, yc_ref, yd_ref, o_ref, bg_ref, pg_ref, wb_ref, wo_ref, xo_ref, yb_ref):
        o = jnp.concatenate([o_ref[h] for h in range(N_HEADS)], axis=1)
        yb = (o * _silu(bg_ref[...])).astype(BF16)
        yb_ref[...] = yb
        ys = (ya_ref[...], yb, yc_ref[...], yd_ref[...])
        merged = jnp.zeros((TMG, D_MODEL), F32)
        for n in range(N_BRANCH):
            merged = merged + _sig(pg_ref[:, n * D_MODEL:(n + 1) * D_MODEL]) * _dot(ys[n], wb_ref[n])
        xo_ref[...] = x_ref[...] + _dot(merged.astype(BF16), wo_ref[...])

    xt = pl.BlockSpec((TMG, D_MODEL), lambda i: (i, 0))
    yt = pl.BlockSpec((TMG, BR), lambda i: (i, 0))
    return _pcall(
        body, grid=(s // TMG,),
        in_specs=[xt, yt, yt, yt, pl.BlockSpec((N_HEADS, TMG, HEAD_DIM), lambda i: (0, i, 0)),
                  pl.BlockSpec((TMG, BR), lambda i: (i, O_BG // BR)), pl.BlockSpec((TMG, N_MERGE), lambda i: (i, 0)),
                  pl.BlockSpec((N_BRANCH, BR, D_MODEL), lambda i: (0, 0, 0)), pl.BlockSpec((D_MODEL, D_MODEL), lambda i: (0, 0))],
        out_specs=[xt, yt],
        out_shape=[jax.ShapeDtypeStruct((s, D_MODEL), F32), jax.ShapeDtypeStruct((s, BR), BF16)],
        operands=(x, ya, yc, yd, o, pm, pg, wb, wo), name=name, comm=comm)


def _merge_bwd(dx, ya, yb, yc, yd, o, pm, pg, wb, wo, name, comm=None):
    s = dx.shape[0]
    nt = s // TMG

    def body(dx_ref, ya_ref, yb_ref, yc_ref, yd_ref, o_ref, bg_ref, pg_ref, wb_ref, wo_ref,
             dpg_ref, dya_ref, do_ref, dbg_ref, dyc_ref, dyd_ref, dwb_ref, dwo_ref, dwb_acc, dwo_acc):
        i = pl.program_id(0)

        @pl.when(i == 0)
        def _():
            dwb_acc[...] = jnp.zeros_like(dwb_acc)
            dwo_acc[...] = jnp.zeros_like(dwo_acc)

        dxb = dx_ref[...].astype(BF16)
        dmerged = _dot_nt(dxb, wo_ref[...])
        ys = (ya_ref[...], yb_ref[...], yc_ref[...], yd_ref[...])
        dys = (dya_ref, None, dyc_ref, dyd_ref)
        merged = jnp.zeros((TMG, D_MODEL), F32)
        for n in range(N_BRANCH):
            gate = _sig(pg_ref[:, n * D_MODEL:(n + 1) * D_MODEL])
            pr = _dot(ys[n], wb_ref[n])
            merged = merged + gate * pr
            dpg_ref[:, n * D_MODEL:(n + 1) * D_MODEL] = (dmerged * pr * gate * (1.0 - gate)).astype(BF16)
            dpr = (gate * dmerged).astype(BF16)
            dwb_acc[n] += _dot_tn(ys[n], dpr)
            dyn = _dot_nt(dpr, wb_ref[n])
            if n == 1:
                bg = bg_ref[...]
                do = dyn * _silu(bg)
                for h in range(N_HEADS):
                    do_ref[h] = do[:, h * HEAD_DIM:(h + 1) * HEAD_DIM]
                dbg_ref[...] = dyn * jnp.concatenate([o_ref[h] for h in range(N_HEADS)], axis=1) * _dsilu(bg)
            else:
                dys[n][...] = dyn
        dwo_acc[...] += _dot_tn(merged.astype(BF16), dxb)

        @pl.when(i == nt - 1)
        def _():
            dwb_ref[...] = dwb_acc[...].astype(BF16)
            dwo_ref[...] = dwo_acc[...].astype(BF16)

    xt = pl.BlockSpec((TMG, D_MODEL), lambda i: (i, 0))
    yt = pl.BlockSpec((TMG, BR), lambda i: (i, 0))
    gt = pl.BlockSpec((TMG, N_MERGE), lambda i: (i, 0))
    wbs = pl.BlockSpec((N_BRANCH, BR, D_MODEL), lambda i: (0, 0, 0))
    wos = pl.BlockSpec((D_MODEL, D_MODEL), lambda i: (0, 0))
    yf = jax.ShapeDtypeStruct((s, BR), F32)
    ht = pl.BlockSpec((N_HEADS, TMG, HEAD_DIM), lambda i: (0, i, 0))
    return _pcall(
        body, grid=(nt,),
        in_specs=[xt, yt, yt, yt, yt, ht, pl.BlockSpec((TMG, BR), lambda i: (i, O_BG // BR)), gt, wbs, wos],
        out_specs=[gt, yt, ht, yt, yt, yt, wbs, wos],
        out_shape=[jax.ShapeDtypeStruct((s, N_MERGE), BF16), yf, jax.ShapeDtypeStruct((N_HEADS, s, HEAD_DIM), F32), yf, yf, yf,
                   jax.ShapeDtypeStruct((N_BRANCH, BR, D_MODEL), BF16), jax.ShapeDtypeStruct((D_MODEL, D_MODEL), BF16)],
        scratch_shapes=[pltpu.VMEM((N_BRANCH, BR, D_MODEL), F32), pltpu.VMEM((D_MODEL, D_MODEL), F32)],
        operands=(dx, ya, yb, yc, yd, o, pm, pg, wb, wo), name=name, comm=comm)


def _layer_fwd(x, wl, tag, attach=None):
    attach = attach or {}

    def riding(stage):
        comm, sink = attach.get(stage, (None, None))
        return comm, (sink or (lambda res: None))

    s = x.shape[0]
    comm, sink = riding("rms_fwd")
    h, res = _rms_fwd(x, wl["norm_g"], "rms_fwd" + tag, comm)
    sink(res)
    comm, sink = riding("proj_mix")
    pm, res = _mm_nn(h, wl["w_mix"], 1792, "proj_mix" + tag, comm)
    sink(res)
    comm, sink = riding("mix_fwd")
    (ya, yc, yd, q, k, v, cc), res = _mix_fwd(pm, wl, "mix_fwd" + tag, comm)
    sink(res)
    comm, sink = riding("proj_merge")
    pg, res = _mm_nn(h, wl["w_merge"], 2048, "proj_merge" + tag, comm)
    sink(res)
    pf, _ = _mm_nn(h, wl["w_f"], N_F, "proj_f" + tag)
    cum = _cum_fwd(pf, wl["f_bias"], "cum_fwd" + tag)
    cum_t = cum[:, :N_HEADS].T
    cq = cum_t.reshape(N_HEADS, s, 1)
    ck = cum_t.reshape(N_HEADS, s // BK, 1, BK)
    comm, sink = riding("attn_fwd")
    (o, lse), res = _attn_fwd(q, k, v, cq, cum_t.reshape(N_HEADS, s // FQ, 1, FQ), "attn_fwd" + tag, comm)
    sink(res)
    comm, sink = riding("merge_fwd")
    (x_next, yb), res = _merge_fwd(x, ya, yc, yd, o, pm, pg, wl["wb"], wl["wo"], "merge_fwd" + tag, comm)
    sink(res)
    saved = dict(x=x, h=h, pm=pm, pg=pg, pf=pf, cc=cc, ya=ya, yb=yb, yc=yc, yd=yd, q=q, k=k, v=v, cq=cq, ck=ck, o=o, lse=lse)
    return x_next, saved


def _blocks_rows(g):
    return g.reshape(N_DEV, g.shape[0] // N_DEV, g.shape[1])


def _blocks_cols(g):
    return g.reshape(N_BRANCH * BR, N_DEV, D_MODEL // N_DEV).transpose(1, 0, 2)


def _layer_bwd(dx_next, sv, wl, tag, dist, riding, extra_small):
    s = dx_next.shape[0]
    (dpg, dya, do, dbg, dyc, dyd, dwb, dwo), rode = _merge_bwd(
        dx_next, sv["ya"], sv["yb"], sv["yc"], sv["yd"], sv["o"], sv["pm"], sv["pg"], wl["wb"], wl["wo"], "merge_bwd" + tag,
        riding)
    dw_merge = _mm_tn(sv["h"], dpg, 2048, "dw_merge" + tag)
    early = [(_blocks_rows(dw_merge), False), (_blocks_cols(dwb), False), (_blocks_rows(dwo), False)] if dist else None
    (dq, dk, dv, dcq, dck), early_out = _attn_bwd(sv["q"], sv["k"], sv["v"], sv["cq"], sv["ck"], sv["o"], sv["lse"], do,
                                                  "attn_bwd" + tag, early)
    dck_cols = jnp.pad(dck.reshape(N_HEADS, s).T, ((0, 0), (0, N_F - N_HEADS)))
    dpf, dfb = _cum_bwd(dcq, dck_cols, sv["pf"], wl["f_bias"], "cum_bwd" + tag)
    dpm, dgw, dgb, vec, dscw, ddww = _mix_bwd(sv["pm"], sv["cc"], dya, dyc, dyd, (dq, dk, dv), dbg, wl, "mix_bwd" + tag)
    dw_mix = _mm_tn(sv["h"], dpm, 1792, "dw_mix" + tag)
    dw_f = _mm_tn(sv["h"], dpf, N_F, "dw_f" + tag)
    causal = jnp.tril(jnp.ones((CHUNK, CHUNK), bool))
    small = dict(
        f_bias=dfb.sum(0)[:N_HEADS],
        sgu_w=jnp.where(causal[None], dgw, 0.0),
        sgu_b=dgb.reshape(CHUNK, N_HEADS, HEAD_DIM).sum(-1).T,
        sgu_ln_g=vec[0], sgu_ln_b=vec[1], conf_dw_b=vec[2], conf_ln_g=vec[3], conf_ln_b=vec[4],
        short_conv_w=dscw.reshape(8, 8, BR).sum(1)[:SHORT_CONV],
        conf_dw_w=ddww.reshape(32, 8, BR).sum(1)[:CONF_CONV],
    )
    slab, spans = _pack([small[nm] for nm in SMALL[1:]])
    late = [(_blocks_rows(dw_mix), False), (_blocks_rows(dw_f), False), (slab, True)] if dist else None
    dh, late_out = _dh(dpm, dpg, dpf, wl["w_mix"], wl["w_merge"], wl["w_f"], "dh" + tag, late)
    dx, dng = _rms_bwd(dh, sv["x"], wl["norm_g"], dx_next, "rms_bwd" + tag)
    small["norm_g"] = dng.sum(0)
    grads = dict(small, w_mix=dw_mix, w_merge=dw_merge, w_f=dw_f, wb=dwb, wo=dwo)
    last_slab, last_spans = _pack([small["norm_g"]] + list(extra_small))
    return dx, grads, early_out, (late_out, spans), ([(last_slab, True)], last_spans), rode


def _prep_layer_small(norm_g, f_bias, sgu_w, sgu_b, sgu_ln_g, sgu_ln_b, conf_dw_b, conf_ln_g, conf_ln_b):
    causal = jnp.tril(jnp.ones((CHUNK, CHUNK), bool))
    gw = jnp.where(causal[None], sgu_w, 0.0)
    row = lambda a: a.reshape(1, -1)
    return dict(
        norm_g=row(norm_g),
        f_bias=jnp.pad(row(f_bias), ((0, 0), (0, N_F - N_HEADS))),
        gw=gw.astype(BF16), gwt=gw.transpose(0, 2, 1).astype(BF16),
        gb=jnp.repeat(sgu_b.T, HEAD_DIM, axis=1),
        sgu_ln_g=row(sgu_ln_g), sgu_ln_b=row(sgu_ln_b),
        conf_dw_b=row(conf_dw_b), conf_ln_g=row(conf_ln_g), conf_ln_b=row(conf_ln_b))


def _local_step(x, target, layers, final_g):
    saved = []
    for l in range(DEPTH):
        x, sv = _layer_fwd(x, layers[l], str(l))
        saved.append(sv)
    loss_p, dx, dfg = _loss_head(x, final_g.reshape(1, D_MODEL), target)
    grads = [None] * DEPTH
    for l in reversed(range(DEPTH)):
        dx, grads[l], _, _, _, _ = _layer_bwd(dx, saved[l], layers[l], str(l), False, None, [])
    return 0.5 / D_MODEL * jnp.sum(loss_p), dx, grads, dfg.sum(0)


def _sum8(a, name):
    _, r, c = a.shape
    tr = r
    while tr * c * a.dtype.itemsize * N_DEV > 4 * 1024 * 1024 and tr % 32 == 0:
        tr //= 2

    def body(a_ref, o_ref):
        acc = a_ref[0].astype(F32)
        for d in range(1, N_DEV):
            acc = acc + a_ref[d].astype(F32)
        o_ref[...] = acc

    return pl.pallas_call(
        body, grid=(r // tr,),
        in_specs=[pl.BlockSpec((N_DEV, tr, c), lambda i: (0, i, 0))],
        out_specs=pl.BlockSpec((tr, c), lambda i: (i, 0)),
        out_shape=jax.ShapeDtypeStruct((r, c), F32), compiler_params=_cp(("parallel",)), name=name)(a)


def _adamw(w, g, m, v, name):
    l, r, c = w.shape
    tr = r
    while tr * c * 4 > 1024 * 1024 and tr % 16 == 0:
        tr //= 2
    c1 = 1.0 - ADAM_B1 ** ADAM_STEP
    c2 = 1.0 - ADAM_B2 ** ADAM_STEP

    def body(w_ref, g_ref, m_ref, v_ref, d_ref, mo_ref, vo_ref):
        gv = g_ref[...]
        mn = ADAM_B1 * m_ref[...] + (1.0 - ADAM_B1) * gv
        vn = ADAM_B2 * v_ref[...] + (1.0 - ADAM_B2) * (gv * gv)
        mo_ref[...] = mn
        vo_ref[...] = vn
        d_ref[...] = -ADAM_LR * ((mn / c1) / (jnp.sqrt(vn / c2) + ADAM_EPS) + ADAM_WD * w_ref[...])

    blk = pl.BlockSpec((1, tr, c), lambda a, i: (a, i, 0))
    shp = jax.ShapeDtypeStruct((l, r, c), F32)
    return pl.pallas_call(
        body, grid=(l, r // tr), in_specs=[blk] * 4, out_specs=[blk] * 3, out_shape=[shp] * 3,
        compiler_params=_cp(("parallel", "parallel")), name=name)(w, g, m, v)


def _pack(parts):
    rows, spans, r = [], [], 0
    for p in parts:
        flat = p.reshape(-1)
        nr = -(-flat.shape[0] // 1024) * 8
        rows.append(jnp.pad(flat, (0, nr * 128 - flat.shape[0])).reshape(nr, 128))
        spans.append((r, nr, p.shape))
        r += nr
    return jnp.concatenate(rows, axis=0), spans


def _unpack(slab, spans):
    out = []
    for r, nr, shape in spans:
        size = math.prod(shape)
        out.append(slab[r:r + nr].reshape(-1)[:size].reshape(shape))
    return out


def _split_w_in(w):
    mix = jnp.concatenate([w[..., 0:1536], w[..., 1540:1796], w[..., 3332:3588], w[..., 1796:2820], w[..., 2820:3332]], axis=-1)
    return mix, w[..., 3588:7684], w[..., 1536:1540]


def _join_w_in(mix, merge, f):
    return jnp.concatenate([mix[..., 0:1536], f, mix[..., 1536:1792], mix[..., 2048:3072], mix[..., 3072:3584],
                            mix[..., 1792:2048], merge], axis=-1)


SMALL = ("norm_g", "f_bias", "sgu_w", "sgu_b", "sgu_ln_g", "sgu_ln_b", "short_conv_w", "conf_dw_w", "conf_dw_b",
         "conf_ln_g", "conf_ln_b")


def kernel(x, norm_g, w_in, f_bias, sgu_w, sgu_b, sgu_ln_g, sgu_ln_b, short_conv_w, conf_dw_w, conf_dw_b, conf_ln_g, conf_ln_b, w_branch, w_out, final_g, loss_target, m_norm_g, m_w_in, m_f_bias, m_sgu_w, m_sgu_b, m_sgu_ln_g, m_sgu_ln_b, m_short_conv_w, m_conf_dw_w, m_conf_dw_b, m_conf_ln_g, m_conf_ln_b, m_w_branch, m_w_out, m_final_g, v_norm_g, v_w_in, v_f_bias, v_sgu_w, v_sgu_b, v_sgu_ln_g, v_sgu_ln_b, v_short_conv_w, v_conf_dw_w, v_conf_dw_b, v_conf_ln_g, v_conf_ln_b, v_w_branch, v_w_out, v_final_g):
    me = 4 * lax.axis_index("x") + 2 * lax.axis_index("y") + lax.axis_index("c")
    rows = D_MODEL // N_DEV
    cshard = BR // N_DEV

    sh = []
    for l in range(DEPTH):
        mix, merge, f = _split_w_in(w_in[l])
        sh.append(dict(mix=mix.astype(BF16), merge=merge.astype(BF16),
                       f=jnp.pad(f, ((0, 0), (0, N_F - N_HEADS))).astype(BF16),
                       wb=w_branch[l].astype(BF16), wo=w_out[l].astype(BF16)))
    conv_slab, conv_spans = _pack([short_conv_w, conf_dw_w])
    layers = [_prep_layer_small(norm_g[l], f_bias[l], sgu_w[l], sgu_b[l], sgu_ln_g[l], sgu_ln_b[l],
                                conf_dw_b[l], conf_ln_g[l], conf_ln_b[l]) for l in range(DEPTH)]
    half = N_MERGE // 2
    merge_halves = [[] for _ in range(DEPTH)]

    def put_first(res):
        conv_full = [_unpack(res[2][d], conv_spans) for d in range(N_DEV)]
        scw_full = jnp.concatenate([cf[0] for cf in conv_full], axis=-1)
        dww_full = jnp.concatenate([cf[1] for cf in conv_full], axis=-1)
        for l in range(DEPTH):
            layers[l].update(scw=jnp.pad(scw_full[l], ((0, 8 - SHORT_CONV), (0, 0))),
                             dww=jnp.pad(dww_full[l], ((0, 32 - CONF_CONV), (0, 0))))
        layers[0].update(w_mix=res[0].reshape(D_MODEL, N_MIX), w_f=res[1].reshape(D_MODEL, N_F))

    def put_in(l):
        def sink(res):
            layers[l].update(w_mix=res[0].reshape(D_MODEL, N_MIX), w_f=res[1].reshape(D_MODEL, N_F))
            put_merge_half(l)(res[2:])
        return sink

    def put_merge_half(l):
        def sink(res):
            merge_halves[l].append(res[0])
            if len(merge_halves[l]) == 2:
                layers[l].update(w_merge=jnp.concatenate(merge_halves[l], axis=-1).reshape(D_MODEL, N_MERGE))
        return sink

    def put_out(l):
        def sink(res):
            layers[l].update(wb=res[0].transpose(1, 2, 0, 3).reshape(N_BRANCH, BR, D_MODEL), wo=res[1].reshape(D_MODEL, D_MODEL))
        return sink

    attach0 = {
        "rms_fwd": ([(sh[0]["mix"], True), (sh[0]["f"], True), (conv_slab, True)], put_first),
        "proj_mix": ([(sh[0]["merge"][:, :half], True)], put_merge_half(0)),
        "mix_fwd": ([(sh[0]["merge"][:, half:], True)], put_merge_half(0)),
        "proj_merge": ([(sh[0]["wb"], True), (sh[0]["wo"], True)], put_out(0)),
        "attn_fwd": ([(sh[1]["mix"], True), (sh[1]["f"], True), (sh[1]["merge"][:, :half], True)], put_in(1)),
        "merge_fwd": ([(sh[1]["merge"][:, half:], True)], put_merge_half(1)),
    }
    attach1 = {"proj_mix": ([(sh[1]["wb"], True), (sh[1]["wo"], True)], put_out(1))}

    xs = x[0]
    xs, sv0 = _layer_fwd(xs, layers[0], "0", attach0)
    xs, sv1 = _layer_fwd(xs, layers[1], "1", attach1)
    loss_p, dx, dfg = _loss_head(xs, final_g.reshape(1, D_MODEL), loss_target[0])
    loss_local = (0.5 / D_MODEL * jnp.sum(loss_p)).reshape(1)
    dx, g1, early1, (late1, spans1), (last1, lspans1), _ = _layer_bwd(dx, sv1, layers[1], "1", True, None, [dfg.sum(0)])
    dx, g0, early0, (late0, spans0), (last0, lspans0), last1_out = _layer_bwd(dx, sv0, layers[0], "0", True, last1, [loss_local])
    last0_out = _exchange(last0, "gather_last")

    red, small = [], []
    for l, (early, late, spans, last, lspans) in enumerate(((early0, late0, spans0, last0_out, lspans0),
                                                            (early1, late1, spans1, last1_out, lspans1))):
        t = str(l)
        red.append(dict(merge=_sum8(early[0], "sum_merge" + t), wb=_sum8(early[1], "sum_wb" + t), wo=_sum8(early[2], "sum_wo" + t),
                        mix=_sum8(late[0], "sum_mix" + t), f=_sum8(late[1], "sum_f" + t)))
        keys = ["norm_g", "final_g" if l == DEPTH - 1 else "loss"] + list(SMALL[1:])
        small.append(dict(zip(keys, _unpack(_sum8(last[0], "sum_last" + t), lspans)
                              + _unpack(_sum8(late[2], "sum_small" + t), spans))))
    gs = {nm: jnp.stack([small[l][nm] for l in range(DEPTH)]) for nm in SMALL}
    gs["final_g"] = small[DEPTH - 1]["final_g"]
    loss = small[0]["loss"][0]
    gs["short_conv_w"] = lax.dynamic_slice_in_dim(gs["short_conv_w"], me * cshard, cshard, axis=2)
    gs["conf_dw_w"] = lax.dynamic_slice_in_dim(gs["conf_dw_w"], me * cshard, cshard, axis=2)
    g_w_in = jnp.stack([_join_w_in(red[l]["mix"], red[l]["merge"], red[l]["f"][:, :N_HEADS]) for l in range(DEPTH)])
    g_w_branch = jnp.stack([red[l]["wb"].reshape(N_BRANCH, BR, rows) for l in range(DEPTH)])
    g_w_out = jnp.stack([red[l]["wo"] for l in range(DEPTH)])

    d_w_in, nm_w_in, nv_w_in = _adamw(w_in, g_w_in, m_w_in, v_w_in, "adamw_w_in")
    flat = lambda a: a.reshape(DEPTH, N_BRANCH * BR, rows)
    d_w_branch, nm_w_branch, nv_w_branch = (a.reshape(w_branch.shape) for a in _adamw(
        flat(w_branch), flat(g_w_branch), flat(m_w_branch), flat(v_w_branch), "adamw_w_branch"))
    d_w_out, nm_w_out, nv_w_out = _adamw(w_out, g_w_out, m_w_out, v_w_out, "adamw_w_out")
    names = SMALL + ("final_g",)
    ws = dict(zip(names, (norm_g, f_bias, sgu_w, sgu_b, sgu_ln_g, sgu_ln_b, short_conv_w, conf_dw_w, conf_dw_b, conf_ln_g,
                          conf_ln_b, final_g)))
    ms = dict(zip(names, (m_norm_g, m_f_bias, m_sgu_w, m_sgu_b, m_sgu_ln_g, m_sgu_ln_b, m_short_conv_w, m_conf_dw_w,
                          m_conf_dw_b, m_conf_ln_g, m_conf_ln_b, m_final_g)))
    vs = dict(zip(names, (v_norm_g, v_f_bias, v_sgu_w, v_sgu_b, v_sgu_ln_g, v_sgu_ln_b, v_short_conv_w, v_conf_dw_w,
                          v_conf_dw_b, v_conf_ln_g, v_conf_ln_b, v_final_g)))
    w_slab, spans = _pack([ws[nm] for nm in names])
    g_slab, _ = _pack([gs[nm] for nm in names])
    m_slab, _ = _pack([ms[nm] for nm in names])
    v_slab, _ = _pack([vs[nm] for nm in names])
    d_s, nm_s, nv_s = (dict(zip(names, _unpack(a[0], spans))) for a in _adamw(w_slab[None], g_slab[None], m_slab[None],
                                                                              v_slab[None], "adamw_small"))

    def ordered(small, w_in_v, w_branch_v, w_out_v):
        return [small["norm_g"], w_in_v, small["f_bias"], small["sgu_w"], small["sgu_b"], small["sgu_ln_g"],
                small["sgu_ln_b"], small["short_conv_w"], small["conf_dw_w"], small["conf_dw_b"], small["conf_ln_g"],
                small["conf_ln_b"], w_branch_v, w_out_v, small["final_g"]]

    return (loss, dx[None], *ordered(gs, g_w_in, g_w_branch, g_w_out), *ordered(d_s, d_w_in, d_w_branch, d_w_out),
            *ordered(nm_s, nm_w_in, nm_w_branch, nm_w_out), *ordered(nv_s, nv_w_in, nv_w_branch, nv_w_out))
```

```python
import functools
import math

import jax
import jax.numpy as jnp
from jax import lax
from jax.experimental import pallas as pl
from jax.experimental.pallas import tpu as pltpu

F32 = jnp.float32
BF16 = jnp.bfloat16

D_MODEL = 1024
DEPTH = 2
N_BRANCH = 4
BR = 256
N_HEADS = 4
HEAD_DIM = 64
CHUNK = 128
SHORT_CONV = 3
CONF_CONV = 31
EPS = 1e-6
N_DEV = 8

ADAM_LR = 0.001
ADAM_B1 = 0.9
ADAM_B2 = 0.999
ADAM_EPS = 1e-08
ADAM_WD = 0.01
ADAM_STEP = 10

O_UV, O_AG, O_QKV, O_BG, O_DG, O_CIN, O_CG, O_GLU = 0, 512, 768, 1536, 1792, 2048, 2816, 3072
N_MIX = 3584
N_MERGE = N_BRANCH * D_MODEL
N_F = 128
IN_COLS = 7684
HALO = 32
TM = 512
TMG = 256
FQ = 1024
BQ = 1024
BK = 512
SUB = 64
CUMB = 512
VMEM_LIMIT = 56 * 1024 * 1024
NEG = -1e30
SCALE = 1.0 / math.sqrt(HEAD_DIM)
GELU_K = math.sqrt(2.0 / math.pi)


def _cp(sem=None):
    return pltpu.CompilerParams(dimension_semantics=sem, vmem_limit_bytes=VMEM_LIMIT)


PEER_ORDER = (6, 4, 2, 7, 5, 3, 1)
RELAYED = (3, 5, 7)


def _xchg(cin, cout, send, recv, loc, modes, phase):
    x, y, c = lax.axis_index("x"), lax.axis_index("y"), lax.axis_index("c")
    me = 4 * x + 2 * y + c

    def peer_of(kk):
        px, py, pc = lax.rem(x + (kk >> 2 & 1), 2), lax.rem(y + (kk >> 1 & 1), 2), lax.rem(c + (kk & 1), 2)
        return (px, py, pc), 4 * px + 2 * py + pc

    def remote(src, dst, a, kk, pid):
        return pltpu.make_async_remote_copy(src_ref=src, dst_ref=dst, send_sem=send.at[a, kk], recv_sem=recv.at[a, kk],
                                            device_id=pid, device_id_type=pl.DeviceIdType.MESH)

    def outgoing(a, kk):
        if modes[a] and kk in RELAYED:
            _, origin = peer_of(kk - 1)
            return remote(cout[a].at[origin], cout[a].at[origin], a, kk, peer_of(1)[0])
        pid, peer = peer_of(kk)
        return remote(cin[a] if modes[a] else cin[a].at[peer], cout[a].at[me], a, kk, pid)

    def arrival(a, kk):
        _, peer = peer_of(kk)
        return remote(cout[a].at[peer], cout[a].at[peer], a, kk, (x, y, c))

    if phase == "relay":
        for kk in RELAYED:
            for a, gather in enumerate(modes):
                if gather:
                    arrival(a, kk - 1).wait_recv()
                    outgoing(a, kk).start()
        return
    for a, gather in enumerate(modes):
        cp = pltpu.make_async_copy(cin[a] if gather else cin[a].at[me], cout[a].at[me], loc.at[a])
        if phase == "start":
            cp.start()
        else:
            cp.wait()
    if phase == "start":
        for kk in PEER_ORDER:
            for a, gather in enumerate(modes):
                if not (gather and kk in RELAYED):
                    outgoing(a, kk).start()
        return
    for kk in PEER_ORDER:
        for a in range(len(modes)):
            outgoing(a, kk).wait_send()
    for kk in PEER_ORDER:
        for a, gather in enumerate(modes):
            if not (gather and kk + 1 in RELAYED):
                arrival(a, kk).wait_recv()


def _xchg_shapes(comm):
    return [jax.ShapeDtypeStruct((N_DEV,) + tuple(a.shape[(0 if gather else 1):]), a.dtype) for a, gather in comm]


def _xchg_sems(n):
    return [pltpu.SemaphoreType.DMA((n, N_DEV)), pltpu.SemaphoreType.DMA((n, N_DEV)), pltpu.SemaphoreType.DMA((n,))]


def _exchange(comm, name):
    n = len(comm)
    modes = [g for _, g in comm]

    def body(*refs):
        cin, cout, (send, recv, loc) = refs[:n], refs[n:2 * n], refs[2 * n:]
        for phase in ("start", "relay", "finish"):
            _xchg(cin, cout, send, recv, loc, modes, phase)

    anyspec = pl.BlockSpec(memory_space=pl.ANY)
    return pl.pallas_call(
        body, in_specs=[anyspec] * n, out_specs=[anyspec] * n, out_shape=_xchg_shapes(comm),
        scratch_shapes=_xchg_sems(n), name=name)(*[a for a, _ in comm])


def _pcall(body, *, grid, in_specs, out_specs, out_shape, operands, name, scratch_shapes=(), comm=None):
    if not comm:
        outs = pl.pallas_call(
            body, grid=grid, in_specs=in_specs, out_specs=out_specs, out_shape=out_shape, scratch_shapes=list(scratch_shapes),
            compiler_params=_cp(("arbitrary",) * len(grid)), name=name)(*operands)
        return list(outs), []
    n, nin, nout, nsc = len(comm), len(operands), len(out_shape), len(scratch_shapes)
    modes = [g for _, g in comm]

    def wrapped(*refs):
        ins, cin = refs[:nin], refs[nin:nin + n]
        outs, cout = refs[nin + n:nin + n + nout], refs[nin + n + nout:nin + 2 * n + nout]
        scratch = refs[nin + 2 * n + nout:]
        own, (send, recv, loc) = scratch[:nsc], scratch[nsc:]
        ids = [pl.program_id(d) for d in range(len(grid))]
        first = functools.reduce(jnp.logical_and, [i == 0 for i in ids])
        last = functools.reduce(jnp.logical_and, [i == g - 1 for i, g in zip(ids, grid)])

        @pl.when(first)
        def _():
            _xchg(cin, cout, send, recv, loc, modes, "start")

        @pl.when(last)
        def _():
            _xchg(cin, cout, send, recv, loc, modes, "relay")

        body(*ins, *outs, *own)

        @pl.when(last)
        def _():
            _xchg(cin, cout, send, recv, loc, modes, "finish")

    anyspec = pl.BlockSpec(memory_space=pl.ANY)
    res = pl.pallas_call(
        wrapped, grid=grid, in_specs=list(in_specs) + [anyspec] * n, out_specs=list(out_specs) + [anyspec] * n,
        out_shape=list(out_shape) + _xchg_shapes(comm), scratch_shapes=list(scratch_shapes) + _xchg_sems(n),
        compiler_params=_cp(("arbitrary",) * len(grid)), name=name)(*operands, *[a for a, _ in comm])
    return list(res[:nout]), list(res[nout:])


def _sig(x):
    return 0.5 * jnp.tanh(0.5 * x) + 0.5


def _silu(x):
    return x * _sig(x)


def _dsilu(x):
    s = _sig(x)
    return s * (1.0 + x * (1.0 - s))


def _gelu(x):
    return 0.5 * x * (1.0 + jnp.tanh(GELU_K * (x + 0.044715 * x * x * x)))


def _dgelu(x):
    t = jnp.tanh(GELU_K * (x + 0.044715 * x * x * x))
    return 0.5 * (1.0 + t) + 0.5 * x * (1.0 - t * t) * GELU_K * (1.0 + 3.0 * 0.044715 * x * x)


def _ln_hat(x):
    mu = jnp.mean(x, axis=-1, keepdims=True)
    xc = x - mu
    rs = lax.rsqrt(jnp.mean(xc * xc, axis=-1, keepdims=True) + EPS)
    return xc * rs, rs


def _ln_bwd(dhat, hat, rs):
    return rs * (dhat - jnp.mean(dhat, axis=-1, keepdims=True) - hat * jnp.mean(dhat * hat, axis=-1, keepdims=True))


def _dot(a, b):
    return jnp.dot(a, b, preferred_element_type=F32)


def _dot_nt(a, b):
    return lax.dot_general(a, b, (((1,), (1,)), ((), ())), preferred_element_type=F32)


def _dot_tn(a, b):
    return lax.dot_general(a, b, (((0,), (0,)), ((), ())), preferred_element_type=F32)


def _fold8(x):
    acc = x[0:8]
    for r in range(1, x.shape[0] // 8):
        acc = acc + x[8 * r:8 * r + 8]
    return acc


def _rms_fwd(x, g, name, comm=None):
    s = x.shape[0]

    def body(x_ref, g_ref, h_ref):
        xv = x_ref[...]
        r = lax.rsqrt(jnp.mean(xv * xv, axis=-1, keepdims=True) + EPS)
        h_ref[...] = (xv * r * g_ref[...]).astype(BF16)

    (h,), couts = _pcall(
        body, grid=(s // TM,),
        in_specs=[pl.BlockSpec((TM, D_MODEL), lambda i: (i, 0)), pl.BlockSpec((1, D_MODEL), lambda i: (0, 0))],
        out_specs=[pl.BlockSpec((TM, D_MODEL), lambda i: (i, 0))],
        out_shape=[jax.ShapeDtypeStruct((s, D_MODEL), BF16)], operands=(x, g), name=name, comm=comm)
    return h, couts


def _rms_bwd(dh, x, g, dx_next, name):
    s = x.shape[0]

    def body(dh_ref, x_ref, g_ref, dxn_ref, dx_ref, dg_ref):
        i = pl.program_id(0)
        xv = x_ref[...]
        r = lax.rsqrt(jnp.mean(xv * xv, axis=-1, keepdims=True) + EPS)
        xn = xv * r
        dhv = dh_ref[...]
        dxn = dhv * g_ref[...]
        dx_ref[...] = dxn_ref[...] + r * (dxn - xn * jnp.mean(dxn * xn, axis=-1, keepdims=True))

        @pl.when(i == 0)
        def _():
            dg_ref[...] = jnp.zeros_like(dg_ref)

        dg_ref[...] += _fold8(dhv * xn)

    tile = pl.BlockSpec((TM, D_MODEL), lambda i: (i, 0))
    return pl.pallas_call(
        body, grid=(s // TM,),
        in_specs=[tile, tile, pl.BlockSpec((1, D_MODEL), lambda i: (0, 0)), tile],
        out_specs=[tile, pl.BlockSpec((8, D_MODEL), lambda i: (0, 0))],
        out_shape=[jax.ShapeDtypeStruct((s, D_MODEL), F32), jax.ShapeDtypeStruct((8, D_MODEL), F32)],
        compiler_params=_cp(("arbitrary",)), name=name)(dh, x, g, dx_next)


def _loss_head(x, g, target):
    s = x.shape[0]

    def body(x_ref, g_ref, t_ref, loss_ref, dx_ref, dg_ref):
        i = pl.program_id(0)
        xv = x_ref[...]
        r = lax.rsqrt(jnp.mean(xv * xv, axis=-1, keepdims=True) + EPS)
        xn = xv * r
        err = xn * g_ref[...] - t_ref[...]
        dy = err * (1.0 / D_MODEL)
        dxn = dy * g_ref[...]
        dx_ref[...] = r * (dxn - xn * jnp.mean(dxn * xn, axis=-1, keepdims=True))

        @pl.when(i == 0)
        def _():
            dg_ref[...] = jnp.zeros_like(dg_ref)
            loss_ref[...] = jnp.zeros_like(loss_ref)

        dg_ref[...] += _fold8(dy * xn)
        loss_ref[...] += _fold8(err * err)

    tile = pl.BlockSpec((TM, D_MODEL), lambda i: (i, 0))
    acc = pl.BlockSpec((8, D_MODEL), lambda i: (0, 0))
    return pl.pallas_call(
        body, grid=(s // TM,),
        in_specs=[tile, pl.BlockSpec((1, D_MODEL), lambda i: (0, 0)), tile],
        out_specs=[acc, tile, acc],
        out_shape=[jax.ShapeDtypeStruct((8, D_MODEL), F32), jax.ShapeDtypeStruct((s, D_MODEL), F32),
                   jax.ShapeDtypeStruct((8, D_MODEL), F32)],
        compiler_params=_cp(("arbitrary",)), name="loss_head")(x, g, target)


def _mm_nn(a, b, tn, name, comm=None, tm=1024):
    m, k = a.shape
    n = b.shape[1]
    tm = min(tm, m)

    def body(a_ref, b_ref, o_ref):
        o_ref[...] = _dot(a_ref[...], b_ref[...])

    (out,), couts = _pcall(
        body, grid=(n // tn, m // tm),
        in_specs=[pl.BlockSpec((tm, k), lambda j, i: (i, 0)), pl.BlockSpec((k, tn), lambda j, i: (0, j))],
        out_specs=[pl.BlockSpec((tm, tn), lambda j, i: (i, j))],
        out_shape=[jax.ShapeDtypeStruct((m, n), F32)], operands=(a, b), name=name, comm=comm)
    return out, couts


def _dh(dpm, dpg, dpf, w_mix, w_merge, w_f, name, comm=None):
    s = dpm.shape[0]
    tm = 1024 if s % 1024 == 0 else 512
    tk1, tk2 = 896, 1024
    n1, n2 = N_MIX // tk1, N_MERGE // tk2

    def body(dpm_ref, dpg_ref, dpf_ref, wm_ref, wg_ref, wf_ref, o_ref):
        j = pl.program_id(1)

        @pl.when(j == 0)
        def _():
            o_ref[...] = _dot_nt(dpf_ref[...], wf_ref[...])

        @pl.when(j < n1)
        def _():
            o_ref[...] += _dot_nt(dpm_ref[...], wm_ref[...])

        @pl.when(j >= n1)
        def _():
            o_ref[...] += _dot_nt(dpg_ref[...], wg_ref[...])

    mix_j = lambda j: jnp.minimum(j, n1 - 1)
    merge_j = lambda j: jnp.maximum(j - n1, 0)
    (out,), couts = _pcall(
        body, grid=(s // tm, n1 + n2),
        in_specs=[pl.BlockSpec((tm, tk1), lambda i, j: (i, mix_j(j))), pl.BlockSpec((tm, tk2), lambda i, j: (i, merge_j(j))),
                  pl.BlockSpec((tm, N_F), lambda i, j: (i, 0)),
                  pl.BlockSpec((D_MODEL, tk1), lambda i, j: (0, mix_j(j))), pl.BlockSpec((D_MODEL, tk2), lambda i, j: (0, merge_j(j))),
                  pl.BlockSpec((D_MODEL, N_F), lambda i, j: (0, 0))],
        out_specs=[pl.BlockSpec((tm, D_MODEL), lambda i, j: (i, 0))],
        out_shape=[jax.ShapeDtypeStruct((s, D_MODEL), F32)], operands=(dpm, dpg, dpf, w_mix, w_merge, w_f), name=name, comm=comm)
    return out, couts


def _mm_tn(a, d, tn, name, tm=1024):
    m, k = a.shape
    n = d.shape[1]
    tm = min(tm, m)
    nm = m // tm

    def body(a_ref, d_ref, o_ref, acc):
        i = pl.program_id(1)

        @pl.when(i == 0)
        def _():
            acc[...] = jnp.zeros_like(acc)

        acc[...] += _dot_tn(a_ref[...], d_ref[...])

        @pl.when(i == nm - 1)
        def _():
            o_ref[...] = acc[...].astype(BF16)

    return pl.pallas_call(
        body, grid=(n // tn, nm),
        in_specs=[pl.BlockSpec((tm, k), lambda j, i: (i, 0)), pl.BlockSpec((tm, tn), lambda j, i: (i, j))],
        out_specs=pl.BlockSpec((k, tn), lambda j, i: (0, j)),
        out_shape=jax.ShapeDtypeStruct((k, n), BF16), scratch_shapes=[pltpu.VMEM((k, tn), F32)],
        compiler_params=_cp(("parallel", "arbitrary")), name=name)(a, d)


def _lane_head():
    return lax.broadcasted_iota(jnp.int32, (1, BR), 1) // HEAD_DIM


def _gmlp_chunk_fwd(p_ref, r0, gw_ref, gb_ref, lg, lb):
    uv = p_ref[r0:r0 + CHUNK, O_UV:O_UV + 2 * BR]
    u = _gelu(uv[:, :BR])
    vhat, rs = _ln_hat(_gelu(uv[:, BR:]))
    vn = (vhat * lg + lb).astype(BF16)
    head = _lane_head()
    mixed = gb_ref[...]
    for h in range(N_HEADS):
        mixed = mixed + jnp.where(head == h, _dot(gw_ref[h], vn), 0.0)
    return uv, u, vhat, rs, vn, mixed


def _tap_groups(k_width):
    groups = []
    for b in range(8):
        taps = [(d // 8, k_width - 1 - d) for d in range(b, k_width, 8)]
        if taps:
            groups.append((b, taps))
    return groups


def _causal_taps(buf, off, r0, nr, k_width):
    lead = 8 * ((k_width - 1) // 8 + 1)
    win = buf[off + r0 - lead:off + r0 + nr, :]
    for b, taps in _tap_groups(k_width):
        shifted = win if b == 0 else pltpu.roll(win, b, 0)
        for a, k in taps:
            yield k, shifted[lead - 8 * a:lead - 8 * a + nr]


def _anticausal_taps(buf, r0, nr, k_width):
    lead = 8 * ((k_width - 1) // 8 + 1)
    win = buf[r0:r0 + nr + lead, :]
    for b, taps in _tap_groups(k_width):
        shifted = win if b == 0 else pltpu.roll(win, nr + lead - b, 0)
        for a, k in taps:
            yield k, shifted[8 * a:8 * a + nr]


def _conv_sub_blocks(rows):
    out = [(r, SUB) for r in range(0, rows - rows % SUB, SUB)]
    if rows % SUB:
        out.append((rows - rows % SUB, rows % SUB))
    return out


def _mix_fwd(pm, wl, name, comm=None):
    s = pm.shape[0]
    nt = s // TM

    def body(p_ref, ph_ref, gw_ref, gb_ref, lg_ref, lb_ref, scw_ref, dww_ref, dwb_ref, clg_ref, clb_ref,
             ya_ref, yc_ref, yd_ref, q_ref, k_ref, v_ref, cc_ref, zbuf, hbuf):
        i = pl.program_id(0)
        for h in range(N_HEADS):
            c0 = O_QKV + h * HEAD_DIM
            q_ref[h] = (p_ref[:, c0:c0 + HEAD_DIM] * SCALE).astype(BF16)
            k_ref[h] = p_ref[:, c0 + BR:c0 + BR + HEAD_DIM].astype(BF16)
            v_ref[h] = p_ref[:, c0 + 2 * BR:c0 + 2 * BR + HEAD_DIM].astype(BF16)
        lg = lg_ref[...]
        lb = lb_ref[...]
        for c in range(TM // CHUNK):
            r0 = c * CHUNK
            _, u, _, _, _, mixed = _gmlp_chunk_fwd(p_ref, r0, gw_ref, gb_ref, lg, lb)
            ag = p_ref[r0:r0 + CHUNK, O_AG:O_AG + BR]
            ya_ref[r0:r0 + CHUNK, :] = (u * mixed * _silu(ag)).astype(BF16)

        first = i > 0
        zbuf[0:HALO, :] = jnp.where(first, ph_ref[:, O_CIN + BR:O_CIN + 2 * BR] * ph_ref[:, O_CIN + 2 * BR:O_CIN + 3 * BR], 0.0)
        zbuf[HALO:HALO + TM, :] = p_ref[:, O_CIN + BR:O_CIN + 2 * BR] * p_ref[:, O_CIN + 2 * BR:O_CIN + 3 * BR]
        hbuf[0:HALO, :] = jnp.where(first, ph_ref[:, O_GLU:O_GLU + BR] * _sig(ph_ref[:, O_GLU + BR:O_GLU + 2 * BR]), 0.0)
        hbuf[HALO:HALO + TM, :] = p_ref[:, O_GLU:O_GLU + BR] * _sig(p_ref[:, O_GLU + BR:O_GLU + 2 * BR])

        clg = clg_ref[...]
        clb = clb_ref[...]
        for r0, nr in _conv_sub_blocks(TM):
            yc = jnp.zeros((nr, BR), F32)
            for k, zk in _causal_taps(zbuf, HALO, r0, nr, SHORT_CONV):
                yc = yc + scw_ref[k:k + 1, :] * zk
            bgate = p_ref[r0:r0 + nr, O_CIN:O_CIN + BR]
            cg = p_ref[r0:r0 + nr, O_CG:O_CG + BR]
            yc_ref[r0:r0 + nr, :] = (bgate * yc * _silu(cg)).astype(BF16)

            cc = jnp.zeros((nr, BR), F32) + dwb_ref[...]
            for k, hk in _causal_taps(hbuf, HALO, r0, nr, CONF_CONV):
                cc = cc + dww_ref[k:k + 1, :] * hk
            cc_ref[r0:r0 + nr, :] = cc
            chat, _ = _ln_hat(cc)
            dg = p_ref[r0:r0 + nr, O_DG:O_DG + BR]
            yd_ref[r0:r0 + nr, :] = (_silu(chat * clg + clb) * _silu(dg)).astype(BF16)

    full = lambda shape: pl.BlockSpec(shape, lambda i: tuple(0 for _ in shape))
    ytile = pl.BlockSpec((TM, BR), lambda i: (i, 0))
    yshape = jax.ShapeDtypeStruct((s, BR), BF16)
    htile = pl.BlockSpec((N_HEADS, TM, HEAD_DIM), lambda i: (0, i, 0))
    hshape = jax.ShapeDtypeStruct((N_HEADS, s, HEAD_DIM), BF16)
    return _pcall(
        body, grid=(nt,),
        in_specs=[pl.BlockSpec((TM, N_MIX), lambda i: (i, 0)),
                  pl.BlockSpec((HALO, N_MIX), lambda i: (jnp.maximum(i * (TM // HALO) - 1, 0), 0)),
                  full((N_HEADS, CHUNK, CHUNK)), full((CHUNK, BR)), full((1, BR)), full((1, BR)),
                  full((8, BR)), full((32, BR)), full((1, BR)), full((1, BR)), full((1, BR))],
        out_specs=[ytile, ytile, ytile, htile, htile, htile, ytile],
        out_shape=[yshape, yshape, yshape, hshape, hshape, hshape, jax.ShapeDtypeStruct((s, BR), F32)],
        scratch_shapes=[pltpu.VMEM((HALO + TM, BR), F32), pltpu.VMEM((HALO + TM, BR), F32)],
        name=name, comm=comm, operands=(
            pm, pm, wl["gw"], wl["gb"], wl["sgu_ln_g"], wl["sgu_ln_b"], wl["scw"], wl["dww"], wl["conf_dw_b"],
            wl["conf_ln_g"], wl["conf_ln_b"]))


def _mix_bwd(pm, cc, dya, dyc, dyd, dqkv, dbg, wl, name):
    s = pm.shape[0]
    nt = s // TM
    ext = TM + HALO

    def body(p_ref, ph_ref, pn_ref, cc_ref, ccn_ref, dya_ref, dyc_ref, dycn_ref, dyd_ref, dydn_ref, dq_ref, dk_ref, dv_ref, dbg_ref,
             gw_ref, gwt_ref, gb_ref, lg_ref, lb_ref, scw_ref, dww_ref, dwb_ref, clg_ref, clb_ref,
             dp_ref, dgw_ref, dgb_ref, vec_ref, dscw_ref, ddww_ref, zbuf, dcb, hbuf, dcc):
        i = pl.program_id(0)

        @pl.when(i == 0)
        def _():
            dgw_ref[...] = jnp.zeros_like(dgw_ref)
            dgb_ref[...] = jnp.zeros_like(dgb_ref)
            vec_ref[...] = jnp.zeros_like(vec_ref)
            dscw_ref[...] = jnp.zeros_like(dscw_ref)
            ddww_ref[...] = jnp.zeros_like(ddww_ref)

        lg = lg_ref[...]
        lb = lb_ref[...]
        head = _lane_head()
        d_lg = jnp.zeros((1, BR), F32)
        d_lb = jnp.zeros((1, BR), F32)
        for c in range(TM // CHUNK):
            r0 = c * CHUNK
            uv, u, vhat, rs, vn, mixed = _gmlp_chunk_fwd(p_ref, r0, gw_ref, gb_ref, lg, lb)
            ag = p_ref[r0:r0 + CHUNK, O_AG:O_AG + BR]
            dy = dya_ref[r0:r0 + CHUNK, :]
            sa = _silu(ag)
            du = dy * mixed * sa
            dmx = dy * u * sa
            dp_ref[r0:r0 + CHUNK, O_AG:O_AG + BR] = (dy * u * mixed * _dsilu(ag)).astype(BF16)
            dgb_ref[...] += dmx
            dmx_b = dmx.astype(BF16)
            dvn = jnp.zeros((CHUNK, BR), F32)
            for h in range(N_HEADS):
                sel = head == h
                dgw_ref[h] += _dot_nt(jnp.where(sel, dmx, 0.0).astype(BF16), vn)
                dvn = dvn + jnp.where(sel, _dot(gwt_ref[h], dmx_b), 0.0)
            d_lg = d_lg + jnp.sum(dvn * vhat, axis=0, keepdims=True)
            d_lb = d_lb + jnp.sum(dvn, axis=0, keepdims=True)
            dv0 = _ln_bwd(dvn * lg, vhat, rs)
            dp_ref[r0:r0 + CHUNK, O_UV:O_UV + BR] = (du * _dgelu(uv[:, :BR])).astype(BF16)
            dp_ref[r0:r0 + CHUNK, O_UV + BR:O_UV + 2 * BR] = (dv0 * _dgelu(uv[:, BR:])).astype(BF16)
        vec_ref[0:1, :] += d_lg
        vec_ref[1:2, :] += d_lb

        for j, g_ref in enumerate((dq_ref, dk_ref, dv_ref)):
            dp_ref[:, O_QKV + j * BR:O_QKV + (j + 1) * BR] = jnp.concatenate(
                [g_ref[h] for h in range(N_HEADS)], axis=1).astype(BF16)
        dp_ref[:, O_BG:O_BG + BR] = dbg_ref[...].astype(BF16)

        first = i > 0
        last = i < nt - 1
        zbuf[0:HALO, :] = jnp.where(first, ph_ref[:, O_CIN + BR:O_CIN + 2 * BR] * ph_ref[:, O_CIN + 2 * BR:O_CIN + 3 * BR], 0.0)
        zbuf[HALO:HALO + TM, :] = p_ref[:, O_CIN + BR:O_CIN + 2 * BR] * p_ref[:, O_CIN + 2 * BR:O_CIN + 3 * BR]
        dcb[0:TM, :] = dyc_ref[...] * p_ref[:, O_CIN:O_CIN + BR] * _silu(p_ref[:, O_CG:O_CG + BR])
        dcb[TM:ext, :] = jnp.where(last, dycn_ref[...] * pn_ref[:, O_CIN:O_CIN + BR] * _silu(pn_ref[:, O_CG:O_CG + BR]), 0.0)
        for r0, nr in _conv_sub_blocks(TM):
            yc = jnp.zeros((nr, BR), F32)
            dz = jnp.zeros((nr, BR), F32)
            dcur = dcb[r0:r0 + nr, :]
            for k, zk in _causal_taps(zbuf, HALO, r0, nr, SHORT_CONV):
                yc = yc + scw_ref[k:k + 1, :] * zk
                dscw_ref[8 * k:8 * k + 8, :] += _fold8(dcur * zk)
            for k, dk in _anticausal_taps(dcb, r0, nr, SHORT_CONV):
                dz = dz + scw_ref[k:k + 1, :] * dk
            dy = dyc_ref[r0:r0 + nr, :]
            bgate = p_ref[r0:r0 + nr, O_CIN:O_CIN + BR]
            cg = p_ref[r0:r0 + nr, O_CG:O_CG + BR]
            dp_ref[r0:r0 + nr, O_CIN:O_CIN + BR] = (dy * yc * _silu(cg)).astype(BF16)
            dp_ref[r0:r0 + nr, O_CIN + BR:O_CIN + 2 * BR] = (dz * p_ref[r0:r0 + nr, O_CIN + 2 * BR:O_CIN + 3 * BR]).astype(BF16)
            dp_ref[r0:r0 + nr, O_CIN + 2 * BR:O_CIN + 3 * BR] = (dz * p_ref[r0:r0 + nr, O_CIN + BR:O_CIN + 2 * BR]).astype(BF16)
            dp_ref[r0:r0 + nr, O_CG:O_CG + BR] = (dy * bgate * yc * _dsilu(cg)).astype(BF16)

        hbuf[0:HALO, :] = jnp.where(first, ph_ref[:, O_GLU:O_GLU + BR] * _sig(ph_ref[:, O_GLU + BR:O_GLU + 2 * BR]), 0.0)
        hbuf[HALO:HALO + TM, :] = p_ref[:, O_GLU:O_GLU + BR] * _sig(p_ref[:, O_GLU + BR:O_GLU + 2 * BR])
        clg = clg_ref[...]
        clb = clb_ref[...]
        d_clg = jnp.zeros((1, BR), F32)
        d_clb = jnp.zeros((1, BR), F32)
        d_dwb = jnp.zeros((1, BR), F32)
        for r0, nr in _conv_sub_blocks(ext):
            in_tile = r0 < TM
            chat, rs = _ln_hat(cc_ref[r0:r0 + nr, :] if in_tile else ccn_ref[...])
            ln = chat * clg + clb
            if in_tile:
                dy = dyd_ref[r0:r0 + nr, :]
                dg = p_ref[r0:r0 + nr, O_DG:O_DG + BR]
            else:
                dy = jnp.where(last, dydn_ref[...], 0.0)
                dg = pn_ref[:, O_DG:O_DG + BR]
            dln = dy * _silu(dg) * _dsilu(ln)
            dc = _ln_bwd(dln * clg, chat, rs)
            dcc[r0:r0 + nr, :] = dc
            if in_tile:
                dp_ref[r0:r0 + nr, O_DG:O_DG + BR] = (dy * _silu(ln) * _dsilu(dg)).astype(BF16)
                d_clg = d_clg + jnp.sum(dln * chat, axis=0, keepdims=True)
                d_clb = d_clb + jnp.sum(dln, axis=0, keepdims=True)
                d_dwb = d_dwb + jnp.sum(dc, axis=0, keepdims=True)
        vec_ref[2:3, :] += d_dwb
        vec_ref[3:4, :] += d_clg
        vec_ref[4:5, :] += d_clb
        for r0, nr in _conv_sub_blocks(TM):
            dcur = dcc[r0:r0 + nr, :]
            dhh = jnp.zeros((nr, BR), F32)
            for k, hk in _causal_taps(hbuf, HALO, r0, nr, CONF_CONV):
                ddww_ref[8 * k:8 * k + 8, :] += _fold8(dcur * hk)
            for k, dk in _anticausal_taps(dcc, r0, nr, CONF_CONV):
                dhh = dhh + dww_ref[k:k + 1, :] * dk
            a = p_ref[r0:r0 + nr, O_GLU:O_GLU + BR]
            sg = _sig(p_ref[r0:r0 + nr, O_GLU + BR:O_GLU + 2 * BR])
            dp_ref[r0:r0 + nr, O_GLU:O_GLU + BR] = (dhh * sg).astype(BF16)
            dp_ref[r0:r0 + nr, O_GLU + BR:O_GLU + 2 * BR] = (dhh * a * sg * (1.0 - sg)).astype(BF16)

    full = lambda shape: pl.BlockSpec(shape, lambda i: tuple(0 for _ in shape))
    rpt = TM // HALO
    prev_map = lambda i: (jnp.maximum(i * rpt - 1, 0), 0)
    next_map = lambda i: (jnp.minimum((i + 1) * rpt, nt * rpt - 1), 0)
    ytile = pl.BlockSpec((TM, BR), lambda i: (i, 0))
    htile = pl.BlockSpec((N_HEADS, TM, HEAD_DIM), lambda i: (0, i, 0))
    return pl.pallas_call(
        body, grid=(nt,),
        in_specs=[pl.BlockSpec((TM, N_MIX), lambda i: (i, 0)), pl.BlockSpec((HALO, N_MIX), prev_map),
                  pl.BlockSpec((HALO, N_MIX), next_map),
                  ytile, pl.BlockSpec((HALO, BR), next_map),
                  ytile, ytile, pl.BlockSpec((HALO, BR), next_map), ytile, pl.BlockSpec((HALO, BR), next_map),
                  htile, htile, htile, ytile,
                  full((N_HEADS, CHUNK, CHUNK)), full((N_HEADS, CHUNK, CHUNK)), full((CHUNK, BR)), full((1, BR)), full((1, BR)),
                  full((8, BR)), full((32, BR)), full((1, BR)), full((1, BR)), full((1, BR))],
        out_specs=[pl.BlockSpec((TM, N_MIX), lambda i: (i, 0)), full((N_HEADS, CHUNK, CHUNK)), full((CHUNK, BR)),
                   full((16, BR)), full((64, BR)), full((256, BR))],
        out_shape=[jax.ShapeDtypeStruct((s, N_MIX), BF16), jax.ShapeDtypeStruct((N_HEADS, CHUNK, CHUNK), F32),
                   jax.ShapeDtypeStruct((CHUNK, BR), F32), jax.ShapeDtypeStruct((16, BR), F32),
                   jax.ShapeDtypeStruct((64, BR), F32), jax.ShapeDtypeStruct((256, BR), F32)],
        scratch_shapes=[pltpu.VMEM((HALO + TM, BR), F32), pltpu.VMEM((ext, BR), F32),
                        pltpu.VMEM((HALO + TM, BR), F32), pltpu.VMEM((ext, BR), F32)],
        compiler_params=_cp(("arbitrary",)), name=name)(
            pm, pm, pm, cc, cc, dya, dyc, dyc, dyd, dyd, *dqkv, dbg,
            wl["gw"], wl["gwt"], wl["gb"], wl["sgu_ln_g"], wl["sgu_ln_b"], wl["scw"], wl["dww"], wl["conf_dw_b"],
            wl["conf_ln_g"], wl["conf_ln_b"])


def _tri(lower):
    r = lax.broadcasted_iota(jnp.int32, (CUMB, CUMB), 0)
    c = lax.broadcasted_iota(jnp.int32, (CUMB, CUMB), 1)
    return jnp.where((r >= c) if lower else (r <= c), 1.0, 0.0).astype(F32)


def _dot_hi(a, b):
    return jnp.dot(a, b, preferred_element_type=F32, precision=lax.Precision.HIGHEST)


def _cum_fwd(pf, fb, name):
    s = pf.shape[0]

    def body(pf_ref, fb_ref, cum_ref, carry):
        i = pl.program_id(0)

        @pl.when(i == 0)
        def _():
            carry[...] = jnp.zeros_like(carry)

        z = pf_ref[...] + fb_ref[...]
        logf = jnp.minimum(z, 0.0) - jnp.log(1.0 + jnp.exp(-jnp.abs(z)))
        cum_ref[...] = _dot_hi(_tri(True), logf) + carry[...]
        carry[...] += jnp.sum(logf, axis=0, keepdims=True)

    return pl.pallas_call(
        body, grid=(s // CUMB,),
        in_specs=[pl.BlockSpec((CUMB, N_F), lambda i: (i, 0)), pl.BlockSpec((1, N_F), lambda i: (0, 0))],
        out_specs=pl.BlockSpec((CUMB, N_F), lambda i: (i, 0)),
        out_shape=jax.ShapeDtypeStruct((s, N_F), F32),
        scratch_shapes=[pltpu.VMEM((1, N_F), F32)],
        compiler_params=_cp(("arbitrary",)), name=name)(pf, fb)


def _cum_bwd(dcq, dck, pf, fb, name):
    s = pf.shape[0]
    nb = s // CUMB

    def body(dcq_ref, dck_ref, pf_ref, fb_ref, dpf_ref, dfb_ref, carry):
        i = pl.program_id(0)

        @pl.when(i == 0)
        def _():
            carry[...] = jnp.zeros_like(carry)
            dfb_ref[...] = jnp.zeros_like(dfb_ref)

        lane = lax.broadcasted_iota(jnp.int32, (1, N_F), 1)
        dc = dck_ref[...]
        for h in range(N_HEADS):
            dc = dc + jnp.where(lane == h, dcq_ref[h], 0.0)
        dlogf = _dot_hi(_tri(False), dc) + carry[...]
        carry[...] += jnp.sum(dc, axis=0, keepdims=True)
        z = pf_ref[...] + fb_ref[...]
        dz = dlogf * (1.0 - _sig(z))
        dpf_ref[...] = dz.astype(BF16)
        dfb_ref[...] += _fold8(dz)

    rev = lambda i: (nb - 1 - i, 0)
    return pl.pallas_call(
        body, grid=(nb,),
        in_specs=[pl.BlockSpec((N_HEADS, CUMB, 1), lambda i: (0, nb - 1 - i, 0)), pl.BlockSpec((CUMB, N_F), rev),
                  pl.BlockSpec((CUMB, N_F), rev), pl.BlockSpec((1, N_F), lambda i: (0, 0))],
        out_specs=[pl.BlockSpec((CUMB, N_F), rev), pl.BlockSpec((8, N_F), lambda i: (0, 0))],
        out_shape=[jax.ShapeDtypeStruct((s, N_F), BF16), jax.ShapeDtypeStruct((8, N_F), F32)],
        scratch_shapes=[pltpu.VMEM((1, N_F), F32)],
        compiler_params=_cp(("arbitrary",)), name=name)(dcq, dck, pf, fb)


def _causal_mask(nr, nc, r0, c0):
    r = lax.broadcasted_iota(jnp.int32, (nr, nc), 0) + r0
    c = lax.broadcasted_iota(jnp.int32, (nr, nc), 1) + c0
    return r >= c


def _attn_fwd(q, k, v, cq, ck, name, comm=None):
    s = q.shape[1]
    nb = s // FQ

    def body(q_ref, k_ref, v_ref, cq_ref, ck_ref, o_ref, lse_ref):
        for qi in range(nb):
            qs = qi * FQ
            qb = q_ref[0, qs:qs + FQ, :]
            cqb = cq_ref[0, qs:qs + FQ, :]

            def block(kj, carry, masked):
                m, l, acc = carry
                ks = pl.multiple_of(kj * FQ, FQ)
                kb = k_ref[0, pl.ds(ks, FQ), :]
                vb = v_ref[0, pl.ds(ks, FQ), :]
                sc = _dot_nt(qb, kb) + (cqb - ck_ref[0, kj])
                if masked:
                    sc = jnp.where(_causal_mask(FQ, FQ, 0, 0), sc, NEG)
                m_new = jnp.maximum(m, jnp.max(sc, axis=-1, keepdims=True))
                alpha = jnp.exp(m - m_new)
                p = jnp.exp(sc - m_new)
                l = alpha * l + jnp.sum(p, axis=-1, keepdims=True)
                acc = alpha * acc + _dot(p.astype(BF16), vb)
                return m_new, l, acc

            carry = (jnp.full((FQ, 1), NEG, F32), jnp.zeros((FQ, 1), F32), jnp.zeros((FQ, HEAD_DIM), F32))
            if qi > 0:
                carry = lax.fori_loop(0, qi, lambda kj, cr: block(kj, cr, False), carry)
            m, l, acc = block(qi, carry, True)
            o_ref[0, qs:qs + FQ, :] = acc / l
            lse_ref[0, qs:qs + FQ, :] = m + jnp.log(l)

    hblk = pl.BlockSpec((1, s, HEAD_DIM), lambda h: (h, 0, 0))
    cblk = pl.BlockSpec((1, s, 1), lambda h: (h, 0, 0))
    return _pcall(
        body, grid=(N_HEADS,),
        in_specs=[hblk, hblk, hblk, cblk, pl.BlockSpec((1, nb, 1, FQ), lambda h: (h, 0, 0, 0))],
        out_specs=[hblk, cblk],
        out_shape=[jax.ShapeDtypeStruct((N_HEADS, s, HEAD_DIM), F32), jax.ShapeDtypeStruct((N_HEADS, s, 1), F32)],
        operands=(q, k, v, cq, ck), name=name, comm=comm)


def _attn_bwd(q, k, v, cq, ck, o, lse, do, name, comm=None):
    s = q.shape[1]
    nq, nk = s // BQ, s // BK

    def body(q_ref, k_ref, v_ref, cq_ref, ck_ref, o_ref, lse_ref, do_ref, dq_ref, dk_ref, dv_ref, dcq_ref, dck_ref, delta):
        delta[...] = jnp.sum(do_ref[0] * o_ref[0], axis=-1, keepdims=True)
        dq_ref[...] = jnp.zeros_like(dq_ref)
        dcq_ref[...] = jnp.zeros_like(dcq_ref)
        for kj in range(nk):
            ks = kj * BK
            kb = k_ref[0, ks:ks + BK, :]
            vb = v_ref[0, ks:ks + BK, :]
            ckb = ck_ref[0, kj]

            def block(qs, nr, carry, masked):
                dk_acc, dv_acc, dck_acc = carry
                rows = pl.ds(qs, nr)
                qb = q_ref[0, rows, :]
                dob = do_ref[0, rows, :].astype(BF16)
                sc = _dot_nt(qb, kb) + (cq_ref[0, rows, :] - ckb)
                p = jnp.exp(sc - lse_ref[0, rows, :])
                if masked:
                    p = jnp.where(_causal_mask(nr, BK, 0, 0), p, 0.0)
                dp = _dot_nt(dob, vb)
                ds = p * (dp - delta[rows, :])
                ds_b = ds.astype(BF16)
                dv_acc = dv_acc + _dot_tn(p.astype(BF16), dob)
                dk_acc = dk_acc + _dot_tn(ds_b, qb)
                dq_ref[0, rows, :] += _dot(ds_b, kb) * SCALE
                dcq_ref[0, rows, :] += jnp.sum(ds, axis=-1, keepdims=True)
                dck_acc = dck_acc - jnp.sum(ds, axis=0, keepdims=True)
                return dk_acc, dv_acc, dck_acc

            carry = (jnp.zeros((BK, HEAD_DIM), F32), jnp.zeros((BK, HEAD_DIM), F32), jnp.zeros((1, BK), F32))
            carry = block(ks, BK, carry, True)
            below = ks + BK
            if below % BQ and below < s:
                carry = block(below, BK, carry, False)
                below += BK
            if below < s:
                carry = lax.fori_loop(below // BQ, nq, lambda qi, cr: block(pl.multiple_of(qi * BQ, BQ), BQ, cr, False), carry)
            dk_ref[0, ks:ks + BK, :] = carry[0]
            dv_ref[0, ks:ks + BK, :] = carry[1]
            dck_ref[0, kj] = carry[2]

    hblk = pl.BlockSpec((1, s, HEAD_DIM), lambda h: (h, 0, 0))
    cblk = pl.BlockSpec((1, s, 1), lambda h: (h, 0, 0))
    kblk = pl.BlockSpec((1, nk, 1, BK), lambda h: (h, 0, 0, 0))
    hshape = jax.ShapeDtypeStruct((N_HEADS, s, HEAD_DIM), F32)
    return _pcall(
        body, grid=(N_HEADS,),
        in_specs=[hblk, hblk, hblk, cblk, kblk, hblk, cblk, hblk],
        out_specs=[hblk, hblk, hblk, cblk, kblk],
        out_shape=[hshape, hshape, hshape, jax.ShapeDtypeStruct((N_HEADS, s, 1), F32),
                   jax.ShapeDtypeStruct((N_HEADS, nk, 1, BK), F32)],
        scratch_shapes=[pltpu.VMEM((s, 1), F32)],
        operands=(q, k, v, cq, ck, o, lse, do), name=name, comm=comm)


def _merge_fwd(x, ya, yc, yd, o, pm, pg, wb, wo, name, comm=None):
    s = x.shape[0]

    def body(x_ref, ya_ref, yc_ref, yd_ref, o_ref, bg_ref, pg_ref, wb_ref, wo_ref, xo_ref, yb_ref):
        o = jnp.concatenate([o_ref[h] for h in range(N_HEADS)], axis=1)
        yb = (o * _silu(bg_ref[...])).astype(BF16)
        yb_ref[...] = yb
        ys = (ya_ref[...], yb, yc_ref[...], yd_ref[...])
        merged = jnp.zeros((TMG, D_MODEL), F32)
        for n in range(N_BRANCH):
            merged = merged + _sig(pg_ref[:, n * D_MODEL:(n + 1) * D_MODEL]) * _dot(ys[n], wb_ref[n])
        xo_ref[...] = x_ref[...] + _dot(merged.astype(BF16), wo_ref[...])

    xt = pl.BlockSpec((TMG, D_MODEL), lambda i: (i, 0))
    yt = pl.BlockSpec((TMG, BR), lambda i: (i, 0))
    return _pcall(
        body, grid=(s // TMG,),
        in_specs=[xt, yt, yt, yt, pl.BlockSpec((N_HEADS, TMG, HEAD_DIM), lambda i: (0, i, 0)),
                  pl.BlockSpec((TMG, BR), lambda i: (i, O_BG // BR)), pl.BlockSpec((TMG, N_MERGE), lambda i: (i, 0)),
                  pl.BlockSpec((N_BRANCH, BR, D_MODEL), lambda i: (0, 0, 0)), pl.BlockSpec((D_MODEL, D_MODEL), lambda i: (0, 0))],
        out_specs=[xt, yt],
        out_shape=[jax.ShapeDtypeStruct((s, D_MODEL), F32), jax.ShapeDtypeStruct((s, BR), BF16)],
        operands=(x, ya, yc, yd, o, pm, pg, wb, wo), name=name, comm=comm)


def _merge_bwd(dx, ya, yb, yc, yd, o, pm, pg, wb, wo, name, comm=None):
    s = dx.shape[0]
    nt = s // TMG

    def body(dx_ref, ya_ref, yb_ref, yc_ref, yd_ref, o_ref, bg_ref, pg_ref, wb_ref, wo_ref,
             dpg_ref, dya_ref, do_ref, dbg_ref, dyc_ref, dyd_ref, dwb_ref, dwo_ref, dwb_acc, dwo_acc):
        i = pl.program_id(0)

        @pl.when(i == 0)
        def _():
            dwb_acc[...] = jnp.zeros_like(dwb_acc)
            dwo_acc[...] = jnp.zeros_like(dwo_acc)

        dxb = dx_ref[...].astype(BF16)
        dmerged = _dot_nt(dxb, wo_ref[...])
        ys = (ya_ref[...], yb_ref[...], yc_ref[...], yd_ref[...])
        dys = (dya_ref, None, dyc_ref, dyd_ref)
        merged = jnp.zeros((TMG, D_MODEL), F32)
        for n in range(N_BRANCH):
            gate = _sig(pg_ref[:, n * D_MODEL:(n + 1) * D_MODEL])
            pr = _dot(ys[n], wb_ref[n])
            merged = merged + gate * pr
            dpg_ref[:, n * D_MODEL:(n + 1) * D_MODEL] = (dmerged * pr * gate * (1.0 - gate)).astype(BF16)
            dpr = (gate * dmerged).astype(BF16)
            dwb_acc[n] += _dot_tn(ys[n], dpr)
            dyn = _dot_nt(dpr, wb_ref[n])
            if n == 1:
                bg = bg_ref[...]
                do = dyn * _silu(bg)
                for h in range(N_HEADS):
                    do_ref[h] = do[:, h * HEAD_DIM:(h + 1) * HEAD_DIM]
                dbg_ref[...] = dyn * jnp.concatenate([o_ref[h] for h in range(N_HEADS)], axis=1) * _dsilu(bg)
            else:
                dys[n][...] = dyn
        dwo_acc[...] += _dot_tn(merged.astype(BF16), dxb)

        @pl.when(i == nt - 1)
        def _():
            dwb_ref[...] = dwb_acc[...].astype(BF16)
            dwo_ref[...] = dwo_acc[...].astype(BF16)

    xt = pl.BlockSpec((TMG, D_MODEL), lambda i: (i, 0))
    yt = pl.BlockSpec((TMG, BR), lambda i: (i, 0))
    gt = pl.BlockSpec((TMG, N_MERGE), lambda i: (i, 0))
    wbs = pl.BlockSpec((N_BRANCH, BR, D_MODEL), lambda i: (0, 0, 0))
    wos = pl.BlockSpec((D_MODEL, D_MODEL), lambda i: (0, 0))
    yf = jax.ShapeDtypeStruct((s, BR), F32)
    ht = pl.BlockSpec((N_HEADS, TMG, HEAD_DIM), lambda i: (0, i, 0))
    return _pcall(
        body, grid=(nt,),
        in_specs=[xt, yt, yt, yt, yt, ht, pl.BlockSpec((TMG, BR), lambda i: (i, O_BG // BR)), gt, wbs, wos],
        out_specs=[gt, yt, ht, yt, yt, yt, wbs, wos],
        out_shape=[jax.ShapeDtypeStruct((s, N_MERGE), BF16), yf, jax.ShapeDtypeStruct((N_HEADS, s, HEAD_DIM), F32), yf, yf, yf,
                   jax.ShapeDtypeStruct((N_BRANCH, BR, D_MODEL), BF16), jax.ShapeDtypeStruct((D_MODEL, D_MODEL), BF16)],
        scratch_shapes=[pltpu.VMEM((N_BRANCH, BR, D_MODEL), F32), pltpu.VMEM((D_MODEL, D_MODEL), F32)],
        operands=(dx, ya, yb, yc, yd, o, pm, pg, wb, wo), name=name, comm=comm)


def _layer_fwd(x, wl, tag, attach=None):
    attach = attach or {}

    def riding(stage):
        comm, sink = attach.get(stage, (None, None))
        return comm, (sink or (lambda res: None))

    s = x.shape[0]
    comm, sink = riding("rms_fwd")
    h, res = _rms_fwd(x, wl["norm_g"], "rms_fwd" + tag, comm)
    sink(res)
    comm, sink = riding("proj_mix")
    pm, res = _mm_nn(h, wl["w_mix"], 1792, "proj_mix" + tag, comm)
    sink(res)
    comm, sink = riding("mix_fwd")
    (ya, yc, yd, q, k, v, cc), res = _mix_fwd(pm, wl, "mix_fwd" + tag, comm)
    sink(res)
    comm, sink = riding("proj_merge")
    pg, res = _mm_nn(h, wl["w_merge"], 2048, "proj_merge" + tag, comm)
    sink(res)
    pf, _ = _mm_nn(h, wl["w_f"], N_F, "proj_f" + tag, tm=2048)
    cum = _cum_fwd(pf, wl["f_bias"], "cum_fwd" + tag)
    cum_t = cum[:, :N_HEADS].T
    cq = cum_t.reshape(N_HEADS, s, 1)
    ck = cum_t.reshape(N_HEADS, s // BK, 1, BK)
    comm, sink = riding("attn_fwd")
    (o, lse), res = _attn_fwd(q, k, v, cq, cum_t.reshape(N_HEADS, s // FQ, 1, FQ), "attn_fwd" + tag, comm)
    sink(res)
    comm, sink = riding("merge_fwd")
    (x_next, yb), res = _merge_fwd(x, ya, yc, yd, o, pm, pg, wl["wb"], wl["wo"], "merge_fwd" + tag, comm)
    sink(res)
    saved = dict(x=x, h=h, pm=pm, pg=pg, pf=pf, cc=cc, ya=ya, yb=yb, yc=yc, yd=yd, q=q, k=k, v=v, cq=cq, ck=ck, o=o, lse=lse)
    return x_next, saved


def _blocks_rows(g):
    return g.reshape(N_DEV, g.shape[0] // N_DEV, g.shape[1])


def _blocks_cols(g):
    return g.reshape(N_BRANCH * BR, N_DEV, D_MODEL // N_DEV).transpose(1, 0, 2)


def _layer_bwd(dx_next, sv, wl, tag, dist, riding, extra_small):
    s = dx_next.shape[0]
    (dpg, dya, do, dbg, dyc, dyd, dwb, dwo), rode = _merge_bwd(
        dx_next, sv["ya"], sv["yb"], sv["yc"], sv["yd"], sv["o"], sv["pm"], sv["pg"], wl["wb"], wl["wo"], "merge_bwd" + tag,
        riding)
    dw_merge = _mm_tn(sv["h"], dpg, 2048, "dw_merge" + tag)
    early = [(_blocks_rows(dw_merge), False), (_blocks_cols(dwb), False), (_blocks_rows(dwo), False)] if dist else None
    (dq, dk, dv, dcq, dck), early_out = _attn_bwd(sv["q"], sv["k"], sv["v"], sv["cq"], sv["ck"], sv["o"], sv["lse"], do,
                                                  "attn_bwd" + tag, early)
    dck_cols = jnp.pad(dck.reshape(N_HEADS, s).T, ((0, 0), (0, N_F - N_HEADS)))
    dpf, dfb = _cum_bwd(dcq, dck_cols, sv["pf"], wl["f_bias"], "cum_bwd" + tag)
    dpm, dgw, dgb, vec, dscw, ddww = _mix_bwd(sv["pm"], sv["cc"], dya, dyc, dyd, (dq, dk, dv), dbg, wl, "mix_bwd" + tag)
    dw_mix = _mm_tn(sv["h"], dpm, 1792, "dw_mix" + tag)
    dw_f = _mm_tn(sv["h"], dpf, N_F, "dw_f" + tag, tm=2048)
    causal = jnp.tril(jnp.ones((CHUNK, CHUNK), bool))
    small = dict(
        f_bias=dfb.sum(0)[:N_HEADS],
        sgu_w=jnp.where(causal[None], dgw, 0.0),
        sgu_b=dgb.reshape(CHUNK, N_HEADS, HEAD_DIM).sum(-1).T,
        sgu_ln_g=vec[0], sgu_ln_b=vec[1], conf_dw_b=vec[2], conf_ln_g=vec[3], conf_ln_b=vec[4],
        short_conv_w=dscw.reshape(8, 8, BR).sum(1)[:SHORT_CONV],
        conf_dw_w=ddww.reshape(32, 8, BR).sum(1)[:CONF_CONV],
    )
    slab, spans = _pack([small[nm] for nm in SMALL[1:]])
    late = [(_blocks_rows(dw_mix), False), (_blocks_rows(dw_f), False), (slab, True)] if dist else None
    dh, late_out = _dh(dpm, dpg, dpf, wl["w_mix"], wl["w_merge"], wl["w_f"], "dh" + tag, late)
    dx, dng = _rms_bwd(dh, sv["x"], wl["norm_g"], dx_next, "rms_bwd" + tag)
    small["norm_g"] = dng.sum(0)
    grads = dict(small, w_mix=dw_mix, w_merge=dw_merge, w_f=dw_f, wb=dwb, wo=dwo)
    last_slab, last_spans = _pack([small["norm_g"]] + list(extra_small))
    return dx, grads, early_out, (late_out, spans), ([(last_slab, True)], last_spans), rode


def _prep_layer_small(norm_g, f_bias, sgu_w, sgu_b, sgu_ln_g, sgu_ln_b, conf_dw_b, conf_ln_g, conf_ln_b):
    causal = jnp.tril(jnp.ones((CHUNK, CHUNK), bool))
    gw = jnp.where(causal[None], sgu_w, 0.0)
    row = lambda a: a.reshape(1, -1)
    return dict(
        norm_g=row(norm_g),
        f_bias=jnp.pad(row(f_bias), ((0, 0), (0, N_F - N_HEADS))),
        gw=gw.astype(BF16), gwt=gw.transpose(0, 2, 1).astype(BF16),
        gb=jnp.repeat(sgu_b.T, HEAD_DIM, axis=1),
        sgu_ln_g=row(sgu_ln_g), sgu_ln_b=row(sgu_ln_b),
        conf_dw_b=row(conf_dw_b), conf_ln_g=row(conf_ln_g), conf_ln_b=row(conf_ln_b))


def _local_step(x, target, layers, final_g):
    saved = []
    for l in range(DEPTH):
        x, sv = _layer_fwd(x, layers[l], str(l))
        saved.append(sv)
    loss_p, dx, dfg = _loss_head(x, final_g.reshape(1, D_MODEL), target)
    grads = [None] * DEPTH
    for l in reversed(range(DEPTH)):
        dx, grads[l], _, _, _, _ = _layer_bwd(dx, saved[l], layers[l], str(l), False, None, [])
    return 0.5 / D_MODEL * jnp.sum(loss_p), dx, grads, dfg.sum(0)


def _sum8(a, name):
    _, r, c = a.shape
    tr = r
    while tr * c * a.dtype.itemsize * N_DEV > 4 * 1024 * 1024 and tr % 32 == 0:
        tr //= 2

    def body(a_ref, o_ref):
        acc = a_ref[0].astype(F32)
        for d in range(1, N_DEV):
            acc = acc + a_ref[d].astype(F32)
        o_ref[...] = acc

    return pl.pallas_call(
        body, grid=(r // tr,),
        in_specs=[pl.BlockSpec((N_DEV, tr, c), lambda i: (0, i, 0))],
        out_specs=pl.BlockSpec((tr, c), lambda i: (i, 0)),
        out_shape=jax.ShapeDtypeStruct((r, c), F32), compiler_params=_cp(("parallel",)), name=name)(a)


def _adamw(w, g, m, v, name):
    l, r, c = w.shape
    tr = r
    while tr * c * 4 > 1024 * 1024 and tr % 16 == 0:
        tr //= 2
    c1 = 1.0 - ADAM_B1 ** ADAM_STEP
    c2 = 1.0 - ADAM_B2 ** ADAM_STEP

    def body(w_ref, g_ref, m_ref, v_ref, d_ref, mo_ref, vo_ref):
        gv = g_ref[...]
        mn = ADAM_B1 * m_ref[...] + (1.0 - ADAM_B1) * gv
        vn = ADAM_B2 * v_ref[...] + (1.0 - ADAM_B2) * (gv * gv)
        mo_ref[...] = mn
        vo_ref[...] = vn
        d_ref[...] = -ADAM_LR * ((mn / c1) / (jnp.sqrt(vn / c2) + ADAM_EPS) + ADAM_WD * w_ref[...])

    blk = pl.BlockSpec((1, tr, c), lambda a, i: (a, i, 0))
    shp = jax.ShapeDtypeStruct((l, r, c), F32)
    return pl.pallas_call(
        body, grid=(l, r // tr), in_specs=[blk] * 4, out_specs=[blk] * 3, out_shape=[shp] * 3,
        compiler_params=_cp(("parallel", "parallel")), name=name)(w, g, m, v)


def _pack(parts):
    rows, spans, r = [], [], 0
    for p in parts:
        flat = p.reshape(-1)
        nr = -(-flat.shape[0] // 1024) * 8
        rows.append(jnp.pad(flat, (0, nr * 128 - flat.shape[0])).reshape(nr, 128))
        spans.append((r, nr, p.shape))
        r += nr
    return jnp.concatenate(rows, axis=0), spans


def _unpack(slab, spans):
    out = []
    for r, nr, shape in spans:
        size = math.prod(shape)
        out.append(slab[r:r + nr].reshape(-1)[:size].reshape(shape))
    return out


def _split_w_in(w):
    mix = jnp.concatenate([w[..., 0:1536], w[..., 1540:1796], w[..., 3332:3588], w[..., 1796:2820], w[..., 2820:3332]], axis=-1)
    return mix, w[..., 3588:7684], w[..., 1536:1540]


def _join_w_in(mix, merge, f):
    return jnp.concatenate([mix[..., 0:1536], f, mix[..., 1536:1792], mix[..., 2048:3072], mix[..., 3072:3584],
                            mix[..., 1792:2048], merge], axis=-1)


SMALL = ("norm_g", "f_bias", "sgu_w", "sgu_b", "sgu_ln_g", "sgu_ln_b", "short_conv_w", "conf_dw_w", "conf_dw_b",
         "conf_ln_g", "conf_ln_b")


def kernel(x, norm_g, w_in, f_bias, sgu_w, sgu_b, sgu_ln_g, sgu_ln_b, short_conv_w, conf_dw_w, conf_dw_b, conf_ln_g, conf_ln_b, w_branch, w_out, final_g, loss_target, m_norm_g, m_w_in, m_f_bias, m_sgu_w, m_sgu_b, m_sgu_ln_g, m_sgu_ln_b, m_short_conv_w, m_conf_dw_w, m_conf_dw_b, m_conf_ln_g, m_conf_ln_b, m_w_branch, m_w_out, m_final_g, v_norm_g, v_w_in, v_f_bias, v_sgu_w, v_sgu_b, v_sgu_ln_g, v_sgu_ln_b, v_short_conv_w, v_conf_dw_w, v_conf_dw_b, v_conf_ln_g, v_conf_ln_b, v_w_branch, v_w_out, v_final_g):
    me = 4 * lax.axis_index("x") + 2 * lax.axis_index("y") + lax.axis_index("c")
    rows = D_MODEL // N_DEV
    cshard = BR // N_DEV

    sh = []
    for l in range(DEPTH):
        mix, merge, f = _split_w_in(w_in[l])
        sh.append(dict(mix=mix.astype(BF16), merge=merge.astype(BF16),
                       f=jnp.pad(f, ((0, 0), (0, N_F - N_HEADS))).astype(BF16),
                       wb=w_branch[l].astype(BF16), wo=w_out[l].astype(BF16)))
    conv_slab, conv_spans = _pack([short_conv_w, conf_dw_w])
    layers = [_prep_layer_small(norm_g[l], f_bias[l], sgu_w[l], sgu_b[l], sgu_ln_g[l], sgu_ln_b[l],
                                conf_dw_b[l], conf_ln_g[l], conf_ln_b[l]) for l in range(DEPTH)]
    half = N_MERGE // 2
    merge_halves = [[] for _ in range(DEPTH)]

    def put_first(res):
        conv_full = [_unpack(res[2][d], conv_spans) for d in range(N_DEV)]
        scw_full = jnp.concatenate([cf[0] for cf in conv_full], axis=-1)
        dww_full = jnp.concatenate([cf[1] for cf in conv_full], axis=-1)
        for l in range(DEPTH):
            layers[l].update(scw=jnp.pad(scw_full[l], ((0, 8 - SHORT_CONV), (0, 0))),
                             dww=jnp.pad(dww_full[l], ((0, 32 - CONF_CONV), (0, 0))))
        layers[0].update(w_mix=res[0].reshape(D_MODEL, N_MIX), w_f=res[1].reshape(D_MODEL, N_F))

    def put_in(l):
        def sink(res):
            layers[l].update(w_mix=res[0].reshape(D_MODEL, N_MIX), w_f=res[1].reshape(D_MODEL, N_F))
            put_merge_half(l)(res[2:])
        return sink

    def put_merge_half(l):
        def sink(res):
            merge_halves[l].append(res[0])
            if len(merge_halves[l]) == 2:
                layers[l].update(w_merge=jnp.concatenate(merge_halves[l], axis=-1).reshape(D_MODEL, N_MERGE))
        return sink

    def put_out(l):
        def sink(res):
            layers[l].update(wb=res[0].transpose(1, 2, 0, 3).reshape(N_BRANCH, BR, D_MODEL), wo=res[1].reshape(D_MODEL, D_MODEL))
        return sink

    attach0 = {
        "rms_fwd": ([(sh[0]["mix"], True), (sh[0]["f"], True), (conv_slab, True)], put_first),
        "proj_mix": ([(sh[0]["merge"][:, :half], True)], put_merge_half(0)),
        "mix_fwd": ([(sh[0]["merge"][:, half:], True)], put_merge_half(0)),
        "proj_merge": ([(sh[0]["wb"], True), (sh[0]["wo"], True)], put_out(0)),
        "attn_fwd": ([(sh[1]["mix"], True), (sh[1]["f"], True), (sh[1]["merge"][:, :half], True)], put_in(1)),
        "merge_fwd": ([(sh[1]["merge"][:, half:], True)], put_merge_half(1)),
    }
    attach1 = {"proj_mix": ([(sh[1]["wb"], True), (sh[1]["wo"], True)], put_out(1))}

    xs = x[0]
    xs, sv0 = _layer_fwd(xs, layers[0], "0", attach0)
    xs, sv1 = _layer_fwd(xs, layers[1], "1", attach1)
    loss_p, dx, dfg = _loss_head(xs, final_g.reshape(1, D_MODEL), loss_target[0])
    loss_local = (0.5 / D_MODEL * jnp.sum(loss_p)).reshape(1)
    dx, g1, early1, (late1, spans1), (last1, lspans1), _ = _layer_bwd(dx, sv1, layers[1], "1", True, None, [dfg.sum(0)])
    dx, g0, early0, (late0, spans0), (last0, lspans0), last1_out = _layer_bwd(dx, sv0, layers[0], "0", True, last1, [loss_local])
    last0_out = _exchange(last0, "gather_last")

    red, small = [], []
    for l, (early, late, spans, last, lspans) in enumerate(((early0, late0, spans0, last0_out, lspans0),
                                                            (early1, late1, spans1, last1_out, lspans1))):
        t = str(l)
        red.append(dict(merge=_sum8(early[0], "sum_merge" + t), wb=_sum8(early[1], "sum_wb" + t), wo=_sum8(early[2], "sum_wo" + t),
                        mix=_sum8(late[0], "sum_mix" + t), f=_sum8(late[1], "sum_f" + t)))
        keys = ["norm_g", "final_g" if l == DEPTH - 1 else "loss"] + list(SMALL[1:])
        small.append(dict(zip(keys, _unpack(_sum8(last[0], "sum_last" + t), lspans)
                              + _unpack(_sum8(late[2], "sum_small" + t), spans))))
    gs = {nm: jnp.stack([small[l][nm] for l in range(DEPTH)]) for nm in SMALL}
    gs["final_g"] = small[DEPTH - 1]["final_g"]
    loss = small[0]["loss"][0]
    gs["short_conv_w"] = lax.dynamic_slice_in_dim(gs["short_conv_w"], me * cshard, cshard, axis=2)
    gs["conf_dw_w"] = lax.dynamic_slice_in_dim(gs["conf_dw_w"], me * cshard, cshard, axis=2)
    g_w_in = jnp.stack([_join_w_in(red[l]["mix"], red[l]["merge"], red[l]["f"][:, :N_HEADS]) for l in range(DEPTH)])
    g_w_branch = jnp.stack([red[l]["wb"].reshape(N_BRANCH, BR, rows) for l in range(DEPTH)])
    g_w_out = jnp.stack([red[l]["wo"] for l in range(DEPTH)])

    d_w_in, nm_w_in, nv_w_in = _adamw(w_in, g_w_in, m_w_in, v_w_in, "adamw_w_in")
    flat = lambda a: a.reshape(DEPTH, N_BRANCH * BR, rows)
    d_w_branch, nm_w_branch, nv_w_branch = (a.reshape(w_branch.shape) for a in _adamw(
        flat(w_branch), flat(g_w_branch), flat(m_w_branch), flat(v_w_branch), "adamw_w_branch"))
    d_w_out, nm_w_out, nv_w_out = _adamw(w_out, g_w_out, m_w_out, v_w_out, "adamw_w_out")
    names = SMALL + ("final_g",)
    ws = dict(zip(names, (norm_g, f_bias, sgu_w, sgu_b, sgu_ln_g, sgu_ln_b, short_conv_w, conf_dw_w, conf_dw_b, conf_ln_g,
                          conf_ln_b, final_g)))
    ms = dict(zip(names, (m_norm_g, m_f_bias, m_sgu_w, m_sgu_b, m_sgu_ln_g, m_sgu_ln_b, m_short_conv_w, m_conf_dw_w,
                          m_conf_dw_b, m_conf_ln_g, m_conf_ln_b, m_final_g)))
    vs = dict(zip(names, (v_norm_g, v_f_bias, v_sgu_w, v_sgu_b, v_sgu_ln_g, v_sgu_ln_b, v_short_conv_w, v_conf_dw_w,
                          v_conf_dw_b, v_conf_ln_g, v_conf_ln_b, v_final_g)))
    w_slab, spans = _pack([ws[nm] for nm in names])
    g_slab, _ = _pack([gs[nm] for nm in names])
    m_slab, _ = _pack([ms[nm] for nm in names])
    v_slab, _ = _pack([vs[nm] for nm in names])
    d_s, nm_s, nv_s = (dict(zip(names, _unpack(a[0], spans))) for a in _adamw(w_slab[None], g_slab[None], m_slab[None],
                                                                              v_slab[None], "adamw_small"))

    def ordered(small, w_in_v, w_branch_v, w_out_v):
        return [small["norm_g"], w_in_v, small["f_bias"], small["sgu_w"], small["sgu_b"], small["sgu_ln_g"],
                small["sgu_ln_b"], small["short_conv_w"], small["conf_dw_w"], small["conf_dw_b"], small["conf_ln_g"],
                small["conf_ln_b"], w_branch_v, w_out_v, small["final_g"]]

    return (loss, dx[None], *ordered(gs, g_w_in, g_w_branch, g_w_out), *ordered(d_s, d_w_in, d_w_branch, d_w_out),
            *ordered(nm_s, nm_w_in, nm_w_branch, nm_w_out), *ordered(nv_s, nv_w_in, nv_w_branch, nv_w_out))
```

```python
import functools
import math

import jax
import jax.numpy as jnp
from jax import lax
from jax.experimental import pallas as pl
from jax.experimental.pallas import tpu as pltpu

F32 = jnp.float32
BF16 = jnp.bfloat16

D_MODEL = 1024
DEPTH = 2
N_BRANCH = 4
BR = 256
N_HEADS = 4
HEAD_DIM = 64
CHUNK = 128
SHORT_CONV = 3
CONF_CONV = 31
EPS = 1e-6
N_DEV = 8

ADAM_LR = 0.001
ADAM_B1 = 0.9
ADAM_B2 = 0.999
ADAM_EPS = 1e-08
ADAM_WD = 0.01
ADAM_STEP = 10

O_UV, O_AG, O_QKV, O_BG, O_DG, O_CIN, O_CG, O_GLU = 0, 512, 768, 1536, 1792, 2048, 2816, 3072
N_MIX = 3584
N_MERGE = N_BRANCH * D_MODEL
N_F = 128
IN_COLS = 7684
HALO = 32
TM = 512
TMG = 256
FQ = 1024
BQ = 1024
BK = 512
SUB = 64
CUMB = 512
VMEM_LIMIT = 56 * 1024 * 1024
NEG = -1e30
SCALE = 1.0 / math.sqrt(HEAD_DIM)
GELU_K = math.sqrt(2.0 / math.pi)


def _cp(sem=None):
    return pltpu.CompilerParams(dimension_semantics=sem, vmem_limit_bytes=VMEM_LIMIT)


PEER_ORDER = (6, 4, 2, 7, 5, 3, 1)
RELAYED = (3, 5, 7)


def _xchg(cin, cout, send, recv, loc, modes, phase):
    x, y, c = lax.axis_index("x"), lax.axis_index("y"), lax.axis_index("c")
    me = 4 * x + 2 * y + c

    def peer_of(kk):
        px, py, pc = lax.rem(x + (kk >> 2 & 1), 2), lax.rem(y + (kk >> 1 & 1), 2), lax.rem(c + (kk & 1), 2)
        return (px, py, pc), 4 * px + 2 * py + pc

    def remote(src, dst, a, kk, pid):
        return pltpu.make_async_remote_copy(src_ref=src, dst_ref=dst, send_sem=send.at[a, kk], recv_sem=recv.at[a, kk],
                                            device_id=pid, device_id_type=pl.DeviceIdType.MESH)

    def outgoing(a, kk):
        if modes[a] and kk in RELAYED:
            _, origin = peer_of(kk - 1)
            return remote(cout[a].at[origin], cout[a].at[origin], a, kk, peer_of(1)[0])
        pid, peer = peer_of(kk)
        return remote(cin[a] if modes[a] else cin[a].at[peer], cout[a].at[me], a, kk, pid)

    def arrival(a, kk):
        _, peer = peer_of(kk)
        return remote(cout[a].at[peer], cout[a].at[peer], a, kk, (x, y, c))

    if phase == "relay":
        for kk in RELAYED:
            for a, gather in enumerate(modes):
                if gather:
                    arrival(a, kk - 1).wait_recv()
                    outgoing(a, kk).start()
        return
    for a, gather in enumerate(modes):
        cp = pltpu.make_async_copy(cin[a] if gather else cin[a].at[me], cout[a].at[me], loc.at[a])
        if phase == "start":
            cp.start()
        else:
            cp.wait()
    if phase == "start":
        for kk in PEER_ORDER:
            for a, gather in enumerate(modes):
                if not (gather and kk in RELAYED):
                    outgoing(a, kk).start()
        return
    for kk in PEER_ORDER:
        for a in range(len(modes)):
            outgoing(a, kk).wait_send()
    for kk in PEER_ORDER:
        for a, gather in enumerate(modes):
            if not (gather and kk + 1 in RELAYED):
                arrival(a, kk).wait_recv()


def _xchg_shapes(comm):
    return [jax.ShapeDtypeStruct((N_DEV,) + tuple(a.shape[(0 if gather else 1):]), a.dtype) for a, gather in comm]


def _xchg_sems(n):
    return [pltpu.SemaphoreType.DMA((n, N_DEV)), pltpu.SemaphoreType.DMA((n, N_DEV)), pltpu.SemaphoreType.DMA((n,))]


def _exchange(comm, name):
    n = len(comm)
    modes = [g for _, g in comm]

    def body(*refs):
        cin, cout, (send, recv, loc) = refs[:n], refs[n:2 * n], refs[2 * n:]
        for phase in ("start", "relay", "finish"):
            _xchg(cin, cout, send, recv, loc, modes, phase)

    anyspec = pl.BlockSpec(memory_space=pl.ANY)
    return pl.pallas_call(
        body, in_specs=[anyspec] * n, out_specs=[anyspec] * n, out_shape=_xchg_shapes(comm),
        scratch_shapes=_xchg_sems(n), name=name)(*[a for a, _ in comm])


def _pcall(body, *, grid, in_specs, out_specs, out_shape, operands, name, scratch_shapes=(), comm=None):
    if not comm:
        outs = pl.pallas_call(
            body, grid=grid, in_specs=in_specs, out_specs=out_specs, out_shape=out_shape, scratch_shapes=list(scratch_shapes),
            compiler_params=_cp(("arbitrary",) * len(grid)), name=name)(*operands)
        return list(outs), []
    n, nin, nout, nsc = len(comm), len(operands), len(out_shape), len(scratch_shapes)
    modes = [g for _, g in comm]

    def wrapped(*refs):
        ins, cin = refs[:nin], refs[nin:nin + n]
        outs, cout = refs[nin + n:nin + n + nout], refs[nin + n + nout:nin + 2 * n + nout]
        scratch = refs[nin + 2 * n + nout:]
        own, (send, recv, loc) = scratch[:nsc], scratch[nsc:]
        ids = [pl.program_id(d) for d in range(len(grid))]
        first = functools.reduce(jnp.logical_and, [i == 0 for i in ids])
        last = functools.reduce(jnp.logical_and, [i == g - 1 for i, g in zip(ids, grid)])

        @pl.when(first)
        def _():
            _xchg(cin, cout, send, recv, loc, modes, "start")

        @pl.when(last)
        def _():
            _xchg(cin, cout, send, recv, loc, modes, "relay")

        body(*ins, *outs, *own)

        @pl.when(last)
        def _():
            _xchg(cin, cout, send, recv, loc, modes, "finish")

    anyspec = pl.BlockSpec(memory_space=pl.ANY)
    res = pl.pallas_call(
        wrapped, grid=grid, in_specs=list(in_specs) + [anyspec] * n, out_specs=list(out_specs) + [anyspec] * n,
        out_shape=list(out_shape) + _xchg_shapes(comm), scratch_shapes=list(scratch_shapes) + _xchg_sems(n),
        compiler_params=_cp(("arbitrary",) * len(grid)), name=name)(*operands, *[a for a, _ in comm])
    return list(res[:nout]), list(res[nout:])


def _sig(x):
    return 0.5 * jnp.tanh(0.5 * x) + 0.5


def _silu(x):
    return x * _sig(x)


def _dsilu(x):
    s = _sig(x)
    return s * (1.0 + x * (1.0 - s))


def _gelu(x):
    return 0.5 * x * (1.0 + jnp.tanh(GELU_K * (x + 0.044715 * x * x * x)))


def _dgelu(x):
    t = jnp.tanh(GELU_K * (x + 0.044715 * x * x * x))
    return 0.5 * (1.0 + t) + 0.5 * x * (1.0 - t * t) * GELU_K * (1.0 + 3.0 * 0.044715 * x * x)


def _ln_hat(x):
    mu = jnp.mean(x, axis=-1, keepdims=True)
    xc = x - mu
    rs = lax.rsqrt(jnp.mean(xc * xc, axis=-1, keepdims=True) + EPS)
    return xc * rs, rs


def _ln_bwd(dhat, hat, rs):
    return rs * (dhat - jnp.mean(dhat, axis=-1, keepdims=True) - hat * jnp.mean(dhat * hat, axis=-1, keepdims=True))


def _dot(a, b):
    return jnp.dot(a, b, preferred_element_type=F32)


def _dot_nt(a, b):
    return lax.dot_general(a, b, (((1,), (1,)), ((), ())), preferred_element_type=F32)


def _dot_tn(a, b):
    return lax.dot_general(a, b, (((0,), (0,)), ((), ())), preferred_element_type=F32)


def _fold8(x):
    acc = x[0:8]
    for r in range(1, x.shape[0] // 8):
        acc = acc + x[8 * r:8 * r + 8]
    return acc


def _rms_fwd(x, g, name, comm=None):
    s = x.shape[0]

    def body(x_ref, g_ref, h_ref):
        xv = x_ref[...]
        r = lax.rsqrt(jnp.mean(xv * xv, axis=-1, keepdims=True) + EPS)
        h_ref[...] = (xv * r * g_ref[...]).astype(BF16)

    (h,), couts = _pcall(
        body, grid=(s // TM,),
        in_specs=[pl.BlockSpec((TM, D_MODEL), lambda i: (i, 0)), pl.BlockSpec((1, D_MODEL), lambda i: (0, 0))],
        out_specs=[pl.BlockSpec((TM, D_MODEL), lambda i: (i, 0))],
        out_shape=[jax.ShapeDtypeStruct((s, D_MODEL), BF16)], operands=(x, g), name=name, comm=comm)
    return h, couts


def _rms_bwd(dh, x, g, dx_next, name):
    s = x.shape[0]

    def body(dh_ref, x_ref, g_ref, dxn_ref, dx_ref, dg_ref):
        i = pl.program_id(0)
        xv = x_ref[...]
        r = lax.rsqrt(jnp.mean(xv * xv, axis=-1, keepdims=True) + EPS)
        xn = xv * r
        dhv = dh_ref[...]
        dxn = dhv * g_ref[...]
        dx_ref[...] = dxn_ref[...] + r * (dxn - xn * jnp.mean(dxn * xn, axis=-1, keepdims=True))

        @pl.when(i == 0)
        def _():
            dg_ref[...] = jnp.zeros_like(dg_ref)

        dg_ref[...] += _fold8(dhv * xn)

    tile = pl.BlockSpec((TM, D_MODEL), lambda i: (i, 0))
    return pl.pallas_call(
        body, grid=(s // TM,),
        in_specs=[tile, tile, pl.BlockSpec((1, D_MODEL), lambda i: (0, 0)), tile],
        out_specs=[tile, pl.BlockSpec((8, D_MODEL), lambda i: (0, 0))],
        out_shape=[jax.ShapeDtypeStruct((s, D_MODEL), F32), jax.ShapeDtypeStruct((8, D_MODEL), F32)],
        compiler_params=_cp(("arbitrary",)), name=name)(dh, x, g, dx_next)


def _loss_head(x, g, target):
    s = x.shape[0]

    def body(x_ref, g_ref, t_ref, loss_ref, dx_ref, dg_ref):
        i = pl.program_id(0)
        xv = x_ref[...]
        r = lax.rsqrt(jnp.mean(xv * xv, axis=-1, keepdims=True) + EPS)
        xn = xv * r
        err = xn * g_ref[...] - t_ref[...]
        dy = err * (1.0 / D_MODEL)
        dxn = dy * g_ref[...]
        dx_ref[...] = r * (dxn - xn * jnp.mean(dxn * xn, axis=-1, keepdims=True))

        @pl.when(i == 0)
        def _():
            dg_ref[...] = jnp.zeros_like(dg_ref)
            loss_ref[...] = jnp.zeros_like(loss_ref)

        dg_ref[...] += _fold8(dy * xn)
        loss_ref[...] += _fold8(err * err)

    tile = pl.BlockSpec((TM, D_MODEL), lambda i: (i, 0))
    acc = pl.BlockSpec((8, D_MODEL), lambda i: (0, 0))
    return pl.pallas_call(
        body, grid=(s // TM,),
        in_specs=[tile, pl.BlockSpec((1, D_MODEL), lambda i: (0, 0)), tile],
        out_specs=[acc, tile, acc],
        out_shape=[jax.ShapeDtypeStruct((8, D_MODEL), F32), jax.ShapeDtypeStruct((s, D_MODEL), F32),
                   jax.ShapeDtypeStruct((8, D_MODEL), F32)],
        compiler_params=_cp(("arbitrary",)), name="loss_head")(x, g, target)


def _mm_nn(a, b, tn, name, comm=None, tm=2048):
    m, k = a.shape
    n = b.shape[1]
    tm = min(tm, m)

    def body(a_ref, b_ref, o_ref):
        o_ref[...] = _dot(a_ref[...], b_ref[...])

    (out,), couts = _pcall(
        body, grid=(n // tn, m // tm),
        in_specs=[pl.BlockSpec((tm, k), lambda j, i: (i, 0)), pl.BlockSpec((k, tn), lambda j, i: (0, j))],
        out_specs=[pl.BlockSpec((tm, tn), lambda j, i: (i, j))],
        out_shape=[jax.ShapeDtypeStruct((m, n), F32)], operands=(a, b), name=name, comm=comm)
    return out, couts


def _dh(dpm, dpg, dpf, w_mix, w_merge, w_f, name, comm=None):
    s = dpm.shape[0]
    tm = 1024 if s % 1024 == 0 else 512
    tk1, tk2 = 896, 1024
    n1, n2 = N_MIX // tk1, N_MERGE // tk2

    def body(dpm_ref, dpg_ref, dpf_ref, wm_ref, wg_ref, wf_ref, o_ref):
        j = pl.program_id(1)

        @pl.when(j == 0)
        def _():
            o_ref[...] = _dot_nt(dpf_ref[...], wf_ref[...])

        @pl.when(j < n1)
        def _():
            o_ref[...] += _dot_nt(dpm_ref[...], wm_ref[...])

        @pl.when(j >= n1)
        def _():
            o_ref[...] += _dot_nt(dpg_ref[...], wg_ref[...])

    mix_j = lambda j: jnp.minimum(j, n1 - 1)
    merge_j = lambda j: jnp.maximum(j - n1, 0)
    (out,), couts = _pcall(
        body, grid=(s // tm, n1 + n2),
        in_specs=[pl.BlockSpec((tm, tk1), lambda i, j: (i, mix_j(j))), pl.BlockSpec((tm, tk2), lambda i, j: (i, merge_j(j))),
                  pl.BlockSpec((tm, N_F), lambda i, j: (i, 0)),
                  pl.BlockSpec((D_MODEL, tk1), lambda i, j: (0, mix_j(j))), pl.BlockSpec((D_MODEL, tk2), lambda i, j: (0, merge_j(j))),
                  pl.BlockSpec((D_MODEL, N_F), lambda i, j: (0, 0))],
        out_specs=[pl.BlockSpec((tm, D_MODEL), lambda i, j: (i, 0))],
        out_shape=[jax.ShapeDtypeStruct((s, D_MODEL), F32)], operands=(dpm, dpg, dpf, w_mix, w_merge, w_f), name=name, comm=comm)
    return out, couts


def _mm_tn(a, d, tn, name, tm=2048):
    m, k = a.shape
    n = d.shape[1]
    tm = min(tm, m)
    nm = m // tm

    def body(a_ref, d_ref, o_ref, acc):
        i = pl.program_id(1)

        @pl.when(i == 0)
        def _():
            acc[...] = jnp.zeros_like(acc)

        acc[...] += _dot_tn(a_ref[...], d_ref[...])

        @pl.when(i == nm - 1)
        def _():
            o_ref[...] = acc[...].astype(BF16)

    return pl.pallas_call(
        body, grid=(n // tn, nm),
        in_specs=[pl.BlockSpec((tm, k), lambda j, i: (i, 0)), pl.BlockSpec((tm, tn), lambda j, i: (i, j))],
        out_specs=pl.BlockSpec((k, tn), lambda j, i: (0, j)),
        out_shape=jax.ShapeDtypeStruct((k, n), BF16), scratch_shapes=[pltpu.VMEM((k, tn), F32)],
        compiler_params=_cp(("parallel", "arbitrary")), name=name)(a, d)


def _lane_head():
    return lax.broadcasted_iota(jnp.int32, (1, BR), 1) // HEAD_DIM


def _gmlp_chunk_fwd(p_ref, r0, gw_ref, gb_ref, lg, lb):
    uv = p_ref[r0:r0 + CHUNK, O_UV:O_UV + 2 * BR]
    u = _gelu(uv[:, :BR])
    vhat, rs = _ln_hat(_gelu(uv[:, BR:]))
    vn = (vhat * lg + lb).astype(BF16)
    head = _lane_head()
    mixed = gb_ref[...]
    for h in range(N_HEADS):
        mixed = mixed + jnp.where(head == h, _dot(gw_ref[h], vn), 0.0)
    return uv, u, vhat, rs, vn, mixed


def _tap_groups(k_width):
    groups = []
    for b in range(8):
        taps = [(d // 8, k_width - 1 - d) for d in range(b, k_width, 8)]
        if taps:
            groups.append((b, taps))
    return groups


def _causal_taps(buf, off, r0, nr, k_width):
    lead = 8 * ((k_width - 1) // 8 + 1)
    win = buf[off + r0 - lead:off + r0 + nr, :]
    for b, taps in _tap_groups(k_width):
        shifted = win if b == 0 else pltpu.roll(win, b, 0)
        for a, k in taps:
            yield k, shifted[lead - 8 * a:lead - 8 * a + nr]


def _anticausal_taps(buf, r0, nr, k_width):
    lead = 8 * ((k_width - 1) // 8 + 1)
    win = buf[r0:r0 + nr + lead, :]
    for b, taps in _tap_groups(k_width):
        shifted = win if b == 0 else pltpu.roll(win, nr + lead - b, 0)
        for a, k in taps:
            yield k, shifted[8 * a:8 * a + nr]


def _conv_sub_blocks(rows):
    out = [(r, SUB) for r in range(0, rows - rows % SUB, SUB)]
    if rows % SUB:
        out.append((rows - rows % SUB, rows % SUB))
    return out


def _mix_fwd(pm, wl, name, comm=None):
    s = pm.shape[0]
    nt = s // TM

    def body(p_ref, ph_ref, gw_ref, gb_ref, lg_ref, lb_ref, scw_ref, dww_ref, dwb_ref, clg_ref, clb_ref,
             ya_ref, yc_ref, yd_ref, q_ref, k_ref, v_ref, cc_ref, zbuf, hbuf):
        i = pl.program_id(0)
        for h in range(N_HEADS):
            c0 = O_QKV + h * HEAD_DIM
            q_ref[h] = (p_ref[:, c0:c0 + HEAD_DIM] * SCALE).astype(BF16)
            k_ref[h] = p_ref[:, c0 + BR:c0 + BR + HEAD_DIM].astype(BF16)
            v_ref[h] = p_ref[:, c0 + 2 * BR:c0 + 2 * BR + HEAD_DIM].astype(BF16)
        lg = lg_ref[...]
        lb = lb_ref[...]
        for c in range(TM // CHUNK):
            r0 = c * CHUNK
            _, u, _, _, _, mixed = _gmlp_chunk_fwd(p_ref, r0, gw_ref, gb_ref, lg, lb)
            ag = p_ref[r0:r0 + CHUNK, O_AG:O_AG + BR]
            ya_ref[r0:r0 + CHUNK, :] = (u * mixed * _silu(ag)).astype(BF16)

        first = i > 0
        zbuf[0:HALO, :] = jnp.where(first, ph_ref[:, O_CIN + BR:O_CIN + 2 * BR] * ph_ref[:, O_CIN + 2 * BR:O_CIN + 3 * BR], 0.0)
        zbuf[HALO:HALO + TM, :] = p_ref[:, O_CIN + BR:O_CIN + 2 * BR] * p_ref[:, O_CIN + 2 * BR:O_CIN + 3 * BR]
        hbuf[0:HALO, :] = jnp.where(first, ph_ref[:, O_GLU:O_GLU + BR] * _sig(ph_ref[:, O_GLU + BR:O_GLU + 2 * BR]), 0.0)
        hbuf[HALO:HALO + TM, :] = p_ref[:, O_GLU:O_GLU + BR] * _sig(p_ref[:, O_GLU + BR:O_GLU + 2 * BR])

        clg = clg_ref[...]
        clb = clb_ref[...]
        for r0, nr in _conv_sub_blocks(TM):
            yc = jnp.zeros((nr, BR), F32)
            for k, zk in _causal_taps(zbuf, HALO, r0, nr, SHORT_CONV):
                yc = yc + scw_ref[k:k + 1, :] * zk
            bgate = p_ref[r0:r0 + nr, O_CIN:O_CIN + BR]
            cg = p_ref[r0:r0 + nr, O_CG:O_CG + BR]
            yc_ref[r0:r0 + nr, :] = (bgate * yc * _silu(cg)).astype(BF16)

            cc = jnp.zeros((nr, BR), F32) + dwb_ref[...]
            for k, hk in _causal_taps(hbuf, HALO, r0, nr, CONF_CONV):
                cc = cc + dww_ref[k:k + 1, :] * hk
            cc_ref[r0:r0 + nr, :] = cc
            chat, _ = _ln_hat(cc)
            dg = p_ref[r0:r0 + nr, O_DG:O_DG + BR]
            yd_ref[r0:r0 + nr, :] = (_silu(chat * clg + clb) * _silu(dg)).astype(BF16)

    full = lambda shape: pl.BlockSpec(shape, lambda i: tuple(0 for _ in shape))
    ytile = pl.BlockSpec((TM, BR), lambda i: (i, 0))
    yshape = jax.ShapeDtypeStruct((s, BR), BF16)
    htile = pl.BlockSpec((N_HEADS, TM, HEAD_DIM), lambda i: (0, i, 0))
    hshape = jax.ShapeDtypeStruct((N_HEADS, s, HEAD_DIM), BF16)
    return _pcall(
        body, grid=(nt,),
        in_specs=[pl.BlockSpec((TM, N_MIX), lambda i: (i, 0)),
                  pl.BlockSpec((HALO, N_MIX), lambda i: (jnp.maximum(i * (TM // HALO) - 1, 0), 0)),
                  full((N_HEADS, CHUNK, CHUNK)), full((CHUNK, BR)), full((1, BR)), full((1, BR)),
                  full((8, BR)), full((32, BR)), full((1, BR)), full((1, BR)), full((1, BR))],
        out_specs=[ytile, ytile, ytile, htile, htile, htile, ytile],
        out_shape=[yshape, yshape, yshape, hshape, hshape, hshape, jax.ShapeDtypeStruct((s, BR), F32)],
        scratch_shapes=[pltpu.VMEM((HALO + TM, BR), F32), pltpu.VMEM((HALO + TM, BR), F32)],
        name=name, comm=comm, operands=(
            pm, pm, wl["gw"], wl["gb"], wl["sgu_ln_g"], wl["sgu_ln_b"], wl["scw"], wl["dww"], wl["conf_dw_b"],
            wl["conf_ln_g"], wl["conf_ln_b"]))


def _mix_bwd(pm, cc, dya, dyc, dyd, dqkv, dbg, wl, name):
    s = pm.shape[0]
    nt = s // TM
    ext = TM + HALO

    def body(p_ref, ph_ref, pn_ref, cc_ref, ccn_ref, dya_ref, dyc_ref, dycn_ref, dyd_ref, dydn_ref, dq_ref, dk_ref, dv_ref, dbg_ref,
             gw_ref, gwt_ref, gb_ref, lg_ref, lb_ref, scw_ref, dww_ref, dwb_ref, clg_ref, clb_ref,
             dp_ref, dgw_ref, dgb_ref, vec_ref, dscw_ref, ddww_ref, zbuf, dcb, hbuf, dcc):
        i = pl.program_id(0)

        @pl.when(i == 0)
        def _():
            dgw_ref[...] = jnp.zeros_like(dgw_ref)
            dgb_ref[...] = jnp.zeros_like(dgb_ref)
            vec_ref[...] = jnp.zeros_like(vec_ref)
            dscw_ref[...] = jnp.zeros_like(dscw_ref)
            ddww_ref[...] = jnp.zeros_like(ddww_ref)

        lg = lg_ref[...]
        lb = lb_ref[...]
        head = _lane_head()
        d_lg = jnp.zeros((1, BR), F32)
        d_lb = jnp.zeros((1, BR), F32)
        for c in range(TM // CHUNK):
            r0 = c * CHUNK
            uv, u, vhat, rs, vn, mixed = _gmlp_chunk_fwd(p_ref, r0, gw_ref, gb_ref, lg, lb)
            ag = p_ref[r0:r0 + CHUNK, O_AG:O_AG + BR]
            dy = dya_ref[r0:r0 + CHUNK, :]
            sa = _silu(ag)
            du = dy * mixed * sa
            dmx = dy * u * sa
            dp_ref[r0:r0 + CHUNK, O_AG:O_AG + BR] = (dy * u * mixed * _dsilu(ag)).astype(BF16)
            dgb_ref[...] += dmx
            dmx_b = dmx.astype(BF16)
            dvn = jnp.zeros((CHUNK, BR), F32)
            for h in range(N_HEADS):
                sel = head == h
                dgw_ref[h] += _dot_nt(jnp.where(sel, dmx, 0.0).astype(BF16), vn)
                dvn = dvn + jnp.where(sel, _dot(gwt_ref[h], dmx_b), 0.0)
            d_lg = d_lg + jnp.sum(dvn * vhat, axis=0, keepdims=True)
            d_lb = d_lb + jnp.sum(dvn, axis=0, keepdims=True)
            dv0 = _ln_bwd(dvn * lg, vhat, rs)
            dp_ref[r0:r0 + CHUNK, O_UV:O_UV + BR] = (du * _dgelu(uv[:, :BR])).astype(BF16)
            dp_ref[r0:r0 + CHUNK, O_UV + BR:O_UV + 2 * BR] = (dv0 * _dgelu(uv[:, BR:])).astype(BF16)
        vec_ref[0:1, :] += d_lg
        vec_ref[1:2, :] += d_lb

        for j, g_ref in enumerate((dq_ref, dk_ref, dv_ref)):
            dp_ref[:, O_QKV + j * BR:O_QKV + (j + 1) * BR] = jnp.concatenate(
                [g_ref[h] for h in range(N_HEADS)], axis=1).astype(BF16)
        dp_ref[:, O_BG:O_BG + BR] = dbg_ref[...].astype(BF16)

        first = i > 0
        last = i < nt - 1
        zbuf[0:HALO, :] = jnp.where(first, ph_ref[:, O_CIN + BR:O_CIN + 2 * BR] * ph_ref[:, O_CIN + 2 * BR:O_CIN + 3 * BR], 0.0)
        zbuf[HALO:HALO + TM, :] = p_ref[:, O_CIN + BR:O_CIN + 2 * BR] * p_ref[:, O_CIN + 2 * BR:O_CIN + 3 * BR]
        dcb[0:TM, :] = dyc_ref[...] * p_ref[:, O_CIN:O_CIN + BR] * _silu(p_ref[:, O_CG:O_CG + BR])
        dcb[TM:ext, :] = jnp.where(last, dycn_ref[...] * pn_ref[:, O_CIN:O_CIN + BR] * _silu(pn_ref[:, O_CG:O_CG + BR]), 0.0)
        for r0, nr in _conv_sub_blocks(TM):
            yc = jnp.zeros((nr, BR), F32)
            dz = jnp.zeros((nr, BR), F32)
            dcur = dcb[r0:r0 + nr, :]
            for k, zk in _causal_taps(zbuf, HALO, r0, nr, SHORT_CONV):
                yc = yc + scw_ref[k:k + 1, :] * zk
                dscw_ref[8 * k:8 * k + 8, :] += _fold8(dcur * zk)
            for k, dk in _anticausal_taps(dcb, r0, nr, SHORT_CONV):
                dz = dz + scw_ref[k:k + 1, :] * dk
            dy = dyc_ref[r0:r0 + nr, :]
            bgate = p_ref[r0:r0 + nr, O_CIN:O_CIN + BR]
            cg = p_ref[r0:r0 + nr, O_CG:O_CG + BR]
            dp_ref[r0:r0 + nr, O_CIN:O_CIN + BR] = (dy * yc * _silu(cg)).astype(BF16)
            dp_ref[r0:r0 + nr, O_CIN + BR:O_CIN + 2 * BR] = (dz * p_ref[r0:r0 + nr, O_CIN + 2 * BR:O_CIN + 3 * BR]).astype(BF16)
            dp_ref[r0:r0 + nr, O_CIN + 2 * BR:O_CIN + 3 * BR] = (dz * p_ref[r0:r0 + nr, O_CIN + BR:O_CIN + 2 * BR]).astype(BF16)
            dp_ref[r0:r0 + nr, O_CG:O_CG + BR] = (dy * bgate * yc * _dsilu(cg)).astype(BF16)

        hbuf[0:HALO, :] = jnp.where(first, ph_ref[:, O_GLU:O_GLU + BR] * _sig(ph_ref[:, O_GLU + BR:O_GLU + 2 * BR]), 0.0)
        hbuf[HALO:HALO + TM, :] = p_ref[:, O_GLU:O_GLU + BR] * _sig(p_ref[:, O_GLU + BR:O_GLU + 2 * BR])
        clg = clg_ref[...]
        clb = clb_ref[...]
        d_clg = jnp.zeros((1, BR), F32)
        d_clb = jnp.zeros((1, BR), F32)
        d_dwb = jnp.zeros((1, BR), F32)
        for r0, nr in _conv_sub_blocks(ext):
            in_tile = r0 < TM
            chat, rs = _ln_hat(cc_ref[r0:r0 + nr, :] if in_tile else ccn_ref[...])
            ln = chat * clg + clb
            if in_tile:
                dy = dyd_ref[r0:r0 + nr, :]
                dg = p_ref[r0:r0 + nr, O_DG:O_DG + BR]
            else:
                dy = jnp.where(last, dydn_ref[...], 0.0)
                dg = pn_ref[:, O_DG:O_DG + BR]
            dln = dy * _silu(dg) * _dsilu(ln)
            dc = _ln_bwd(dln * clg, chat, rs)
            dcc[r0:r0 + nr, :] = dc
            if in_tile:
                dp_ref[r0:r0 + nr, O_DG:O_DG + BR] = (dy * _silu(ln) * _dsilu(dg)).astype(BF16)
                d_clg = d_clg + jnp.sum(dln * chat, axis=0, keepdims=True)
                d_clb = d_clb + jnp.sum(dln, axis=0, keepdims=True)
                d_dwb = d_dwb + jnp.sum(dc, axis=0, keepdims=True)
        vec_ref[2:3, :] += d_dwb
        vec_ref[3:4, :] += d_clg
        vec_ref[4:5, :] += d_clb
        for r0, nr in _conv_sub_blocks(TM):
            dcur = dcc[r0:r0 + nr, :]
            dhh = jnp.zeros((nr, BR), F32)
            for k, hk in _causal_taps(hbuf, HALO, r0, nr, CONF_CONV):
                ddww_ref[8 * k:8 * k + 8, :] += _fold8(dcur * hk)
            for k, dk in _anticausal_taps(dcc, r0, nr, CONF_CONV):
                dhh = dhh + dww_ref[k:k + 1, :] * dk
            a = p_ref[r0:r0 + nr, O_GLU:O_GLU + BR]
            sg = _sig(p_ref[r0:r0 + nr, O_GLU + BR:O_GLU + 2 * BR])
            dp_ref[r0:r0 + nr, O_GLU:O_GLU + BR] = (dhh * sg).astype(BF16)
            dp_ref[r0:r0 + nr, O_GLU + BR:O_GLU + 2 * BR] = (dhh * a * sg * (1.0 - sg)).astype(BF16)

    full = lambda shape: pl.BlockSpec(shape, lambda i: tuple(0 for _ in shape))
    rpt = TM // HALO
    prev_map = lambda i: (jnp.maximum(i * rpt - 1, 0), 0)
    next_map = lambda i: (jnp.minimum((i + 1) * rpt, nt * rpt - 1), 0)
    ytile = pl.BlockSpec((TM, BR), lambda i: (i, 0))
    htile = pl.BlockSpec((N_HEADS, TM, HEAD_DIM), lambda i: (0, i, 0))
    return pl.pallas_call(
        body, grid=(nt,),
        in_specs=[pl.BlockSpec((TM, N_MIX), lambda i: (i, 0)), pl.BlockSpec((HALO, N_MIX), prev_map),
                  pl.BlockSpec((HALO, N_MIX), next_map),
                  ytile, pl.BlockSpec((HALO, BR), next_map),
                  ytile, ytile, pl.BlockSpec((HALO, BR), next_map), ytile, pl.BlockSpec((HALO, BR), next_map),
                  htile, htile, htile, ytile,
                  full((N_HEADS, CHUNK, CHUNK)), full((N_HEADS, CHUNK, CHUNK)), full((CHUNK, BR)), full((1, BR)), full((1, BR)),
                  full((8, BR)), full((32, BR)), full((1, BR)), full((1, BR)), full((1, BR))],
        out_specs=[pl.BlockSpec((TM, N_MIX), lambda i: (i, 0)), full((N_HEADS, CHUNK, CHUNK)), full((CHUNK, BR)),
                   full((16, BR)), full((64, BR)), full((256, BR))],
        out_shape=[jax.ShapeDtypeStruct((s, N_MIX), BF16), jax.ShapeDtypeStruct((N_HEADS, CHUNK, CHUNK), F32),
                   jax.ShapeDtypeStruct((CHUNK, BR), F32), jax.ShapeDtypeStruct((16, BR), F32),
                   jax.ShapeDtypeStruct((64, BR), F32), jax.ShapeDtypeStruct((256, BR), F32)],
        scratch_shapes=[pltpu.VMEM((HALO + TM, BR), F32), pltpu.VMEM((ext, BR), F32),
                        pltpu.VMEM((HALO + TM, BR), F32), pltpu.VMEM((ext, BR), F32)],
        compiler_params=_cp(("arbitrary",)), name=name)(
            pm, pm, pm, cc, cc, dya, dyc, dyc, dyd, dyd, *dqkv, dbg,
            wl["gw"], wl["gwt"], wl["gb"], wl["sgu_ln_g"], wl["sgu_ln_b"], wl["scw"], wl["dww"], wl["conf_dw_b"],
            wl["conf_ln_g"], wl["conf_ln_b"])


def _tri(lower):
    r = lax.broadcasted_iota(jnp.int32, (CUMB, CUMB), 0)
    c = lax.broadcasted_iota(jnp.int32, (CUMB, CUMB), 1)
    return jnp.where((r >= c) if lower else (r <= c), 1.0, 0.0).astype(F32)


def _dot_hi(a, b):
    return jnp.dot(a, b, preferred_element_type=F32, precision=lax.Precision.HIGHEST)


def _cum_fwd(pf, fb, name):
    s = pf.shape[0]

    def body(pf_ref, fb_ref, cum_ref, carry):
        i = pl.program_id(0)

        @pl.when(i == 0)
        def _():
            carry[...] = jnp.zeros_like(carry)

        z = pf_ref[...] + fb_ref[...]
        logf = jnp.minimum(z, 0.0) - jnp.log(1.0 + jnp.exp(-jnp.abs(z)))
        cum_ref[...] = _dot_hi(_tri(True), logf) + carry[...]
        carry[...] += jnp.sum(logf, axis=0, keepdims=True)

    return pl.pallas_call(
        body, grid=(s // CUMB,),
        in_specs=[pl.BlockSpec((CUMB, N_F), lambda i: (i, 0)), pl.BlockSpec((1, N_F), lambda i: (0, 0))],
        out_specs=pl.BlockSpec((CUMB, N_F), lambda i: (i, 0)),
        out_shape=jax.ShapeDtypeStruct((s, N_F), F32),
        scratch_shapes=[pltpu.VMEM((1, N_F), F32)],
        compiler_params=_cp(("arbitrary",)), name=name)(pf, fb)


def _cum_bwd(dcq, dck, pf, fb, name):
    s = pf.shape[0]
    nb = s // CUMB

    def body(dcq_ref, dck_ref, pf_ref, fb_ref, dpf_ref, dfb_ref, carry):
        i = pl.program_id(0)

        @pl.when(i == 0)
        def _():
            carry[...] = jnp.zeros_like(carry)
            dfb_ref[...] = jnp.zeros_like(dfb_ref)

        lane = lax.broadcasted_iota(jnp.int32, (1, N_F), 1)
        dc = dck_ref[...]
        for h in range(N_HEADS):
            dc = dc + jnp.where(lane == h, dcq_ref[h], 0.0)
        dlogf = _dot_hi(_tri(False), dc) + carry[...]
        carry[...] += jnp.sum(dc, axis=0, keepdims=True)
        z = pf_ref[...] + fb_ref[...]
        dz = dlogf * (1.0 - _sig(z))
        dpf_ref[...] = dz.astype(BF16)
        dfb_ref[...] += _fold8(dz)

    rev = lambda i: (nb - 1 - i, 0)
    return pl.pallas_call(
        body, grid=(nb,),
        in_specs=[pl.BlockSpec((N_HEADS, CUMB, 1), lambda i: (0, nb - 1 - i, 0)), pl.BlockSpec((CUMB, N_F), rev),
                  pl.BlockSpec((CUMB, N_F), rev), pl.BlockSpec((1, N_F), lambda i: (0, 0))],
        out_specs=[pl.BlockSpec((CUMB, N_F), rev), pl.BlockSpec((8, N_F), lambda i: (0, 0))],
        out_shape=[jax.ShapeDtypeStruct((s, N_F), BF16), jax.ShapeDtypeStruct((8, N_F), F32)],
        scratch_shapes=[pltpu.VMEM((1, N_F), F32)],
        compiler_params=_cp(("arbitrary",)), name=name)(dcq, dck, pf, fb)


def _causal_mask(nr, nc, r0, c0):
    r = lax.broadcasted_iota(jnp.int32, (nr, nc), 0) + r0
    c = lax.broadcasted_iota(jnp.int32, (nr, nc), 1) + c0
    return r >= c


def _attn_fwd(q, k, v, cq, ck, name, comm=None):
    s = q.shape[1]
    nb = s // FQ

    def body(q_ref, k_ref, v_ref, cq_ref, ck_ref, o_ref, lse_ref):
        for qi in range(nb):
            qs = qi * FQ
            qb = q_ref[0, qs:qs + FQ, :]
            cqb = cq_ref[0, qs:qs + FQ, :]

            def block(kj, carry, masked):
                m, l, acc = carry
                ks = pl.multiple_of(kj * FQ, FQ)
                kb = k_ref[0, pl.ds(ks, FQ), :]
                vb = v_ref[0, pl.ds(ks, FQ), :]
                sc = _dot_nt(qb, kb) + (cqb - ck_ref[0, kj])
                if masked:
                    sc = jnp.where(_causal_mask(FQ, FQ, 0, 0), sc, NEG)
                m_new = jnp.maximum(m, jnp.max(sc, axis=-1, keepdims=True))
                alpha = jnp.exp(m - m_new)
                p = jnp.exp(sc - m_new)
                l = alpha * l + jnp.sum(p, axis=-1, keepdims=True)
                acc = alpha * acc + _dot(p.astype(BF16), vb)
                return m_new, l, acc

            carry = (jnp.full((FQ, 1), NEG, F32), jnp.zeros((FQ, 1), F32), jnp.zeros((FQ, HEAD_DIM), F32))
            if qi > 0:
                carry = lax.fori_loop(0, qi, lambda kj, cr: block(kj, cr, False), carry)
            m, l, acc = block(qi, carry, True)
            o_ref[0, qs:qs + FQ, :] = acc / l
            lse_ref[0, qs:qs + FQ, :] = m + jnp.log(l)

    hblk = pl.BlockSpec((1, s, HEAD_DIM), lambda h: (h, 0, 0))
    cblk = pl.BlockSpec((1, s, 1), lambda h: (h, 0, 0))
    return _pcall(
        body, grid=(N_HEADS,),
        in_specs=[hblk, hblk, hblk, cblk, pl.BlockSpec((1, nb, 1, FQ), lambda h: (h, 0, 0, 0))],
        out_specs=[hblk, cblk],
        out_shape=[jax.ShapeDtypeStruct((N_HEADS, s, HEAD_DIM), F32), jax.ShapeDtypeStruct((N_HEADS, s, 1), F32)],
        operands=(q, k, v, cq, ck), name=name, comm=comm)


def _attn_bwd(q, k, v, cq, ck, o, lse, do, name, comm=None):
    s = q.shape[1]
    nq, nk = s // BQ, s // BK

    def body(q_ref, k_ref, v_ref, cq_ref, ck_ref, o_ref, lse_ref, do_ref, dq_ref, dk_ref, dv_ref, dcq_ref, dck_ref, delta):
        delta[...] = jnp.sum(do_ref[0] * o_ref[0], axis=-1, keepdims=True)
        dq_ref[...] = jnp.zeros_like(dq_ref)
        dcq_ref[...] = jnp.zeros_like(dcq_ref)
        for kj in range(nk):
            ks = kj * BK
            kb = k_ref[0, ks:ks + BK, :]
            vb = v_ref[0, ks:ks + BK, :]
            ckb = ck_ref[0, kj]

            def block(qs, nr, carry, masked):
                dk_acc, dv_acc, dck_acc = carry
                rows = pl.ds(qs, nr)
                qb = q_ref[0, rows, :]
                dob = do_ref[0, rows, :].astype(BF16)
                sc = _dot_nt(qb, kb) + (cq_ref[0, rows, :] - ckb)
                p = jnp.exp(sc - lse_ref[0, rows, :])
                if masked:
                    p = jnp.where(_causal_mask(nr, BK, 0, 0), p, 0.0)
                dp = _dot_nt(dob, vb)
                ds = p * (dp - delta[rows, :])
                ds_b = ds.astype(BF16)
                dv_acc = dv_acc + _dot_tn(p.astype(BF16), dob)
                dk_acc = dk_acc + _dot_tn(ds_b, qb)
                dq_ref[0, rows, :] += _dot(ds_b, kb) * SCALE
                dcq_ref[0, rows, :] += jnp.sum(ds, axis=-1, keepdims=True)
                dck_acc = dck_acc - jnp.sum(ds, axis=0, keepdims=True)
                return dk_acc, dv_acc, dck_acc

            carry = (jnp.zeros((BK, HEAD_DIM), F32), jnp.zeros((BK, HEAD_DIM), F32), jnp.zeros((1, BK), F32))
            carry = block(ks, BK, carry, True)
            below = ks + BK
            if below % BQ and below < s:
                carry = block(below, BK, carry, False)
                below += BK
            if below < s:
                carry = lax.fori_loop(below // BQ, nq, lambda qi, cr: block(pl.multiple_of(qi * BQ, BQ), BQ, cr, False), carry)
            dk_ref[0, ks:ks + BK, :] = carry[0]
            dv_ref[0, ks:ks + BK, :] = carry[1]
            dck_ref[0, kj] = carry[2]

    hblk = pl.BlockSpec((1, s, HEAD_DIM), lambda h: (h, 0, 0))
    cblk = pl.BlockSpec((1, s, 1), lambda h: (h, 0, 0))
    kblk = pl.BlockSpec((1, nk, 1, BK), lambda h: (h, 0, 0, 0))
    hshape = jax.ShapeDtypeStruct((N_HEADS, s, HEAD_DIM), F32)
    return _pcall(
        body, grid=(N_HEADS,),
        in_specs=[hblk, hblk, hblk, cblk, kblk, hblk, cblk, hblk],
        out_specs=[hblk, hblk, hblk, cblk, kblk],
        out_shape=[hshape, hshape, hshape, jax.ShapeDtypeStruct((N_HEADS, s, 1), F32),
                   jax.ShapeDtypeStruct((N_HEADS, nk, 1, BK), F32)],
        scratch_shapes=[pltpu.VMEM((s, 1), F32)],
        operands=(q, k, v, cq, ck, o, lse, do), name=name, comm=comm)


def _merge_fwd(x, ya, yc, yd, o, pm, pg, wb, wo, name, comm=None):
    s = x.shape[0]

    def body(x_ref, ya_ref, yc_ref, yd_ref, o_ref, bg_ref, pg_ref, wb_ref, wo_ref, xo_ref, yb_ref):
        o = jnp.concatenate([o_ref[h] for h in range(N_HEADS)], axis=1)
        yb = (o * _silu(bg_ref[...])).astype(BF16)
        yb_ref[...] = yb
        ys = (ya_ref[...], yb, yc_ref[...], yd_ref[...])
        merged = jnp.zeros((TMG, D_MODEL), F32)
        for n in range(N_BRANCH):
            merged = merged + _sig(pg_ref[:, n * D_MODEL:(n + 1) * D_MODEL]) * _dot(ys[n], wb_ref[n])
        xo_ref[...] = x_ref[...] + _dot(merged.astype(BF16), wo_ref[...])

    xt = pl.BlockSpec((TMG, D_MODEL), lambda i: (i, 0))
    yt = pl.BlockSpec((TMG, BR), lambda i: (i, 0))
    return _pcall(
        body, grid=(s // TMG,),
        in_specs=[xt, yt, yt, yt, pl.BlockSpec((N_HEADS, TMG, HEAD_DIM), lambda i: (0, i, 0)),
                  pl.BlockSpec((TMG, BR), lambda i: (i, O_BG // BR)), pl.BlockSpec((TMG, N_MERGE), lambda i: (i, 0)),
                  pl.BlockSpec((N_BRANCH, BR, D_MODEL), lambda i: (0, 0, 0)), pl.BlockSpec((D_MODEL, D_MODEL), lambda i: (0, 0))],
        out_specs=[xt, yt],
        out_shape=[jax.ShapeDtypeStruct((s, D_MODEL), F32), jax.ShapeDtypeStruct((s, BR), BF16)],
        operands=(x, ya, yc, yd, o, pm, pg, wb, wo), name=name, comm=comm)


def _merge_bwd(dx, ya, yb, yc, yd, o, pm, pg, wb, wo, name, comm=None):
    s = dx.shape[0]
    nt = s // TMG

    def body(dx_ref, ya_ref, yb_ref, yc_ref, yd_ref, o_ref, bg_ref, pg_ref, wb_ref, wo_ref,
             dpg_ref, dya_ref, do_ref, dbg_ref, dyc_ref, dyd_ref, dwb_ref, dwo_ref, dwb_acc, dwo_acc):
        i = pl.program_id(0)

        @pl.when(i == 0)
        def _():
            dwb_acc[...] = jnp.zeros_like(dwb_acc)
            dwo_acc[...] = jnp.zeros_like(dwo_acc)

        dxb = dx_ref[...].astype(BF16)
        dmerged = _dot_nt(dxb, wo_ref[...])
        ys = (ya_ref[...], yb_ref[...], yc_ref[...], yd_ref[...])
        dys = (dya_ref, None, dyc_ref, dyd_ref)
        merged = jnp.zeros((TMG, D_MODEL), F32)
        for n in range(N_BRANCH):
            gate = _sig(pg_ref[:, n * D_MODEL:(n + 1) * D_MODEL])
            pr = _dot(ys[n], wb_ref[n])
            merged = merged + gate * pr
            dpg_ref[:, n * D_MODEL:(n + 1) * D_MODEL] = (dmerged * pr * gate * (1.0 - gate)).astype(BF16)
            dpr = (gate * dmerged).astype(BF16)
            dwb_acc[n] += _dot_tn(ys[n], dpr)
            dyn = _dot_nt(dpr, wb_ref[n])
            if n == 1:
                bg = bg_ref[...]
                do = dyn * _silu(bg)
                for h in range(N_HEADS):
                    do_ref[h] = do[:, h * HEAD_DIM:(h + 1) * HEAD_DIM]
                dbg_ref[...] = dyn * jnp.concatenate([o_ref[h] for h in range(N_HEADS)], axis=1) * _dsilu(bg)
            else:
                dys[n][...] = dyn
        dwo_acc[...] += _dot_tn(merged.astype(BF16), dxb)

        @pl.when(i == nt - 1)
        def _():
            dwb_ref[...] = dwb_acc[...].astype(BF16)
            dwo_ref[...] = dwo_acc[...].astype(BF16)

    xt = pl.BlockSpec((TMG, D_MODEL), lambda i: (i, 0))
    yt = pl.BlockSpec((TMG, BR), lambda i: (i, 0))
    gt = pl.BlockSpec((TMG, N_MERGE), lambda i: (i, 0))
    wbs = pl.BlockSpec((N_BRANCH, BR, D_MODEL), lambda i: (0, 0, 0))
    wos = pl.BlockSpec((D_MODEL, D_MODEL), lambda i: (0, 0))
    yf = jax.ShapeDtypeStruct((s, BR), F32)
    ht = pl.BlockSpec((N_HEADS, TMG, HEAD_DIM), lambda i: (0, i, 0))
    return _pcall(
        body, grid=(nt,),
        in_specs=[xt, yt, yt, yt, yt, ht, pl.BlockSpec((TMG, BR), lambda i: (i, O_BG // BR)), gt, wbs, wos],
        out_specs=[gt, yt, ht, yt, yt, yt, wbs, wos],
        out_shape=[jax.ShapeDtypeStruct((s, N_MERGE), BF16), yf, jax.ShapeDtypeStruct((N_HEADS, s, HEAD_DIM), F32), yf, yf, yf,
                   jax.ShapeDtypeStruct((N_BRANCH, BR, D_MODEL), BF16), jax.ShapeDtypeStruct((D_MODEL, D_MODEL), BF16)],
        scratch_shapes=[pltpu.VMEM((N_BRANCH, BR, D_MODEL), F32), pltpu.VMEM((D_MODEL, D_MODEL), F32)],
        operands=(dx, ya, yb, yc, yd, o, pm, pg, wb, wo), name=name, comm=comm)


def _layer_fwd(x, wl, tag, attach=None):
    attach = attach or {}

    def riding(stage):
        comm, sink = attach.get(stage, (None, None))
        return comm, (sink or (lambda res: None))

    s = x.shape[0]
    comm, sink = riding("rms_fwd")
    h, res = _rms_fwd(x, wl["norm_g"], "rms_fwd" + tag, comm)
    sink(res)
    comm, sink = riding("proj_mix")
    pm, res = _mm_nn(h, wl["w_mix"], 1792, "proj_mix" + tag, comm)
    sink(res)
    comm, sink = riding("mix_fwd")
    (ya, yc, yd, q, k, v, cc), res = _mix_fwd(pm, wl, "mix_fwd" + tag, comm)
    sink(res)
    comm, sink = riding("proj_merge")
    pg, res = _mm_nn(h, wl["w_merge"], 2048, "proj_merge" + tag, comm)
    sink(res)
    pf, _ = _mm_nn(h, wl["w_f"], N_F, "proj_f" + tag, tm=4096)
    cum = _cum_fwd(pf, wl["f_bias"], "cum_fwd" + tag)
    cum_t = cum[:, :N_HEADS].T
    cq = cum_t.reshape(N_HEADS, s, 1)
    ck = cum_t.reshape(N_HEADS, s // BK, 1, BK)
    comm, sink = riding("attn_fwd")
    (o, lse), res = _attn_fwd(q, k, v, cq, cum_t.reshape(N_HEADS, s // FQ, 1, FQ), "attn_fwd" + tag, comm)
    sink(res)
    comm, sink = riding("merge_fwd")
    (x_next, yb), res = _merge_fwd(x, ya, yc, yd, o, pm, pg, wl["wb"], wl["wo"], "merge_fwd" + tag, comm)
    sink(res)
    saved = dict(x=x, h=h, pm=pm, pg=pg, pf=pf, cc=cc, ya=ya, yb=yb, yc=yc, yd=yd, q=q, k=k, v=v, cq=cq, ck=ck, o=o, lse=lse)
    return x_next, saved


def _blocks_rows(g):
    return g.reshape(N_DEV, g.shape[0] // N_DEV, g.shape[1])


def _blocks_cols(g):
    return g.reshape(N_BRANCH * BR, N_DEV, D_MODEL // N_DEV).transpose(1, 0, 2)


def _layer_bwd(dx_next, sv, wl, tag, dist, riding, extra_small):
    s = dx_next.shape[0]
    (dpg, dya, do, dbg, dyc, dyd, dwb, dwo), rode = _merge_bwd(
        dx_next, sv["ya"], sv["yb"], sv["yc"], sv["yd"], sv["o"], sv["pm"], sv["pg"], wl["wb"], wl["wo"], "merge_bwd" + tag,
        riding)
    dw_merge = _mm_tn(sv["h"], dpg, 2048, "dw_merge" + tag)
    early = [(_blocks_rows(dw_merge), False), (_blocks_cols(dwb), False), (_blocks_rows(dwo), False)] if dist else None
    (dq, dk, dv, dcq, dck), early_out = _attn_bwd(sv["q"], sv["k"], sv["v"], sv["cq"], sv["ck"], sv["o"], sv["lse"], do,
                                                  "attn_bwd" + tag, early)
    dck_cols = jnp.pad(dck.reshape(N_HEADS, s).T, ((0, 0), (0, N_F - N_HEADS)))
    dpf, dfb = _cum_bwd(dcq, dck_cols, sv["pf"], wl["f_bias"], "cum_bwd" + tag)
    dpm, dgw, dgb, vec, dscw, ddww = _mix_bwd(sv["pm"], sv["cc"], dya, dyc, dyd, (dq, dk, dv), dbg, wl, "mix_bwd" + tag)
    dw_mix = _mm_tn(sv["h"], dpm, 1792, "dw_mix" + tag)
    dw_f = _mm_tn(sv["h"], dpf, N_F, "dw_f" + tag, tm=4096)
    causal = jnp.tril(jnp.ones((CHUNK, CHUNK), bool))
    small = dict(
        f_bias=dfb.sum(0)[:N_HEADS],
        sgu_w=jnp.where(causal[None], dgw, 0.0),
        sgu_b=dgb.reshape(CHUNK, N_HEADS, HEAD_DIM).sum(-1).T,
        sgu_ln_g=vec[0], sgu_ln_b=vec[1], conf_dw_b=vec[2], conf_ln_g=vec[3], conf_ln_b=vec[4],
        short_conv_w=dscw.reshape(8, 8, BR).sum(1)[:SHORT_CONV],
        conf_dw_w=ddww.reshape(32, 8, BR).sum(1)[:CONF_CONV],
    )
    slab, spans = _pack([small[nm] for nm in SMALL[1:]])
    late = [(_blocks_rows(dw_mix), False), (_blocks_rows(dw_f), False), (slab, True)] if dist else None
    dh, late_out = _dh(dpm, dpg, dpf, wl["w_mix"], wl["w_merge"], wl["w_f"], "dh" + tag, late)
    dx, dng = _rms_bwd(dh, sv["x"], wl["norm_g"], dx_next, "rms_bwd" + tag)
    small["norm_g"] = dng.sum(0)
    grads = dict(small, w_mix=dw_mix, w_merge=dw_merge, w_f=dw_f, wb=dwb, wo=dwo)
    last_slab, last_spans = _pack([small["norm_g"]] + list(extra_small))
    return dx, grads, early_out, (late_out, spans), ([(last_slab, True)], last_spans), rode


def _prep_layer_small(norm_g, f_bias, sgu_w, sgu_b, sgu_ln_g, sgu_ln_b, conf_dw_b, conf_ln_g, conf_ln_b):
    causal = jnp.tril(jnp.ones((CHUNK, CHUNK), bool))
    gw = jnp.where(causal[None], sgu_w, 0.0)
    row = lambda a: a.reshape(1, -1)
    return dict(
        norm_g=row(norm_g),
        f_bias=jnp.pad(row(f_bias), ((0, 0), (0, N_F - N_HEADS))),
        gw=gw.astype(BF16), gwt=gw.transpose(0, 2, 1).astype(BF16),
        gb=jnp.repeat(sgu_b.T, HEAD_DIM, axis=1),
        sgu_ln_g=row(sgu_ln_g), sgu_ln_b=row(sgu_ln_b),
        conf_dw_b=row(conf_dw_b), conf_ln_g=row(conf_ln_g), conf_ln_b=row(conf_ln_b))


def _local_step(x, target, layers, final_g):
    saved = []
    for l in range(DEPTH):
        x, sv = _layer_fwd(x, layers[l], str(l))
        saved.append(sv)
    loss_p, dx, dfg = _loss_head(x, final_g.reshape(1, D_MODEL), target)
    grads = [None] * DEPTH
    for l in reversed(range(DEPTH)):
        dx, grads[l], _, _, _, _ = _layer_bwd(dx, saved[l], layers[l], str(l), False, None, [])
    return 0.5 / D_MODEL * jnp.sum(loss_p), dx, grads, dfg.sum(0)


def _sum8(a, name):
    _, r, c = a.shape
    tr = r
    while tr * c * a.dtype.itemsize * N_DEV > 4 * 1024 * 1024 and tr % 32 == 0:
        tr //= 2

    def body(a_ref, o_ref):
        acc = a_ref[0].astype(F32)
        for d in range(1, N_DEV):
            acc = acc + a_ref[d].astype(F32)
        o_ref[...] = acc

    return pl.pallas_call(
        body, grid=(r // tr,),
        in_specs=[pl.BlockSpec((N_DEV, tr, c), lambda i: (0, i, 0))],
        out_specs=pl.BlockSpec((tr, c), lambda i: (i, 0)),
        out_shape=jax.ShapeDtypeStruct((r, c), F32), compiler_params=_cp(("parallel",)), name=name)(a)


def _adamw(w, g, m, v, name):
    l, r, c = w.shape
    tr = r
    while tr * c * 4 > 1024 * 1024 and tr % 16 == 0:
        tr //= 2
    c1 = 1.0 - ADAM_B1 ** ADAM_STEP
    c2 = 1.0 - ADAM_B2 ** ADAM_STEP

    def body(w_ref, g_ref, m_ref, v_ref, d_ref, mo_ref, vo_ref):
        gv = g_ref[...]
        mn = ADAM_B1 * m_ref[...] + (1.0 - ADAM_B1) * gv
        vn = ADAM_B2 * v_ref[...] + (1.0 - ADAM_B2) * (gv * gv)
        mo_ref[...] = mn
        vo_ref[...] = vn
        d_ref[...] = -ADAM_LR * ((mn / c1) / (jnp.sqrt(vn / c2) + ADAM_EPS) + ADAM_WD * w_ref[...])

    blk = pl.BlockSpec((1, tr, c), lambda a, i: (a, i, 0))
    shp = jax.ShapeDtypeStruct((l, r, c), F32)
    return pl.pallas_call(
        body, grid=(l, r // tr), in_specs=[blk] * 4, out_specs=[blk] * 3, out_shape=[shp] * 3,
        compiler_params=_cp(("parallel", "parallel")), name=name)(w, g, m, v)


def _pack(parts):
    rows, spans, r = [], [], 0
    for p in parts:
        flat = p.reshape(-1)
        nr = -(-flat.shape[0] // 1024) * 8
        rows.append(jnp.pad(flat, (0, nr * 128 - flat.shape[0])).reshape(nr, 128))
        spans.append((r, nr, p.shape))
        r += nr
    return jnp.concatenate(rows, axis=0), spans


def _unpack(slab, spans):
    out = []
    for r, nr, shape in spans:
        size = math.prod(shape)
        out.append(slab[r:r + nr].reshape(-1)[:size].reshape(shape))
    return out


def _split_w_in(w):
    mix = jnp.concatenate([w[..., 0:1536], w[..., 1540:1796], w[..., 3332:3588], w[..., 1796:2820], w[..., 2820:3332]], axis=-1)
    return mix, w[..., 3588:7684], w[..., 1536:1540]


def _join_w_in(mix, merge, f):
    return jnp.concatenate([mix[..., 0:1536], f, mix[..., 1536:1792], mix[..., 2048:3072], mix[..., 3072:3584],
                            mix[..., 1792:2048], merge], axis=-1)


SMALL = ("norm_g", "f_bias", "sgu_w", "sgu_b", "sgu_ln_g", "sgu_ln_b", "short_conv_w", "conf_dw_w", "conf_dw_b",
         "conf_ln_g", "conf_ln_b")


def kernel(x, norm_g, w_in, f_bias, sgu_w, sgu_b, sgu_ln_g, sgu_ln_b, short_conv_w, conf_dw_w, conf_dw_b, conf_ln_g, conf_ln_b, w_branch, w_out, final_g, loss_target, m_norm_g, m_w_in, m_f_bias, m_sgu_w, m_sgu_b, m_sgu_ln_g, m_sgu_ln_b, m_short_conv_w, m_conf_dw_w, m_conf_dw_b, m_conf_ln_g, m_conf_ln_b, m_w_branch, m_w_out, m_final_g, v_norm_g, v_w_in, v_f_bias, v_sgu_w, v_sgu_b, v_sgu_ln_g, v_sgu_ln_b, v_short_conv_w, v_conf_dw_w, v_conf_dw_b, v_conf_ln_g, v_conf_ln_b, v_w_branch, v_w_out, v_final_g):
    me = 4 * lax.axis_index("x") + 2 * lax.axis_index("y") + lax.axis_index("c")
    rows = D_MODEL // N_DEV
    cshard = BR // N_DEV

    sh = []
    for l in range(DEPTH):
        mix, merge, f = _split_w_in(w_in[l])
        sh.append(dict(mix=mix.astype(BF16), merge=merge.astype(BF16),
                       f=jnp.pad(f, ((0, 0), (0, N_F - N_HEADS))).astype(BF16),
                       wb=w_branch[l].astype(BF16), wo=w_out[l].astype(BF16)))
    conv_slab, conv_spans = _pack([short_conv_w, conf_dw_w])
    layers = [_prep_layer_small(norm_g[l], f_bias[l], sgu_w[l], sgu_b[l], sgu_ln_g[l], sgu_ln_b[l],
                                conf_dw_b[l], conf_ln_g[l], conf_ln_b[l]) for l in range(DEPTH)]
    half = N_MERGE // 2
    merge_halves = [[] for _ in range(DEPTH)]

    def put_first(res):
        conv_full = [_unpack(res[2][d], conv_spans) for d in range(N_DEV)]
        scw_full = jnp.concatenate([cf[0] for cf in conv_full], axis=-1)
        dww_full = jnp.concatenate([cf[1] for cf in conv_full], axis=-1)
        for l in range(DEPTH):
            layers[l].update(scw=jnp.pad(scw_full[l], ((0, 8 - SHORT_CONV), (0, 0))),
                             dww=jnp.pad(dww_full[l], ((0, 32 - CONF_CONV), (0, 0))))
        layers[0].update(w_mix=res[0].reshape(D_MODEL, N_MIX), w_f=res[1].reshape(D_MODEL, N_F))

    def put_in(l):
        def sink(res):
            layers[l].update(w_mix=res[0].reshape(D_MODEL, N_MIX), w_f=res[1].reshape(D_MODEL, N_F))
            put_merge_half(l)(res[2:])
        return sink

    def put_merge_half(l):
        def sink(res):
            merge_halves[l].append(res[0])
            if len(merge_halves[l]) == 2:
                layers[l].update(w_merge=jnp.concatenate(merge_halves[l], axis=-1).reshape(D_MODEL, N_MERGE))
        return sink

    def put_out(l):
        def sink(res):
            layers[l].update(wb=res[0].transpose(1, 2, 0, 3).reshape(N_BRANCH, BR, D_MODEL), wo=res[1].reshape(D_MODEL, D_MODEL))
        return sink

    attach0 = {
        "rms_fwd": ([(sh[0]["mix"], True), (sh[0]["f"], True), (conv_slab, True)], put_first),
        "proj_mix": ([(sh[0]["merge"][:, :half], True)], put_merge_half(0)),
        "mix_fwd": ([(sh[0]["merge"][:, half:], True)], put_merge_half(0)),
        "proj_merge": ([(sh[0]["wb"], True), (sh[0]["wo"], True)], put_out(0)),
        "attn_fwd": ([(sh[1]["mix"], True), (sh[1]["f"], True), (sh[1]["merge"][:, :half], True)], put_in(1)),
        "merge_fwd": ([(sh[1]["merge"][:, half:], True)], put_merge_half(1)),
    }
    attach1 = {"proj_mix": ([(sh[1]["wb"], True), (sh[1]["wo"], True)], put_out(1))}

    xs = x[0]
    xs, sv0 = _layer_fwd(xs, layers[0], "0", attach0)
    xs, sv1 = _layer_fwd(xs, layers[1], "1", attach1)
    loss_p, dx, dfg = _loss_head(xs, final_g.reshape(1, D_MODEL), loss_target[0])
    loss_local = (0.5 / D_MODEL * jnp.sum(loss_p)).reshape(1)
    dx, g1, early1, (late1, spans1), (last1, lspans1), _ = _layer_bwd(dx, sv1, layers[1], "1", True, None, [dfg.sum(0)])
    dx, g0, early0, (late0, spans0), (last0, lspans0), last1_out = _layer_bwd(dx, sv0, layers[0], "0", True, last1, [loss_local])
    last0_out = _exchange(last0, "gather_last")

    red, small = [], []
    for l, (early, late, spans, last, lspans) in enumerate(((early0, late0, spans0, last0_out, lspans0),
                                                            (early1, late1, spans1, last1_out, lspans1))):
        t = str(l)
        red.append(dict(merge=_sum8(early[0], "sum_merge" + t), wb=_sum8(early[1], "sum_wb" + t), wo=_sum8(early[2], "sum_wo" + t),
                        mix=_sum8(late[0], "sum_mix" + t), f=_sum8(late[1], "sum_f" + t)))
        keys = ["norm_g", "final_g" if l == DEPTH - 1 else "loss"] + list(SMALL[1:])
        small.append(dict(zip(keys, _unpack(_sum8(last[0], "sum_last" + t), lspans)
                              + _unpack(_sum8(late[2], "sum_small" + t), spans))))
    gs = {nm: jnp.stack([small[l][nm] for l in range(DEPTH)]) for nm in SMALL}
    gs["final_g"] = small[DEPTH - 1]["final_g"]
    loss = small[0]["loss"][0]
    gs["short_conv_w"] = lax.dynamic_slice_in_dim(gs["short_conv_w"], me * cshard, cshard, axis=2)
    gs["conf_dw_w"] = lax.dynamic_slice_in_dim(gs["conf_dw_w"], me * cshard, cshard, axis=2)
    g_w_in = jnp.stack([_join_w_in(red[l]["mix"], red[l]["merge"], red[l]["f"][:, :N_HEADS]) for l in range(DEPTH)])
    g_w_branch = jnp.stack([red[l]["wb"].reshape(N_BRANCH, BR, rows) for l in range(DEPTH)])
    g_w_out = jnp.stack([red[l]["wo"] for l in range(DEPTH)])

    d_w_in, nm_w_in, nv_w_in = _adamw(w_in, g_w_in, m_w_in, v_w_in, "adamw_w_in")
    flat = lambda a: a.reshape(DEPTH, N_BRANCH * BR, rows)
    d_w_branch, nm_w_branch, nv_w_branch = (a.reshape(w_branch.shape) for a in _adamw(
        flat(w_branch), flat(g_w_branch), flat(m_w_branch), flat(v_w_branch), "adamw_w_branch"))
    d_w_out, nm_w_out, nv_w_out = _adamw(w_out, g_w_out, m_w_out, v_w_out, "adamw_w_out")
    names = SMALL + ("final_g",)
    ws = dict(zip(names, (norm_g, f_bias, sgu_w, sgu_b, sgu_ln_g, sgu_ln_b, short_conv_w, conf_dw_w, conf_dw_b, conf_ln_g,
                          conf_ln_b, final_g)))
    ms = dict(zip(names, (m_norm_g, m_f_bias, m_sgu_w, m_sgu_b, m_sgu_ln_g, m_sgu_ln_b, m_short_conv_w, m_conf_dw_w,
                          m_conf_dw_b, m_conf_ln_g, m_conf_ln_b, m_final_g)))
    vs = dict(zip(names, (v_norm_g, v_f_bias, v_sgu_w, v_sgu_b, v_sgu_ln_g, v_sgu_ln_b, v_short_conv_w, v_conf_dw_w,
                          v_conf_dw_b, v_conf_ln_g, v_conf_ln_b, v_final_g)))
    w_slab, spans = _pack([ws[nm] for nm in names])
    g_slab, _ = _pack([gs[nm] for nm in names])
    m_slab, _ = _pack([ms[nm] for nm in names])
    v_slab, _ = _pack([vs[nm] for nm in names])
    d_s, nm_s, nv_s = (dict(zip(names, _unpack(a[0], spans))) for a in _adamw(w_slab[None], g_slab[None], m_slab[None],
                                                                              v_slab[None], "adamw_small"))

    def ordered(small, w_in_v, w_branch_v, w_out_v):
        return [small["norm_g"], w_in_v, small["f_bias"], small["sgu_w"], small["sgu_b"], small["sgu_ln_g"],
                small["sgu_ln_b"], small["short_conv_w"], small["conf_dw_w"], small["conf_dw_b"], small["conf_ln_g"],
                small["conf_ln_b"], w_branch_v, w_out_v, small["final_g"]]

    return (loss, dx[None], *ordered(gs, g_w_in, g_w_branch, g_w_out), *ordered(d_s, d_w_in, d_w_branch, d_w_out),
            *ordered(nm_s, nm_w_in, nm_w_branch, nm_w_out), *ordered(nv_s, nv_w_in, nv_w_branch, nv_w_out))
```

```python
import functools
import math

import jax
import jax.numpy as jnp
from jax import lax
from jax.experimental import pallas as pl
from jax.experimental.pallas import tpu as pltpu

F32 = jnp.float32
BF16 = jnp.bfloat16

D_MODEL = 1024
DEPTH = 2
N_BRANCH = 4
BR = 256
N_HEADS = 4
HEAD_DIM = 64
CHUNK = 128
SHORT_CONV = 3
CONF_CONV = 31
EPS = 1e-6
N_DEV = 8

ADAM_LR = 0.001
ADAM_B1 = 0.9
ADAM_B2 = 0.999
ADAM_EPS = 1e-08
ADAM_WD = 0.01
ADAM_STEP = 10

O_UV, O_AG, O_QKV, O_BG, O_DG, O_CIN, O_CG, O_GLU = 0, 512, 768, 1536, 1792, 2048, 2816, 3072
N_MIX = 3584
N_MERGE = N_BRANCH * D_MODEL
N_F = 128
IN_COLS = 7684
HALO = 32
TM = 512
TMG = 256
FQ = 1024
BQ = 1024
BK = 512
SUB = 64
CUMB = 512
MERGE_COLS = 256
VMEM_LIMIT = 56 * 1024 * 1024
NEG = -1e30
SCALE = 1.0 / math.sqrt(HEAD_DIM)
GELU_K = math.sqrt(2.0 / math.pi)


def _cp(sem=None):
    return pltpu.CompilerParams(dimension_semantics=sem, vmem_limit_bytes=VMEM_LIMIT)


PEER_ORDER = (6, 4, 2, 7, 5, 3, 1)
RELAYED = (3, 5, 7)


def _xchg(cin, cout, send, recv, loc, modes, phase):
    x, y, c = lax.axis_index("x"), lax.axis_index("y"), lax.axis_index("c")
    me = 4 * x + 2 * y + c

    def peer_of(kk):
        px, py, pc = lax.rem(x + (kk >> 2 & 1), 2), lax.rem(y + (kk >> 1 & 1), 2), lax.rem(c + (kk & 1), 2)
        return (px, py, pc), 4 * px + 2 * py + pc

    def remote(src, dst, a, kk, pid):
        return pltpu.make_async_remote_copy(src_ref=src, dst_ref=dst, send_sem=send.at[a, kk], recv_sem=recv.at[a, kk],
                                            device_id=pid, device_id_type=pl.DeviceIdType.MESH)

    def outgoing(a, kk):
        if modes[a] and kk in RELAYED:
            _, origin = peer_of(kk - 1)
            return remote(cout[a].at[origin], cout[a].at[origin], a, kk, peer_of(1)[0])
        pid, peer = peer_of(kk)
        return remote(cin[a] if modes[a] else cin[a].at[peer], cout[a].at[me], a, kk, pid)

    def arrival(a, kk):
        _, peer = peer_of(kk)
        return remote(cout[a].at[peer], cout[a].at[peer], a, kk, (x, y, c))

    if phase == "relay":
        for kk in RELAYED:
            for a, gather in enumerate(modes):
                if gather:
                    arrival(a, kk - 1).wait_recv()
                    outgoing(a, kk).start()
        return
    for a, gather in enumerate(modes):
        cp = pltpu.make_async_copy(cin[a] if gather else cin[a].at[me], cout[a].at[me], loc.at[a])
        if phase == "start":
            cp.start()
        else:
            cp.wait()
    if phase == "start":
        for kk in PEER_ORDER:
            for a, gather in enumerate(modes):
                if not (gather and kk in RELAYED):
                    outgoing(a, kk).start()
        return
    for kk in PEER_ORDER:
        for a in range(len(modes)):
            outgoing(a, kk).wait_send()
    for kk in PEER_ORDER:
        for a, gather in enumerate(modes):
            if not (gather and kk + 1 in RELAYED):
                arrival(a, kk).wait_recv()


def _xchg_shapes(comm):
    return [jax.ShapeDtypeStruct((N_DEV,) + tuple(a.shape[(0 if gather else 1):]), a.dtype) for a, gather in comm]


def _xchg_sems(n):
    return [pltpu.SemaphoreType.DMA((n, N_DEV)), pltpu.SemaphoreType.DMA((n, N_DEV)), pltpu.SemaphoreType.DMA((n,))]


def _exchange(comm, name):
    n = len(comm)
    modes = [g for _, g in comm]

    def body(*refs):
        cin, cout, (send, recv, loc) = refs[:n], refs[n:2 * n], refs[2 * n:]
        for phase in ("start", "relay", "finish"):
            _xchg(cin, cout, send, recv, loc, modes, phase)

    anyspec = pl.BlockSpec(memory_space=pl.ANY)
    return pl.pallas_call(
        body, in_specs=[anyspec] * n, out_specs=[anyspec] * n, out_shape=_xchg_shapes(comm),
        scratch_shapes=_xchg_sems(n), name=name)(*[a for a, _ in comm])


def _pcall(body, *, grid, in_specs, out_specs, out_shape, operands, name, scratch_shapes=(), comm=None):
    if not comm:
        outs = pl.pallas_call(
            body, grid=grid, in_specs=in_specs, out_specs=out_specs, out_shape=out_shape, scratch_shapes=list(scratch_shapes),
            compiler_params=_cp(("arbitrary",) * len(grid)), name=name)(*operands)
        return list(outs), []
    n, nin, nout, nsc = len(comm), len(operands), len(out_shape), len(scratch_shapes)
    modes = [g for _, g in comm]

    def wrapped(*refs):
        ins, cin = refs[:nin], refs[nin:nin + n]
        outs, cout = refs[nin + n:nin + n + nout], refs[nin + n + nout:nin + 2 * n + nout]
        scratch = refs[nin + 2 * n + nout:]
        own, (send, recv, loc) = scratch[:nsc], scratch[nsc:]
        ids = [pl.program_id(d) for d in range(len(grid))]
        first = functools.reduce(jnp.logical_and, [i == 0 for i in ids])
        last = functools.reduce(jnp.logical_and, [i == g - 1 for i, g in zip(ids, grid)])

        @pl.when(first)
        def _():
            _xchg(cin, cout, send, recv, loc, modes, "start")

        @pl.when(last)
        def _():
            _xchg(cin, cout, send, recv, loc, modes, "relay")

        body(*ins, *outs, *own)

        @pl.when(last)
        def _():
            _xchg(cin, cout, send, recv, loc, modes, "finish")

    anyspec = pl.BlockSpec(memory_space=pl.ANY)
    res = pl.pallas_call(
        wrapped, grid=grid, in_specs=list(in_specs) + [anyspec] * n, out_specs=list(out_specs) + [anyspec] * n,
        out_shape=list(out_shape) + _xchg_shapes(comm), scratch_shapes=list(scratch_shapes) + _xchg_sems(n),
        compiler_params=_cp(("arbitrary",) * len(grid)), name=name)(*operands, *[a for a, _ in comm])
    return list(res[:nout]), list(res[nout:])


def _sig(x):
    return 0.5 * jnp.tanh(0.5 * x) + 0.5


def _silu(x):
    return x * _sig(x)


def _dsilu(x):
    s = _sig(x)
    return s * (1.0 + x * (1.0 - s))


def _gelu(x):
    return 0.5 * x * (1.0 + jnp.tanh(GELU_K * (x + 0.044715 * x * x * x)))


def _dgelu(x):
    t = jnp.tanh(GELU_K * (x + 0.044715 * x * x * x))
    return 0.5 * (1.0 + t) + 0.5 * x * (1.0 - t * t) * GELU_K * (1.0 + 3.0 * 0.044715 * x * x)


def _ln_hat(x):
    mu = jnp.mean(x, axis=-1, keepdims=True)
    xc = x - mu
    rs = lax.rsqrt(jnp.mean(xc * xc, axis=-1, keepdims=True) + EPS)
    return xc * rs, rs


def _ln_bwd(dhat, hat, rs):
    return rs * (dhat - jnp.mean(dhat, axis=-1, keepdims=True) - hat * jnp.mean(dhat * hat, axis=-1, keepdims=True))


def _dot(a, b):
    return jnp.dot(a, b, preferred_element_type=F32)


def _dot_nt(a, b):
    return lax.dot_general(a, b, (((1,), (1,)), ((), ())), preferred_element_type=F32)


def _dot_tn(a, b):
    return lax.dot_general(a, b, (((0,), (0,)), ((), ())), preferred_element_type=F32)


def _fold8(x):
    acc = x[0:8]
    for r in range(1, x.shape[0] // 8):
        acc = acc + x[8 * r:8 * r + 8]
    return acc


def _rms_fwd(x, g, name, comm=None):
    s = x.shape[0]

    def body(x_ref, g_ref, h_ref):
        xv = x_ref[...]
        r = lax.rsqrt(jnp.mean(xv * xv, axis=-1, keepdims=True) + EPS)
        h_ref[...] = (xv * r * g_ref[...]).astype(BF16)

    (h,), couts = _pcall(
        body, grid=(s // TM,),
        in_specs=[pl.BlockSpec((TM, D_MODEL), lambda i: (i, 0)), pl.BlockSpec((1, D_MODEL), lambda i: (0, 0))],
        out_specs=[pl.BlockSpec((TM, D_MODEL), lambda i: (i, 0))],
        out_shape=[jax.ShapeDtypeStruct((s, D_MODEL), BF16)], operands=(x, g), name=name, comm=comm)
    return h, couts


def _rms_bwd(dh, x, g, dx_next, name):
    s = x.shape[0]

    def body(dh_ref, x_ref, g_ref, dxn_ref, dx_ref, dg_ref):
        i = pl.program_id(0)
        xv = x_ref[...]
        r = lax.rsqrt(jnp.mean(xv * xv, axis=-1, keepdims=True) + EPS)
        xn = xv * r
        dhv = dh_ref[...]
        dxn = dhv * g_ref[...]
        dx_ref[...] = dxn_ref[...] + r * (dxn - xn * jnp.mean(dxn * xn, axis=-1, keepdims=True))

        @pl.when(i == 0)
        def _():
            dg_ref[...] = jnp.zeros_like(dg_ref)

        dg_ref[...] += _fold8(dhv * xn)

    tile = pl.BlockSpec((TM, D_MODEL), lambda i: (i, 0))
    return pl.pallas_call(
        body, grid=(s // TM,),
        in_specs=[tile, tile, pl.BlockSpec((1, D_MODEL), lambda i: (0, 0)), tile],
        out_specs=[tile, pl.BlockSpec((8, D_MODEL), lambda i: (0, 0))],
        out_shape=[jax.ShapeDtypeStruct((s, D_MODEL), F32), jax.ShapeDtypeStruct((8, D_MODEL), F32)],
        compiler_params=_cp(("arbitrary",)), name=name)(dh, x, g, dx_next)


def _loss_head(x, g, target):
    s = x.shape[0]

    def body(x_ref, g_ref, t_ref, loss_ref, dx_ref, dg_ref):
        i = pl.program_id(0)
        xv = x_ref[...]
        r = lax.rsqrt(jnp.mean(xv * xv, axis=-1, keepdims=True) + EPS)
        xn = xv * r
        err = xn * g_ref[...] - t_ref[...]
        dy = err * (1.0 / D_MODEL)
        dxn = dy * g_ref[...]
        dx_ref[...] = r * (dxn - xn * jnp.mean(dxn * xn, axis=-1, keepdims=True))

        @pl.when(i == 0)
        def _():
            dg_ref[...] = jnp.zeros_like(dg_ref)
            loss_ref[...] = jnp.zeros_like(loss_ref)

        dg_ref[...] += _fold8(dy * xn)
        loss_ref[...] += _fold8(err * err)

    tile = pl.BlockSpec((TM, D_MODEL), lambda i: (i, 0))
    acc = pl.BlockSpec((8, D_MODEL), lambda i: (0, 0))
    return pl.pallas_call(
        body, grid=(s // TM,),
        in_specs=[tile, pl.BlockSpec((1, D_MODEL), lambda i: (0, 0)), tile],
        out_specs=[acc, tile, acc],
        out_shape=[jax.ShapeDtypeStruct((8, D_MODEL), F32), jax.ShapeDtypeStruct((s, D_MODEL), F32),
                   jax.ShapeDtypeStruct((8, D_MODEL), F32)],
        compiler_params=_cp(("arbitrary",)), name="loss_head")(x, g, target)


def _mm_nn(a, b, tn, name, comm=None, tm=1024):
    m, k = a.shape
    n = b.shape[1]
    tm = min(tm, m)

    def body(a_ref, b_ref, o_ref):
        o_ref[...] = _dot(a_ref[...], b_ref[...])

    (out,), couts = _pcall(
        body, grid=(n // tn, m // tm),
        in_specs=[pl.BlockSpec((tm, k), lambda j, i: (i, 0)), pl.BlockSpec((k, tn), lambda j, i: (0, j))],
        out_specs=[pl.BlockSpec((tm, tn), lambda j, i: (i, j))],
        out_shape=[jax.ShapeDtypeStruct((m, n), F32)], operands=(a, b), name=name, comm=comm)
    return out, couts


def _dh(dpm, dpg, dpf, w_mix, w_merge, w_f, name, comm=None):
    s = dpm.shape[0]
    tm = 1024 if s % 1024 == 0 else 512
    tk1, tk2 = 896, 1024
    n1, n2 = N_MIX // tk1, N_MERGE // tk2

    def body(dpm_ref, dpg_ref, dpf_ref, wm_ref, wg_ref, wf_ref, o_ref):
        j = pl.program_id(1)

        @pl.when(j == 0)
        def _():
            o_ref[...] = _dot_nt(dpf_ref[...], wf_ref[...])

        @pl.when(j < n1)
        def _():
            o_ref[...] += _dot_nt(dpm_ref[...], wm_ref[...])

        @pl.when(j >= n1)
        def _():
            o_ref[...] += _dot_nt(dpg_ref[...], wg_ref[...])

    mix_j = lambda j: jnp.minimum(j, n1 - 1)
    merge_j = lambda j: jnp.maximum(j - n1, 0)
    (out,), couts = _pcall(
        body, grid=(s // tm, n1 + n2),
        in_specs=[pl.BlockSpec((tm, tk1), lambda i, j: (i, mix_j(j))), pl.BlockSpec((tm, tk2), lambda i, j: (i, merge_j(j))),
                  pl.BlockSpec((tm, N_F), lambda i, j: (i, 0)),
                  pl.BlockSpec((D_MODEL, tk1), lambda i, j: (0, mix_j(j))), pl.BlockSpec((D_MODEL, tk2), lambda i, j: (0, merge_j(j))),
                  pl.BlockSpec((D_MODEL, N_F), lambda i, j: (0, 0))],
        out_specs=[pl.BlockSpec((tm, D_MODEL), lambda i, j: (i, 0))],
        out_shape=[jax.ShapeDtypeStruct((s, D_MODEL), F32)], operands=(dpm, dpg, dpf, w_mix, w_merge, w_f), name=name, comm=comm)
    return out, couts


def _mm_tn(a, d, tn, name, tm=1024):
    m, k = a.shape
    n = d.shape[1]
    tm = min(tm, m)
    nm = m // tm

    def body(a_ref, d_ref, o_ref, acc):
        i = pl.program_id(1)

        @pl.when(i == 0)
        def _():
            acc[...] = jnp.zeros_like(acc)

        acc[...] += _dot_tn(a_ref[...], d_ref[...])

        @pl.when(i == nm - 1)
        def _():
            o_ref[...] = acc[...].astype(BF16)

    return pl.pallas_call(
        body, grid=(n // tn, nm),
        in_specs=[pl.BlockSpec((tm, k), lambda j, i: (i, 0)), pl.BlockSpec((tm, tn), lambda j, i: (i, j))],
        out_specs=pl.BlockSpec((k, tn), lambda j, i: (0, j)),
        out_shape=jax.ShapeDtypeStruct((k, n), BF16), scratch_shapes=[pltpu.VMEM((k, tn), F32)],
        compiler_params=_cp(("parallel", "arbitrary")), name=name)(a, d)


def _lane_head():
    return lax.broadcasted_iota(jnp.int32, (1, BR), 1) // HEAD_DIM


def _gmlp_chunk_fwd(p_ref, r0, gw_ref, gb_ref, lg, lb):
    uv = p_ref[r0:r0 + CHUNK, O_UV:O_UV + 2 * BR]
    u = _gelu(uv[:, :BR])
    vhat, rs = _ln_hat(_gelu(uv[:, BR:]))
    vn = (vhat * lg + lb).astype(BF16)
    head = _lane_head()
    mixed = gb_ref[...]
    for h in range(N_HEADS):
        mixed = mixed + jnp.where(head == h, _dot(gw_ref[h], vn), 0.0)
    return uv, u, vhat, rs, vn, mixed


def _tap_groups(k_width):
    groups = []
    for b in range(8):
        taps = [(d // 8, k_width - 1 - d) for d in range(b, k_width, 8)]
        if taps:
            groups.append((b, taps))
    return groups


def _causal_taps(buf, off, r0, nr, k_width):
    lead = 8 * ((k_width - 1) // 8 + 1)
    win = buf[off + r0 - lead:off + r0 + nr, :]
    for b, taps in _tap_groups(k_width):
        shifted = win if b == 0 else pltpu.roll(win, b, 0)
        for a, k in taps:
            yield k, shifted[lead - 8 * a:lead - 8 * a + nr]


def _anticausal_taps(buf, r0, nr, k_width):
    lead = 8 * ((k_width - 1) // 8 + 1)
    win = buf[r0:r0 + nr + lead, :]
    for b, taps in _tap_groups(k_width):
        shifted = win if b == 0 else pltpu.roll(win, nr + lead - b, 0)
        for a, k in taps:
            yield k, shifted[8 * a:8 * a + nr]


def _conv_sub_blocks(rows):
    out = [(r, SUB) for r in range(0, rows - rows % SUB, SUB)]
    if rows % SUB:
        out.append((rows - rows % SUB, rows % SUB))
    return out


def _mix_fwd(pm, wl, name, comm=None):
    s = pm.shape[0]
    nt = s // TM

    def body(p_ref, ph_ref, gw_ref, gb_ref, lg_ref, lb_ref, scw_ref, dww_ref, dwb_ref, clg_ref, clb_ref,
             ya_ref, yc_ref, yd_ref, q_ref, k_ref, v_ref, cc_ref, zbuf, hbuf):
        i = pl.program_id(0)
        for h in range(N_HEADS):
            c0 = O_QKV + h * HEAD_DIM
            q_ref[h] = (p_ref[:, c0:c0 + HEAD_DIM] * SCALE).astype(BF16)
            k_ref[h] = p_ref[:, c0 + BR:c0 + BR + HEAD_DIM].astype(BF16)
            v_ref[h] = p_ref[:, c0 + 2 * BR:c0 + 2 * BR + HEAD_DIM].astype(BF16)
        lg = lg_ref[...]
        lb = lb_ref[...]
        for c in range(TM // CHUNK):
            r0 = c * CHUNK
            _, u, _, _, _, mixed = _gmlp_chunk_fwd(p_ref, r0, gw_ref, gb_ref, lg, lb)
            ag = p_ref[r0:r0 + CHUNK, O_AG:O_AG + BR]
            ya_ref[r0:r0 + CHUNK, :] = (u * mixed * _silu(ag)).astype(BF16)

        first = i > 0
        zbuf[0:HALO, :] = jnp.where(first, ph_ref[:, O_CIN + BR:O_CIN + 2 * BR] * ph_ref[:, O_CIN + 2 * BR:O_CIN + 3 * BR], 0.0)
        zbuf[HALO:HALO + TM, :] = p_ref[:, O_CIN + BR:O_CIN + 2 * BR] * p_ref[:, O_CIN + 2 * BR:O_CIN + 3 * BR]
        hbuf[0:HALO, :] = jnp.where(first, ph_ref[:, O_GLU:O_GLU + BR] * _sig(ph_ref[:, O_GLU + BR:O_GLU + 2 * BR]), 0.0)
        hbuf[HALO:HALO + TM, :] = p_ref[:, O_GLU:O_GLU + BR] * _sig(p_ref[:, O_GLU + BR:O_GLU + 2 * BR])

        clg = clg_ref[...]
        clb = clb_ref[...]
        for r0, nr in _conv_sub_blocks(TM):
            yc = jnp.zeros((nr, BR), F32)
            for k, zk in _causal_taps(zbuf, HALO, r0, nr, SHORT_CONV):
                yc = yc + scw_ref[k:k + 1, :] * zk
            bgate = p_ref[r0:r0 + nr, O_CIN:O_CIN + BR]
            cg = p_ref[r0:r0 + nr, O_CG:O_CG + BR]
            yc_ref[r0:r0 + nr, :] = (bgate * yc * _silu(cg)).astype(BF16)

            cc = jnp.zeros((nr, BR), F32) + dwb_ref[...]
            for k, hk in _causal_taps(hbuf, HALO, r0, nr, CONF_CONV):
                cc = cc + dww_ref[k:k + 1, :] * hk
            cc_ref[r0:r0 + nr, :] = cc
            chat, _ = _ln_hat(cc)
            dg = p_ref[r0:r0 + nr, O_DG:O_DG + BR]
            yd_ref[r0:r0 + nr, :] = (_silu(chat * clg + clb) * _silu(dg)).astype(BF16)

    full = lambda shape: pl.BlockSpec(shape, lambda i: tuple(0 for _ in shape))
    ytile = pl.BlockSpec((TM, BR), lambda i: (i, 0))
    yshape = jax.ShapeDtypeStruct((s, BR), BF16)
    htile = pl.BlockSpec((N_HEADS, TM, HEAD_DIM), lambda i: (0, i, 0))
    hshape = jax.ShapeDtypeStruct((N_HEADS, s, HEAD_DIM), BF16)
    return _pcall(
        body, grid=(nt,),
        in_specs=[pl.BlockSpec((TM, N_MIX), lambda i: (i, 0)),
                  pl.BlockSpec((HALO, N_MIX), lambda i: (jnp.maximum(i * (TM // HALO) - 1, 0), 0)),
                  full((N_HEADS, CHUNK, CHUNK)), full((CHUNK, BR)), full((1, BR)), full((1, BR)),
                  full((8, BR)), full((32, BR)), full((1, BR)), full((1, BR)), full((1, BR))],
        out_specs=[ytile, ytile, ytile, htile, htile, htile, ytile],
        out_shape=[yshape, yshape, yshape, hshape, hshape, hshape, jax.ShapeDtypeStruct((s, BR), F32)],
        scratch_shapes=[pltpu.VMEM((HALO + TM, BR), F32), pltpu.VMEM((HALO + TM, BR), F32)],
        name=name, comm=comm, operands=(
            pm, pm, wl["gw"], wl["gb"], wl["sgu_ln_g"], wl["sgu_ln_b"], wl["scw"], wl["dww"], wl["conf_dw_b"],
            wl["conf_ln_g"], wl["conf_ln_b"]))


def _mix_bwd(pm, cc, dya, dyc, dyd, dqkv, dbg, wl, name):
    s = pm.shape[0]
    nt = s // TM
    ext = TM + HALO

    def body(p_ref, ph_ref, pn_ref, cc_ref, ccn_ref, dya_ref, dyc_ref, dycn_ref, dyd_ref, dydn_ref, dq_ref, dk_ref, dv_ref, dbg_ref,
             gw_ref, gwt_ref, gb_ref, lg_ref, lb_ref, scw_ref, dww_ref, dwb_ref, clg_ref, clb_ref,
             dp_ref, dgw_ref, dgb_ref, vec_ref, dscw_ref, ddww_ref, zbuf, dcb, hbuf, dcc):
        i = pl.program_id(0)

        @pl.when(i == 0)
        def _():
            dgw_ref[...] = jnp.zeros_like(dgw_ref)
            dgb_ref[...] = jnp.zeros_like(dgb_ref)
            vec_ref[...] = jnp.zeros_like(vec_ref)
            dscw_ref[...] = jnp.zeros_like(dscw_ref)
            ddww_ref[...] = jnp.zeros_like(ddww_ref)

        lg = lg_ref[...]
        lb = lb_ref[...]
        head = _lane_head()
        d_lg = jnp.zeros((1, BR), F32)
        d_lb = jnp.zeros((1, BR), F32)
        for c in range(TM // CHUNK):
            r0 = c * CHUNK
            uv, u, vhat, rs, vn, mixed = _gmlp_chunk_fwd(p_ref, r0, gw_ref, gb_ref, lg, lb)
            ag = p_ref[r0:r0 + CHUNK, O_AG:O_AG + BR]
            dy = dya_ref[r0:r0 + CHUNK, :]
            sa = _silu(ag)
            du = dy * mixed * sa
            dmx = dy * u * sa
            dp_ref[r0:r0 + CHUNK, O_AG:O_AG + BR] = (dy * u * mixed * _dsilu(ag)).astype(BF16)
            dgb_ref[...] += dmx
            dmx_b = dmx.astype(BF16)
            dvn = jnp.zeros((CHUNK, BR), F32)
            for h in range(N_HEADS):
                sel = head == h
                dgw_ref[h] += _dot_nt(jnp.where(sel, dmx, 0.0).astype(BF16), vn)
                dvn = dvn + jnp.where(sel, _dot(gwt_ref[h], dmx_b), 0.0)
            d_lg = d_lg + jnp.sum(dvn * vhat, axis=0, keepdims=True)
            d_lb = d_lb + jnp.sum(dvn, axis=0, keepdims=True)
            dv0 = _ln_bwd(dvn * lg, vhat, rs)
            dp_ref[r0:r0 + CHUNK, O_UV:O_UV + BR] = (du * _dgelu(uv[:, :BR])).astype(BF16)
            dp_ref[r0:r0 + CHUNK, O_UV + BR:O_UV + 2 * BR] = (dv0 * _dgelu(uv[:, BR:])).astype(BF16)
        vec_ref[0:1, :] += d_lg
        vec_ref[1:2, :] += d_lb

        for j, g_ref in enumerate((dq_ref, dk_ref, dv_ref)):
            dp_ref[:, O_QKV + j * BR:O_QKV + (j + 1) * BR] = jnp.concatenate(
                [g_ref[h] for h in range(N_HEADS)], axis=1).astype(BF16)
        dp_ref[:, O_BG:O_BG + BR] = dbg_ref[...].astype(BF16)

        first = i > 0
        last = i < nt - 1
        zbuf[0:HALO, :] = jnp.where(first, ph_ref[:, O_CIN + BR:O_CIN + 2 * BR] * ph_ref[:, O_CIN + 2 * BR:O_CIN + 3 * BR], 0.0)
        zbuf[HALO:HALO + TM, :] = p_ref[:, O_CIN + BR:O_CIN + 2 * BR] * p_ref[:, O_CIN + 2 * BR:O_CIN + 3 * BR]
        dcb[0:TM, :] = dyc_ref[...] * p_ref[:, O_CIN:O_CIN + BR] * _silu(p_ref[:, O_CG:O_CG + BR])
        dcb[TM:ext, :] = jnp.where(last, dycn_ref[...] * pn_ref[:, O_CIN:O_CIN + BR] * _silu(pn_ref[:, O_CG:O_CG + BR]), 0.0)
        for r0, nr in _conv_sub_blocks(TM):
            yc = jnp.zeros((nr, BR), F32)
            dz = jnp.zeros((nr, BR), F32)
            dcur = dcb[r0:r0 + nr, :]
            for k, zk in _causal_taps(zbuf, HALO, r0, nr, SHORT_CONV):
                yc = yc + scw_ref[k:k + 1, :] * zk
                dscw_ref[8 * k:8 * k + 8, :] += _fold8(dcur * zk)
            for k, dk in _anticausal_taps(dcb, r0, nr, SHORT_CONV):
                dz = dz + scw_ref[k:k + 1, :] * dk
            dy = dyc_ref[r0:r0 + nr, :]
            bgate = p_ref[r0:r0 + nr, O_CIN:O_CIN + BR]
            cg = p_ref[r0:r0 + nr, O_CG:O_CG + BR]
            dp_ref[r0:r0 + nr, O_CIN:O_CIN + BR] = (dy * yc * _silu(cg)).astype(BF16)
            dp_ref[r0:r0 + nr, O_CIN + BR:O_CIN + 2 * BR] = (dz * p_ref[r0:r0 + nr, O_CIN + 2 * BR:O_CIN + 3 * BR]).astype(BF16)
            dp_ref[r0:r0 + nr, O_CIN + 2 * BR:O_CIN + 3 * BR] = (dz * p_ref[r0:r0 + nr, O_CIN + BR:O_CIN + 2 * BR]).astype(BF16)
            dp_ref[r0:r0 + nr, O_CG:O_CG + BR] = (dy * bgate * yc * _dsilu(cg)).astype(BF16)

        hbuf[0:HALO, :] = jnp.where(first, ph_ref[:, O_GLU:O_GLU + BR] * _sig(ph_ref[:, O_GLU + BR:O_GLU + 2 * BR]), 0.0)
        hbuf[HALO:HALO + TM, :] = p_ref[:, O_GLU:O_GLU + BR] * _sig(p_ref[:, O_GLU + BR:O_GLU + 2 * BR])
        clg = clg_ref[...]
        clb = clb_ref[...]
        d_clg = jnp.zeros((1, BR), F32)
        d_clb = jnp.zeros((1, BR), F32)
        d_dwb = jnp.zeros((1, BR), F32)
        for r0, nr in _conv_sub_blocks(ext):
            in_tile = r0 < TM
            chat, rs = _ln_hat(cc_ref[r0:r0 + nr, :] if in_tile else ccn_ref[...])
            ln = chat * clg + clb
            if in_tile:
                dy = dyd_ref[r0:r0 + nr, :]
                dg = p_ref[r0:r0 + nr, O_DG:O_DG + BR]
            else:
                dy = jnp.where(last, dydn_ref[...], 0.0)
                dg = pn_ref[:, O_DG:O_DG + BR]
            dln = dy * _silu(dg) * _dsilu(ln)
            dc = _ln_bwd(dln * clg, chat, rs)
            dcc[r0:r0 + nr, :] = dc
            if in_tile:
                dp_ref[r0:r0 + nr, O_DG:O_DG + BR] = (dy * _silu(ln) * _dsilu(dg)).astype(BF16)
                d_clg = d_clg + jnp.sum(dln * chat, axis=0, keepdims=True)
                d_clb = d_clb + jnp.sum(dln, axis=0, keepdims=True)
                d_dwb = d_dwb + jnp.sum(dc, axis=0, keepdims=True)
        vec_ref[2:3, :] += d_dwb
        vec_ref[3:4, :] += d_clg
        vec_ref[4:5, :] += d_clb
        for r0, nr in _conv_sub_blocks(TM):
            dcur = dcc[r0:r0 + nr, :]
            dhh = jnp.zeros((nr, BR), F32)
            for k, hk in _causal_taps(hbuf, HALO, r0, nr, CONF_CONV):
                ddww_ref[8 * k:8 * k + 8, :] += _fold8(dcur * hk)
            for k, dk in _anticausal_taps(dcc, r0, nr, CONF_CONV):
                dhh = dhh + dww_ref[k:k + 1, :] * dk
            a = p_ref[r0:r0 + nr, O_GLU:O_GLU + BR]
            sg = _sig(p_ref[r0:r0 + nr, O_GLU + BR:O_GLU + 2 * BR])
            dp_ref[r0:r0 + nr, O_GLU:O_GLU + BR] = (dhh * sg).astype(BF16)
            dp_ref[r0:r0 + nr, O_GLU + BR:O_GLU + 2 * BR] = (dhh * a * sg * (1.0 - sg)).astype(BF16)

    full = lambda shape: pl.BlockSpec(shape, lambda i: tuple(0 for _ in shape))
    rpt = TM // HALO
    prev_map = lambda i: (jnp.maximum(i * rpt - 1, 0), 0)
    next_map = lambda i: (jnp.minimum((i + 1) * rpt, nt * rpt - 1), 0)
    ytile = pl.BlockSpec((TM, BR), lambda i: (i, 0))
    htile = pl.BlockSpec((N_HEADS, TM, HEAD_DIM), lambda i: (0, i, 0))
    return pl.pallas_call(
        body, grid=(nt,),
        in_specs=[pl.BlockSpec((TM, N_MIX), lambda i: (i, 0)), pl.BlockSpec((HALO, N_MIX), prev_map),
                  pl.BlockSpec((HALO, N_MIX), next_map),
                  ytile, pl.BlockSpec((HALO, BR), next_map),
                  ytile, ytile, pl.BlockSpec((HALO, BR), next_map), ytile, pl.BlockSpec((HALO, BR), next_map),
                  htile, htile, htile, ytile,
                  full((N_HEADS, CHUNK, CHUNK)), full((N_HEADS, CHUNK, CHUNK)), full((CHUNK, BR)), full((1, BR)), full((1, BR)),
                  full((8, BR)), full((32, BR)), full((1, BR)), full((1, BR)), full((1, BR))],
        out_specs=[pl.BlockSpec((TM, N_MIX), lambda i: (i, 0)), full((N_HEADS, CHUNK, CHUNK)), full((CHUNK, BR)),
                   full((16, BR)), full((64, BR)), full((256, BR))],
        out_shape=[jax.ShapeDtypeStruct((s, N_MIX), BF16), jax.ShapeDtypeStruct((N_HEADS, CHUNK, CHUNK), F32),
                   jax.ShapeDtypeStruct((CHUNK, BR), F32), jax.ShapeDtypeStruct((16, BR), F32),
                   jax.ShapeDtypeStruct((64, BR), F32), jax.ShapeDtypeStruct((256, BR), F32)],
        scratch_shapes=[pltpu.VMEM((HALO + TM, BR), F32), pltpu.VMEM((ext, BR), F32),
                        pltpu.VMEM((HALO + TM, BR), F32), pltpu.VMEM((ext, BR), F32)],
        compiler_params=_cp(("arbitrary",)), name=name)(
            pm, pm, pm, cc, cc, dya, dyc, dyc, dyd, dyd, *dqkv, dbg,
            wl["gw"], wl["gwt"], wl["gb"], wl["sgu_ln_g"], wl["sgu_ln_b"], wl["scw"], wl["dww"], wl["conf_dw_b"],
            wl["conf_ln_g"], wl["conf_ln_b"])


def _tri(lower):
    r = lax.broadcasted_iota(jnp.int32, (CUMB, CUMB), 0)
    c = lax.broadcasted_iota(jnp.int32, (CUMB, CUMB), 1)
    return jnp.where((r >= c) if lower else (r <= c), 1.0, 0.0).astype(F32)


def _dot_hi(a, b):
    return jnp.dot(a, b, preferred_element_type=F32, precision=lax.Precision.HIGHEST)


def _cum_fwd(pf, fb, name):
    s = pf.shape[0]

    def body(pf_ref, fb_ref, cum_ref, carry):
        i = pl.program_id(0)

        @pl.when(i == 0)
        def _():
            carry[...] = jnp.zeros_like(carry)

        z = pf_ref[...] + fb_ref[...]
        logf = jnp.minimum(z, 0.0) - jnp.log(1.0 + jnp.exp(-jnp.abs(z)))
        cum_ref[...] = _dot_hi(_tri(True), logf) + carry[...]
        carry[...] += jnp.sum(logf, axis=0, keepdims=True)

    return pl.pallas_call(
        body, grid=(s // CUMB,),
        in_specs=[pl.BlockSpec((CUMB, N_F), lambda i: (i, 0)), pl.BlockSpec((1, N_F), lambda i: (0, 0))],
        out_specs=pl.BlockSpec((CUMB, N_F), lambda i: (i, 0)),
        out_shape=jax.ShapeDtypeStruct((s, N_F), F32),
        scratch_shapes=[pltpu.VMEM((1, N_F), F32)],
        compiler_params=_cp(("arbitrary",)), name=name)(pf, fb)


def _cum_bwd(dcq, dck, pf, fb, name):
    s = pf.shape[0]
    nb = s // CUMB

    def body(dcq_ref, dck_ref, pf_ref, fb_ref, dpf_ref, dfb_ref, carry):
        i = pl.program_id(0)

        @pl.when(i == 0)
        def _():
            carry[...] = jnp.zeros_like(carry)
            dfb_ref[...] = jnp.zeros_like(dfb_ref)

        lane = lax.broadcasted_iota(jnp.int32, (1, N_F), 1)
        dc = dck_ref[...]
        for h in range(N_HEADS):
            dc = dc + jnp.where(lane == h, dcq_ref[h], 0.0)
        dlogf = _dot_hi(_tri(False), dc) + carry[...]
        carry[...] += jnp.sum(dc, axis=0, keepdims=True)
        z = pf_ref[...] + fb_ref[...]
        dz = dlogf * (1.0 - _sig(z))
        dpf_ref[...] = dz.astype(BF16)
        dfb_ref[...] += _fold8(dz)

    rev = lambda i: (nb - 1 - i, 0)
    return pl.pallas_call(
        body, grid=(nb,),
        in_specs=[pl.BlockSpec((N_HEADS, CUMB, 1), lambda i: (0, nb - 1 - i, 0)), pl.BlockSpec((CUMB, N_F), rev),
                  pl.BlockSpec((CUMB, N_F), rev), pl.BlockSpec((1, N_F), lambda i: (0, 0))],
        out_specs=[pl.BlockSpec((CUMB, N_F), rev), pl.BlockSpec((8, N_F), lambda i: (0, 0))],
        out_shape=[jax.ShapeDtypeStruct((s, N_F), BF16), jax.ShapeDtypeStruct((8, N_F), F32)],
        scratch_shapes=[pltpu.VMEM((1, N_F), F32)],
        compiler_params=_cp(("arbitrary",)), name=name)(dcq, dck, pf, fb)


def _causal_mask(nr, nc, r0, c0):
    r = lax.broadcasted_iota(jnp.int32, (nr, nc), 0) + r0
    c = lax.broadcasted_iota(jnp.int32, (nr, nc), 1) + c0
    return r >= c


def _attn_fwd(q, k, v, cq, ck, name, comm=None):
    s = q.shape[1]
    nb = s // FQ

    def body(q_ref, k_ref, v_ref, cq_ref, ck_ref, o_ref, lse_ref):
        for qi in range(nb):
            qs = qi * FQ
            qb = q_ref[0, qs:qs + FQ, :]
            cqb = cq_ref[0, qs:qs + FQ, :]

            def block(kj, carry, masked):
                m, l, acc = carry
                ks = pl.multiple_of(kj * FQ, FQ)
                kb = k_ref[0, pl.ds(ks, FQ), :]
                vb = v_ref[0, pl.ds(ks, FQ), :]
                sc = _dot_nt(qb, kb) + (cqb - ck_ref[0, kj])
                if masked:
                    sc = jnp.where(_causal_mask(FQ, FQ, 0, 0), sc, NEG)
                m_new = jnp.maximum(m, jnp.max(sc, axis=-1, keepdims=True))
                alpha = jnp.exp(m - m_new)
                p = jnp.exp(sc - m_new)
                l = alpha * l + jnp.sum(p, axis=-1, keepdims=True)
                acc = alpha * acc + _dot(p.astype(BF16), vb)
                return m_new, l, acc

            carry = (jnp.full((FQ, 1), NEG, F32), jnp.zeros((FQ, 1), F32), jnp.zeros((FQ, HEAD_DIM), F32))
            if qi > 0:
                carry = lax.fori_loop(0, qi, lambda kj, cr: block(kj, cr, False), carry)
            m, l, acc = block(qi, carry, True)
            o_ref[0, qs:qs + FQ, :] = acc / l
            lse_ref[0, qs:qs + FQ, :] = m + jnp.log(l)

    hblk = pl.BlockSpec((1, s, HEAD_DIM), lambda h: (h, 0, 0))
    cblk = pl.BlockSpec((1, s, 1), lambda h: (h, 0, 0))
    return _pcall(
        body, grid=(N_HEADS,),
        in_specs=[hblk, hblk, hblk, cblk, pl.BlockSpec((1, nb, 1, FQ), lambda h: (h, 0, 0, 0))],
        out_specs=[hblk, cblk],
        out_shape=[jax.ShapeDtypeStruct((N_HEADS, s, HEAD_DIM), F32), jax.ShapeDtypeStruct((N_HEADS, s, 1), F32)],
        operands=(q, k, v, cq, ck), name=name, comm=comm)


def _attn_bwd(q, k, v, cq, ck, o, lse, do, name, comm=None):
    s = q.shape[1]
    nq, nk = s // BQ, s // BK

    def body(q_ref, k_ref, v_ref, cq_ref, ck_ref, o_ref, lse_ref, do_ref, dq_ref, dk_ref, dv_ref, dcq_ref, dck_ref, delta):
        delta[...] = jnp.sum(do_ref[0] * o_ref[0], axis=-1, keepdims=True)
        dq_ref[...] = jnp.zeros_like(dq_ref)
        dcq_ref[...] = jnp.zeros_like(dcq_ref)
        for kj in range(nk):
            ks = kj * BK
            kb = k_ref[0, ks:ks + BK, :]
            vb = v_ref[0, ks:ks + BK, :]
            ckb = ck_ref[0, kj]

            def block(qs, nr, carry, masked):
                dk_acc, dv_acc, dck_acc = carry
                rows = pl.ds(qs, nr)
                qb = q_ref[0, rows, :]
                dob = do_ref[0, rows, :].astype(BF16)
                sc = _dot_nt(qb, kb) + (cq_ref[0, rows, :] - ckb)
                p = jnp.exp(sc - lse_ref[0, rows, :])
                if masked:
                    p = jnp.where(_causal_mask(nr, BK, 0, 0), p, 0.0)
                dp = _dot_nt(dob, vb)
                ds = p * (dp - delta[rows, :])
                ds_b = ds.astype(BF16)
                dv_acc = dv_acc + _dot_tn(p.astype(BF16), dob)
                dk_acc = dk_acc + _dot_tn(ds_b, qb)
                dq_ref[0, rows, :] += _dot(ds_b, kb) * SCALE
                dcq_ref[0, rows, :] += jnp.sum(ds, axis=-1, keepdims=True)
                dck_acc = dck_acc - jnp.sum(ds, axis=0, keepdims=True)
                return dk_acc, dv_acc, dck_acc

            carry = (jnp.zeros((BK, HEAD_DIM), F32), jnp.zeros((BK, HEAD_DIM), F32), jnp.zeros((1, BK), F32))
            carry = block(ks, BK, carry, True)
            below = ks + BK
            if below % BQ and below < s:
                carry = block(below, BK, carry, False)
                below += BK
            if below < s:
                carry = lax.fori_loop(below // BQ, nq, lambda qi, cr: block(pl.multiple_of(qi * BQ, BQ), BQ, cr, False), carry)
            dk_ref[0, ks:ks + BK, :] = carry[0]
            dv_ref[0, ks:ks + BK, :] = carry[1]
            dck_ref[0, kj] = carry[2]

    hblk = pl.BlockSpec((1, s, HEAD_DIM), lambda h: (h, 0, 0))
    cblk = pl.BlockSpec((1, s, 1), lambda h: (h, 0, 0))
    kblk = pl.BlockSpec((1, nk, 1, BK), lambda h: (h, 0, 0, 0))
    hshape = jax.ShapeDtypeStruct((N_HEADS, s, HEAD_DIM), F32)
    return _pcall(
        body, grid=(N_HEADS,),
        in_specs=[hblk, hblk, hblk, cblk, kblk, hblk, cblk, hblk],
        out_specs=[hblk, hblk, hblk, cblk, kblk],
        out_shape=[hshape, hshape, hshape, jax.ShapeDtypeStruct((N_HEADS, s, 1), F32),
                   jax.ShapeDtypeStruct((N_HEADS, nk, 1, BK), F32)],
        scratch_shapes=[pltpu.VMEM((s, 1), F32)],
        operands=(q, k, v, cq, ck, o, lse, do), name=name, comm=comm)


def _merge_fwd(x, ya, yc, yd, o, pm, pg, wb, wo, name, comm=None):
    s = x.shape[0]

    def body(x_ref, ya_ref, yc_ref, yd_ref, o_ref, bg_ref, pg_ref, wb_ref, wo_ref, xo_ref, yb_ref):
        o = jnp.concatenate([o_ref[h] for h in range(N_HEADS)], axis=1)
        yb = (o * _silu(bg_ref[...])).astype(BF16)
        yb_ref[...] = yb
        ys = (ya_ref[...], yb, yc_ref[...], yd_ref[...])
        out = x_ref[...]
        for c0 in range(0, D_MODEL, MERGE_COLS):
            merged = jnp.zeros((TMG, MERGE_COLS), F32)
            for n in range(N_BRANCH):
                merged = merged + _sig(pg_ref[:, n * D_MODEL + c0:n * D_MODEL + c0 + MERGE_COLS]) * _dot(
                    ys[n], wb_ref[n, :, c0:c0 + MERGE_COLS])
            out = out + _dot(merged.astype(BF16), wo_ref[c0:c0 + MERGE_COLS, :])
        xo_ref[...] = out

    xt = pl.BlockSpec((TMG, D_MODEL), lambda i: (i, 0))
    yt = pl.BlockSpec((TMG, BR), lambda i: (i, 0))
    return _pcall(
        body, grid=(s // TMG,),
        in_specs=[xt, yt, yt, yt, pl.BlockSpec((N_HEADS, TMG, HEAD_DIM), lambda i: (0, i, 0)),
                  pl.BlockSpec((TMG, BR), lambda i: (i, O_BG // BR)), pl.BlockSpec((TMG, N_MERGE), lambda i: (i, 0)),
                  pl.BlockSpec((N_BRANCH, BR, D_MODEL), lambda i: (0, 0, 0)), pl.BlockSpec((D_MODEL, D_MODEL), lambda i: (0, 0))],
        out_specs=[xt, yt],
        out_shape=[jax.ShapeDtypeStruct((s, D_MODEL), F32), jax.ShapeDtypeStruct((s, BR), BF16)],
        operands=(x, ya, yc, yd, o, pm, pg, wb, wo), name=name, comm=comm)


def _merge_bwd(dx, ya, yb, yc, yd, o, pm, pg, wb, wo, name, comm=None):
    s = dx.shape[0]
    nt = s // TMG

    def body(dx_ref, ya_ref, yb_ref, yc_ref, yd_ref, o_ref, bg_ref, pg_ref, wb_ref, wo_ref,
             dpg_ref, dya_ref, do_ref, dbg_ref, dyc_ref, dyd_ref, dwb_ref, dwo_ref, dwb_acc, dwo_acc):
        i = pl.program_id(0)

        @pl.when(i == 0)
        def _():
            dwb_acc[...] = jnp.zeros_like(dwb_acc)
            dwo_acc[...] = jnp.zeros_like(dwo_acc)

        dxb = dx_ref[...].astype(BF16)
        dmerged = _dot_nt(dxb, wo_ref[...])
        ys = (ya_ref[...], yb_ref[...], yc_ref[...], yd_ref[...])
        dys = (dya_ref, None, dyc_ref, dyd_ref)
        merged = jnp.zeros((TMG, D_MODEL), F32)
        for n in range(N_BRANCH):
            gate = _sig(pg_ref[:, n * D_MODEL:(n + 1) * D_MODEL])
            pr = _dot(ys[n], wb_ref[n])
            merged = merged + gate * pr
            dpg_ref[:, n * D_MODEL:(n + 1) * D_MODEL] = (dmerged * pr * gate * (1.0 - gate)).astype(BF16)
            dpr = (gate * dmerged).astype(BF16)
            dwb_acc[n] += _dot_tn(ys[n], dpr)
            dyn = _dot_nt(dpr, wb_ref[n])
            if n == 1:
                bg = bg_ref[...]
                do = dyn * _silu(bg)
                for h in range(N_HEADS):
                    do_ref[h] = do[:, h * HEAD_DIM:(h + 1) * HEAD_DIM]
                dbg_ref[...] = dyn * jnp.concatenate([o_ref[h] for h in range(N_HEADS)], axis=1) * _dsilu(bg)
            else:
                dys[n][...] = dyn
        dwo_acc[...] += _dot_tn(merged.astype(BF16), dxb)

        @pl.when(i == nt - 1)
        def _():
            dwb_ref[...] = dwb_acc[...].astype(BF16)
            dwo_ref[...] = dwo_acc[...].astype(BF16)

    xt = pl.BlockSpec((TMG, D_MODEL), lambda i: (i, 0))
    yt = pl.BlockSpec((TMG, BR), lambda i: (i, 0))
    gt = pl.BlockSpec((TMG, N_MERGE), lambda i: (i, 0))
    wbs = pl.BlockSpec((N_BRANCH, BR, D_MODEL), lambda i: (0, 0, 0))
    wos = pl.BlockSpec((D_MODEL, D_MODEL), lambda i: (0, 0))
    yf = jax.ShapeDtypeStruct((s, BR), F32)
    ht = pl.BlockSpec((N_HEADS, TMG, HEAD_DIM), lambda i: (0, i, 0))
    return _pcall(
        body, grid=(nt,),
        in_specs=[xt, yt, yt, yt, yt, ht, pl.BlockSpec((TMG, BR), lambda i: (i, O_BG // BR)), gt, wbs, wos],
        out_specs=[gt, yt, ht, yt, yt, yt, wbs, wos],
        out_shape=[jax.ShapeDtypeStruct((s, N_MERGE), BF16), yf, jax.ShapeDtypeStruct((N_HEADS, s, HEAD_DIM), F32), yf, yf, yf,
                   jax.ShapeDtypeStruct((N_BRANCH, BR, D_MODEL), BF16), jax.ShapeDtypeStruct((D_MODEL, D_MODEL), BF16)],
        scratch_shapes=[pltpu.VMEM((N_BRANCH, BR, D_MODEL), F32), pltpu.VMEM((D_MODEL, D_MODEL), F32)],
        operands=(dx, ya, yb, yc, yd, o, pm, pg, wb, wo), name=name, comm=comm)


def _layer_fwd(x, wl, tag, attach=None):
    attach = attach or {}

    def riding(stage):
        comm, sink = attach.get(stage, (None, None))
        return comm, (sink or (lambda res: None))

    s = x.shape[0]
    comm, sink = riding("rms_fwd")
    h, res = _rms_fwd(x, wl["norm_g"], "rms_fwd" + tag, comm)
    sink(res)
    comm, sink = riding("proj_mix")
    pm, res = _mm_nn(h, wl["w_mix"], 1792, "proj_mix" + tag, comm)
    sink(res)
    comm, sink = riding("mix_fwd")
    (ya, yc, yd, q, k, v, cc), res = _mix_fwd(pm, wl, "mix_fwd" + tag, comm)
    sink(res)
    comm, sink = riding("proj_merge")
    pg, res = _mm_nn(h, wl["w_merge"], 2048, "proj_merge" + tag, comm)
    sink(res)
    pf, _ = _mm_nn(h, wl["w_f"], N_F, "proj_f" + tag, tm=2048)
    cum = _cum_fwd(pf, wl["f_bias"], "cum_fwd" + tag)
    cum_t = cum[:, :N_HEADS].T
    cq = cum_t.reshape(N_HEADS, s, 1)
    ck = cum_t.reshape(N_HEADS, s // BK, 1, BK)
    comm, sink = riding("attn_fwd")
    (o, lse), res = _attn_fwd(q, k, v, cq, cum_t.reshape(N_HEADS, s // FQ, 1, FQ), "attn_fwd" + tag, comm)
    sink(res)
    comm, sink = riding("merge_fwd")
    (x_next, yb), res = _merge_fwd(x, ya, yc, yd, o, pm, pg, wl["wb"], wl["wo"], "merge_fwd" + tag, comm)
    sink(res)
    saved = dict(x=x, h=h, pm=pm, pg=pg, pf=pf, cc=cc, ya=ya, yb=yb, yc=yc, yd=yd, q=q, k=k, v=v, cq=cq, ck=ck, o=o, lse=lse)
    return x_next, saved


def _blocks_rows(g):
    return g.reshape(N_DEV, g.shape[0] // N_DEV, g.shape[1])


def _blocks_cols(g):
    return g.reshape(N_BRANCH * BR, N_DEV, D_MODEL // N_DEV).transpose(1, 0, 2)


def _layer_bwd(dx_next, sv, wl, tag, dist, riding, extra_small):
    s = dx_next.shape[0]
    (dpg, dya, do, dbg, dyc, dyd, dwb, dwo), rode = _merge_bwd(
        dx_next, sv["ya"], sv["yb"], sv["yc"], sv["yd"], sv["o"], sv["pm"], sv["pg"], wl["wb"], wl["wo"], "merge_bwd" + tag,
        riding)
    dw_merge = _mm_tn(sv["h"], dpg, 2048, "dw_merge" + tag)
    early = [(_blocks_rows(dw_merge), False), (_blocks_cols(dwb), False), (_blocks_rows(dwo), False)] if dist else None
    (dq, dk, dv, dcq, dck), early_out = _attn_bwd(sv["q"], sv["k"], sv["v"], sv["cq"], sv["ck"], sv["o"], sv["lse"], do,
                                                  "attn_bwd" + tag, early)
    dck_cols = jnp.pad(dck.reshape(N_HEADS, s).T, ((0, 0), (0, N_F - N_HEADS)))
    dpf, dfb = _cum_bwd(dcq, dck_cols, sv["pf"], wl["f_bias"], "cum_bwd" + tag)
    dpm, dgw, dgb, vec, dscw, ddww = _mix_bwd(sv["pm"], sv["cc"], dya, dyc, dyd, (dq, dk, dv), dbg, wl, "mix_bwd" + tag)
    dw_mix = _mm_tn(sv["h"], dpm, 1792, "dw_mix" + tag)
    dw_f = _mm_tn(sv["h"], dpf, N_F, "dw_f" + tag, tm=2048)
    causal = jnp.tril(jnp.ones((CHUNK, CHUNK), bool))
    small = dict(
        f_bias=dfb.sum(0)[:N_HEADS],
        sgu_w=jnp.where(causal[None], dgw, 0.0),
        sgu_b=dgb.reshape(CHUNK, N_HEADS, HEAD_DIM).sum(-1).T,
        sgu_ln_g=vec[0], sgu_ln_b=vec[1], conf_dw_b=vec[2], conf_ln_g=vec[3], conf_ln_b=vec[4],
        short_conv_w=dscw.reshape(8, 8, BR).sum(1)[:SHORT_CONV],
        conf_dw_w=ddww.reshape(32, 8, BR).sum(1)[:CONF_CONV],
    )
    slab, spans = _pack([small[nm] for nm in SMALL[1:]])
    late = [(_blocks_rows(dw_mix), False), (_blocks_rows(dw_f), False), (slab, True)] if dist else None
    dh, late_out = _dh(dpm, dpg, dpf, wl["w_mix"], wl["w_merge"], wl["w_f"], "dh" + tag, late)
    dx, dng = _rms_bwd(dh, sv["x"], wl["norm_g"], dx_next, "rms_bwd" + tag)
    small["norm_g"] = dng.sum(0)
    grads = dict(small, w_mix=dw_mix, w_merge=dw_merge, w_f=dw_f, wb=dwb, wo=dwo)
    last_slab, last_spans = _pack([small["norm_g"]] + list(extra_small))
    return dx, grads, early_out, (late_out, spans), ([(last_slab, True)], last_spans), rode


def _prep_layer_small(norm_g, f_bias, sgu_w, sgu_b, sgu_ln_g, sgu_ln_b, conf_dw_b, conf_ln_g, conf_ln_b):
    causal = jnp.tril(jnp.ones((CHUNK, CHUNK), bool))
    gw = jnp.where(causal[None], sgu_w, 0.0)
    row = lambda a: a.reshape(1, -1)
    return dict(
        norm_g=row(norm_g),
        f_bias=jnp.pad(row(f_bias), ((0, 0), (0, N_F - N_HEADS))),
        gw=gw.astype(BF16), gwt=gw.transpose(0, 2, 1).astype(BF16),
        gb=jnp.repeat(sgu_b.T, HEAD_DIM, axis=1),
        sgu_ln_g=row(sgu_ln_g), sgu_ln_b=row(sgu_ln_b),
        conf_dw_b=row(conf_dw_b), conf_ln_g=row(conf_ln_g), conf_ln_b=row(conf_ln_b))


def _local_step(x, target, layers, final_g):
    saved = []
    for l in range(DEPTH):
        x, sv = _layer_fwd(x, layers[l], str(l))
        saved.append(sv)
    loss_p, dx, dfg = _loss_head(x, final_g.reshape(1, D_MODEL), target)
    grads = [None] * DEPTH
    for l in reversed(range(DEPTH)):
        dx, grads[l], _, _, _, _ = _layer_bwd(dx, saved[l], layers[l], str(l), False, None, [])
    return 0.5 / D_MODEL * jnp.sum(loss_p), dx, grads, dfg.sum(0)


def _sum8(a, name):
    _, r, c = a.shape
    tr = r
    while tr * c * a.dtype.itemsize * N_DEV > 4 * 1024 * 1024 and tr % 32 == 0:
        tr //= 2

    def body(a_ref, o_ref):
        acc = a_ref[0].astype(F32)
        for d in range(1, N_DEV):
            acc = acc + a_ref[d].astype(F32)
        o_ref[...] = acc

    return pl.pallas_call(
        body, grid=(r // tr,),
        in_specs=[pl.BlockSpec((N_DEV, tr, c), lambda i: (0, i, 0))],
        out_specs=pl.BlockSpec((tr, c), lambda i: (i, 0)),
        out_shape=jax.ShapeDtypeStruct((r, c), F32), compiler_params=_cp(("parallel",)), name=name)(a)


def _adamw(w, g, m, v, name):
    l, r, c = w.shape
    tr = r
    while tr * c * 4 > 1024 * 1024 and tr % 16 == 0:
        tr //= 2
    c1 = 1.0 - ADAM_B1 ** ADAM_STEP
    c2 = 1.0 - ADAM_B2 ** ADAM_STEP

    def body(w_ref, g_ref, m_ref, v_ref, d_ref, mo_ref, vo_ref):
        gv = g_ref[...]
        mn = ADAM_B1 * m_ref[...] + (1.0 - ADAM_B1) * gv
        vn = ADAM_B2 * v_ref[...] + (1.0 - ADAM_B2) * (gv * gv)
        mo_ref[...] = mn
        vo_ref[...] = vn
        d_ref[...] = -ADAM_LR * ((mn / c1) / (jnp.sqrt(vn / c2) + ADAM_EPS) + ADAM_WD * w_ref[...])

    blk = pl.BlockSpec((1, tr, c), lambda a, i: (a, i, 0))
    shp = jax.ShapeDtypeStruct((l, r, c), F32)
    return pl.pallas_call(
        body, grid=(l, r // tr), in_specs=[blk] * 4, out_specs=[blk] * 3, out_shape=[shp] * 3,
        compiler_params=_cp(("parallel", "parallel")), name=name)(w, g, m, v)


def _pack(parts):
    rows, spans, r = [], [], 0
    for p in parts:
        flat = p.reshape(-1)
        nr = -(-flat.shape[0] // 1024) * 8
        rows.append(jnp.pad(flat, (0, nr * 128 - flat.shape[0])).reshape(nr, 128))
        spans.append((r, nr, p.shape))
        r += nr
    return jnp.concatenate(rows, axis=0), spans


def _unpack(slab, spans):
    out = []
    for r, nr, shape in spans:
        size = math.prod(shape)
        out.append(slab[r:r + nr].reshape(-1)[:size].reshape(shape))
    return out


def _split_w_in(w):
    mix = jnp.concatenate([w[..., 0:1536], w[..., 1540:1796], w[..., 3332:3588], w[..., 1796:2820], w[..., 2820:3332]], axis=-1)
    return mix, w[..., 3588:7684], w[..., 1536:1540]


def _join_w_in(mix, merge, f):
    return jnp.concatenate([mix[..., 0:1536], f, mix[..., 1536:1792], mix[..., 2048:3072], mix[..., 3072:3584],
                            mix[..., 1792:2048], merge], axis=-1)


SMALL = ("norm_g", "f_bias", "sgu_w", "sgu_b", "sgu_ln_g", "sgu_ln_b", "short_conv_w", "conf_dw_w", "conf_dw_b",
         "conf_ln_g", "conf_ln_b")


def kernel(x, norm_g, w_in, f_bias, sgu_w, sgu_b, sgu_ln_g, sgu_ln_b, short_conv_w, conf_dw_w, conf_dw_b, conf_ln_g, conf_ln_b, w_branch, w_out, final_g, loss_target, m_norm_g, m_w_in, m_f_bias, m_sgu_w, m_sgu_b, m_sgu_ln_g, m_sgu_ln_b, m_short_conv_w, m_conf_dw_w, m_conf_dw_b, m_conf_ln_g, m_conf_ln_b, m_w_branch, m_w_out, m_final_g, v_norm_g, v_w_in, v_f_bias, v_sgu_w, v_sgu_b, v_sgu_ln_g, v_sgu_ln_b, v_short_conv_w, v_conf_dw_w, v_conf_dw_b, v_conf_ln_g, v_conf_ln_b, v_w_branch, v_w_out, v_final_g):
    me = 4 * lax.axis_index("x") + 2 * lax.axis_index("y") + lax.axis_index("c")
    rows = D_MODEL // N_DEV
    cshard = BR // N_DEV

    sh = []
    for l in range(DEPTH):
        mix, merge, f = _split_w_in(w_in[l])
        sh.append(dict(mix=mix.astype(BF16), merge=merge.astype(BF16),
                       f=jnp.pad(f, ((0, 0), (0, N_F - N_HEADS))).astype(BF16),
                       wb=w_branch[l].astype(BF16), wo=w_out[l].astype(BF16)))
    conv_slab, conv_spans = _pack([short_conv_w, conf_dw_w])
    layers = [_prep_layer_small(norm_g[l], f_bias[l], sgu_w[l], sgu_b[l], sgu_ln_g[l], sgu_ln_b[l],
                                conf_dw_b[l], conf_ln_g[l], conf_ln_b[l]) for l in range(DEPTH)]
    half = N_MERGE // 2
    merge_halves = [[] for _ in range(DEPTH)]

    def put_first(res):
        conv_full = [_unpack(res[2][d], conv_spans) for d in range(N_DEV)]
        scw_full = jnp.concatenate([cf[0] for cf in conv_full], axis=-1)
        dww_full = jnp.concatenate([cf[1] for cf in conv_full], axis=-1)
        for l in range(DEPTH):
            layers[l].update(scw=jnp.pad(scw_full[l], ((0, 8 - SHORT_CONV), (0, 0))),
                             dww=jnp.pad(dww_full[l], ((0, 32 - CONF_CONV), (0, 0))))
        layers[0].update(w_mix=res[0].reshape(D_MODEL, N_MIX), w_f=res[1].reshape(D_MODEL, N_F))

    def put_in(l):
        def sink(res):
            layers[l].update(w_mix=res[0].reshape(D_MODEL, N_MIX), w_f=res[1].reshape(D_MODEL, N_F))
            put_merge_half(l)(res[2:])
        return sink

    def put_merge_half(l):
        def sink(res):
            merge_halves[l].append(res[0])
            if len(merge_halves[l]) == 2:
                layers[l].update(w_merge=jnp.concatenate(merge_halves[l], axis=-1).reshape(D_MODEL, N_MERGE))
        return sink

    def put_out(l):
        def sink(res):
            layers[l].update(wb=res[0].transpose(1, 2, 0, 3).reshape(N_BRANCH, BR, D_MODEL), wo=res[1].reshape(D_MODEL, D_MODEL))
        return sink

    attach0 = {
        "rms_fwd": ([(sh[0]["mix"], True), (sh[0]["f"], True), (conv_slab, True)], put_first),
        "proj_mix": ([(sh[0]["merge"][:, :half], True)], put_merge_half(0)),
        "mix_fwd": ([(sh[0]["merge"][:, half:], True)], put_merge_half(0)),
        "proj_merge": ([(sh[0]["wb"], True), (sh[0]["wo"], True)], put_out(0)),
        "attn_fwd": ([(sh[1]["mix"], True), (sh[1]["f"], True), (sh[1]["merge"][:, :half], True)], put_in(1)),
        "merge_fwd": ([(sh[1]["merge"][:, half:], True)], put_merge_half(1)),
    }
    attach1 = {"proj_mix": ([(sh[1]["wb"], True), (sh[1]["wo"], True)], put_out(1))}

    xs = x[0]
    xs, sv0 = _layer_fwd(xs, layers[0], "0", attach0)
    xs, sv1 = _layer_fwd(xs, layers[1], "1", attach1)
    loss_p, dx, dfg = _loss_head(xs, final_g.reshape(1, D_MODEL), loss_target[0])
    loss_local = (0.5 / D_MODEL * jnp.sum(loss_p)).reshape(1)
    dx, g1, early1, (late1, spans1), (last1, lspans1), _ = _layer_bwd(dx, sv1, layers[1], "1", True, None, [dfg.sum(0)])
    dx, g0, early0, (late0, spans0), (last0, lspans0), last1_out = _layer_bwd(dx, sv0, layers[0], "0", True, last1, [loss_local])
    last0_out = _exchange(last0, "gather_last")

    red, small = [], []
    for l, (early, late, spans, last, lspans) in enumerate(((early0, late0, spans0, last0_out, lspans0),
                                                            (early1, late1, spans1, last1_out, lspans1))):
        t = str(l)
        red.append(dict(merge=_sum8(early[0], "sum_merge" + t), wb=_sum8(early[1], "sum_wb" + t), wo=_sum8(early[2], "sum_wo" + t),
                        mix=_sum8(late[0], "sum_mix" + t), f=_sum8(late[1], "sum_f" + t)))
        keys = ["norm_g", "final_g" if l == DEPTH - 1 else "loss"] + list(SMALL[1:])
        small.append(dict(zip(keys, _unpack(_sum8(last[0], "sum_last" + t), lspans)
                              + _unpack(_sum8(late[2], "sum_small" + t), spans))))
    gs = {nm: jnp.stack([small[l][nm] for l in range(DEPTH)]) for nm in SMALL}
    gs["final_g"] = small[DEPTH - 1]["final_g"]
    loss = small[0]["loss"][0]
    gs["short_conv_w"] = lax.dynamic_slice_in_dim(gs["short_conv_w"], me * cshard, cshard, axis=2)
    gs["conf_dw_w"] = lax.dynamic_slice_in_dim(gs["conf_dw_w"], me * cshard, cshard, axis=2)
    g_w_in = jnp.stack([_join_w_in(red[l]["mix"], red[l]["merge"], red[l]["f"][:, :N_HEADS]) for l in range(DEPTH)])
    g_w_branch = jnp.stack([red[l]["wb"].reshape(N_BRANCH, BR, rows) for l in range(DEPTH)])
    g_w_out = jnp.stack([red[l]["wo"] for l in range(DEPTH)])

    d_w_in, nm_w_in, nv_w_in = _adamw(w_in, g_w_in, m_w_in, v_w_in, "adamw_w_in")
    flat = lambda a: a.reshape(DEPTH, N_BRANCH * BR, rows)
    d_w_branch, nm_w_branch, nv_w_branch = (a.reshape(w_branch.shape) for a in _adamw(
        flat(w_branch), flat(g_w_branch), flat(m_w_branch), flat(v_w_branch), "adamw_w_branch"))
    d_w_out, nm_w_out, nv_w_out = _adamw(w_out, g_w_out, m_w_out, v_w_out, "adamw_w_out")
    names = SMALL + ("final_g",)
    ws = dict(zip(names, (norm_g, f_bias, sgu_w, sgu_b, sgu_ln_g, sgu_ln_b, short_conv_w, conf_dw_w, conf_dw_b, conf_ln_g,
                          conf_ln_b, final_g)))
    ms = dict(zip(names, (m_norm_g, m_f_bias, m_sgu_w, m_sgu_b, m_sgu_ln_g, m_sgu_ln_b, m_short_conv_w, m_conf_dw_w,
                          m_conf_dw_b, m_conf_ln_g, m_conf_ln_b, m_final_g)))
    vs = dict(zip(names, (v_norm_g, v_f_bias, v_sgu_w, v_sgu_b, v_sgu_ln_g, v_sgu_ln_b, v_short_conv_w, v_conf_dw_w,
                          v_conf_dw_b, v_conf_ln_g, v_conf_ln_b, v_final_g)))
    w_slab, spans = _pack([ws[nm] for nm in names])
    g_slab, _ = _pack([gs[nm] for nm in names])
    m_slab, _ = _pack([ms[nm] for nm in names])
    v_slab, _ = _pack([vs[nm] for nm in names])
    d_s, nm_s, nv_s = (dict(zip(names, _unpack(a[0], spans))) for a in _adamw(w_slab[None], g_slab[None], m_slab[None],
                                                                              v_slab[None], "adamw_small"))

    def ordered(small, w_in_v, w_branch_v, w_out_v):
        return [small["norm_g"], w_in_v, small["f_bias"], small["sgu_w"], small["sgu_b"], small["sgu_ln_g"],
                small["sgu_ln_b"], small["short_conv_w"], small["conf_dw_w"], small["conf_dw_b"], small["conf_ln_g"],
                small["conf_ln_b"], w_branch_v, w_out_v, small["final_g"]]

    return (loss, dx[None], *ordered(gs, g_w_in, g_w_branch, g_w_out), *ordered(d_s, d_w_in, d_w_branch, d_w_out),
            *ordered(nm_s, nm_w_in, nm_w_branch, nm_w_out), *ordered(nv_s, nv_w_in, nv_w_branch, nv_w_out))
```
